```python
import math
import jax
import jax.numpy as jnp
from jax import lax
import numpy as np

D_MODEL = 2048
BATCH = 4
SEQ = 2048
DEPTH = 2
DEC_BATCH = 128
DEC_SEQ = 4
PAST_LEN = 16384
PAGE_SIZE = 128

N_EVEN = (DEPTH + 1) // 2
N_ODD = DEPTH // 2
DEEPNORM_ALPHA = (2 * DEPTH) ** 0.25
DEEPNORM_BETA = (8 * DEPTH) ** -0.25
LN_EPS = 1e-5

MIX_HALF = D_MODEL // 2
ML_HEADS = 4
ML_DV = MIX_HALF // ML_HEADS
ML_DK = ML_DV // 2
ML_CHUNK = 64
ML_GATE_CAP = 15.0
ML_NORM_EPS = 1e-6
ML_QK = ML_HEADS * ML_DK
ML_V = ML_HEADS * ML_DV
ML_SIZES = (ML_QK, ML_QK, ML_V, ML_V, ML_HEADS, ML_HEADS)
ML_COLS = sum(ML_SIZES)
RW_DH = 64
RW_HEADS = MIX_HALF // RW_DH
RW_WIDTH = RW_HEADS * RW_DH
RW_DECAY_LORA = 64
RW_A_LORA = 64
RW_GATE_LORA = 128
RW_GN_EPS = 64e-5
RW_SIZES = (RW_WIDTH, RW_WIDTH, RW_WIDTH, RW_DECAY_LORA, RW_A_LORA, RW_GATE_LORA)
RW_COLS = sum(RW_SIZES)
AB_COLS = ML_COLS + RW_COLS
AB_WIDTH = ML_V + RW_WIDTH
GD_DK = 128
GD_DV = 128
GD_K_HEADS = D_MODEL // GD_DK
GD_V_HEADS = 2 * GD_K_HEADS
GD_CONV = 4
GD_CHUNK = 64
GD_EPS = 1e-6
GD_QK = GD_K_HEADS * GD_DK
GD_V_WIDTH = GD_V_HEADS * GD_DV
GD_CONV_DIM = 2 * GD_QK + GD_V_WIDTH
GD_SIZES = (GD_CONV_DIM, GD_V_WIDTH, GD_V_HEADS, GD_V_HEADS)
GD_COLS = sum(GD_SIZES)
MOE_GROUPS = 4
MOE_PER_GROUP = 8
MOE_EXPERTS = MOE_GROUPS * MOE_PER_GROUP
MOE_TOPK = 2
MOE_FF = D_MODEL // 4
PLE_DIM = 256

kernel_name = 'hybrid_mlstm_rwkv7_gdn_hmoe_step'


def split_cols(a, sizes):
    return jnp.split(a, np.cumsum(sizes)[:-1].tolist(), axis=-1)


def layer_norm(x, g, b):
    xf = x.astype(jnp.float32)
    xc = xf - jnp.mean(xf, axis=-1, keepdims=True)
    var = jnp.mean(xc * xc, axis=-1, keepdims=True)
    return (xc * lax.rsqrt(var + LN_EPS) * g + b).astype(x.dtype)


def rms_norm(x, w, eps):
    return x * lax.rsqrt(jnp.mean(x * x, axis=-1, keepdims=True) + eps) * w


def l2_normalize(x, eps):
    return x * lax.rsqrt(jnp.sum(x * x, axis=-1, keepdims=True) + eps)


def soft_cap(x, cap):
    return cap * jnp.tanh(x / cap)


def to_chunks(a, n_chunks):
    b, t, h = a.shape[:3]
    a = a.reshape(b, n_chunks, t // n_chunks, h, *a.shape[3:])
    return jnp.moveaxis(a, (1, 3), (0, 2))


def from_chunks(a):
    a = jnp.moveaxis(a, (0, 2), (1, 3))
    return a.reshape(a.shape[0], a.shape[1] * a.shape[2], *a.shape[3:])


def mlstm_chunked(q, k, v, i_pre, log_f, c0, n0, m0):
    t = q.shape[1]
    nc = t // math.gcd(t, ML_CHUNK)
    size = t // nc
    causal = jnp.tril(jnp.ones((size, size), dtype=bool))

    def step(carry, inp):
        c, n, m = carry
        q_i, k_i, v_i, ig_i, lf_i = inp
        b = jnp.cumsum(lf_i, axis=-1)
        d = jnp.where(causal, b[..., :, None] - b[..., None, :] + ig_i[..., None, :], -jnp.inf)
        inter = b + m[..., None]
        m_t = jnp.maximum(inter, jnp.max(d, axis=-1))
        w_intra = jnp.exp(d - m_t[..., None])
        w_inter = jnp.exp(inter - m_t)
        s = jnp.einsum('bhtk,bhsk->bhts', q_i, k_i) * w_intra
        num = w_inter[..., None] * jnp.einsum('bhtk,bhkv->bhtv', q_i, c) + jnp.einsum('bhts,bhsv->bhtv', s, v_i)
        den = w_inter * jnp.einsum('bhtk,bhk->bht', q_i, n) + jnp.sum(s, axis=-1)
        h = num / jnp.maximum(jnp.abs(den), jnp.exp(-m_t))[..., None]
        w_last = w_intra[..., -1, :]
        decay = w_inter[..., -1]
        c = decay[..., None, None] * c + jnp.einsum('bhs,bhsk,bhsv->bhkv', w_last, k_i, v_i)
        n = decay[..., None] * n + jnp.einsum('bhs,bhsk->bhk', w_last, k_i)
        return (c, n, m_t[..., -1]), h

    xs = (to_chunks(q, nc), to_chunks(k, nc), to_chunks(v, nc), to_chunks(i_pre, nc), to_chunks(log_f, nc))
    (c, n, m), h = lax.scan(step, (c0, n0, m0), xs)
    return from_chunks(h), c, n, m


def mlstm_group(cols, c0, n0, m0, b_i, b_f, norm_w):
    bsz, t, _ = cols.shape
    q, k, v, o, i_pre, f_pre = split_cols(cols.astype(jnp.float32), ML_SIZES)
    q = q.reshape(bsz, t, ML_HEADS, ML_DK)
    k = k.reshape(bsz, t, ML_HEADS, ML_DK) * (ML_DK ** -0.5)
    v = v.reshape(bsz, t, ML_HEADS, ML_DV)
    i_pre = soft_cap(i_pre + b_i, ML_GATE_CAP)
    log_f = jax.nn.log_sigmoid(soft_cap(f_pre + b_f, ML_GATE_CAP))
    h, c, n, m = mlstm_chunked(q, k, v, i_pre, log_f, c0.astype(jnp.float32),
                               n0.astype(jnp.float32), m0.astype(jnp.float32))
    h = rms_norm(h, norm_w.reshape(ML_HEADS, ML_DV), ML_NORM_EPS)
    h = h * jax.nn.sigmoid(o).reshape(bsz, t, ML_HEADS, ML_DV)
    return h.reshape(bsz, t, ML_V), c, n, m


def rwkv7_scan(r, w_log, k, v, a, b, s0):
    def step(s, inp):
        r_t, w_t, k_t, v_t, a_t, b_t = inp
        sa = jnp.einsum('bhvk,bhk->bhv', s, a_t)
        s = s * jnp.exp(w_t)[:, :, None, :] + sa[..., None] * b_t[:, :, None, :] + v_t[..., None] * k_t[:, :, None, :]
        return s, jnp.einsum('bhvk,bhk->bhv', s, r_t)

    xs = tuple(jnp.moveaxis(z, 1, 0) for z in (r, w_log, k, v, a, b))
    s, out = lax.scan(step, s0, xs)
    return jnp.moveaxis(out, 0, 1), s


def rwkv_group(cols, shift0, s0, mu, w0, w2, a0, a2, g2, k_k, k_a, r_k, ln_w, ln_b):
    bsz, t, _ = cols.shape
    cols = cols.astype(jnp.float32)
    prev = jnp.concatenate([shift0.astype(jnp.float32)[:, None, :], cols[:, :-1]], axis=1)
    new_shift = cols[:, -1]
    mixed = cols + (prev - cols) * mu
    r, k, v, wl, al, gl = split_cols(mixed, RW_SIZES)
    w_log = -jnp.exp(-jax.nn.softplus(-(w0 + jnp.tanh(wl) @ w2)) - 0.5)
    a = jax.nn.sigmoid(a0 + al @ a2)
    g = jax.nn.sigmoid(gl) @ g2
    heads = lambda z: z.reshape(bsz, t, RW_HEADS, RW_DH)
    kk = heads(k * k_k)
    kk = kk / jnp.maximum(jnp.sqrt(jnp.sum(kk * kk, axis=-1, keepdims=True)), 1e-12)
    k = k * (1.0 + (a - 1.0) * k_a)
    r, k, v, a, w_log = heads(r), heads(k), heads(v), heads(a), heads(w_log)
    out, s = rwkv7_scan(r, w_log, k, v, -kk, kk * a, s0.astype(jnp.float32))
    oc = out - jnp.mean(out, axis=-1, keepdims=True)
    out = oc * lax.rsqrt(jnp.mean(oc * oc, axis=-1, keepdims=True) + RW_GN_EPS)
    out = out.reshape(bsz, t, RW_WIDTH) * ln_w + ln_b
    bonus = jnp.sum(r * k * r_k, axis=-1, keepdims=True) * v
    out = (out + bonus.reshape(bsz, t, RW_WIDTH)) * g
    return out, s, new_shift


def gated_delta_chunked(q, k, v, g, beta, s0):
    t = q.shape[1]
    nc = t // math.gcd(t, GD_CHUNK)
    size = t // nc
    qc, kc, vc = to_chunks(q, nc), to_chunks(k, nc), to_chunks(v, nc)
    gc = jnp.cumsum(to_chunks(g, nc), axis=-1)
    bc = to_chunks(beta, nc)
    incl = jnp.tril(jnp.ones((size, size), dtype=bool))
    strict = jnp.tril(jnp.ones((size, size), dtype=bool), -1)
    diff = gc[..., :, None] - gc[..., None, :]
    decay = jnp.where(incl, jnp.exp(jnp.where(incl, diff, 0.0)), 0.0)
    kb = kc * bc[..., None]
    a_mat = jnp.where(strict, jnp.einsum('...tk,...sk->...ts', kb, kc) * decay, 0.0)
    eye = jnp.eye(size, dtype=jnp.float32)
    tmat = lax.linalg.triangular_solve(eye + a_mat, jnp.broadcast_to(eye, a_mat.shape),
                                       left_side=True, lower=True, unit_diagonal=True)
    u = tmat @ (vc * bc[..., None])
    w = tmat @ (kb * jnp.exp(gc)[..., None])

    def step(s, inp):
        q_i, k_i, u_i, w_i, g_i, dec_i = inp
        v_new = u_i - w_i @ s
        attn = jnp.einsum('bhtk,bhsk->bhts', q_i, k_i) * dec_i
        o = (q_i * jnp.exp(g_i)[..., None]) @ s + attn @ v_new
        g_last = g_i[..., -1:]
        s = s * jnp.exp(g_last)[..., None] + jnp.einsum('bhsk,bhsv->bhkv', k_i * jnp.exp(g_last - g_i)[..., None], v_new)
        return s, o

    s, o = lax.scan(step, s0, (qc, kc, u, w, gc, decay))
    return from_chunks(o), s


def gdn_group(x, conv_buf, s0, w_in, conv_w, a_log, dt_bias, norm_w, w_out):
    bsz, t, _ = x.shape
    proj = jnp.einsum('btd,dc->btc', x, w_in).astype(jnp.float32)
    qkv, z, b, a = split_cols(proj, GD_SIZES)
    full = jnp.concatenate([conv_buf.astype(jnp.float32), qkv], axis=1)
    new_buf = full[:, t:]
    conv = lax.conv_general_dilated(full, conv_w.astype(jnp.float32)[:, None, :], (1,), 'VALID',
                                    dimension_numbers=('NWC', 'WIO', 'NWC'),
                                    feature_group_count=GD_CONV_DIM)
    q, k, v = split_cols(jax.nn.silu(conv), (GD_QK, GD_QK, GD_V_WIDTH))
    q = l2_normalize(q.reshape(bsz, t, GD_K_HEADS, GD_DK), GD_EPS) * (GD_DK ** -0.5)
    k = l2_normalize(k.reshape(bsz, t, GD_K_HEADS, GD_DK), GD_EPS)
    rep = GD_V_HEADS // GD_K_HEADS
    q = jnp.repeat(q, rep, axis=2)
    k = jnp.repeat(k, rep, axis=2)
    v = v.reshape(bsz, t, GD_V_HEADS, GD_DV)
    beta = jax.nn.sigmoid(b)
    g = -jnp.exp(a_log) * jax.nn.softplus(a + dt_bias)
    o, s = gated_delta_chunked(q, k, v, g, beta, s0.astype(jnp.float32))
    o = rms_norm(o, norm_w, GD_EPS) * jax.nn.silu(z.reshape(bsz, t, GD_V_HEADS, GD_DV))
    out = jnp.einsum('btc,cd->btd', o.reshape(bsz, t, GD_V_WIDTH).astype(x.dtype), w_out)
    return out, s, new_buf


def hier_moe(x, w_group, w_expert, w_gate, w_up, w_down):
    bsz, t, _ = x.shape
    group_prob = jax.nn.softmax(jnp.einsum('btd,dg->btg', x, w_group).astype(jnp.float32), axis=-1)
    g_val, g_idx = lax.top_k(group_prob, 1)
    e_logits = jnp.einsum('btd,de->bte', x, w_expert).astype(jnp.float32)
    e_logits = e_logits.reshape(bsz, t, MOE_GROUPS, MOE_PER_GROUP)
    in_group = jnp.take_along_axis(e_logits, g_idx[..., None], axis=2)[..., 0, :]
    top_val, top_idx = lax.top_k(in_group, MOE_TOPK)
    weights = jax.nn.softmax(top_val, axis=-1) * g_val
    expert_id = g_idx * MOE_PER_GROUP + top_idx
    combine = jnp.einsum('btk,btke->bte', weights, jax.nn.one_hot(expert_id, MOE_EXPERTS, dtype=jnp.float32))
    h = jax.nn.silu(jnp.einsum('btd,edf->btef', x, w_gate)) * jnp.einsum('btd,edf->btef', x, w_up)
    h = h * combine.astype(h.dtype)[..., None]
    return jnp.einsum('btef,efd->btd', h, w_down)


def trunk(x, p, st, prm):
    st_c, st_n, st_m, st_rs, st_sh, st_gs, st_gc = st
    new_c, new_n, new_m, new_rs, new_sh, new_gs, new_gc = [], [], [], [], [], [], []
    for layer in range(DEPTH):
        li = layer // 2
        if layer % 2 == 0:
            proj = jnp.einsum('btd,dc->btc', x, prm['w_in_ab'][li])
            h_ml, c, n, m = mlstm_group(proj[..., :ML_COLS], st_c[li], st_n[li], st_m[li],
                                        prm['ml_b_i'][li], prm['ml_b_f'][li], prm['ml_norm_w'][li])
            h_rw, rs, sh = rwkv_group(proj[..., ML_COLS:], st_sh[li], st_rs[li], prm['rw_mu'][li],
                                      prm['rw_w0'][li], prm['rw_w2'][li], prm['rw_a0'][li], prm['rw_a2'][li],
                                      prm['rw_g2'][li], prm['rw_k_k'][li], prm['rw_k_a'][li], prm['rw_r_k'][li],
                                      prm['rw_ln_w'][li], prm['rw_ln_b'][li])
            heads_out = jnp.concatenate([h_ml, h_rw], axis=-1).astype(x.dtype)
            mix = jnp.einsum('btc,cd->btd', heads_out, prm['w_out_ab'][li])
            new_c.append(c)
            new_n.append(n)
            new_m.append(m)
            new_rs.append(rs)
            new_sh.append(sh)
        else:
            mix, gs, gcb = gdn_group(x, st_gc[li], st_gs[li], prm['gd_w_in'][li], prm['gd_conv_w'][li],
                                     prm['gd_a_log'][li], prm['gd_dt_bias'][li], prm['gd_norm_w'][li],
                                     prm['gd_w_out'][li])
            new_gs.append(gs)
            new_gc.append(gcb)
        x = layer_norm(DEEPNORM_ALPHA * x + mix, prm['ln_mix_g'][layer], prm['ln_mix_b'][layer])
        ffn = hier_moe(x, prm['moe_w_group'][layer], prm['moe_w_expert'][layer], prm['moe_w_gate'][layer],
                       prm['moe_w_up'][layer], prm['moe_w_down'][layer])
        x = layer_norm(DEEPNORM_ALPHA * x + ffn, prm['ln_ffn_g'][layer], prm['ln_ffn_b'][layer])
        gate = jax.nn.sigmoid(jnp.einsum('btd,de->bte', x, prm['ple_w_gate'][layer]))
        x = x + gate * jnp.einsum('btp,pd->btd', p[layer], prm['ple_w_proj'][layer])
    return x, (jnp.stack(new_c), jnp.stack(new_n), jnp.stack(new_m), jnp.stack(new_rs),
               jnp.stack(new_sh), jnp.stack(new_gs), jnp.stack(new_gc))


def setup_inputs(seed: int = 0) -> dict:
    key = jax.random.key(seed)
    ks = iter(jax.random.split(key, 64))
    nrm = lambda shape, scale: jax.random.normal(next(ks), shape, jnp.float32) * scale
    uni = lambda shape, lo, hi: jax.random.uniform(next(ks), shape, jnp.float32, lo, hi)
    d = D_MODEL
    return {
        'x_prompt': nrm((BATCH, SEQ, d), 1.0),
        'x_sample': nrm((DEC_BATCH, DEC_SEQ, d), 1.0),
        'state_mlstm_c': nrm((N_EVEN, DEC_BATCH, ML_HEADS, ML_DK, ML_DV), 0.1),
        'state_mlstm_n': nrm((N_EVEN, DEC_BATCH, ML_HEADS, ML_DK), 0.5),
        'state_mlstm_m': nrm((N_EVEN, DEC_BATCH, ML_HEADS), 1.0),
        'state_rwkv_s': nrm((N_EVEN, DEC_BATCH, RW_HEADS, RW_DH, RW_DH), 0.1),
        'state_rwkv_shift': nrm((N_EVEN, DEC_BATCH, RW_COLS), 1.0),
        'state_gdn_s': nrm((N_ODD, DEC_BATCH, GD_V_HEADS, GD_DK, GD_DV), 0.05),
        'state_gdn_conv': nrm((N_ODD, DEC_BATCH, GD_CONV - 1, GD_CONV_DIM), 1.0),
        'p_prompt': nrm((DEPTH, BATCH, SEQ, PLE_DIM), 1.0),
        'p_sample': nrm((DEPTH, DEC_BATCH, DEC_SEQ, PLE_DIM), 1.0),
        'w_in_ab': nrm((N_EVEN, d, AB_COLS), d ** -0.5),
        'ml_b_i': nrm((N_EVEN, ML_HEADS), 0.5),
        'ml_b_f': uni((N_EVEN, ML_HEADS), 3.0, 6.0),
        'ml_norm_w': 1.0 + nrm((N_EVEN, ML_V), 0.02),
        'rw_mu': uni((N_EVEN, RW_COLS), 0.0, 1.0),
        'rw_w0': nrm((N_EVEN, RW_WIDTH), 0.5) - 1.0,
        'rw_w2': nrm((N_EVEN, RW_DECAY_LORA, RW_WIDTH), 0.1),
        'rw_a0': nrm((N_EVEN, RW_WIDTH), 0.1),
        'rw_a2': nrm((N_EVEN, RW_A_LORA, RW_WIDTH), 0.1),
        'rw_g2': nrm((N_EVEN, RW_GATE_LORA, RW_WIDTH), RW_GATE_LORA ** -0.5),
        'rw_k_k': 0.85 + nrm((N_EVEN, RW_WIDTH), 0.05),
        'rw_k_a': 1.0 + nrm((N_EVEN, RW_WIDTH), 0.05),
        'rw_r_k': nrm((N_EVEN, RW_HEADS, RW_DH), 0.1),
        'rw_ln_w': 1.0 + nrm((N_EVEN, RW_WIDTH), 0.02),
        'rw_ln_b': nrm((N_EVEN, RW_WIDTH), 0.02),
        'w_out_ab': nrm((N_EVEN, AB_WIDTH, d), AB_WIDTH ** -0.5 * DEEPNORM_BETA),
        'gd_w_in': nrm((N_ODD, d, GD_COLS), d ** -0.5),
        'gd_conv_w': nrm((N_ODD, GD_CONV, GD_CONV_DIM), 0.5),
        'gd_a_log': jnp.log(uni((N_ODD, GD_V_HEADS), 1.0, 16.0)),
        'gd_dt_bias': nrm((N_ODD, GD_V_HEADS), 0.1) - 2.0,
        'gd_norm_w': 1.0 + nrm((N_ODD, GD_DV), 0.02),
        'gd_w_out': nrm((N_ODD, GD_V_WIDTH, d), GD_V_WIDTH ** -0.5 * DEEPNORM_BETA),
        'ln_mix_g': 1.0 + nrm((DEPTH, d), 0.02),
        'ln_mix_b': nrm((DEPTH, d), 0.02),
        'moe_w_group': nrm((DEPTH, d, MOE_GROUPS), d ** -0.5),
        'moe_w_expert': nrm((DEPTH, d, MOE_EXPERTS), d ** -0.5),
        'moe_w_gate': nrm((DEPTH, MOE_EXPERTS, d, MOE_FF), d ** -0.5),
        'moe_w_up': nrm((DEPTH, MOE_EXPERTS, d, MOE_FF), d ** -0.5),
        'moe_w_down': nrm((DEPTH, MOE_EXPERTS, MOE_FF, d), MOE_FF ** -0.5 * DEEPNORM_BETA),
        'ln_ffn_g': 1.0 + nrm((DEPTH, d), 0.02),
        'ln_ffn_b': nrm((DEPTH, d), 0.02),
        'ple_w_gate': nrm((DEPTH, d, d), d ** -0.5),
        'ple_w_proj': nrm((DEPTH, PLE_DIM, d), PLE_DIM ** -0.5 * DEEPNORM_BETA),
    }


def reference(x_prompt, x_sample, state_mlstm_c, state_mlstm_n, state_mlstm_m, state_rwkv_s,
              state_rwkv_shift, state_gdn_s, state_gdn_conv, p_prompt, p_sample,
              w_in_ab, ml_b_i, ml_b_f, ml_norm_w, rw_mu, rw_w0, rw_w2, rw_a0, rw_a2, rw_g2,
              rw_k_k, rw_k_a, rw_r_k, rw_ln_w, rw_ln_b, w_out_ab,
              gd_w_in, gd_conv_w, gd_a_log, gd_dt_bias, gd_norm_w, gd_w_out,
              ln_mix_g, ln_mix_b, moe_w_group, moe_w_expert, moe_w_gate, moe_w_up, moe_w_down,
              ln_ffn_g, ln_ffn_b, ple_w_gate, ple_w_proj):
    prm = dict(w_in_ab=w_in_ab, ml_b_i=ml_b_i, ml_b_f=ml_b_f, ml_norm_w=ml_norm_w, rw_mu=rw_mu,
               rw_w0=rw_w0, rw_w2=rw_w2, rw_a0=rw_a0, rw_a2=rw_a2, rw_g2=rw_g2, rw_k_k=rw_k_k,
               rw_k_a=rw_k_a, rw_r_k=rw_r_k, rw_ln_w=rw_ln_w, rw_ln_b=rw_ln_b, w_out_ab=w_out_ab,
               gd_w_in=gd_w_in, gd_conv_w=gd_conv_w, gd_a_log=gd_a_log, gd_dt_bias=gd_dt_bias,
               gd_norm_w=gd_norm_w, gd_w_out=gd_w_out, ln_mix_g=ln_mix_g, ln_mix_b=ln_mix_b,
               moe_w_group=moe_w_group, moe_w_expert=moe_w_expert, moe_w_gate=moe_w_gate,
               moe_w_up=moe_w_up, moe_w_down=moe_w_down, ln_ffn_g=ln_ffn_g, ln_ffn_b=ln_ffn_b,
               ple_w_gate=ple_w_gate, ple_w_proj=ple_w_proj)
    nb = x_prompt.shape[0]
    f32 = jnp.float32
    zero_state = (jnp.zeros((N_EVEN, nb, ML_HEADS, ML_DK, ML_DV), f32),
                  jnp.zeros((N_EVEN, nb, ML_HEADS, ML_DK), f32),
                  jnp.zeros((N_EVEN, nb, ML_HEADS), f32),
                  jnp.zeros((N_EVEN, nb, RW_HEADS, RW_DH, RW_DH), f32),
                  jnp.zeros((N_EVEN, nb, RW_COLS), f32),
                  jnp.zeros((N_ODD, nb, GD_V_HEADS, GD_DK, GD_DV), f32),
                  jnp.zeros((N_ODD, nb, GD_CONV - 1, GD_CONV_DIM), f32))
    y_prompt, prompt_state = trunk(x_prompt, p_prompt, zero_state, prm)
    p_ml_c, p_ml_n, p_ml_m, p_rw_s, p_rw_shift, p_gd_s, p_gd_conv = prompt_state
    sample_in = (state_mlstm_c, state_mlstm_n, state_mlstm_m, state_rwkv_s, state_rwkv_shift,
                 state_gdn_s, state_gdn_conv)
    y_sample, sample_state = trunk(x_sample, p_sample, sample_in, prm)
    s_ml_c, s_ml_n, s_ml_m, s_rw_s, s_rw_shift, s_gd_s, s_gd_conv = sample_state
    return (y_prompt, y_sample, p_ml_c, p_ml_n, p_ml_m, p_rw_s, p_rw_shift, p_gd_s, p_gd_conv,
            s_ml_c, s_ml_n, s_ml_m, s_rw_s, s_rw_shift, s_gd_s, s_gd_conv)
```

```python
import functools

import jax
import jax.numpy as jnp
import numpy as np
from jax import lax
from jax.experimental import pallas as pl
from jax.experimental.pallas import tpu as pltpu

F32 = jnp.float32
BF16 = jnp.bfloat16

D_MODEL = 2048
DEPTH = 2
DEEPNORM_ALPHA = (2 * DEPTH) ** 0.25
LN_EPS = 1e-5
MIX_HALF = D_MODEL // 2
ML_HEADS = 4
ML_DV = MIX_HALF // ML_HEADS
ML_DK = ML_DV // 2
ML_CHUNK = 64
ML_GATE_CAP = 15.0
ML_NORM_EPS = 1e-6
ML_QK = ML_HEADS * ML_DK
ML_V = ML_HEADS * ML_DV
ML_COLS = 2 * ML_QK + 2 * ML_V + 2 * ML_HEADS
RW_DH = 64
RW_HEADS = MIX_HALF // RW_DH
RW_WIDTH = RW_HEADS * RW_DH
RW_DECAY_LORA = 64
RW_A_LORA = 64
RW_GATE_LORA = 128
RW_GN_EPS = 64e-5
RW_COLS = 3 * RW_WIDTH + RW_DECAY_LORA + RW_A_LORA + RW_GATE_LORA
RW_CHUNK = 64
GD_DK = 128
GD_DV = 128
GD_K_HEADS = D_MODEL // GD_DK
GD_V_HEADS = 2 * GD_K_HEADS
GD_CONV = 4
GD_CHUNK = 64
GD_EPS = 1e-6
GD_QK = GD_K_HEADS * GD_DK
GD_V_WIDTH = GD_V_HEADS * GD_DV
GD_CONV_DIM = 2 * GD_QK + GD_V_WIDTH
MOE_GROUPS = 4
MOE_PER_GROUP = 8
MOE_EXPERTS = MOE_GROUPS * MOE_PER_GROUP
MOE_TOPK = 2
MOE_FF = D_MODEL // 4
PLE_DIM = 256

LANES = 128
SUBLANES = 8
NEG = -1e30
VMEM_LIMIT = 56 * 1024 * 1024


def _cparams(sem):
    return pltpu.CompilerParams(dimension_semantics=sem, vmem_limit_bytes=VMEM_LIMIT)


def _dot(a, b):
    return jnp.dot(a.astype(BF16), b.astype(BF16), preferred_element_type=F32)


def _dot_nt(a, b):
    return lax.dot_general(a.astype(BF16), b.astype(BF16), (((1,), (1,)), ((), ())),
                           preferred_element_type=F32)


def _dot_tn(a, b):
    return lax.dot_general(a.astype(BF16), b.astype(BF16), (((0,), (0,)), ((), ())),
                           preferred_element_type=F32)


def _split3(x):
    hi = x.astype(BF16)
    r1 = x - hi.astype(F32)
    mid = r1.astype(BF16)
    lo = (r1 - mid.astype(F32)).astype(BF16)
    return hi, mid, lo


def _dot_exact_rhs(a, b01):
    hi, mid, lo = _split3(a)
    b = b01.astype(BF16)
    return (jnp.dot(hi, b, preferred_element_type=F32) + jnp.dot(mid, b, preferred_element_type=F32)
            + jnp.dot(lo, b, preferred_element_type=F32))


def _dot_exact_lhs(a01, b):
    hi, mid, lo = _split3(b)
    a = a01.astype(BF16)
    return (jnp.dot(a, hi, preferred_element_type=F32) + jnp.dot(a, mid, preferred_element_type=F32)
            + jnp.dot(a, lo, preferred_element_type=F32))


def _iota2(shape, dim):
    return lax.broadcasted_iota(jnp.int32, shape, dim)


def _col_to_row(col, eye):
    return jnp.sum(jnp.where(eye, col, 0.0), axis=0, keepdims=True)


def _sigmoid(x):
    return 1.0 / (1.0 + jnp.exp(-x))


def _softplus(x):
    return jnp.maximum(x, 0.0) + jnp.log1p(jnp.exp(-jnp.abs(x)))


def _neumann_inverse(n_mat, size):
    eye = (_iota2((size, size), 0) == _iota2((size, size), 1)).astype(F32)
    t = eye + n_mat
    x = n_mat
    steps = max(int(np.ceil(np.log2(size))) - 1, 0)
    for _ in range(steps):
        x = _dot(x, x)
        t = t + _dot(t, x)
    return t


def _mm_kernel(x_ref, w_ref, o_ref, wb_ref):
    @pl.when(pl.program_id(1) == 0)
    def _():
        wb_ref[...] = w_ref[...].astype(BF16)

    o_ref[...] = jnp.dot(x_ref[...].astype(BF16), wb_ref[...],
                         preferred_element_type=F32).astype(o_ref.dtype)


def matmul(x, w, n_out, tn, tm=512, out_dtype=F32):
    m, k = x.shape
    assert w.shape[0] == k and n_out % tn == 0 and m % tm == 0 and tn % LANES == 0
    return pl.pallas_call(
        _mm_kernel,
        grid=(n_out // tn, m // tm),
        in_specs=[pl.BlockSpec((tm, k), lambda j, i: (i, 0)),
                  pl.BlockSpec((k, tn), lambda j, i: (0, j))],
        out_specs=pl.BlockSpec((tm, tn), lambda j, i: (i, j)),
        out_shape=jax.ShapeDtypeStruct((m, n_out), out_dtype),
        scratch_shapes=[pltpu.VMEM((k, tn), BF16)],
        compiler_params=_cparams(("arbitrary", "arbitrary")),
        name="matmul",
    )(x, w)


def _ln_res_kernel(x_ref, y_ref, g_ref, b_ref, o_ref, ob_ref):
    z = DEEPNORM_ALPHA * x_ref[...] + y_ref[...]
    zc = z - jnp.mean(z, axis=-1, keepdims=True)
    var = jnp.mean(zc * zc, axis=-1, keepdims=True)
    out = zc * lax.rsqrt(var + LN_EPS) * g_ref[...] + b_ref[...]
    o_ref[...] = out
    ob_ref[...] = out.astype(BF16)


def ln_residual(x, y, g, b, tm=256):
    m, d = x.shape
    row = pl.BlockSpec((tm, d), lambda i: (i, 0))
    vec = pl.BlockSpec((1, d), lambda i: (0, 0))
    return pl.pallas_call(
        _ln_res_kernel,
        grid=(m // tm,),
        in_specs=[row, row, vec, vec],
        out_specs=[row, row],
        out_shape=[jax.ShapeDtypeStruct((m, d), F32), jax.ShapeDtypeStruct((m, d), BF16)],
        compiler_params=_cparams(("arbitrary",)),
        name="ln_residual",
    )(x, y, g.reshape(1, d), b.reshape(1, d))


def _ple_kernel(xb_ref, p_ref, x_ref, wg_ref, wp_ref, o_ref, ob_ref, wgb_ref, wpb_ref):
    @pl.when(pl.program_id(1) == 0)
    def _():
        wgb_ref[...] = wg_ref[...].astype(BF16)
        wpb_ref[...] = wp_ref[...].astype(BF16)

    gate = _sigmoid(jnp.dot(xb_ref[...], wgb_ref[...], preferred_element_type=F32))
    emb = jnp.dot(p_ref[...].astype(BF16), wpb_ref[...], preferred_element_type=F32)
    out = x_ref[...] + gate * emb
    o_ref[...] = out
    ob_ref[...] = out.astype(BF16)


def ple(xb, p, x, wg, wp, tm=512, tn=512):
    m, d = x.shape
    pd = p.shape[1]
    return pl.pallas_call(
        _ple_kernel,
        grid=(d // tn, m // tm),
        in_specs=[pl.BlockSpec((tm, d), lambda j, i: (i, 0)),
                  pl.BlockSpec((tm, pd), lambda j, i: (i, 0)),
                  pl.BlockSpec((tm, tn), lambda j, i: (i, j)),
                  pl.BlockSpec((d, tn), lambda j, i: (0, j)),
                  pl.BlockSpec((pd, tn), lambda j, i: (0, j))],
        out_specs=[pl.BlockSpec((tm, tn), lambda j, i: (i, j))] * 2,
        out_shape=[jax.ShapeDtypeStruct((m, d), F32), jax.ShapeDtypeStruct((m, d), BF16)],
        scratch_shapes=[pltpu.VMEM((d, tn), BF16), pltpu.VMEM((pd, tn), BF16)],
        compiler_params=_cparams(("arbitrary", "arbitrary")),
        name="ple",
    )(xb, p, x, wg, wp)


def _mlstm_kernel(q_ref, k_ref, v_ref, o_ref, g_ref, c0_ref, n0_ref, m0_ref, nw_ref, gb_ref,
                  h_ref, c_ref, n_ref, m_ref, *, chunk, t_valid):
    ci = pl.program_id(1)

    @pl.when(ci == 0)
    def _():
        c_ref[...] = c0_ref[...]
        n_ref[...] = n0_ref[...]
        m_ref[...] = m0_ref[...]

    size = chunk
    rows = _iota2((size, size), 0)
    cols = _iota2((size, size), 1)
    eye = rows == cols
    causal = cols <= rows
    gates = g_ref[:, 0:2 * ML_HEADS] + gb_ref[...]
    capped = ML_GATE_CAP * jnp.tanh(gates / ML_GATE_CAP)
    t_idx = ci * size + _iota2((size, 1), 0)
    valid = t_idx < t_valid
    ipre_all = jnp.where(valid, capped[:, 0:ML_HEADS], NEG)
    logf_all = jnp.where(valid, -_softplus(-capped[:, ML_HEADS:2 * ML_HEADS]), 0.0)
    for h in range(ML_HEADS):
        q = q_ref[:, h * ML_DK:(h + 1) * ML_DK]
        k = k_ref[:, h * ML_DK:(h + 1) * ML_DK] * (ML_DK ** -0.5)
        v = v_ref[:, h * ML_DV:(h + 1) * ML_DV]
        ig_col = ipre_all[:, h:h + 1]
        lf_col = logf_all[:, h:h + 1]
        ig_row = _col_to_row(ig_col, eye)
        lf_row = _col_to_row(lf_col, eye)
        b_col = jnp.sum(jnp.where(causal, lf_row, 0.0), axis=1, keepdims=True)
        b_row = jnp.sum(jnp.where(rows <= cols, lf_col, 0.0), axis=0, keepdims=True)
        d = jnp.where(causal, b_col - b_row + ig_row, NEG)
        m_prev = m_ref[0, :, h:h + 1]
        inter = b_col + m_prev
        m_t = jnp.maximum(inter, jnp.max(d, axis=1, keepdims=True))
        w_intra = jnp.exp(d - m_t)
        w_inter = jnp.exp(inter - m_t)
        s = _dot_nt(q, k) * w_intra
        c_prev = c_ref[0, h]
        n_prev = n_ref[0, h:h + 1, :]
        num = w_inter * _dot(q, c_prev) + _dot(s, v)
        den = w_inter * jnp.sum(q * n_prev, axis=1, keepdims=True) + jnp.sum(s, axis=1, keepdims=True)
        hid = num / jnp.maximum(jnp.abs(den), jnp.exp(-m_t))
        b_last = b_col[size - 1:size, :]
        m_last = m_t[size - 1:size, :]
        w_last = jnp.exp(b_last - b_col + ig_col - m_last)
        decay = w_inter[size - 1:size, :]
        kw = k * w_last
        c_ref[0, h] = decay * c_prev + _dot_tn(kw, v)
        n_ref[0, h:h + 1, :] = decay * n_prev + jnp.sum(kw, axis=0, keepdims=True)
        m_ref[0, :, h:h + 1] = m_last
        hid = hid * lax.rsqrt(jnp.mean(hid * hid, axis=-1, keepdims=True) + ML_NORM_EPS)
        hid = hid * nw_ref[:, h * ML_DV:(h + 1) * ML_DV]
        hid = hid * _sigmoid(o_ref[:, h * ML_DV:(h + 1) * ML_DV])
        h_ref[:, h * ML_DV:(h + 1) * ML_DV] = hid.astype(h_ref.dtype)


def mlstm_group(proj, row0, bsz, t_pad, t_valid, chunk, c0, n0, m0, b_i, b_f, norm_w):
    nc = t_pad // chunk
    blk0 = row0 // chunk
    assert row0 % chunk == 0 and t_pad % chunk == 0
    rmap = lambda c0_: (lambda b, c: (blk0 + b * nc + c, c0_))
    qk_w, v_w = ML_QK, ML_V
    in_specs = [
        pl.BlockSpec((chunk, qk_w), rmap(0)),
        pl.BlockSpec((chunk, qk_w), rmap(1)),
        pl.BlockSpec((chunk, v_w), rmap(1)),
        pl.BlockSpec((chunk, v_w), rmap(2)),
        pl.BlockSpec((chunk, LANES), rmap((2 * qk_w + 2 * v_w) // LANES)),
        pl.BlockSpec((1, ML_HEADS, ML_DK, ML_DV), lambda b, c: (b, 0, 0, 0)),
        pl.BlockSpec((1, ML_HEADS, ML_DK), lambda b, c: (b, 0, 0)),
        pl.BlockSpec((1, 1, ML_HEADS), lambda b, c: (b, 0, 0)),
        pl.BlockSpec((1, v_w), lambda b, c: (0, 0)),
        pl.BlockSpec((1, 2 * ML_HEADS), lambda b, c: (0, 0)),
    ]
    out_specs = [
        pl.BlockSpec((chunk, v_w), lambda b, c: (b * nc + c, 0)),
        pl.BlockSpec((1, ML_HEADS, ML_DK, ML_DV), lambda b, c: (b, 0, 0, 0)),
        pl.BlockSpec((1, ML_HEADS, ML_DK), lambda b, c: (b, 0, 0)),
        pl.BlockSpec((1, 1, ML_HEADS), lambda b, c: (b, 0, 0)),
    ]
    out_shape = [
        jax.ShapeDtypeStruct((bsz * t_pad, v_w), BF16),
        jax.ShapeDtypeStruct((bsz, ML_HEADS, ML_DK, ML_DV), F32),
        jax.ShapeDtypeStruct((bsz, ML_HEADS, ML_DK), F32),
        jax.ShapeDtypeStruct((bsz, 1, ML_HEADS), F32),
    ]
    gate_bias = jnp.concatenate([b_i, b_f]).reshape(1, 2 * ML_HEADS)
    h, c, n, m = pl.pallas_call(
        functools.partial(_mlstm_kernel, chunk=chunk, t_valid=t_valid),
        grid=(bsz, nc),
        in_specs=in_specs,
        out_specs=out_specs,
        out_shape=out_shape,
        compiler_params=_cparams(("arbitrary", "arbitrary")),
        name="mlstm",
    )(proj, proj, proj, proj, proj, c0, n0, m0.reshape(bsz, 1, ML_HEADS), norm_w.reshape(1, v_w), gate_bias)
    return h, c, n, m.reshape(bsz, ML_HEADS)


def _head_block_ones(width, head):
    idx = np.arange(width) // head
    return jnp.asarray((idx[:, None] == idx[None, :]).astype(np.float32), dtype=BF16)


def _rwkv_prep_kernel(cur_ref, tail_ref, first_ref, mu_ref, w0_ref, a0_ref, kk_ref, ka_ref, rk_ref,
                      w2_ref, a2_ref, g2_ref, blk_ref,
                      r_ref, wl_ref, k_ref, v_ref, an_ref, bb_ref, bonus_ref, g_ref,
                      *, tt, t_pad, multi):
    cur = cur_ref[...]
    rolled = pltpu.roll(cur, 1, 0)
    row = _iota2((tt, 1), 0)
    if multi:
        prev = jnp.where((row & (t_pad - 1)) == 0, first_ref[...], rolled)
    else:
        starts_seq = (pl.program_id(0) % (t_pad // tt)) == 0
        head = jnp.where(starts_seq, first_ref[0], tail_ref[SUBLANES - 1:SUBLANES, :])
        prev = jnp.where(row == 0, head, rolled)
    mixed = cur + (prev - cur) * mu_ref[...]
    w = RW_WIDTH
    r = mixed[:, 0:w]
    k = mixed[:, w:2 * w]
    v = mixed[:, 2 * w:3 * w]
    c0 = 3 * w
    wl = mixed[:, c0:c0 + RW_DECAY_LORA]
    al = mixed[:, c0 + RW_DECAY_LORA:c0 + RW_DECAY_LORA + RW_A_LORA]
    gl = mixed[:, c0 + RW_DECAY_LORA + RW_A_LORA:]
    w_log = -jnp.exp(-_softplus(-(w0_ref[...] + _dot(jnp.tanh(wl), w2_ref[...]))) - 0.5)
    a = _sigmoid(a0_ref[...] + _dot(al, a2_ref[...]))
    g = _dot(_sigmoid(gl), g2_ref[...])
    kk = k * kk_ref[...]
    blk = blk_ref[...]
    kkn = kk / jnp.maximum(jnp.sqrt(_dot_exact_rhs(kk * kk, blk)), 1e-12)
    k2 = k * (1.0 + (a - 1.0) * ka_ref[...])
    r_ref[...] = r
    wl_ref[...] = w_log
    k_ref[...] = k2
    v_ref[...] = v
    an_ref[...] = -kkn
    bb_ref[...] = kkn * a
    bonus_ref[...] = _dot_exact_rhs(r * k2 * rk_ref[...], blk) * v
    g_ref[...] = g


def rwkv_prep(proj, row0, bsz, t_pad, shift0, prm, tt):
    n = bsz * t_pad
    multi = tt > t_pad
    assert row0 % tt == 0 and n % tt == 0 and (tt % t_pad == 0 if multi else t_pad % tt == 0)
    blk0 = row0 // tt
    c = RW_COLS
    if multi:
        first = jnp.zeros((bsz, t_pad, c), F32).at[:, 0].set(shift0).reshape(n, c)
        first_spec = pl.BlockSpec((tt, c), lambda i: (i, 0))
    else:
        first = shift0.reshape(bsz, 1, c)
        first_spec = pl.BlockSpec((1, 1, c), lambda i: (i // (t_pad // tt), 0, 0))
    tail_blk = tt // SUBLANES
    vec = lambda width: pl.BlockSpec((1, width), lambda i: (0, 0))
    full = lambda a, b: pl.BlockSpec((a, b), lambda i: (0, 0))
    w = RW_WIDTH
    out_spec = pl.BlockSpec((tt, w), lambda i: (i, 0))
    return pl.pallas_call(
        functools.partial(_rwkv_prep_kernel, tt=tt, t_pad=t_pad, multi=multi),
        grid=(n // tt,),
        in_specs=[pl.BlockSpec((tt, c), lambda i: (blk0 + i, 0)),
                  pl.BlockSpec((SUBLANES, c), lambda i: (jnp.maximum((blk0 + i) * tail_blk - 1, 0), 0)),
                  first_spec, vec(c), vec(w), vec(w), vec(w), vec(w), vec(w),
                  full(RW_DECAY_LORA, w), full(RW_A_LORA, w), full(RW_GATE_LORA, w), full(w, w)],
        out_specs=[out_spec] * 8,
        out_shape=[jax.ShapeDtypeStruct((n, w), F32)] * 8,
        compiler_params=_cparams(("arbitrary",)),
        name="rwkv_prep",
    )(proj, proj, first, prm['mu'].reshape(1, c), prm['w0'].reshape(1, w), prm['a0'].reshape(1, w),
      prm['k_k'].reshape(1, w), prm['k_a'].reshape(1, w), prm['r_k'].reshape(1, w),
      prm['w2'], prm['a2'], prm['g2'], _head_block_ones(w, RW_DH))


def _rwkv_chunk_kernel(r_ref, w_ref, k_ref, v_ref, a_ref, b_ref, bonus_ref, g_ref, s0_ref,
                       lnw_ref, lnb_ref, o_ref, s_ref, acc_ref, *, chunk, t_valid, hb):
    ci = pl.program_id(2)

    @pl.when(ci == 0)
    def _():
        s_ref[...] = s0_ref[...]

    size = chunk
    rows = _iota2((size, size), 0)
    cols = _iota2((size, size), 1)
    strict = cols < rows
    rows2 = _iota2((size, 2 * size), 0)
    cols2 = _iota2((size, 2 * size), 1)
    mask_ak = (cols2 >= size) & (cols2 - size < rows2)
    mask_o = jnp.where(cols2 >= size, cols2 - size, cols2) <= rows2
    valid = (ci * size + _iota2((size, 1), 0)) < t_valid
    w = jnp.where(valid, w_ref[...], 0.0)
    a = jnp.where(valid, a_ref[...], 0.0)
    b = jnp.where(valid, b_ref[...], 0.0)
    k = jnp.where(valid, k_ref[...], 0.0)
    v = v_ref[...]
    lam = _dot_exact_lhs((cols <= rows).astype(F32), w)
    e_pos = jnp.exp(lam)
    e_neg = jnp.exp(-lam)
    at = a * jnp.exp(lam - w)
    bt = b * e_neg
    kt = k * e_neg
    rt = r_ref[...] * e_pos
    for h in range(hb):
        sl = slice(h * RW_DH, (h + 1) * RW_DH)
        at_h, rt_h, v_h = at[:, sl], rt[:, sl], v[:, sl]
        bk = jnp.concatenate([bt[:, sl], kt[:, sl]], axis=0)
        pa = _dot_nt(at_h, bk)
        pr = _dot_nt(rt_h, bk)
        t_inv = _neumann_inverse(jnp.where(strict, pa[:, :size], 0.0), size)
        s0 = s_ref[0, h]
        zv = jnp.concatenate([jnp.zeros_like(v_h), v_h], axis=0)
        u = _dot(t_inv, _dot_nt(at_h, s0) + _dot(jnp.where(mask_ak, pa, 0.0), zv))
        uv = jnp.concatenate([u, v_h], axis=0)
        o = _dot_nt(rt_h, s0) + _dot(jnp.where(mask_o, pr, 0.0), uv)
        s_ref[0, h] = (s0 + _dot_tn(uv, bk)) * e_pos[size - 1:size, sl]
        oc = o - jnp.mean(o, axis=-1, keepdims=True)
        acc_ref[:, sl] = oc * lax.rsqrt(jnp.mean(oc * oc, axis=-1, keepdims=True) + RW_GN_EPS)
    out = (acc_ref[...] * lnw_ref[...] + lnb_ref[...] + bonus_ref[...]) * g_ref[...]
    o_ref[...] = out.astype(o_ref.dtype)


def rwkv_chunk(prep, bsz, t_pad, t_valid, chunk, s0, ln_w, ln_b, hb):
    nc = t_pad // chunk
    hg = RW_HEADS // hb
    wb = hb * RW_DH
    tok = pl.BlockSpec((chunk, wb), lambda b, g, c: (b * nc + c, g))
    st = pl.BlockSpec((1, hb, RW_DH, RW_DH), lambda b, g, c: (b, g, 0, 0))
    vec = pl.BlockSpec((1, wb), lambda b, g, c: (0, g))
    return pl.pallas_call(
        functools.partial(_rwkv_chunk_kernel, chunk=chunk, t_valid=t_valid, hb=hb),
        grid=(bsz, hg, nc),
        in_specs=[tok] * 8 + [st, vec, vec],
        out_specs=[tok, st],
        out_shape=[jax.ShapeDtypeStruct((bsz * t_pad, RW_WIDTH), BF16),
                   jax.ShapeDtypeStruct((bsz, RW_HEADS, RW_DH, RW_DH), F32)],
        scratch_shapes=[pltpu.VMEM((chunk, wb), F32)],
        compiler_params=_cparams(("arbitrary", "arbitrary", "arbitrary")),
        name="rwkv_chunk",
    )(*prep, s0, ln_w.reshape(1, RW_WIDTH), ln_b.reshape(1, RW_WIDTH))


def _gdn_prep_kernel(cur_ref, tail_ref, hist_ref, cw_ref, o_ref, *, tt, t_pad, multi, heads_per_tile):
    cur = cur_ref[...]
    cw = cw_ref[...]
    row = _iota2((tt, 1), 0)
    acc = cur * cw[GD_CONV - 1:GD_CONV, :]
    if multi:
        hexp = hist_ref[...]
        pos = row & (t_pad - 1)
        for j in range(1, GD_CONV):
            back = GD_CONV - 1 - j
            hj = pltpu.roll(hexp, tt - back, 0) if back else hexp
            prev = jnp.where(pos >= j, pltpu.roll(cur, j, 0), hj)
            acc = acc + prev * cw[back:back + 1, :]
    else:
        starts_seq = (pl.program_id(0) % (t_pad // tt)) == 0
        hist = jnp.where(starts_seq, hist_ref[0], tail_ref[...])
        row8 = _iota2((SUBLANES, 1), 0)
        for j in range(1, GD_CONV):
            back = GD_CONV - 1 - j
            rolled = pltpu.roll(cur, j, 0)
            head = jnp.where(row8 < j, pltpu.roll(hist, j, 0), rolled[:SUBLANES])
            prev = jnp.concatenate([head, rolled[SUBLANES:]], axis=0)
            acc = acc + prev * cw[back:back + 1, :]
    act = acc * _sigmoid(acc)
    kind = pl.program_id(1)
    scale = jnp.where(kind == 0, GD_DK ** -0.5, 1.0)
    for h in range(heads_per_tile):
        sl = slice(h * GD_DK, (h + 1) * GD_DK)
        x = act[:, sl]
        normed = x * lax.rsqrt(jnp.sum(x * x, axis=-1, keepdims=True) + GD_EPS) * scale
        o_ref[:, sl] = jnp.where(kind < 2, normed, x)


def gdn_prep(proj, row0, bsz, t_pad, conv_buf, conv_w, tt):
    n = bsz * t_pad
    multi = tt > t_pad
    assert row0 % tt == 0 and n % tt == 0 and (tt % t_pad == 0 if multi else t_pad % tt == 0)
    blk0 = row0 // tt
    c = GD_CONV_DIM
    tc = GD_QK
    nbuf = GD_CONV - 1
    if multi:
        hist = jnp.pad(conv_buf, ((0, 0), (0, t_pad - nbuf), (0, 0))).reshape(n, c)
        hist_spec = pl.BlockSpec((tt, tc), lambda i, j: (i, j))
    else:
        hist = jnp.pad(conv_buf, ((0, 0), (SUBLANES - nbuf, 0), (0, 0)))
        hist_spec = pl.BlockSpec((1, SUBLANES, tc), lambda i, j: (i // (t_pad // tt), 0, j))
    tail_blk = tt // SUBLANES
    return pl.pallas_call(
        functools.partial(_gdn_prep_kernel, tt=tt, t_pad=t_pad, multi=multi, heads_per_tile=tc // GD_DK),
        grid=(n // tt, c // tc),
        in_specs=[pl.BlockSpec((tt, tc), lambda i, j: (blk0 + i, j)),
                  pl.BlockSpec((SUBLANES, tc), lambda i, j: (jnp.maximum((blk0 + i) * tail_blk - 1, 0), j)),
                  hist_spec,
                  pl.BlockSpec((GD_CONV, tc), lambda i, j: (0, j))],
        out_specs=pl.BlockSpec((tt, tc), lambda i, j: (i, j)),
        out_shape=jax.ShapeDtypeStruct((n, c), F32),
        compiler_params=_cparams(("arbitrary", "arbitrary")),
        name="gdn_prep",
    )(proj, proj, hist, conv_w)


def _gdn_chunk_kernel(q_ref, k_ref, v_ref, z_ref, gl_ref, al_ref, dt_ref, nw_ref, s0_ref,
                      o_ref, s_ref, *, chunk, t_valid, hpg):
    ci = pl.program_id(2)
    hg = pl.program_id(1)

    @pl.when(ci == 0)
    def _():
        s_ref[...] = s0_ref[...]

    size = chunk
    rows = _iota2((size, size), 0)
    cols = _iota2((size, size), 1)
    eye = rows == cols
    incl = cols <= rows
    strict = cols < rows
    valid = (ci * size + _iota2((size, 1), 0)) < t_valid
    logits = gl_ref[...]
    beta_all = _sigmoid(logits)
    g_all = -jnp.exp(al_ref[...]) * _softplus(logits + dt_ref[...])
    lane = _iota2((size, LANES), 1)
    rep = GD_V_HEADS // GD_K_HEADS
    for kh in range(hpg // rep):
        ksl = slice(kh * GD_DK, (kh + 1) * GD_DK)
        q = q_ref[:, ksl]
        k = k_ref[:, ksl]
        qk = _dot_nt(q, k)
        kk = _dot_nt(k, k)
        for r in range(rep):
            hl = kh * rep + r
            head = hg * hpg + hl
            vsl = slice(hl * GD_DV, (hl + 1) * GD_DV)
            beta = jnp.sum(jnp.where(lane == head, beta_all, 0.0), axis=1, keepdims=True)
            g = jnp.sum(jnp.where(lane == head + GD_V_HEADS, g_all, 0.0), axis=1, keepdims=True)
            beta = jnp.where(valid, beta, 0.0)
            g = jnp.where(valid, g, 0.0)
            g_row = _col_to_row(g, eye)
            gc = jnp.sum(jnp.where(incl, g_row, 0.0), axis=1, keepdims=True)
            gc_row = jnp.sum(jnp.where(rows <= cols, g, 0.0), axis=0, keepdims=True)
            decay = jnp.where(incl, jnp.exp(jnp.where(incl, gc - gc_row, 0.0)), 0.0)
            t_inv = _neumann_inverse(-jnp.where(strict, kk * beta * decay, 0.0), size)
            v = v_ref[:, vsl]
            uw = _dot(t_inv, jnp.concatenate([v * beta, k * (beta * jnp.exp(gc))], axis=1))
            s = s_ref[0, hl]
            v_new = uw[:, :GD_DV] - _dot(uw[:, GD_DV:], s)
            o = _dot(q * jnp.exp(gc), s) + _dot(qk * decay, v_new)
            g_last = gc[size - 1:size, :]
            s_ref[0, hl] = s * jnp.exp(g_last) + _dot_tn(k * jnp.exp(g_last - gc), v_new)
            o = o * lax.rsqrt(jnp.mean(o * o, axis=-1, keepdims=True) + GD_EPS) * nw_ref[...]
            z = z_ref[:, vsl]
            o_ref[:, vsl] = (o * (z * _sigmoid(z))).astype(o_ref.dtype)


def gdn_chunk(qkv, proj, gates, row0, bsz, t_pad, t_valid, chunk, s0, a_log, dt_bias, norm_w, hpg):
    nc = t_pad // chunk
    hgs = GD_V_HEADS // hpg
    rep = GD_V_HEADS // GD_K_HEADS
    kw = (hpg // rep) * GD_DK
    vw = hpg * GD_DV
    blk0 = row0 // chunk
    pad_row = lambda x: jnp.pad(x, (GD_V_HEADS, LANES - 2 * GD_V_HEADS)).reshape(1, LANES)
    st = pl.BlockSpec((1, hpg, GD_DK, GD_DV), lambda b, g, c: (b, g, 0, 0))
    vec = pl.BlockSpec((1, LANES), lambda b, g, c: (0, 0))
    return pl.pallas_call(
        functools.partial(_gdn_chunk_kernel, chunk=chunk, t_valid=t_valid, hpg=hpg),
        grid=(bsz, hgs, nc),
        in_specs=[pl.BlockSpec((chunk, kw), lambda b, g, c: (b * nc + c, g)),
                  pl.BlockSpec((chunk, kw), lambda b, g, c: (b * nc + c, GD_QK // kw + g)),
                  pl.BlockSpec((chunk, vw), lambda b, g, c: (b * nc + c, 2 * GD_QK // vw + g)),
                  pl.BlockSpec((chunk, vw), lambda b, g, c: (blk0 + b * nc + c, GD_CONV_DIM // vw + g)),
                  pl.BlockSpec((chunk, LANES), lambda b, g, c: (blk0 + b * nc + c, 0)),
                  vec, vec, vec, st],
        out_specs=[pl.BlockSpec((chunk, vw), lambda b, g, c: (b * nc + c, g)), st],
        out_shape=[jax.ShapeDtypeStruct((bsz * t_pad, GD_V_WIDTH), BF16),
                   jax.ShapeDtypeStruct((bsz, GD_V_HEADS, GD_DK, GD_DV), F32)],
        compiler_params=_cparams(("arbitrary", "arbitrary", "arbitrary")),
        name="gdn_chunk",
    )(qkv, qkv, qkv, proj, gates, pad_row(a_log), pad_row(dt_bias), norm_w.reshape(1, GD_DV), s0)


MOE_TM = 256


def _moe_ffn_kernel(te_ref, tv_ref, x_ref, wt_ref, wg_ref, wu_ref, wd_ref, o_ref):
    i = pl.program_id(0)

    @pl.when(tv_ref[i] == 1)
    def _():
        x = x_ref[...]
        gate = jnp.dot(x, wg_ref[0].astype(BF16), preferred_element_type=F32)
        up = jnp.dot(x, wu_ref[0].astype(BF16), preferred_element_type=F32)
        hid = gate * _sigmoid(gate) * up * wt_ref[...]
        o_ref[...] = jnp.dot(hid.astype(BF16), wd_ref[0].astype(BF16), preferred_element_type=F32)

    @pl.when(tv_ref[i] == 0)
    def _():
        o_ref[...] = jnp.zeros_like(o_ref)


def moe_ffn(x_sorted, row_weight, tile_expert, tile_valid, w_gate, w_up, w_down):
    rows, d = x_sorted.shape
    ff = w_gate.shape[2]
    grid_spec = pltpu.PrefetchScalarGridSpec(
        num_scalar_prefetch=2,
        grid=(rows // MOE_TM,),
        in_specs=[pl.BlockSpec((MOE_TM, d), lambda i, te, tv: (i, 0)),
                  pl.BlockSpec((MOE_TM, 1), lambda i, te, tv: (i, 0)),
                  pl.BlockSpec((1, d, ff), lambda i, te, tv: (te[i], 0, 0)),
                  pl.BlockSpec((1, d, ff), lambda i, te, tv: (te[i], 0, 0)),
                  pl.BlockSpec((1, ff, d), lambda i, te, tv: (te[i], 0, 0))],
        out_specs=pl.BlockSpec((MOE_TM, d), lambda i, te, tv: (i, 0)),
    )
    return pl.pallas_call(
        _moe_ffn_kernel,
        grid_spec=grid_spec,
        out_shape=jax.ShapeDtypeStruct((rows, d), F32),
        compiler_params=_cparams(("arbitrary",)),
        name="moe_ffn",
    )(tile_expert, tile_valid, x_sorted, row_weight, w_gate, w_up, w_down)


def hier_moe(xb, row_valid, w_group, w_expert, w_gate, w_up, w_down):
    n = xb.shape[0]
    w_router = jnp.pad(jnp.concatenate([w_group, w_expert], axis=1),
                       ((0, 0), (0, LANES - MOE_GROUPS - MOE_EXPERTS)))
    logits = matmul(xb, w_router, LANES, LANES)
    group_prob = jax.nn.softmax(logits[:, :MOE_GROUPS], axis=-1)
    g_val, g_idx = lax.top_k(group_prob, 1)
    e_logits = logits[:, MOE_GROUPS:MOE_GROUPS + MOE_EXPERTS].reshape(n, MOE_GROUPS, MOE_PER_GROUP)
    in_group = jnp.take_along_axis(e_logits, g_idx[..., None], axis=1)[:, 0, :]
    top_val, top_idx = lax.top_k(in_group, MOE_TOPK)
    weights = jax.nn.softmax(top_val, axis=-1) * g_val
    expert_id = g_idx * MOE_PER_GROUP + top_idx

    n_pairs = n * MOE_TOPK
    eid = jnp.where(row_valid[:, None], expert_id, MOE_EXPERTS).reshape(n_pairs).astype(jnp.int32)
    order = jnp.argsort(eid, stable=True)
    rank = jnp.zeros((n_pairs,), jnp.int32).at[order].set(jnp.arange(n_pairs, dtype=jnp.int32))
    counts = jnp.zeros((MOE_EXPERTS + 1,), jnp.int32).at[eid].add(1)
    starts = jnp.cumsum(counts) - counts
    tiles_per = (counts[:MOE_EXPERTS] + MOE_TM - 1) // MOE_TM
    tile_starts = jnp.cumsum(tiles_per) - tiles_per
    n_rows = int(np.ceil(n_pairs / MOE_TM) + MOE_EXPERTS) * MOE_TM
    n_tiles = n_rows // MOE_TM
    pad_starts = jnp.concatenate([tile_starts * MOE_TM, jnp.array([n_rows], jnp.int32)])
    dest = jnp.where(eid < MOE_EXPERTS, pad_starts[eid] + rank - starts[eid], n_rows)
    tok = jnp.arange(n_pairs, dtype=jnp.int32) // MOE_TOPK
    row_token = jnp.zeros((n_rows,), jnp.int32).at[dest].set(tok, mode='drop')
    row_weight = jnp.zeros((n_rows,), F32).at[dest].set(weights.reshape(n_pairs), mode='drop')
    tile_idx = jnp.arange(n_tiles, dtype=jnp.int32)
    tile_expert = jnp.sum(tile_idx[:, None] >= (tile_starts + tiles_per)[None, :], axis=1).astype(jnp.int32)
    tile_valid = (tile_expert < MOE_EXPERTS).astype(jnp.int32)
    last_used = jnp.max(jnp.where(tiles_per > 0, jnp.arange(MOE_EXPERTS, dtype=jnp.int32), 0))
    tile_expert = jnp.where(tile_valid == 1, tile_expert, last_used)

    x_sorted = jnp.take(xb, row_token, axis=0)
    y_sorted = moe_ffn(x_sorted, row_weight.reshape(n_rows, 1), tile_expert, tile_valid, w_gate, w_up, w_down)
    dest2 = jnp.minimum(dest, n_rows - 1).reshape(n, MOE_TOPK)
    return jnp.take(y_sorted, dest2[:, 0], axis=0) + jnp.take(y_sorted, dest2[:, 1], axis=0)


SAMPLE_T_PAD = 8


def kernel(x_prompt, x_sample, state_mlstm_c, state_mlstm_n, state_mlstm_m, state_rwkv_s,
           state_rwkv_shift, state_gdn_s, state_gdn_conv, p_prompt, p_sample,
           w_in_ab, ml_b_i, ml_b_f, ml_norm_w, rw_mu, rw_w0, rw_w2, rw_a0, rw_a2, rw_g2,
           rw_k_k, rw_k_a, rw_r_k, rw_ln_w, rw_ln_b, w_out_ab,
           gd_w_in, gd_conv_w, gd_a_log, gd_dt_bias, gd_norm_w, gd_w_out,
           ln_mix_g, ln_mix_b, moe_w_group, moe_w_expert, moe_w_gate, moe_w_up, moe_w_down,
           ln_ffn_g, ln_ffn_b, ple_w_gate, ple_w_proj):
    pb, pt, d = x_prompt.shape
    sb, st, _ = x_sample.shape
    n_p = pb * pt
    n_s = sb * SAMPLE_T_PAD
    n = n_p + n_s
    pad_t = lambda a: jnp.pad(a, ((0, 0), (0, SAMPLE_T_PAD - st), (0, 0)))
    merge = lambda a_p, a_s: jnp.concatenate(
        [a_p.reshape(n_p, a_p.shape[-1]), pad_t(a_s).reshape(n_s, a_s.shape[-1])], axis=0)
    x = merge(x_prompt, x_sample)
    xb = x.astype(BF16)
    row = jnp.arange(n)
    row_valid = (row < n_p) | (((row - n_p) % SAMPLE_T_PAD) < st)
    segs = (dict(row0=0, bsz=pb, t_pad=pt, t_valid=pt), dict(row0=n_p, bsz=sb, t_pad=SAMPLE_T_PAD, t_valid=st))
    zeros = lambda *shape: jnp.zeros(shape, F32)
    seg_rows = lambda a, sg: a[sg['row0']:sg['row0'] + sg['bsz'] * sg['t_pad']].reshape(
        sg['bsz'], sg['t_pad'], a.shape[-1])

    new_states = [[[] for _ in range(7)] for _ in segs]
    for layer in range(DEPTH):
        li = layer // 2
        if layer % 2 == 0:
            w_in = w_in_ab[li]
            proj_ml = matmul(xb, w_in, 3200, 640)
            proj_rw = matmul(xb, w_in[:, ML_COLS:].astype(BF16), RW_COLS, RW_COLS // 2)
            rw_prm = dict(mu=rw_mu[li], w0=rw_w0[li], w2=rw_w2[li], a0=rw_a0[li], a2=rw_a2[li], g2=rw_g2[li],
                          k_k=rw_k_k[li], k_a=rw_k_a[li], r_k=rw_r_k[li])
            heads = []
            for si, sg in enumerate(segs):
                bsz, t_pad, t_valid, row0 = sg['bsz'], sg['t_pad'], sg['t_valid'], sg['row0']
                if si == 0:
                    c0, n0, m0 = zeros(bsz, ML_HEADS, ML_DK, ML_DV), zeros(bsz, ML_HEADS, ML_DK), zeros(bsz, ML_HEADS)
                    s0, sh0 = zeros(bsz, RW_HEADS, RW_DH, RW_DH), zeros(bsz, RW_COLS)
                else:
                    c0, n0, m0 = state_mlstm_c[li], state_mlstm_n[li], state_mlstm_m[li]
                    s0, sh0 = state_rwkv_s[li], state_rwkv_shift[li]
                chunk = min(ML_CHUNK, t_pad)
                h_ml, c, nn, m = mlstm_group(proj_ml, row0, bsz, t_pad, t_valid, chunk, c0, n0, m0,
                                             ml_b_i[li], ml_b_f[li], ml_norm_w[li])
                prep = rwkv_prep(proj_rw, row0, bsz, t_pad, sh0, rw_prm, 256)
                h_rw, rs = rwkv_chunk(prep, bsz, t_pad, t_valid, min(RW_CHUNK, t_pad), s0,
                                      rw_ln_w[li], rw_ln_b[li], 4)
                heads.append(jnp.concatenate([h_ml, h_rw], axis=1))
                new_shift = seg_rows(proj_rw, sg)[:, t_valid - 1]
                for slot, val in zip(range(5), (c, nn, m, rs, new_shift)):
                    new_states[si][slot].append(val)
            heads_out = jnp.concatenate(heads, axis=0)
            mix = matmul(heads_out, w_out_ab[li], d, 512)
        else:
            w_in = gd_w_in[li]
            n_qkvz = GD_CONV_DIM + GD_V_WIDTH
            proj = matmul(xb, w_in, n_qkvz, 1024)
            w_gates = jnp.pad(w_in[:, n_qkvz:], ((0, 0), (0, LANES - 2 * GD_V_HEADS)))
            gates = matmul(xb, w_gates, LANES, LANES)
            outs = []
            for si, sg in enumerate(segs):
                bsz, t_pad, t_valid, row0 = sg['bsz'], sg['t_pad'], sg['t_valid'], sg['row0']
                if si == 0:
                    s0, buf0 = zeros(bsz, GD_V_HEADS, GD_DK, GD_DV), zeros(bsz, GD_CONV - 1, GD_CONV_DIM)
                else:
                    s0, buf0 = state_gdn_s[li], state_gdn_conv[li]
                qkv = gdn_prep(proj, row0, bsz, t_pad, buf0, gd_conv_w[li], 256)
                o, gs = gdn_chunk(qkv, proj, gates, row0, bsz, t_pad, t_valid, min(GD_CHUNK, t_pad), s0,
                                  gd_a_log[li], gd_dt_bias[li], gd_norm_w[li], 4)
                outs.append(o)
                full = jnp.concatenate([buf0, seg_rows(proj, sg)[:, :t_valid, :GD_CONV_DIM]], axis=1)
                new_states[si][5].append(gs)
                new_states[si][6].append(full[:, t_valid:])
            mix = matmul(jnp.concatenate(outs, axis=0), gd_w_out[li], d, 512)
        x, xb = ln_residual(x, mix, ln_mix_g[layer], ln_mix_b[layer])
        ffn = hier_moe(xb, row_valid, moe_w_group[layer], moe_w_expert[layer], moe_w_gate[layer],
                       moe_w_up[layer], moe_w_down[layer])
        x, xb = ln_residual(x, ffn, ln_ffn_g[layer], ln_ffn_b[layer])
        x, xb = ple(xb, merge(p_prompt[layer], p_sample[layer]), x, ple_w_gate[layer], ple_w_proj[layer])
    y_prompt = x[:n_p].reshape(pb, pt, d)
    y_sample = x[n_p:].reshape(sb, SAMPLE_T_PAD, d)[:, :st]
    stack = lambda vals: jnp.stack(vals)
    return (y_prompt, y_sample) + tuple(stack(v) for v in new_states[0]) + tuple(stack(v) for v in new_states[1])
```

```python
import functools

import jax
import jax.numpy as jnp
import numpy as np
from jax import lax
from jax.experimental import pallas as pl
from jax.experimental.pallas import tpu as pltpu

F32 = jnp.float32
BF16 = jnp.bfloat16

D_MODEL = 2048
DEPTH = 2
DEEPNORM_ALPHA = (2 * DEPTH) ** 0.25
LN_EPS = 1e-5
MIX_HALF = D_MODEL // 2
ML_HEADS = 4
ML_DV = MIX_HALF // ML_HEADS
ML_DK = ML_DV // 2
ML_CHUNK = 64
ML_GATE_CAP = 15.0
ML_NORM_EPS = 1e-6
ML_QK = ML_HEADS * ML_DK
ML_V = ML_HEADS * ML_DV
ML_COLS = 2 * ML_QK + 2 * ML_V + 2 * ML_HEADS
RW_DH = 64
RW_HEADS = MIX_HALF // RW_DH
RW_WIDTH = RW_HEADS * RW_DH
RW_DECAY_LORA = 64
RW_A_LORA = 64
RW_GATE_LORA = 128
RW_GN_EPS = 64e-5
RW_COLS = 3 * RW_WIDTH + RW_DECAY_LORA + RW_A_LORA + RW_GATE_LORA
RW_CHUNK = 64
GD_DK = 128
GD_DV = 128
GD_K_HEADS = D_MODEL // GD_DK
GD_V_HEADS = 2 * GD_K_HEADS
GD_CONV = 4
GD_CHUNK = 64
GD_EPS = 1e-6
GD_QK = GD_K_HEADS * GD_DK
GD_V_WIDTH = GD_V_HEADS * GD_DV
GD_CONV_DIM = 2 * GD_QK + GD_V_WIDTH
MOE_GROUPS = 4
MOE_PER_GROUP = 8
MOE_EXPERTS = MOE_GROUPS * MOE_PER_GROUP
MOE_TOPK = 2
MOE_FF = D_MODEL // 4
PLE_DIM = 256

LANES = 128
SUBLANES = 8
NEG = -1e30
VMEM_LIMIT = 56 * 1024 * 1024


def _cparams(sem):
    return pltpu.CompilerParams(dimension_semantics=sem, vmem_limit_bytes=VMEM_LIMIT)


def _dot(a, b):
    return jnp.dot(a.astype(BF16), b.astype(BF16), preferred_element_type=F32)


def _dot_nt(a, b):
    return lax.dot_general(a.astype(BF16), b.astype(BF16), (((1,), (1,)), ((), ())),
                           preferred_element_type=F32)


def _dot_tn(a, b):
    return lax.dot_general(a.astype(BF16), b.astype(BF16), (((0,), (0,)), ((), ())),
                           preferred_element_type=F32)


def _split3(x):
    hi = x.astype(BF16)
    r1 = x - hi.astype(F32)
    mid = r1.astype(BF16)
    lo = (r1 - mid.astype(F32)).astype(BF16)
    return hi, mid, lo


def _dot_exact_rhs(a, b01):
    hi, mid, lo = _split3(a)
    b = b01.astype(BF16)
    return (jnp.dot(hi, b, preferred_element_type=F32) + jnp.dot(mid, b, preferred_element_type=F32)
            + jnp.dot(lo, b, preferred_element_type=F32))


def _dot_exact_lhs(a01, b):
    hi, mid, lo = _split3(b)
    a = a01.astype(BF16)
    return (jnp.dot(a, hi, preferred_element_type=F32) + jnp.dot(a, mid, preferred_element_type=F32)
            + jnp.dot(a, lo, preferred_element_type=F32))


def _iota2(shape, dim):
    return lax.broadcasted_iota(jnp.int32, shape, dim)


def _col_to_row(col, eye):
    return jnp.sum(jnp.where(eye, col, 0.0), axis=0, keepdims=True)


def _sigmoid(x):
    return 1.0 / (1.0 + jnp.exp(-x))


def _softplus(x):
    return jnp.maximum(x, 0.0) + jnp.log1p(jnp.exp(-jnp.abs(x)))


def _neumann_inverse(n_mat, size):
    eye = (_iota2((size, size), 0) == _iota2((size, size), 1)).astype(F32)
    t = eye + n_mat
    x = n_mat
    steps = max(int(np.ceil(np.log2(size))) - 1, 0)
    for _ in range(steps):
        x = _dot(x, x)
        t = t + _dot(t, x)
    return t


def _mm_kernel(x_ref, w_ref, o_ref, wb_ref):
    @pl.when(pl.program_id(1) == 0)
    def _():
        wb_ref[...] = w_ref[...].astype(BF16)

    o_ref[...] = jnp.dot(x_ref[...].astype(BF16), wb_ref[...],
                         preferred_element_type=F32).astype(o_ref.dtype)


def matmul(x, w, n_out, tn, tm=512, out_dtype=F32):
    m, k = x.shape
    assert w.shape[0] == k and n_out % tn == 0 and m % tm == 0 and tn % LANES == 0
    return pl.pallas_call(
        _mm_kernel,
        grid=(n_out // tn, m // tm),
        in_specs=[pl.BlockSpec((tm, k), lambda j, i: (i, 0)),
                  pl.BlockSpec((k, tn), lambda j, i: (0, j))],
        out_specs=pl.BlockSpec((tm, tn), lambda j, i: (i, j)),
        out_shape=jax.ShapeDtypeStruct((m, n_out), out_dtype),
        scratch_shapes=[pltpu.VMEM((k, tn), BF16)],
        compiler_params=_cparams(("arbitrary", "arbitrary")),
        name="matmul",
    )(x, w)


def _ln_res_kernel(x_ref, y_ref, g_ref, b_ref, o_ref, ob_ref):
    z = DEEPNORM_ALPHA * x_ref[...] + y_ref[...]
    zc = z - jnp.mean(z, axis=-1, keepdims=True)
    var = jnp.mean(zc * zc, axis=-1, keepdims=True)
    out = zc * lax.rsqrt(var + LN_EPS) * g_ref[...] + b_ref[...]
    o_ref[...] = out
    ob_ref[...] = out.astype(BF16)


def ln_residual(x, y, g, b, tm=256):
    m, d = x.shape
    row = pl.BlockSpec((tm, d), lambda i: (i, 0))
    vec = pl.BlockSpec((1, d), lambda i: (0, 0))
    return pl.pallas_call(
        _ln_res_kernel,
        grid=(m // tm,),
        in_specs=[row, row, vec, vec],
        out_specs=[row, row],
        out_shape=[jax.ShapeDtypeStruct((m, d), F32), jax.ShapeDtypeStruct((m, d), BF16)],
        compiler_params=_cparams(("arbitrary",)),
        name="ln_residual",
    )(x, y, g.reshape(1, d), b.reshape(1, d))


def _ple_kernel(xb_ref, p_ref, x_ref, wg_ref, wp_ref, o_ref, ob_ref, wgb_ref, wpb_ref):
    @pl.when(pl.program_id(1) == 0)
    def _():
        wgb_ref[...] = wg_ref[0].astype(BF16)
        wpb_ref[...] = wp_ref[0].astype(BF16)

    gate = _sigmoid(jnp.dot(xb_ref[...], wgb_ref[...], preferred_element_type=F32))
    emb = jnp.dot(p_ref[...].astype(BF16), wpb_ref[...], preferred_element_type=F32)
    out = x_ref[...] + gate * emb
    o_ref[...] = out
    ob_ref[...] = out.astype(BF16)


def ple(xb, p, x, layer, wg, wp, tm=512, tn=512):
    m, d = x.shape
    pd = p.shape[1]
    return pl.pallas_call(
        _ple_kernel,
        grid=(d // tn, m // tm),
        in_specs=[pl.BlockSpec((tm, d), lambda j, i: (i, 0)),
                  pl.BlockSpec((tm, pd), lambda j, i: (i, 0)),
                  pl.BlockSpec((tm, tn), lambda j, i: (i, j)),
                  pl.BlockSpec((1, d, tn), lambda j, i: (layer, 0, j)),
                  pl.BlockSpec((1, pd, tn), lambda j, i: (layer, 0, j))],
        out_specs=[pl.BlockSpec((tm, tn), lambda j, i: (i, j))] * 2,
        out_shape=[jax.ShapeDtypeStruct((m, d), F32), jax.ShapeDtypeStruct((m, d), BF16)],
        scratch_shapes=[pltpu.VMEM((d, tn), BF16), pltpu.VMEM((pd, tn), BF16)],
        compiler_params=_cparams(("arbitrary", "arbitrary")),
        name="ple",
    )(xb, p, x, wg, wp)


def _mlstm_kernel(q_ref, k_ref, v_ref, o_ref, g_ref, c0_ref, n0_ref, m0_ref, nw_ref, gb_ref, _dst_ref,
                  h_ref, c_ref, n_ref, m_ref, *, chunk, t_valid):
    ci = pl.program_id(1)

    @pl.when(ci == 0)
    def _():
        c_ref[...] = c0_ref[...]
        n_ref[...] = n0_ref[...]
        m_ref[...] = m0_ref[...]

    size = chunk
    rows = _iota2((size, size), 0)
    cols = _iota2((size, size), 1)
    eye = rows == cols
    causal = cols <= rows
    gates = g_ref[:, 0:2 * ML_HEADS] + gb_ref[...]
    capped = ML_GATE_CAP * jnp.tanh(gates / ML_GATE_CAP)
    t_idx = ci * size + _iota2((size, 1), 0)
    valid = t_idx < t_valid
    ipre_all = jnp.where(valid, capped[:, 0:ML_HEADS], NEG)
    logf_all = jnp.where(valid, -_softplus(-capped[:, ML_HEADS:2 * ML_HEADS]), 0.0)
    heads = range(ML_HEADS)
    q = [q_ref[:, h * ML_DK:(h + 1) * ML_DK] for h in heads]
    k = [k_ref[:, h * ML_DK:(h + 1) * ML_DK] * (ML_DK ** -0.5) for h in heads]
    v = [v_ref[:, h * ML_DV:(h + 1) * ML_DV] for h in heads]
    qk = [_dot_nt(q[h], k[h]) for h in heads]
    c_prev = [c_ref[0, h] for h in heads]
    n_prev = [n_ref[0, h:h + 1, :] for h in heads]
    qc = [_dot(q[h], c_prev[h]) for h in heads]
    s, w_inter, m_t, kw, decay = [], [], [], [], []
    for h in heads:
        ig_col = ipre_all[:, h:h + 1]
        lf_col = logf_all[:, h:h + 1]
        ig_row = _col_to_row(ig_col, eye)
        lf_row = _col_to_row(lf_col, eye)
        b_col = jnp.sum(jnp.where(causal, lf_row, 0.0), axis=1, keepdims=True)
        b_row = jnp.sum(jnp.where(rows <= cols, lf_col, 0.0), axis=0, keepdims=True)
        d = jnp.where(causal, b_col - b_row + ig_row, NEG)
        inter = b_col + m_ref[0, :, h:h + 1]
        m_h = jnp.maximum(inter, jnp.max(d, axis=1, keepdims=True))
        w_h = jnp.exp(inter - m_h)
        s.append(qk[h] * jnp.exp(d - m_h))
        b_last = b_col[size - 1:size, :]
        m_last = m_h[size - 1:size, :]
        w_last = jnp.exp(b_last - b_col + ig_col - m_last)
        kw.append(k[h] * w_last)
        w_inter.append(w_h)
        m_t.append(m_h)
        decay.append(w_h[size - 1:size, :])
        m_ref[0, :, h:h + 1] = m_last
    sv = [_dot(s[h], v[h]) for h in heads]
    kv = [_dot_tn(kw[h], v[h]) for h in heads]
    for h in heads:
        num = w_inter[h] * qc[h] + sv[h]
        den = (w_inter[h] * jnp.sum(q[h] * n_prev[h], axis=1, keepdims=True)
               + jnp.sum(s[h], axis=1, keepdims=True))
        hid = num / jnp.maximum(jnp.abs(den), jnp.exp(-m_t[h]))
        c_ref[0, h] = decay[h] * c_prev[h] + kv[h]
        n_ref[0, h:h + 1, :] = decay[h] * n_prev[h] + jnp.sum(kw[h], axis=0, keepdims=True)
        hid = hid * lax.rsqrt(jnp.mean(hid * hid, axis=-1, keepdims=True) + ML_NORM_EPS)
        hid = hid * nw_ref[:, h * ML_DV:(h + 1) * ML_DV]
        hid = hid * _sigmoid(o_ref[:, h * ML_DV:(h + 1) * ML_DV])
        h_ref[:, h * ML_DV:(h + 1) * ML_DV] = hid.astype(h_ref.dtype)


def mlstm_group(proj, dst, row0, bsz, t_pad, t_valid, chunk, c0, n0, m0, b_i, b_f, norm_w):
    nc = t_pad // chunk
    blk0 = row0 // chunk
    assert row0 % chunk == 0 and t_pad % chunk == 0
    rmap = lambda c0_: (lambda b, c: (blk0 + b * nc + c, c0_))
    qk_w, v_w = ML_QK, ML_V
    in_specs = [
        pl.BlockSpec((chunk, qk_w), rmap(0)),
        pl.BlockSpec((chunk, qk_w), rmap(1)),
        pl.BlockSpec((chunk, v_w), rmap(1)),
        pl.BlockSpec((chunk, v_w), rmap(2)),
        pl.BlockSpec((chunk, LANES), rmap((2 * qk_w + 2 * v_w) // LANES)),
        pl.BlockSpec((1, ML_HEADS, ML_DK, ML_DV), lambda b, c: (b, 0, 0, 0)),
        pl.BlockSpec((1, ML_HEADS, ML_DK), lambda b, c: (b, 0, 0)),
        pl.BlockSpec((1, 1, ML_HEADS), lambda b, c: (b, 0, 0)),
        pl.BlockSpec((1, v_w), lambda b, c: (0, 0)),
        pl.BlockSpec((1, 2 * ML_HEADS), lambda b, c: (0, 0)),
        pl.BlockSpec(memory_space=pl.ANY),
    ]
    out_specs = [
        pl.BlockSpec((chunk, v_w), lambda b, c: (blk0 + b * nc + c, 0)),
        pl.BlockSpec((1, ML_HEADS, ML_DK, ML_DV), lambda b, c: (b, 0, 0, 0)),
        pl.BlockSpec((1, ML_HEADS, ML_DK), lambda b, c: (b, 0, 0)),
        pl.BlockSpec((1, 1, ML_HEADS), lambda b, c: (b, 0, 0)),
    ]
    out_shape = [
        jax.ShapeDtypeStruct(dst.shape, dst.dtype),
        jax.ShapeDtypeStruct((bsz, ML_HEADS, ML_DK, ML_DV), F32),
        jax.ShapeDtypeStruct((bsz, ML_HEADS, ML_DK), F32),
        jax.ShapeDtypeStruct((bsz, 1, ML_HEADS), F32),
    ]
    gate_bias = jnp.concatenate([b_i, b_f]).reshape(1, 2 * ML_HEADS)
    h, c, n, m = pl.pallas_call(
        functools.partial(_mlstm_kernel, chunk=chunk, t_valid=t_valid),
        grid=(bsz, nc),
        in_specs=in_specs,
        out_specs=out_specs,
        out_shape=out_shape,
        input_output_aliases={10: 0},
        compiler_params=_cparams(("arbitrary", "arbitrary")),
        name="mlstm",
    )(proj, proj, proj, proj, proj, c0, n0, m0.reshape(bsz, 1, ML_HEADS), norm_w.reshape(1, v_w), gate_bias, dst)
    return h, c, n, m.reshape(bsz, ML_HEADS)


def _head_block_ones(width, head):
    idx = np.arange(width) // head
    return jnp.asarray((idx[:, None] == idx[None, :]).astype(np.float32), dtype=BF16)


def _rwkv_prep_kernel(cur_ref, tail_ref, first_ref, mu_ref, w0_ref, a0_ref, kk_ref, ka_ref, rk_ref,
                      w2_ref, a2_ref, g2_ref, blk_ref,
                      r_ref, wl_ref, k_ref, v_ref, an_ref, bb_ref, bonus_ref, g_ref,
                      *, tt, t_pad, multi):
    cur = cur_ref[...]
    rolled = pltpu.roll(cur, 1, 0)
    row = _iota2((tt, 1), 0)
    if multi:
        prev = jnp.where((row & (t_pad - 1)) == 0, first_ref[...], rolled)
    else:
        starts_seq = (pl.program_id(0) % (t_pad // tt)) == 0
        head = jnp.where(starts_seq, first_ref[0], tail_ref[SUBLANES - 1:SUBLANES, :])
        prev = jnp.where(row == 0, head, rolled)
    mixed = cur + (prev - cur) * mu_ref[...]
    w = RW_WIDTH
    r = mixed[:, 0:w]
    k = mixed[:, w:2 * w]
    v = mixed[:, 2 * w:3 * w]
    c0 = 3 * w
    wl = mixed[:, c0:c0 + RW_DECAY_LORA]
    al = mixed[:, c0 + RW_DECAY_LORA:c0 + RW_DECAY_LORA + RW_A_LORA]
    gl = mixed[:, c0 + RW_DECAY_LORA + RW_A_LORA:]
    w_log = -jnp.exp(-_softplus(-(w0_ref[...] + _dot(jnp.tanh(wl), w2_ref[...]))) - 0.5)
    a = _sigmoid(a0_ref[...] + _dot(al, a2_ref[...]))
    g = _dot(_sigmoid(gl), g2_ref[...])
    kk = k * kk_ref[...]
    blk = blk_ref[...]
    kkn = kk / jnp.maximum(jnp.sqrt(_dot_exact_rhs(kk * kk, blk)), 1e-12)
    k2 = k * (1.0 + (a - 1.0) * ka_ref[...])
    r_ref[...] = r
    wl_ref[...] = w_log
    k_ref[...] = k2
    v_ref[...] = v
    an_ref[...] = -kkn
    bb_ref[...] = kkn * a
    bonus_ref[...] = _dot_exact_rhs(r * k2 * rk_ref[...], blk) * v
    g_ref[...] = g


def rwkv_prep(proj, row0, bsz, t_pad, shift0, prm, tt):
    n = bsz * t_pad
    multi = tt > t_pad
    assert row0 % tt == 0 and n % tt == 0 and (tt % t_pad == 0 if multi else t_pad % tt == 0)
    blk0 = row0 // tt
    c = RW_COLS
    if multi:
        first = jnp.zeros((bsz, t_pad, c), F32).at[:, 0].set(shift0).reshape(n, c)
        first_spec = pl.BlockSpec((tt, c), lambda i: (i, 0))
    else:
        first = shift0.reshape(bsz, 1, c)
        first_spec = pl.BlockSpec((1, 1, c), lambda i: (i // (t_pad // tt), 0, 0))
    tail_blk = tt // SUBLANES
    vec = lambda width: pl.BlockSpec((1, width), lambda i: (0, 0))
    full = lambda a, b: pl.BlockSpec((a, b), lambda i: (0, 0))
    w = RW_WIDTH
    out_spec = pl.BlockSpec((tt, w), lambda i: (i, 0))
    return pl.pallas_call(
        functools.partial(_rwkv_prep_kernel, tt=tt, t_pad=t_pad, multi=multi),
        grid=(n // tt,),
        in_specs=[pl.BlockSpec((tt, c), lambda i: (blk0 + i, 0)),
                  pl.BlockSpec((SUBLANES, c), lambda i: (jnp.maximum((blk0 + i) * tail_blk - 1, 0), 0)),
                  first_spec, vec(c), vec(w), vec(w), vec(w), vec(w), vec(w),
                  full(RW_DECAY_LORA, w), full(RW_A_LORA, w), full(RW_GATE_LORA, w), full(w, w)],
        out_specs=[out_spec] * 8,
        out_shape=[jax.ShapeDtypeStruct((n, w), F32)] * 8,
        compiler_params=_cparams(("arbitrary",)),
        name="rwkv_prep",
    )(proj, proj, first, prm['mu'].reshape(1, c), prm['w0'].reshape(1, w), prm['a0'].reshape(1, w),
      prm['k_k'].reshape(1, w), prm['k_a'].reshape(1, w), prm['r_k'].reshape(1, w),
      prm['w2'], prm['a2'], prm['g2'], _head_block_ones(w, RW_DH))


def _neumann_inverse_many(n_mats, size):
    eye = (_iota2((size, size), 0) == _iota2((size, size), 1)).astype(F32)
    ts = [eye + n_mat for n_mat in n_mats]
    xs = list(n_mats)
    steps = max(int(np.ceil(np.log2(size))) - 1, 0)
    for _ in range(steps):
        xs = [_dot(x, x) for x in xs]
        ts = [t + _dot(t, x) for t, x in zip(ts, xs)]
    return ts


def _rwkv_chunk_kernel(r_ref, w_ref, k_ref, v_ref, a_ref, b_ref, bonus_ref, g_ref, s0_ref,
                       lnw_ref, lnb_ref, _dst_ref, o_ref, s_ref, acc_ref, *, chunk, t_valid, hb):
    ci = pl.program_id(2)

    @pl.when(ci == 0)
    def _():
        s_ref[...] = s0_ref[...]

    size = chunk
    rows = _iota2((size, size), 0)
    cols = _iota2((size, size), 1)
    strict = cols < rows
    rows2 = _iota2((size, 2 * size), 0)
    cols2 = _iota2((size, 2 * size), 1)
    mask_ak = (cols2 >= size) & (cols2 - size < rows2)
    mask_o = jnp.where(cols2 >= size, cols2 - size, cols2) <= rows2
    valid = (ci * size + _iota2((size, 1), 0)) < t_valid
    w = jnp.where(valid, w_ref[...], 0.0)
    a = jnp.where(valid, a_ref[...], 0.0)
    b = jnp.where(valid, b_ref[...], 0.0)
    k = jnp.where(valid, k_ref[...], 0.0)
    v = v_ref[...]
    lam = _dot_exact_lhs((cols <= rows).astype(F32), w)
    e_pos = jnp.exp(lam)
    e_neg = jnp.exp(-lam)
    at = a * jnp.exp(lam - w)
    bt = b * e_neg
    kt = k * e_neg
    rt = r_ref[...] * e_pos
    heads = range(hb)
    sls = [slice(h * RW_DH, (h + 1) * RW_DH) for h in heads]
    at_h = [at[:, sl] for sl in sls]
    rt_h = [rt[:, sl] for sl in sls]
    v_h = [v[:, sl] for sl in sls]
    bk = [jnp.concatenate([bt[:, sl], kt[:, sl]], axis=0) for sl in sls]
    s0 = [s_ref[0, h] for h in heads]
    pa = [_dot_nt(at_h[h], bk[h]) for h in heads]
    pr = [_dot_nt(rt_h[h], bk[h]) for h in heads]
    as0 = [_dot_nt(at_h[h], s0[h]) for h in heads]
    rs0 = [_dot_nt(rt_h[h], s0[h]) for h in heads]
    zv = [jnp.concatenate([jnp.zeros_like(v_h[h]), v_h[h]], axis=0) for h in heads]
    rhs = [as0[h] + _dot(jnp.where(mask_ak, pa[h], 0.0), zv[h]) for h in heads]
    t_inv = _neumann_inverse_many([jnp.where(strict, pa[h][:, :size], 0.0) for h in heads], size)
    u = [_dot(t_inv[h], rhs[h]) for h in heads]
    uv = [jnp.concatenate([u[h], v_h[h]], axis=0) for h in heads]
    o = [rs0[h] + _dot(jnp.where(mask_o, pr[h], 0.0), uv[h]) for h in heads]
    ds = [_dot_tn(uv[h], bk[h]) for h in heads]
    for h in heads:
        s_ref[0, h] = (s0[h] + ds[h]) * e_pos[size - 1:size, sls[h]]
        oc = o[h] - jnp.mean(o[h], axis=-1, keepdims=True)
        acc_ref[:, sls[h]] = oc * lax.rsqrt(jnp.mean(oc * oc, axis=-1, keepdims=True) + RW_GN_EPS)
    out = (acc_ref[...] * lnw_ref[...] + lnb_ref[...] + bonus_ref[...]) * g_ref[...]
    o_ref[...] = out.astype(o_ref.dtype)


def rwkv_chunk(prep, dst, dst_col0, row0, bsz, t_pad, t_valid, chunk, s0, ln_w, ln_b, hb):
    nc = t_pad // chunk
    hg = RW_HEADS // hb
    wb = hb * RW_DH
    blk0 = row0 // chunk
    assert dst_col0 % wb == 0 and row0 % chunk == 0
    tok = pl.BlockSpec((chunk, wb), lambda b, g, c: (b * nc + c, g))
    st = pl.BlockSpec((1, hb, RW_DH, RW_DH), lambda b, g, c: (b, g, 0, 0))
    vec = pl.BlockSpec((1, wb), lambda b, g, c: (0, g))
    return pl.pallas_call(
        functools.partial(_rwkv_chunk_kernel, chunk=chunk, t_valid=t_valid, hb=hb),
        grid=(bsz, hg, nc),
        in_specs=[tok] * 8 + [st, vec, vec, pl.BlockSpec(memory_space=pl.ANY)],
        out_specs=[pl.BlockSpec((chunk, wb), lambda b, g, c: (blk0 + b * nc + c, dst_col0 // wb + g)), st],
        out_shape=[jax.ShapeDtypeStruct(dst.shape, dst.dtype),
                   jax.ShapeDtypeStruct((bsz, RW_HEADS, RW_DH, RW_DH), F32)],
        scratch_shapes=[pltpu.VMEM((chunk, wb), F32)],
        input_output_aliases={11: 0},
        compiler_params=_cparams(("arbitrary", "arbitrary", "arbitrary")),
        name="rwkv_chunk",
    )(*prep, s0, ln_w.reshape(1, RW_WIDTH), ln_b.reshape(1, RW_WIDTH), dst)


def _gdn_prep_kernel(cur_ref, tail_ref, hist_ref, cw_ref, o_ref, *, tt, t_pad, multi, heads_per_tile):
    cur = cur_ref[...]
    cw = cw_ref[...]
    row = _iota2((tt, 1), 0)
    acc = cur * cw[GD_CONV - 1:GD_CONV, :]
    if multi:
        hexp = hist_ref[...]
        pos = row & (t_pad - 1)
        for j in range(1, GD_CONV):
            back = GD_CONV - 1 - j
            hj = pltpu.roll(hexp, tt - back, 0) if back else hexp
            prev = jnp.where(pos >= j, pltpu.roll(cur, j, 0), hj)
            acc = acc + prev * cw[back:back + 1, :]
    else:
        starts_seq = (pl.program_id(0) % (t_pad // tt)) == 0
        hist = jnp.where(starts_seq, hist_ref[0], tail_ref[...])
        row8 = _iota2((SUBLANES, 1), 0)
        for j in range(1, GD_CONV):
            back = GD_CONV - 1 - j
            rolled = pltpu.roll(cur, j, 0)
            head = jnp.where(row8 < j, pltpu.roll(hist, j, 0), rolled[:SUBLANES])
            prev = jnp.concatenate([head, rolled[SUBLANES:]], axis=0)
            acc = acc + prev * cw[back:back + 1, :]
    act = acc * _sigmoid(acc)
    kind = pl.program_id(1)
    scale = jnp.where(kind == 0, GD_DK ** -0.5, 1.0)
    for h in range(heads_per_tile):
        sl = slice(h * GD_DK, (h + 1) * GD_DK)
        x = act[:, sl]
        normed = x * lax.rsqrt(jnp.sum(x * x, axis=-1, keepdims=True) + GD_EPS) * scale
        o_ref[:, sl] = jnp.where(kind < 2, normed, x)


def gdn_prep(proj, row0, bsz, t_pad, conv_buf, conv_w, tt):
    n = bsz * t_pad
    multi = tt > t_pad
    assert row0 % tt == 0 and n % tt == 0 and (tt % t_pad == 0 if multi else t_pad % tt == 0)
    blk0 = row0 // tt
    c = GD_CONV_DIM
    tc = GD_QK
    nbuf = GD_CONV - 1
    if multi:
        hist = jnp.pad(conv_buf, ((0, 0), (0, t_pad - nbuf), (0, 0))).reshape(n, c)
        hist_spec = pl.BlockSpec((tt, tc), lambda i, j: (i, j))
    else:
        hist = jnp.pad(conv_buf, ((0, 0), (SUBLANES - nbuf, 0), (0, 0)))
        hist_spec = pl.BlockSpec((1, SUBLANES, tc), lambda i, j: (i // (t_pad // tt), 0, j))
    tail_blk = tt // SUBLANES
    return pl.pallas_call(
        functools.partial(_gdn_prep_kernel, tt=tt, t_pad=t_pad, multi=multi, heads_per_tile=tc // GD_DK),
        grid=(n // tt, c // tc),
        in_specs=[pl.BlockSpec((tt, tc), lambda i, j: (blk0 + i, j)),
                  pl.BlockSpec((SUBLANES, tc), lambda i, j: (jnp.maximum((blk0 + i) * tail_blk - 1, 0), j)),
                  hist_spec,
                  pl.BlockSpec((GD_CONV, tc), lambda i, j: (0, j))],
        out_specs=pl.BlockSpec((tt, tc), lambda i, j: (i, j)),
        out_shape=jax.ShapeDtypeStruct((n, c), F32),
        compiler_params=_cparams(("arbitrary", "arbitrary")),
        name="gdn_prep",
    )(proj, proj, hist, conv_w)


def _gdn_chunk_kernel(q_ref, k_ref, v_ref, z_ref, gl_ref, al_ref, dt_ref, nw_ref, s0_ref, _dst_ref,
                      o_ref, s_ref, *, chunk, t_valid, hpg):
    ci = pl.program_id(2)
    hg = pl.program_id(1)

    @pl.when(ci == 0)
    def _():
        s_ref[...] = s0_ref[...]

    size = chunk
    rows = _iota2((size, size), 0)
    cols = _iota2((size, size), 1)
    eye = rows == cols
    incl = cols <= rows
    strict = cols < rows
    valid = (ci * size + _iota2((size, 1), 0)) < t_valid
    logits = gl_ref[...]
    beta_all = _sigmoid(logits)
    g_all = -jnp.exp(al_ref[...]) * _softplus(logits + dt_ref[...])
    lane = _iota2((size, LANES), 1)
    rep = GD_V_HEADS // GD_K_HEADS
    kheads = range(hpg // rep)
    heads = range(hpg)
    q = [q_ref[:, kh * GD_DK:(kh + 1) * GD_DK] for kh in kheads]
    k = [k_ref[:, kh * GD_DK:(kh + 1) * GD_DK] for kh in kheads]
    qk = [_dot_nt(q[kh], k[kh]) for kh in kheads]
    kk = [_dot_nt(k[kh], k[kh]) for kh in kheads]
    vsl = [slice(hl * GD_DV, (hl + 1) * GD_DV) for hl in heads]
    s = [s_ref[0, hl] for hl in heads]
    beta, gc, decay = [], [], []
    for hl in heads:
        head = hg * hpg + hl
        beta_h = jnp.sum(jnp.where(lane == head, beta_all, 0.0), axis=1, keepdims=True)
        g = jnp.sum(jnp.where(lane == head + GD_V_HEADS, g_all, 0.0), axis=1, keepdims=True)
        beta_h = jnp.where(valid, beta_h, 0.0)
        g = jnp.where(valid, g, 0.0)
        g_row = _col_to_row(g, eye)
        gc_h = jnp.sum(jnp.where(incl, g_row, 0.0), axis=1, keepdims=True)
        gc_row = jnp.sum(jnp.where(rows <= cols, g, 0.0), axis=0, keepdims=True)
        beta.append(beta_h)
        gc.append(gc_h)
        decay.append(jnp.where(incl, jnp.exp(jnp.where(incl, gc_h - gc_row, 0.0)), 0.0))
    qs = [_dot(q[hl // rep] * jnp.exp(gc[hl]), s[hl]) for hl in heads]
    t_inv = _neumann_inverse_many(
        [-jnp.where(strict, kk[hl // rep] * beta[hl] * decay[hl], 0.0) for hl in heads], size)
    uw = [_dot(t_inv[hl], jnp.concatenate([v_ref[:, vsl[hl]] * beta[hl],
                                           k[hl // rep] * (beta[hl] * jnp.exp(gc[hl]))], axis=1))
          for hl in heads]
    ws = [_dot(uw[hl][:, GD_DV:], s[hl]) for hl in heads]
    v_new = [uw[hl][:, :GD_DV] - ws[hl] for hl in heads]
    av = [_dot(qk[hl // rep] * decay[hl], v_new[hl]) for hl in heads]
    g_last = [gc[hl][size - 1:size, :] for hl in heads]
    kv = [_dot_tn(k[hl // rep] * jnp.exp(g_last[hl] - gc[hl]), v_new[hl]) for hl in heads]
    for hl in heads:
        s_ref[0, hl] = s[hl] * jnp.exp(g_last[hl]) + kv[hl]
        o = qs[hl] + av[hl]
        o = o * lax.rsqrt(jnp.mean(o * o, axis=-1, keepdims=True) + GD_EPS) * nw_ref[...]
        z = z_ref[:, vsl[hl]]
        o_ref[:, vsl[hl]] = (o * (z * _sigmoid(z))).astype(o_ref.dtype)


def gdn_chunk(qkv, proj, gates, dst, row0, bsz, t_pad, t_valid, chunk, s0, a_log, dt_bias, norm_w, hpg):
    nc = t_pad // chunk
    hgs = GD_V_HEADS // hpg
    rep = GD_V_HEADS // GD_K_HEADS
    kw = (hpg // rep) * GD_DK
    vw = hpg * GD_DV
    blk0 = row0 // chunk
    pad_row = lambda x: jnp.pad(x, (GD_V_HEADS, LANES - 2 * GD_V_HEADS)).reshape(1, LANES)
    st = pl.BlockSpec((1, hpg, GD_DK, GD_DV), lambda b, g, c: (b, g, 0, 0))
    vec = pl.BlockSpec((1, LANES), lambda b, g, c: (0, 0))
    return pl.pallas_call(
        functools.partial(_gdn_chunk_kernel, chunk=chunk, t_valid=t_valid, hpg=hpg),
        grid=(bsz, hgs, nc),
        in_specs=[pl.BlockSpec((chunk, kw), lambda b, g, c: (b * nc + c, g)),
                  pl.BlockSpec((chunk, kw), lambda b, g, c: (b * nc + c, GD_QK // kw + g)),
                  pl.BlockSpec((chunk, vw), lambda b, g, c: (b * nc + c, 2 * GD_QK // vw + g)),
                  pl.BlockSpec((chunk, vw), lambda b, g, c: (blk0 + b * nc + c, GD_CONV_DIM // vw + g)),
                  pl.BlockSpec((chunk, LANES), lambda b, g, c: (blk0 + b * nc + c, 0)),
                  vec, vec, vec, st, pl.BlockSpec(memory_space=pl.ANY)],
        out_specs=[pl.BlockSpec((chunk, vw), lambda b, g, c: (blk0 + b * nc + c, g)), st],
        out_shape=[jax.ShapeDtypeStruct(dst.shape, dst.dtype),
                   jax.ShapeDtypeStruct((bsz, GD_V_HEADS, GD_DK, GD_DV), F32)],
        input_output_aliases={9: 0},
        compiler_params=_cparams(("arbitrary", "arbitrary", "arbitrary")),
        name="gdn_chunk",
    )(qkv, qkv, qkv, proj, gates, pad_row(a_log), pad_row(dt_bias), norm_w.reshape(1, GD_DV), s0, dst)


MOE_TM = 256


def _router_kernel(x_ref, w_ref, wt_ref, id_ref):
    logits = jnp.dot(x_ref[...], w_ref[...].astype(BF16), preferred_element_type=F32)
    lane = _iota2(logits.shape, 1).astype(F32)
    first_of = lambda hit: jnp.min(jnp.where(hit, lane, float(LANES)), axis=1, keepdims=True)
    gl = jnp.where(lane < MOE_GROUPS, logits, NEG)
    gmax = jnp.max(gl, axis=1, keepdims=True)
    g_val = 1.0 / jnp.sum(jnp.exp(gl - gmax), axis=1, keepdims=True)
    lo = MOE_GROUPS + first_of(gl == gmax) * MOE_PER_GROUP
    vals = jnp.where((lane >= lo) & (lane < lo + MOE_PER_GROUP), logits, NEG)
    top1 = jnp.max(vals, axis=1, keepdims=True)
    i1 = first_of(vals == top1)
    vals2 = jnp.where(lane == i1, NEG, vals)
    top2 = jnp.max(vals2, axis=1, keepdims=True)
    i2 = first_of(vals2 == top2)
    e2 = jnp.exp(top2 - top1)
    w1 = (1.0 / (1.0 + e2)) * g_val
    w2 = (e2 / (1.0 + e2)) * g_val
    wt_ref[...] = jnp.where(lane == 0, w1, jnp.where(lane == 1, w2, 0.0))
    ids = jnp.where(lane == 0, i1 - MOE_GROUPS, jnp.where(lane == 1, i2 - MOE_GROUPS, 0.0))
    id_ref[...] = ids.astype(jnp.int32)


def moe_router(xb, w_router, tm=512):
    n, d = xb.shape
    out = pl.BlockSpec((tm, LANES), lambda i: (i, 0))
    return pl.pallas_call(
        _router_kernel,
        grid=(n // tm,),
        in_specs=[pl.BlockSpec((tm, d), lambda i: (i, 0)), pl.BlockSpec((d, LANES), lambda i: (0, 0))],
        out_specs=[out, out],
        out_shape=[jax.ShapeDtypeStruct((n, LANES), F32), jax.ShapeDtypeStruct((n, LANES), jnp.int32)],
        compiler_params=_cparams(("arbitrary",)),
        name="moe_router",
    )(xb, w_router)


def _moe_ffn_kernel(te_ref, tv_ref, x_ref, wg_ref, wu_ref, wd_ref, o_ref):
    i = pl.program_id(0)

    @pl.when(tv_ref[i] == 1)
    def _():
        x = x_ref[...].astype(BF16)
        gate = jnp.dot(x, wg_ref[0, 0].astype(BF16), preferred_element_type=F32)
        up = jnp.dot(x, wu_ref[0, 0].astype(BF16), preferred_element_type=F32)
        hid = gate * _sigmoid(gate) * up
        o_ref[...] = jnp.dot(hid.astype(BF16), wd_ref[0, 0].astype(BF16), preferred_element_type=F32)

    @pl.when(tv_ref[i] == 0)
    def _():
        o_ref[...] = jnp.zeros_like(o_ref)


def moe_ffn(x_sorted, tile_expert, tile_valid, layer, w_gate, w_up, w_down):
    rows, d = x_sorted.shape
    ff = w_gate.shape[3]
    grid_spec = pltpu.PrefetchScalarGridSpec(
        num_scalar_prefetch=2,
        grid=(rows // MOE_TM,),
        in_specs=[pl.BlockSpec((MOE_TM, d), lambda i, te, tv: (i, 0)),
                  pl.BlockSpec((1, 1, d, ff), lambda i, te, tv: (layer, te[i], 0, 0)),
                  pl.BlockSpec((1, 1, d, ff), lambda i, te, tv: (layer, te[i], 0, 0)),
                  pl.BlockSpec((1, 1, ff, d), lambda i, te, tv: (layer, te[i], 0, 0))],
        out_specs=pl.BlockSpec((MOE_TM, d), lambda i, te, tv: (i, 0)),
    )
    return pl.pallas_call(
        _moe_ffn_kernel,
        grid_spec=grid_spec,
        out_shape=jax.ShapeDtypeStruct((rows, d), F32),
        compiler_params=_cparams(("arbitrary",)),
        name="moe_ffn",
    )(tile_expert, tile_valid, x_sorted, w_gate, w_up, w_down)


def _ln_moe_kernel(x_ref, ya_ref, yb_ref, wt_ref, g_ref, b_ref, o_ref, ob_ref):
    wt = wt_ref[...]
    z = DEEPNORM_ALPHA * x_ref[...] + (wt[:, 0:1] * ya_ref[...] + wt[:, 1:2] * yb_ref[...])
    zc = z - jnp.mean(z, axis=-1, keepdims=True)
    var = jnp.mean(zc * zc, axis=-1, keepdims=True)
    out = zc * lax.rsqrt(var + LN_EPS) * g_ref[...] + b_ref[...]
    o_ref[...] = out
    ob_ref[...] = out.astype(BF16)


def ln_residual_moe(x, ya, yb, wt, g, b, tm=256):
    m, d = x.shape
    row = pl.BlockSpec((tm, d), lambda i: (i, 0))
    vec = pl.BlockSpec((1, d), lambda i: (0, 0))
    return pl.pallas_call(
        _ln_moe_kernel,
        grid=(m // tm,),
        in_specs=[row, row, row, pl.BlockSpec((tm, LANES), lambda i: (i, 0)), vec, vec],
        out_specs=[row, row],
        out_shape=[jax.ShapeDtypeStruct((m, d), F32), jax.ShapeDtypeStruct((m, d), BF16)],
        compiler_params=_cparams(("arbitrary",)),
        name="ln_residual_moe",
    )(x, ya, yb, wt, g.reshape(1, d), b.reshape(1, d))


def _table_lookup(table, idx):
    hit = idx[:, None] == jnp.arange(table.shape[0], dtype=idx.dtype)[None, :]
    return jnp.sum(jnp.where(hit, table[None, :], 0), axis=1)


def hier_moe(x, xb, layer, w_group, w_expert, w_gate, w_up, w_down):
    n = x.shape[0]
    w_router = jnp.pad(jnp.concatenate([w_group, w_expert], axis=1),
                       ((0, 0), (0, LANES - MOE_GROUPS - MOE_EXPERTS)))
    wt, ids = moe_router(xb, w_router)

    n_pairs = n * MOE_TOPK
    n_tiles = -(-n_pairs // MOE_TM) + MOE_EXPERTS
    n_rows = n_tiles * MOE_TM
    i32 = jnp.int32
    eid = ids[:, :MOE_TOPK].reshape(n_pairs)
    order = jnp.argsort(eid, stable=True).astype(i32)
    rank = jnp.argsort(order).astype(i32)
    experts = jnp.arange(MOE_EXPERTS, dtype=i32)
    counts = jnp.sum((eid[:, None] == experts[None, :]).astype(i32), axis=0)
    starts = jnp.cumsum(counts) - counts
    tiles_per = (counts + MOE_TM - 1) // MOE_TM
    tile_starts = jnp.cumsum(tiles_per) - tiles_per
    dest = _table_lookup(tile_starts * MOE_TM - starts, eid) + rank
    tile_idx = jnp.arange(n_tiles, dtype=i32)
    tile_expert = jnp.sum((tile_idx[:, None] >= (tile_starts + tiles_per)[None, :]).astype(i32), axis=1)
    tile_valid = (tile_expert < MOE_EXPERTS).astype(i32)
    last_used = jnp.max(jnp.where(tiles_per > 0, experts, 0))
    tile_expert = jnp.where(tile_valid == 1, tile_expert, last_used)
    off = (tile_idx - _table_lookup(tile_starts, tile_expert))[:, None] * MOE_TM + jnp.arange(MOE_TM, dtype=i32)[None, :]
    cnt = _table_lookup(counts, tile_expert)[:, None]
    src = jnp.clip(_table_lookup(starts, tile_expert)[:, None] + off, 0, n_pairs - 1)
    row_ok = (off < cnt) & (tile_valid[:, None] == 1)
    row_token = jnp.where(row_ok, jnp.take(order, src.reshape(n_rows)).reshape(n_tiles, MOE_TM) // MOE_TOPK, 0)

    x_sorted = jnp.take(x, row_token.reshape(n_rows), axis=0)
    y_sorted = moe_ffn(x_sorted, tile_expert, tile_valid, layer, w_gate, w_up, w_down)
    dest2 = dest.reshape(n, MOE_TOPK)
    return jnp.take(y_sorted, dest2[:, 0], axis=0), jnp.take(y_sorted, dest2[:, 1], axis=0), wt


SAMPLE_T_PAD = 8


def kernel(x_prompt, x_sample, state_mlstm_c, state_mlstm_n, state_mlstm_m, state_rwkv_s,
           state_rwkv_shift, state_gdn_s, state_gdn_conv, p_prompt, p_sample,
           w_in_ab, ml_b_i, ml_b_f, ml_norm_w, rw_mu, rw_w0, rw_w2, rw_a0, rw_a2, rw_g2,
           rw_k_k, rw_k_a, rw_r_k, rw_ln_w, rw_ln_b, w_out_ab,
           gd_w_in, gd_conv_w, gd_a_log, gd_dt_bias, gd_norm_w, gd_w_out,
           ln_mix_g, ln_mix_b, moe_w_group, moe_w_expert, moe_w_gate, moe_w_up, moe_w_down,
           ln_ffn_g, ln_ffn_b, ple_w_gate, ple_w_proj):
    pb, pt, d = x_prompt.shape
    sb, st, _ = x_sample.shape
    n_p = pb * pt
    n_s = sb * SAMPLE_T_PAD
    n = n_p + n_s
    pad_t = lambda a: jnp.pad(a, ((0, 0), (0, SAMPLE_T_PAD - st), (0, 0)))
    merge = lambda a_p, a_s: jnp.concatenate(
        [a_p.reshape(n_p, a_p.shape[-1]), pad_t(a_s).reshape(n_s, a_s.shape[-1])], axis=0)
    x = merge(x_prompt, x_sample)
    xb = x.astype(BF16)
    segs = (dict(row0=0, bsz=pb, t_pad=pt, t_valid=pt, rw_hb=8, gd_hpg=8),
            dict(row0=n_p, bsz=sb, t_pad=SAMPLE_T_PAD, t_valid=st, rw_hb=16, gd_hpg=16))
    zeros = lambda *shape: jnp.zeros(shape, F32)

    def last_rows(a, sg, count):
        assert sg['t_valid'] >= count
        idx = (sg['row0'] + jnp.arange(sg['bsz'])[:, None] * sg['t_pad']
               + (sg['t_valid'] - count + jnp.arange(count))[None, :])
        return jnp.take(a, idx.reshape(-1), axis=0).reshape(sg['bsz'], count, a.shape[-1])

    new_states = [[[] for _ in range(7)] for _ in segs]
    for layer in range(DEPTH):
        li = layer // 2
        if layer % 2 == 0:
            w_in = w_in_ab[li]
            proj_ml = matmul(xb, w_in, 3200, 640)
            proj_rw = matmul(xb, w_in[:, ML_COLS:].astype(BF16), RW_COLS, RW_COLS // 2)
            rw_prm = dict(mu=rw_mu[li], w0=rw_w0[li], w2=rw_w2[li], a0=rw_a0[li], a2=rw_a2[li], g2=rw_g2[li],
                          k_k=rw_k_k[li], k_a=rw_k_a[li], r_k=rw_r_k[li])
            heads_out = jnp.zeros((n, ML_V + RW_WIDTH), BF16)
            for si, sg in enumerate(segs):
                bsz, t_pad, t_valid, row0 = sg['bsz'], sg['t_pad'], sg['t_valid'], sg['row0']
                if si == 0:
                    c0, n0, m0 = zeros(bsz, ML_HEADS, ML_DK, ML_DV), zeros(bsz, ML_HEADS, ML_DK), zeros(bsz, ML_HEADS)
                    s0, sh0 = zeros(bsz, RW_HEADS, RW_DH, RW_DH), zeros(bsz, RW_COLS)
                else:
                    c0, n0, m0 = state_mlstm_c[li], state_mlstm_n[li], state_mlstm_m[li]
                    s0, sh0 = state_rwkv_s[li], state_rwkv_shift[li]
                heads_out, c, nn, m = mlstm_group(proj_ml, heads_out, row0, bsz, t_pad, t_valid,
                                                  min(ML_CHUNK, t_pad), c0, n0, m0,
                                                  ml_b_i[li], ml_b_f[li], ml_norm_w[li])
                prep = rwkv_prep(proj_rw, row0, bsz, t_pad, sh0, rw_prm, 256)
                heads_out, rs = rwkv_chunk(prep, heads_out, ML_V, row0, bsz, t_pad, t_valid,
                                           min(RW_CHUNK, t_pad), s0, rw_ln_w[li], rw_ln_b[li], sg['rw_hb'])
                new_shift = last_rows(proj_rw, sg, 1)[:, 0]
                for slot, val in zip(range(5), (c, nn, m, rs, new_shift)):
                    new_states[si][slot].append(val)
            mix = matmul(heads_out, w_out_ab[li], d, 512)
        else:
            w_in = gd_w_in[li]
            n_qkvz = GD_CONV_DIM + GD_V_WIDTH
            proj = matmul(xb, w_in, n_qkvz, 1024)
            w_gates = jnp.pad(w_in[:, n_qkvz:], ((0, 0), (0, LANES - 2 * GD_V_HEADS)))
            gates = matmul(xb, w_gates, LANES, LANES)
            gd_out = jnp.zeros((n, GD_V_WIDTH), BF16)
            for si, sg in enumerate(segs):
                bsz, t_pad, t_valid, row0 = sg['bsz'], sg['t_pad'], sg['t_valid'], sg['row0']
                if si == 0:
                    s0, buf0 = zeros(bsz, GD_V_HEADS, GD_DK, GD_DV), zeros(bsz, GD_CONV - 1, GD_CONV_DIM)
                else:
                    s0, buf0 = state_gdn_s[li], state_gdn_conv[li]
                qkv = gdn_prep(proj, row0, bsz, t_pad, buf0, gd_conv_w[li], 256)
                gd_out, gs = gdn_chunk(qkv, proj, gates, gd_out, row0, bsz, t_pad, t_valid,
                                       min(GD_CHUNK, t_pad), s0, gd_a_log[li], gd_dt_bias[li],
                                       gd_norm_w[li], sg['gd_hpg'])
                new_states[si][5].append(gs)
                new_states[si][6].append(last_rows(proj, sg, GD_CONV - 1)[:, :, :GD_CONV_DIM])
            mix = matmul(gd_out, gd_w_out[li], d, 512)
        x, xb = ln_residual(x, mix, ln_mix_g[layer], ln_mix_b[layer])
        ya, yb, wt = hier_moe(x, xb, layer, moe_w_group[layer], moe_w_expert[layer],
                              moe_w_gate, moe_w_up, moe_w_down)
        x, xb = ln_residual_moe(x, ya, yb, wt, ln_ffn_g[layer], ln_ffn_b[layer])
        x, xb = ple(xb, merge(p_prompt[layer], p_sample[layer]), x, layer, ple_w_gate, ple_w_proj)
    y_prompt = x[:n_p].reshape(pb, pt, d)
    y_sample = x[n_p:].reshape(sb, SAMPLE_T_PAD, d)[:, :st]
    stack = lambda vals: jnp.stack(vals)
    return (y_prompt, y_sample) + tuple(stack(v) for v in new_states[0]) + tuple(stack(v) for v in new_states[1])
```

```python
import functools

import jax
import jax.numpy as jnp
import numpy as np
from jax import lax
from jax.experimental import pallas as pl
from jax.experimental.pallas import tpu as pltpu

F32 = jnp.float32
BF16 = jnp.bfloat16

D_MODEL = 2048
DEPTH = 2
DEEPNORM_ALPHA = (2 * DEPTH) ** 0.25
LN_EPS = 1e-5
MIX_HALF = D_MODEL // 2
ML_HEADS = 4
ML_DV = MIX_HALF // ML_HEADS
ML_DK = ML_DV // 2
ML_CHUNK = 64
ML_GATE_CAP = 15.0
ML_NORM_EPS = 1e-6
ML_QK = ML_HEADS * ML_DK
ML_V = ML_HEADS * ML_DV
ML_COLS = 2 * ML_QK + 2 * ML_V + 2 * ML_HEADS
RW_DH = 64
RW_HEADS = MIX_HALF // RW_DH
RW_WIDTH = RW_HEADS * RW_DH
RW_DECAY_LORA = 64
RW_A_LORA = 64
RW_GATE_LORA = 128
RW_GN_EPS = 64e-5
RW_COLS = 3 * RW_WIDTH + RW_DECAY_LORA + RW_A_LORA + RW_GATE_LORA
RW_CHUNK = 64
GD_DK = 128
GD_DV = 128
GD_K_HEADS = D_MODEL // GD_DK
GD_V_HEADS = 2 * GD_K_HEADS
GD_CONV = 4
GD_CHUNK = 64
GD_EPS = 1e-6
GD_QK = GD_K_HEADS * GD_DK
GD_V_WIDTH = GD_V_HEADS * GD_DV
GD_CONV_DIM = 2 * GD_QK + GD_V_WIDTH
MOE_GROUPS = 4
MOE_PER_GROUP = 8
MOE_EXPERTS = MOE_GROUPS * MOE_PER_GROUP
MOE_TOPK = 2
MOE_FF = D_MODEL // 4
PLE_DIM = 256

LANES = 128
SUBLANES = 8
NEG = -1e30
VMEM_LIMIT = 56 * 1024 * 1024


def _cparams(sem):
    return pltpu.CompilerParams(dimension_semantics=sem, vmem_limit_bytes=VMEM_LIMIT)


def _dot(a, b):
    return jnp.dot(a.astype(BF16), b.astype(BF16), preferred_element_type=F32)


def _dot_nt(a, b):
    return lax.dot_general(a.astype(BF16), b.astype(BF16), (((1,), (1,)), ((), ())),
                           preferred_element_type=F32)


def _dot_tn(a, b):
    return lax.dot_general(a.astype(BF16), b.astype(BF16), (((0,), (0,)), ((), ())),
                           preferred_element_type=F32)


def _split3(x):
    hi = x.astype(BF16)
    r1 = x - hi.astype(F32)
    mid = r1.astype(BF16)
    lo = (r1 - mid.astype(F32)).astype(BF16)
    return hi, mid, lo


def _dot_exact_rhs(a, b01):
    hi, mid, lo = _split3(a)
    b = b01.astype(BF16)
    return (jnp.dot(hi, b, preferred_element_type=F32) + jnp.dot(mid, b, preferred_element_type=F32)
            + jnp.dot(lo, b, preferred_element_type=F32))


def _dot_exact_lhs(a01, b):
    hi, mid, lo = _split3(b)
    a = a01.astype(BF16)
    return (jnp.dot(a, hi, preferred_element_type=F32) + jnp.dot(a, mid, preferred_element_type=F32)
            + jnp.dot(a, lo, preferred_element_type=F32))


def _iota2(shape, dim):
    return lax.broadcasted_iota(jnp.int32, shape, dim)


def _col_to_row(col, eye):
    return jnp.sum(jnp.where(eye, col, 0.0), axis=0, keepdims=True)


def _sigmoid(x):
    return 1.0 / (1.0 + jnp.exp(-x))


def _softplus(x):
    return jnp.maximum(x, 0.0) + jnp.log1p(jnp.exp(-jnp.abs(x)))


def _neumann_inverse(n_mat, size):
    eye = (_iota2((size, size), 0) == _iota2((size, size), 1)).astype(F32)
    t = eye + n_mat
    x = n_mat
    steps = max(int(np.ceil(np.log2(size))) - 1, 0)
    for _ in range(steps):
        x = _dot(x, x)
        t = t + _dot(t, x)
    return t


def _mm_kernel(x_ref, w_ref, o_ref, wb_ref):
    @pl.when(pl.program_id(1) == 0)
    def _():
        wb_ref[...] = w_ref[...].astype(BF16)

    o_ref[...] = jnp.dot(x_ref[...].astype(BF16), wb_ref[...],
                         preferred_element_type=F32).astype(o_ref.dtype)


def matmul(x, w, n_out, tn, tm=512, out_dtype=F32):
    m, k = x.shape
    assert w.shape[0] == k and n_out % tn == 0 and m % tm == 0 and tn % LANES == 0
    return pl.pallas_call(
        _mm_kernel,
        grid=(n_out // tn, m // tm),
        in_specs=[pl.BlockSpec((tm, k), lambda j, i: (i, 0)),
                  pl.BlockSpec((k, tn), lambda j, i: (0, j))],
        out_specs=pl.BlockSpec((tm, tn), lambda j, i: (i, j)),
        out_shape=jax.ShapeDtypeStruct((m, n_out), out_dtype),
        scratch_shapes=[pltpu.VMEM((k, tn), BF16)],
        compiler_params=_cparams(("arbitrary", "arbitrary")),
        name="matmul",
    )(x, w)


def _ln_res_kernel(x_ref, y_ref, g_ref, b_ref, o_ref, ob_ref):
    z = DEEPNORM_ALPHA * x_ref[...] + y_ref[...]
    zc = z - jnp.mean(z, axis=-1, keepdims=True)
    var = jnp.mean(zc * zc, axis=-1, keepdims=True)
    out = zc * lax.rsqrt(var + LN_EPS) * g_ref[...] + b_ref[...]
    o_ref[...] = out
    ob_ref[...] = out.astype(BF16)


def ln_residual(x, y, g, b, tm=256):
    m, d = x.shape
    row = pl.BlockSpec((tm, d), lambda i: (i, 0))
    vec = pl.BlockSpec((1, d), lambda i: (0, 0))
    return pl.pallas_call(
        _ln_res_kernel,
        grid=(m // tm,),
        in_specs=[row, row, vec, vec],
        out_specs=[row, row],
        out_shape=[jax.ShapeDtypeStruct((m, d), F32), jax.ShapeDtypeStruct((m, d), BF16)],
        compiler_params=_cparams(("arbitrary",)),
        name="ln_residual",
    )(x, y, g.reshape(1, d), b.reshape(1, d))


def _ple_kernel(xb_ref, p_ref, x_ref, wg_ref, wp_ref, o_ref, ob_ref, wgb_ref, wpb_ref):
    @pl.when(pl.program_id(1) == 0)
    def _():
        wgb_ref[...] = wg_ref[0].astype(BF16)
        wpb_ref[...] = wp_ref[0].astype(BF16)

    gate = _sigmoid(jnp.dot(xb_ref[...], wgb_ref[...], preferred_element_type=F32))
    emb = jnp.dot(p_ref[...].astype(BF16), wpb_ref[...], preferred_element_type=F32)
    out = x_ref[...] + gate * emb
    o_ref[...] = out
    ob_ref[...] = out.astype(BF16)


def ple(xb, p, x, layer, wg, wp, tm=512, tn=512):
    m, d = x.shape
    pd = p.shape[1]
    return pl.pallas_call(
        _ple_kernel,
        grid=(d // tn, m // tm),
        in_specs=[pl.BlockSpec((tm, d), lambda j, i: (i, 0)),
                  pl.BlockSpec((tm, pd), lambda j, i: (i, 0)),
                  pl.BlockSpec((tm, tn), lambda j, i: (i, j)),
                  pl.BlockSpec((1, d, tn), lambda j, i: (layer, 0, j)),
                  pl.BlockSpec((1, pd, tn), lambda j, i: (layer, 0, j))],
        out_specs=[pl.BlockSpec((tm, tn), lambda j, i: (i, j))] * 2,
        out_shape=[jax.ShapeDtypeStruct((m, d), F32), jax.ShapeDtypeStruct((m, d), BF16)],
        scratch_shapes=[pltpu.VMEM((d, tn), BF16), pltpu.VMEM((pd, tn), BF16)],
        compiler_params=_cparams(("arbitrary", "arbitrary")),
        name="ple",
    )(xb, p, x, wg, wp)


def _mlstm_kernel(q_ref, k_ref, v_ref, o_ref, g_ref, c0_ref, n0_ref, m0_ref, nw_ref, gb_ref, _dst_ref,
                  h_ref, c_ref, n_ref, m_ref, *, chunk, t_valid):
    ci = pl.program_id(1)

    @pl.when(ci == 0)
    def _():
        c_ref[...] = c0_ref[...]
        n_ref[...] = n0_ref[...]
        m_ref[...] = m0_ref[...]

    size = chunk
    rows = _iota2((size, size), 0)
    cols = _iota2((size, size), 1)
    eye = rows == cols
    causal = cols <= rows
    gates = g_ref[:, 0:2 * ML_HEADS] + gb_ref[...]
    capped = ML_GATE_CAP * jnp.tanh(gates / ML_GATE_CAP)
    t_idx = ci * size + _iota2((size, 1), 0)
    valid = t_idx < t_valid
    ipre_all = jnp.where(valid, capped[:, 0:ML_HEADS], NEG)
    logf_all = jnp.where(valid, -_softplus(-capped[:, ML_HEADS:2 * ML_HEADS]), 0.0)
    heads = range(ML_HEADS)
    q = [q_ref[:, h * ML_DK:(h + 1) * ML_DK] for h in heads]
    k = [k_ref[:, h * ML_DK:(h + 1) * ML_DK] * (ML_DK ** -0.5) for h in heads]
    v = [v_ref[:, h * ML_DV:(h + 1) * ML_DV] for h in heads]
    qk = [_dot_nt(q[h], k[h]) for h in heads]
    c_prev = [c_ref[0, h] for h in heads]
    n_prev = [n_ref[0, h:h + 1, :] for h in heads]
    qc = [_dot(q[h], c_prev[h]) for h in heads]
    s, w_inter, m_t, kw, decay = [], [], [], [], []
    for h in heads:
        ig_col = ipre_all[:, h:h + 1]
        lf_col = logf_all[:, h:h + 1]
        ig_row = _col_to_row(ig_col, eye)
        lf_row = _col_to_row(lf_col, eye)
        b_col = jnp.sum(jnp.where(causal, lf_row, 0.0), axis=1, keepdims=True)
        b_row = jnp.sum(jnp.where(rows <= cols, lf_col, 0.0), axis=0, keepdims=True)
        d = jnp.where(causal, b_col - b_row + ig_row, NEG)
        inter = b_col + m_ref[0, :, h:h + 1]
        m_h = jnp.maximum(inter, jnp.max(d, axis=1, keepdims=True))
        w_h = jnp.exp(inter - m_h)
        s.append(qk[h] * jnp.exp(d - m_h))
        b_last = b_col[size - 1:size, :]
        m_last = m_h[size - 1:size, :]
        w_last = jnp.exp(b_last - b_col + ig_col - m_last)
        kw.append(k[h] * w_last)
        w_inter.append(w_h)
        m_t.append(m_h)
        decay.append(w_h[size - 1:size, :])
        m_ref[0, :, h:h + 1] = m_last
    sv = [_dot(s[h], v[h]) for h in heads]
    kv = [_dot_tn(kw[h], v[h]) for h in heads]
    for h in heads:
        num = w_inter[h] * qc[h] + sv[h]
        den = (w_inter[h] * jnp.sum(q[h] * n_prev[h], axis=1, keepdims=True)
               + jnp.sum(s[h], axis=1, keepdims=True))
        hid = num / jnp.maximum(jnp.abs(den), jnp.exp(-m_t[h]))
        c_ref[0, h] = decay[h] * c_prev[h] + kv[h]
        n_ref[0, h:h + 1, :] = decay[h] * n_prev[h] + jnp.sum(kw[h], axis=0, keepdims=True)
        hid = hid * lax.rsqrt(jnp.mean(hid * hid, axis=-1, keepdims=True) + ML_NORM_EPS)
        hid = hid * nw_ref[:, h * ML_DV:(h + 1) * ML_DV]
        hid = hid * _sigmoid(o_ref[:, h * ML_DV:(h + 1) * ML_DV])
        h_ref[:, h * ML_DV:(h + 1) * ML_DV] = hid.astype(h_ref.dtype)


def mlstm_group(proj, dst, row0, bsz, t_pad, t_valid, chunk, c0, n0, m0, b_i, b_f, norm_w):
    nc = t_pad // chunk
    blk0 = row0 // chunk
    assert row0 % chunk == 0 and t_pad % chunk == 0
    rmap = lambda c0_: (lambda b, c: (blk0 + b * nc + c, c0_))
    qk_w, v_w = ML_QK, ML_V
    in_specs = [
        pl.BlockSpec((chunk, qk_w), rmap(0)),
        pl.BlockSpec((chunk, qk_w), rmap(1)),
        pl.BlockSpec((chunk, v_w), rmap(1)),
        pl.BlockSpec((chunk, v_w), rmap(2)),
        pl.BlockSpec((chunk, LANES), rmap((2 * qk_w + 2 * v_w) // LANES)),
        pl.BlockSpec((1, ML_HEADS, ML_DK, ML_DV), lambda b, c: (b, 0, 0, 0)),
        pl.BlockSpec((1, ML_HEADS, ML_DK), lambda b, c: (b, 0, 0)),
        pl.BlockSpec((1, 1, ML_HEADS), lambda b, c: (b, 0, 0)),
        pl.BlockSpec((1, v_w), lambda b, c: (0, 0)),
        pl.BlockSpec((1, 2 * ML_HEADS), lambda b, c: (0, 0)),
        pl.BlockSpec(memory_space=pl.ANY),
    ]
    out_specs = [
        pl.BlockSpec((chunk, v_w), lambda b, c: (blk0 + b * nc + c, 0)),
        pl.BlockSpec((1, ML_HEADS, ML_DK, ML_DV), lambda b, c: (b, 0, 0, 0)),
        pl.BlockSpec((1, ML_HEADS, ML_DK), lambda b, c: (b, 0, 0)),
        pl.BlockSpec((1, 1, ML_HEADS), lambda b, c: (b, 0, 0)),
    ]
    out_shape = [
        jax.ShapeDtypeStruct(dst.shape, dst.dtype),
        jax.ShapeDtypeStruct((bsz, ML_HEADS, ML_DK, ML_DV), F32),
        jax.ShapeDtypeStruct((bsz, ML_HEADS, ML_DK), F32),
        jax.ShapeDtypeStruct((bsz, 1, ML_HEADS), F32),
    ]
    gate_bias = jnp.concatenate([b_i, b_f]).reshape(1, 2 * ML_HEADS)
    h, c, n, m = pl.pallas_call(
        functools.partial(_mlstm_kernel, chunk=chunk, t_valid=t_valid),
        grid=(bsz, nc),
        in_specs=in_specs,
        out_specs=out_specs,
        out_shape=out_shape,
        input_output_aliases={10: 0},
        compiler_params=_cparams(("arbitrary", "arbitrary")),
        name="mlstm",
    )(proj, proj, proj, proj, proj, c0, n0, m0.reshape(bsz, 1, ML_HEADS), norm_w.reshape(1, v_w), gate_bias, dst)
    return h, c, n, m.reshape(bsz, ML_HEADS)


def _head_block_ones(width, head):
    idx = np.arange(width) // head
    return jnp.asarray((idx[:, None] == idx[None, :]).astype(np.float32), dtype=BF16)


def _rwkv_prep_kernel(cur_ref, tail_ref, first_ref, mu_ref, w0_ref, a0_ref, kk_ref, ka_ref, rk_ref,
                      w2_ref, a2_ref, g2_ref, blk_ref,
                      r_ref, wl_ref, k_ref, v_ref, an_ref, bb_ref, bonus_ref, g_ref,
                      *, tt, t_pad, multi):
    cur = cur_ref[...]
    rolled = pltpu.roll(cur, 1, 0)
    row = _iota2((tt, 1), 0)
    if multi:
        prev = jnp.where((row & (t_pad - 1)) == 0, first_ref[...], rolled)
    else:
        starts_seq = (pl.program_id(0) % (t_pad // tt)) == 0
        head = jnp.where(starts_seq, first_ref[0], tail_ref[SUBLANES - 1:SUBLANES, :])
        prev = jnp.where(row == 0, head, rolled)
    mixed = cur + (prev - cur) * mu_ref[...]
    w = RW_WIDTH
    r = mixed[:, 0:w]
    k = mixed[:, w:2 * w]
    v = mixed[:, 2 * w:3 * w]
    c0 = 3 * w
    wl = mixed[:, c0:c0 + RW_DECAY_LORA]
    al = mixed[:, c0 + RW_DECAY_LORA:c0 + RW_DECAY_LORA + RW_A_LORA]
    gl = mixed[:, c0 + RW_DECAY_LORA + RW_A_LORA:]
    w_log = -jnp.exp(-_softplus(-(w0_ref[...] + _dot(jnp.tanh(wl), w2_ref[...]))) - 0.5)
    a = _sigmoid(a0_ref[...] + _dot(al, a2_ref[...]))
    g = _dot(_sigmoid(gl), g2_ref[...])
    kk = k * kk_ref[...]
    blk = blk_ref[...]
    kkn = kk / jnp.maximum(jnp.sqrt(_dot_exact_rhs(kk * kk, blk)), 1e-12)
    k2 = k * (1.0 + (a - 1.0) * ka_ref[...])
    r_ref[...] = r
    wl_ref[...] = w_log
    k_ref[...] = k2
    v_ref[...] = v
    an_ref[...] = -kkn
    bb_ref[...] = kkn * a
    bonus_ref[...] = _dot_exact_rhs(r * k2 * rk_ref[...], blk) * v
    g_ref[...] = g


def rwkv_prep(proj, row0, bsz, t_pad, shift0, prm, tt):
    n = bsz * t_pad
    multi = tt > t_pad
    assert row0 % tt == 0 and n % tt == 0 and (tt % t_pad == 0 if multi else t_pad % tt == 0)
    blk0 = row0 // tt
    c = RW_COLS
    if multi:
        first = jnp.zeros((bsz, t_pad, c), F32).at[:, 0].set(shift0).reshape(n, c)
        first_spec = pl.BlockSpec((tt, c), lambda i: (i, 0))
    else:
        first = shift0.reshape(bsz, 1, c)
        first_spec = pl.BlockSpec((1, 1, c), lambda i: (i // (t_pad // tt), 0, 0))
    tail_blk = tt // SUBLANES
    vec = lambda width: pl.BlockSpec((1, width), lambda i: (0, 0))
    full = lambda a, b: pl.BlockSpec((a, b), lambda i: (0, 0))
    w = RW_WIDTH
    out_spec = pl.BlockSpec((tt, w), lambda i: (i, 0))
    return pl.pallas_call(
        functools.partial(_rwkv_prep_kernel, tt=tt, t_pad=t_pad, multi=multi),
        grid=(n // tt,),
        in_specs=[pl.BlockSpec((tt, c), lambda i: (blk0 + i, 0)),
                  pl.BlockSpec((SUBLANES, c), lambda i: (jnp.maximum((blk0 + i) * tail_blk - 1, 0), 0)),
                  first_spec, vec(c), vec(w), vec(w), vec(w), vec(w), vec(w),
                  full(RW_DECAY_LORA, w), full(RW_A_LORA, w), full(RW_GATE_LORA, w), full(w, w)],
        out_specs=[out_spec] * 8,
        out_shape=[jax.ShapeDtypeStruct((n, w), F32)] * 8,
        compiler_params=_cparams(("arbitrary",)),
        name="rwkv_prep",
    )(proj, proj, first, prm['mu'].reshape(1, c), prm['w0'].reshape(1, w), prm['a0'].reshape(1, w),
      prm['k_k'].reshape(1, w), prm['k_a'].reshape(1, w), prm['r_k'].reshape(1, w),
      prm['w2'], prm['a2'], prm['g2'], _head_block_ones(w, RW_DH))


def _neumann_inverse_many(n_mats, size):
    eye = (_iota2((size, size), 0) == _iota2((size, size), 1)).astype(F32)
    ts = [eye + n_mat for n_mat in n_mats]
    xs = list(n_mats)
    steps = max(int(np.ceil(np.log2(size))) - 1, 0)
    for _ in range(steps):
        xs = [_dot(x, x) for x in xs]
        ts = [t + _dot(t, x) for t, x in zip(ts, xs)]
    return ts


def _rwkv_chunk_kernel(r_ref, w_ref, k_ref, v_ref, a_ref, b_ref, bonus_ref, g_ref, s0_ref,
                       lnw_ref, lnb_ref, _dst_ref, o_ref, s_ref, acc_ref, *, chunk, t_valid, hb):
    ci = pl.program_id(2)

    @pl.when(ci == 0)
    def _():
        s_ref[...] = s0_ref[...]

    size = chunk
    rows = _iota2((size, size), 0)
    cols = _iota2((size, size), 1)
    strict = cols < rows
    rows2 = _iota2((size, 2 * size), 0)
    cols2 = _iota2((size, 2 * size), 1)
    mask_ak = (cols2 >= size) & (cols2 - size < rows2)
    mask_o = jnp.where(cols2 >= size, cols2 - size, cols2) <= rows2
    valid = (ci * size + _iota2((size, 1), 0)) < t_valid
    w = jnp.where(valid, w_ref[...], 0.0)
    a = jnp.where(valid, a_ref[...], 0.0)
    b = jnp.where(valid, b_ref[...], 0.0)
    k = jnp.where(valid, k_ref[...], 0.0)
    v = v_ref[...]
    lam = _dot_exact_lhs((cols <= rows).astype(F32), w)
    e_pos = jnp.exp(lam)
    e_neg = jnp.exp(-lam)
    at = a * jnp.exp(lam - w)
    bt = b * e_neg
    kt = k * e_neg
    rt = r_ref[...] * e_pos
    heads = range(hb)
    sls = [slice(h * RW_DH, (h + 1) * RW_DH) for h in heads]
    at_h = [at[:, sl] for sl in sls]
    rt_h = [rt[:, sl] for sl in sls]
    v_h = [v[:, sl] for sl in sls]
    bk = [jnp.concatenate([bt[:, sl], kt[:, sl]], axis=0) for sl in sls]
    s0 = [s_ref[0, h] for h in heads]
    pa = [_dot_nt(at_h[h], bk[h]) for h in heads]
    pr = [_dot_nt(rt_h[h], bk[h]) for h in heads]
    as0 = [_dot_nt(at_h[h], s0[h]) for h in heads]
    rs0 = [_dot_nt(rt_h[h], s0[h]) for h in heads]
    zv = [jnp.concatenate([jnp.zeros_like(v_h[h]), v_h[h]], axis=0) for h in heads]
    rhs = [as0[h] + _dot(jnp.where(mask_ak, pa[h], 0.0), zv[h]) for h in heads]
    t_inv = _neumann_inverse_many([jnp.where(strict, pa[h][:, :size], 0.0) for h in heads], size)
    u = [_dot(t_inv[h], rhs[h]) for h in heads]
    uv = [jnp.concatenate([u[h], v_h[h]], axis=0) for h in heads]
    o = [rs0[h] + _dot(jnp.where(mask_o, pr[h], 0.0), uv[h]) for h in heads]
    ds = [_dot_tn(uv[h], bk[h]) for h in heads]
    for h in heads:
        s_ref[0, h] = (s0[h] + ds[h]) * e_pos[size - 1:size, sls[h]]
        oc = o[h] - jnp.mean(o[h], axis=-1, keepdims=True)
        acc_ref[:, sls[h]] = oc * lax.rsqrt(jnp.mean(oc * oc, axis=-1, keepdims=True) + RW_GN_EPS)
    out = (acc_ref[...] * lnw_ref[...] + lnb_ref[...] + bonus_ref[...]) * g_ref[...]
    o_ref[...] = out.astype(o_ref.dtype)


def rwkv_chunk(prep, dst, dst_col0, row0, bsz, t_pad, t_valid, chunk, s0, ln_w, ln_b, hb):
    nc = t_pad // chunk
    hg = RW_HEADS // hb
    wb = hb * RW_DH
    blk0 = row0 // chunk
    assert dst_col0 % wb == 0 and row0 % chunk == 0
    tok = pl.BlockSpec((chunk, wb), lambda b, g, c: (b * nc + c, g))
    st = pl.BlockSpec((1, hb, RW_DH, RW_DH), lambda b, g, c: (b, g, 0, 0))
    vec = pl.BlockSpec((1, wb), lambda b, g, c: (0, g))
    return pl.pallas_call(
        functools.partial(_rwkv_chunk_kernel, chunk=chunk, t_valid=t_valid, hb=hb),
        grid=(bsz, hg, nc),
        in_specs=[tok] * 8 + [st, vec, vec, pl.BlockSpec(memory_space=pl.ANY)],
        out_specs=[pl.BlockSpec((chunk, wb), lambda b, g, c: (blk0 + b * nc + c, dst_col0 // wb + g)), st],
        out_shape=[jax.ShapeDtypeStruct(dst.shape, dst.dtype),
                   jax.ShapeDtypeStruct((bsz, RW_HEADS, RW_DH, RW_DH), F32)],
        scratch_shapes=[pltpu.VMEM((chunk, wb), F32)],
        input_output_aliases={11: 0},
        compiler_params=_cparams(("arbitrary", "arbitrary", "arbitrary")),
        name="rwkv_chunk",
    )(*prep, s0, ln_w.reshape(1, RW_WIDTH), ln_b.reshape(1, RW_WIDTH), dst)


def _gdn_prep_kernel(cur_ref, tail_ref, hist_ref, cw_ref, o_ref, *, tt, t_pad, multi, heads_per_tile):
    cur = cur_ref[...]
    cw = cw_ref[...]
    row = _iota2((tt, 1), 0)
    acc = cur * cw[GD_CONV - 1:GD_CONV, :]
    if multi:
        hexp = hist_ref[...]
        pos = row & (t_pad - 1)
        for j in range(1, GD_CONV):
            back = GD_CONV - 1 - j
            hj = pltpu.roll(hexp, tt - back, 0) if back else hexp
            prev = jnp.where(pos >= j, pltpu.roll(cur, j, 0), hj)
            acc = acc + prev * cw[back:back + 1, :]
    else:
        starts_seq = (pl.program_id(0) % (t_pad // tt)) == 0
        hist = jnp.where(starts_seq, hist_ref[0], tail_ref[...])
        row8 = _iota2((SUBLANES, 1), 0)
        for j in range(1, GD_CONV):
            back = GD_CONV - 1 - j
            rolled = pltpu.roll(cur, j, 0)
            head = jnp.where(row8 < j, pltpu.roll(hist, j, 0), rolled[:SUBLANES])
            prev = jnp.concatenate([head, rolled[SUBLANES:]], axis=0)
            acc = acc + prev * cw[back:back + 1, :]
    act = acc * _sigmoid(acc)
    kind = pl.program_id(1)
    scale = jnp.where(kind == 0, GD_DK ** -0.5, 1.0)
    for h in range(heads_per_tile):
        sl = slice(h * GD_DK, (h + 1) * GD_DK)
        x = act[:, sl]
        normed = x * lax.rsqrt(jnp.sum(x * x, axis=-1, keepdims=True) + GD_EPS) * scale
        o_ref[:, sl] = jnp.where(kind < 2, normed, x)


def gdn_prep(proj, row0, bsz, t_pad, conv_buf, conv_w, tt):
    n = bsz * t_pad
    multi = tt > t_pad
    assert row0 % tt == 0 and n % tt == 0 and (tt % t_pad == 0 if multi else t_pad % tt == 0)
    blk0 = row0 // tt
    c = GD_CONV_DIM
    tc = GD_QK
    nbuf = GD_CONV - 1
    if multi:
        hist = jnp.pad(conv_buf, ((0, 0), (0, t_pad - nbuf), (0, 0))).reshape(n, c)
        hist_spec = pl.BlockSpec((tt, tc), lambda i, j: (i, j))
    else:
        hist = jnp.pad(conv_buf, ((0, 0), (SUBLANES - nbuf, 0), (0, 0)))
        hist_spec = pl.BlockSpec((1, SUBLANES, tc), lambda i, j: (i // (t_pad // tt), 0, j))
    tail_blk = tt // SUBLANES
    return pl.pallas_call(
        functools.partial(_gdn_prep_kernel, tt=tt, t_pad=t_pad, multi=multi, heads_per_tile=tc // GD_DK),
        grid=(n // tt, c // tc),
        in_specs=[pl.BlockSpec((tt, tc), lambda i, j: (blk0 + i, j)),
                  pl.BlockSpec((SUBLANES, tc), lambda i, j: (jnp.maximum((blk0 + i) * tail_blk - 1, 0), j)),
                  hist_spec,
                  pl.BlockSpec((GD_CONV, tc), lambda i, j: (0, j))],
        out_specs=pl.BlockSpec((tt, tc), lambda i, j: (i, j)),
        out_shape=jax.ShapeDtypeStruct((n, c), F32),
        compiler_params=_cparams(("arbitrary", "arbitrary")),
        name="gdn_prep",
    )(proj, proj, hist, conv_w)


def _gdn_chunk_kernel(q_ref, k_ref, v_ref, z_ref, gl_ref, al_ref, dt_ref, nw_ref, s0_ref, _dst_ref,
                      o_ref, s_ref, *, chunk, t_valid, hpg):
    ci = pl.program_id(2)
    hg = pl.program_id(1)

    @pl.when(ci == 0)
    def _():
        s_ref[...] = s0_ref[...]

    size = chunk
    rows = _iota2((size, size), 0)
    cols = _iota2((size, size), 1)
    eye = rows == cols
    incl = cols <= rows
    strict = cols < rows
    valid = (ci * size + _iota2((size, 1), 0)) < t_valid
    logits = gl_ref[...]
    beta_all = _sigmoid(logits)
    g_all = -jnp.exp(al_ref[...]) * _softplus(logits + dt_ref[...])
    lane = _iota2((size, LANES), 1)
    rep = GD_V_HEADS // GD_K_HEADS
    kheads = range(hpg // rep)
    heads = range(hpg)
    q = [q_ref[:, kh * GD_DK:(kh + 1) * GD_DK] for kh in kheads]
    k = [k_ref[:, kh * GD_DK:(kh + 1) * GD_DK] for kh in kheads]
    qk = [_dot_nt(q[kh], k[kh]) for kh in kheads]
    kk = [_dot_nt(k[kh], k[kh]) for kh in kheads]
    vsl = [slice(hl * GD_DV, (hl + 1) * GD_DV) for hl in heads]
    s = [s_ref[0, hl] for hl in heads]
    beta, gc, decay = [], [], []
    for hl in heads:
        head = hg * hpg + hl
        beta_h = jnp.sum(jnp.where(lane == head, beta_all, 0.0), axis=1, keepdims=True)
        g = jnp.sum(jnp.where(lane == head + GD_V_HEADS, g_all, 0.0), axis=1, keepdims=True)
        beta_h = jnp.where(valid, beta_h, 0.0)
        g = jnp.where(valid, g, 0.0)
        g_row = _col_to_row(g, eye)
        gc_h = jnp.sum(jnp.where(incl, g_row, 0.0), axis=1, keepdims=True)
        gc_row = jnp.sum(jnp.where(rows <= cols, g, 0.0), axis=0, keepdims=True)
        beta.append(beta_h)
        gc.append(gc_h)
        decay.append(jnp.where(incl, jnp.exp(jnp.where(incl, gc_h - gc_row, 0.0)), 0.0))
    qs = [_dot(q[hl // rep] * jnp.exp(gc[hl]), s[hl]) for hl in heads]
    t_inv = _neumann_inverse_many(
        [-jnp.where(strict, kk[hl // rep] * beta[hl] * decay[hl], 0.0) for hl in heads], size)
    uw = [_dot(t_inv[hl], jnp.concatenate([v_ref[:, vsl[hl]] * beta[hl],
                                           k[hl // rep] * (beta[hl] * jnp.exp(gc[hl]))], axis=1))
          for hl in heads]
    ws = [_dot(uw[hl][:, GD_DV:], s[hl]) for hl in heads]
    v_new = [uw[hl][:, :GD_DV] - ws[hl] for hl in heads]
    av = [_dot(qk[hl // rep] * decay[hl], v_new[hl]) for hl in heads]
    g_last = [gc[hl][size - 1:size, :] for hl in heads]
    kv = [_dot_tn(k[hl // rep] * jnp.exp(g_last[hl] - gc[hl]), v_new[hl]) for hl in heads]
    for hl in heads:
        s_ref[0, hl] = s[hl] * jnp.exp(g_last[hl]) + kv[hl]
        o = qs[hl] + av[hl]
        o = o * lax.rsqrt(jnp.mean(o * o, axis=-1, keepdims=True) + GD_EPS) * nw_ref[...]
        z = z_ref[:, vsl[hl]]
        o_ref[:, vsl[hl]] = (o * (z * _sigmoid(z))).astype(o_ref.dtype)


def gdn_chunk(qkv, proj, gates, dst, row0, bsz, t_pad, t_valid, chunk, s0, a_log, dt_bias, norm_w, hpg):
    nc = t_pad // chunk
    hgs = GD_V_HEADS // hpg
    rep = GD_V_HEADS // GD_K_HEADS
    kw = (hpg // rep) * GD_DK
    vw = hpg * GD_DV
    blk0 = row0 // chunk
    pad_row = lambda x: jnp.pad(x, (GD_V_HEADS, LANES - 2 * GD_V_HEADS)).reshape(1, LANES)
    st = pl.BlockSpec((1, hpg, GD_DK, GD_DV), lambda b, g, c: (b, g, 0, 0))
    vec = pl.BlockSpec((1, LANES), lambda b, g, c: (0, 0))
    return pl.pallas_call(
        functools.partial(_gdn_chunk_kernel, chunk=chunk, t_valid=t_valid, hpg=hpg),
        grid=(bsz, hgs, nc),
        in_specs=[pl.BlockSpec((chunk, kw), lambda b, g, c: (b * nc + c, g)),
                  pl.BlockSpec((chunk, kw), lambda b, g, c: (b * nc + c, GD_QK // kw + g)),
                  pl.BlockSpec((chunk, vw), lambda b, g, c: (b * nc + c, 2 * GD_QK // vw + g)),
                  pl.BlockSpec((chunk, vw), lambda b, g, c: (blk0 + b * nc + c, GD_CONV_DIM // vw + g)),
                  pl.BlockSpec((chunk, LANES), lambda b, g, c: (blk0 + b * nc + c, 0)),
                  vec, vec, vec, st, pl.BlockSpec(memory_space=pl.ANY)],
        out_specs=[pl.BlockSpec((chunk, vw), lambda b, g, c: (blk0 + b * nc + c, g)), st],
        out_shape=[jax.ShapeDtypeStruct(dst.shape, dst.dtype),
                   jax.ShapeDtypeStruct((bsz, GD_V_HEADS, GD_DK, GD_DV), F32)],
        input_output_aliases={9: 0},
        compiler_params=_cparams(("arbitrary", "arbitrary", "arbitrary")),
        name="gdn_chunk",
    )(qkv, qkv, qkv, proj, gates, pad_row(a_log), pad_row(dt_bias), norm_w.reshape(1, GD_DV), s0, dst)


MOE_TM = 256


def _router_kernel(x_ref, w_ref, wt_ref, id_ref):
    logits = jnp.dot(x_ref[...], w_ref[...].astype(BF16), preferred_element_type=F32)
    lane = _iota2(logits.shape, 1).astype(F32)
    first_of = lambda hit: jnp.min(jnp.where(hit, lane, float(LANES)), axis=1, keepdims=True)
    gl = jnp.where(lane < MOE_GROUPS, logits, NEG)
    gmax = jnp.max(gl, axis=1, keepdims=True)
    g_val = 1.0 / jnp.sum(jnp.exp(gl - gmax), axis=1, keepdims=True)
    lo = MOE_GROUPS + first_of(gl == gmax) * MOE_PER_GROUP
    vals = jnp.where((lane >= lo) & (lane < lo + MOE_PER_GROUP), logits, NEG)
    top1 = jnp.max(vals, axis=1, keepdims=True)
    i1 = first_of(vals == top1)
    vals2 = jnp.where(lane == i1, NEG, vals)
    top2 = jnp.max(vals2, axis=1, keepdims=True)
    i2 = first_of(vals2 == top2)
    e2 = jnp.exp(top2 - top1)
    w1 = (1.0 / (1.0 + e2)) * g_val
    w2 = (e2 / (1.0 + e2)) * g_val
    wt_ref[...] = jnp.where(lane == 0, w1, jnp.where(lane == 1, w2, 0.0))
    ids = jnp.where(lane == 0, i1 - MOE_GROUPS, jnp.where(lane == 1, i2 - MOE_GROUPS, 0.0))
    id_ref[...] = ids.astype(jnp.int32)


def moe_router(xb, w_router, tm=512):
    n, d = xb.shape
    out = pl.BlockSpec((tm, LANES), lambda i: (i, 0))
    return pl.pallas_call(
        _router_kernel,
        grid=(n // tm,),
        in_specs=[pl.BlockSpec((tm, d), lambda i: (i, 0)), pl.BlockSpec((d, LANES), lambda i: (0, 0))],
        out_specs=[out, out],
        out_shape=[jax.ShapeDtypeStruct((n, LANES), F32), jax.ShapeDtypeStruct((n, LANES), jnp.int32)],
        compiler_params=_cparams(("arbitrary",)),
        name="moe_router",
    )(xb, w_router)


def _moe_ffn_kernel(te_ref, tv_ref, idx_ref, idx_next_ref, x_hbm, wg_ref, wu_ref, wd_ref, o_ref,
                    xbuf, wgb, wub, wdb, sem):
    i = pl.program_id(0)
    n_tiles = pl.num_programs(0)
    slot = i % 2

    def gather(rows_ref, into):
        for r in range(MOE_TM):
            pltpu.make_async_copy(x_hbm.at[pl.ds(rows_ref[0, 0, r], 1)],
                                  xbuf.at[into, pl.ds(r, 1)], sem.at[into]).start()

    @pl.when((i == 0) & (tv_ref[0] == 1))
    def _():
        gather(idx_ref, 0)

    nxt = jnp.minimum(i + 1, n_tiles - 1)

    @pl.when((i + 1 < n_tiles) & (tv_ref[nxt] == 1))
    def _():
        gather(idx_next_ref, 1 - slot)

    @pl.when(tv_ref[i] == 1)
    def _():
        pltpu.make_async_copy(xbuf.at[slot], xbuf.at[slot], sem.at[slot]).wait()

        @pl.when((i == 0) | (te_ref[i] != te_ref[jnp.maximum(i - 1, 0)]))
        def _():
            wgb[...] = wg_ref[0, 0].astype(BF16)
            wub[...] = wu_ref[0, 0].astype(BF16)
            wdb[...] = wd_ref[0, 0].astype(BF16)

        x = xbuf[slot].astype(BF16)
        gate = jnp.dot(x, wgb[...], preferred_element_type=F32)
        up = jnp.dot(x, wub[...], preferred_element_type=F32)
        hid = gate * _sigmoid(gate) * up
        o_ref[...] = jnp.dot(hid.astype(BF16), wdb[...], preferred_element_type=F32)

    @pl.when(tv_ref[i] == 0)
    def _():
        o_ref[...] = jnp.zeros_like(o_ref)


def moe_ffn(x, row_token, tile_expert, tile_valid, layer, w_gate, w_up, w_down):
    n_tiles = row_token.shape[0]
    d = x.shape[1]
    ff = w_gate.shape[3]
    grid_spec = pltpu.PrefetchScalarGridSpec(
        num_scalar_prefetch=2,
        grid=(n_tiles,),
        in_specs=[pl.BlockSpec((1, 1, MOE_TM), lambda i, te, tv: (i, 0, 0), memory_space=pltpu.SMEM),
                  pl.BlockSpec((1, 1, MOE_TM), lambda i, te, tv: (jnp.minimum(i + 1, n_tiles - 1), 0, 0),
                               memory_space=pltpu.SMEM),
                  pl.BlockSpec(memory_space=pl.ANY),
                  pl.BlockSpec((1, 1, d, ff), lambda i, te, tv: (layer, te[i], 0, 0)),
                  pl.BlockSpec((1, 1, d, ff), lambda i, te, tv: (layer, te[i], 0, 0)),
                  pl.BlockSpec((1, 1, ff, d), lambda i, te, tv: (layer, te[i], 0, 0))],
        out_specs=pl.BlockSpec((MOE_TM, d), lambda i, te, tv: (i, 0)),
        scratch_shapes=[pltpu.VMEM((2, MOE_TM, d), F32),
                        pltpu.VMEM((d, ff), BF16), pltpu.VMEM((d, ff), BF16), pltpu.VMEM((ff, d), BF16),
                        pltpu.SemaphoreType.DMA((2,))],
    )
    return pl.pallas_call(
        _moe_ffn_kernel,
        grid_spec=grid_spec,
        out_shape=jax.ShapeDtypeStruct((n_tiles * MOE_TM, d), F32),
        compiler_params=_cparams(("arbitrary",)),
        name="moe_ffn",
    )(tile_expert, tile_valid, row_token, row_token, x, w_gate, w_up, w_down)


def _ln_moe_kernel(x_ref, ya_ref, yb_ref, wt_ref, g_ref, b_ref, o_ref, ob_ref):
    wt = wt_ref[...]
    z = DEEPNORM_ALPHA * x_ref[...] + (wt[:, 0:1] * ya_ref[...] + wt[:, 1:2] * yb_ref[...])
    zc = z - jnp.mean(z, axis=-1, keepdims=True)
    var = jnp.mean(zc * zc, axis=-1, keepdims=True)
    out = zc * lax.rsqrt(var + LN_EPS) * g_ref[...] + b_ref[...]
    o_ref[...] = out
    ob_ref[...] = out.astype(BF16)


def ln_residual_moe(x, ya, yb, wt, g, b, tm=256):
    m, d = x.shape
    row = pl.BlockSpec((tm, d), lambda i: (i, 0))
    vec = pl.BlockSpec((1, d), lambda i: (0, 0))
    return pl.pallas_call(
        _ln_moe_kernel,
        grid=(m // tm,),
        in_specs=[row, row, row, pl.BlockSpec((tm, LANES), lambda i: (i, 0)), vec, vec],
        out_specs=[row, row],
        out_shape=[jax.ShapeDtypeStruct((m, d), F32), jax.ShapeDtypeStruct((m, d), BF16)],
        compiler_params=_cparams(("arbitrary",)),
        name="ln_residual_moe",
    )(x, ya, yb, wt, g.reshape(1, d), b.reshape(1, d))


def _table_lookup(table, idx):
    hit = idx[:, None] == jnp.arange(table.shape[0], dtype=idx.dtype)[None, :]
    return jnp.sum(jnp.where(hit, table[None, :], 0), axis=1)


def hier_moe(x, xb, layer, w_group, w_expert, w_gate, w_up, w_down):
    n = x.shape[0]
    w_router = jnp.pad(jnp.concatenate([w_group, w_expert], axis=1),
                       ((0, 0), (0, LANES - MOE_GROUPS - MOE_EXPERTS)))
    wt, ids = moe_router(xb, w_router)

    n_pairs = n * MOE_TOPK
    n_tiles = -(-n_pairs // MOE_TM) + MOE_EXPERTS
    n_rows = n_tiles * MOE_TM
    i32 = jnp.int32
    eid = ids[:, :MOE_TOPK].reshape(n_pairs)
    order = jnp.argsort(eid, stable=True).astype(i32)
    rank = jnp.argsort(order).astype(i32)
    experts = jnp.arange(MOE_EXPERTS, dtype=i32)
    counts = jnp.sum((eid[:, None] == experts[None, :]).astype(i32), axis=0)
    starts = jnp.cumsum(counts) - counts
    tiles_per = (counts + MOE_TM - 1) // MOE_TM
    tile_starts = jnp.cumsum(tiles_per) - tiles_per
    dest = _table_lookup(tile_starts * MOE_TM - starts, eid) + rank
    tile_idx = jnp.arange(n_tiles, dtype=i32)
    tile_expert = jnp.sum((tile_idx[:, None] >= (tile_starts + tiles_per)[None, :]).astype(i32), axis=1)
    tile_valid = (tile_expert < MOE_EXPERTS).astype(i32)
    last_used = jnp.max(jnp.where(tiles_per > 0, experts, 0))
    tile_expert = jnp.where(tile_valid == 1, tile_expert, last_used)
    off = (tile_idx - _table_lookup(tile_starts, tile_expert))[:, None] * MOE_TM + jnp.arange(MOE_TM, dtype=i32)[None, :]
    cnt = _table_lookup(counts, tile_expert)[:, None]
    src = jnp.clip(_table_lookup(starts, tile_expert)[:, None] + off, 0, n_pairs - 1)
    row_ok = (off < cnt) & (tile_valid[:, None] == 1)
    picked = jnp.take(order, src.reshape(n_rows), mode='clip').reshape(n_tiles, MOE_TM)
    row_token = jnp.where(row_ok, picked // MOE_TOPK, 0).reshape(n_tiles, 1, MOE_TM)

    y_sorted = moe_ffn(x, row_token, tile_expert, tile_valid, layer, w_gate, w_up, w_down)
    dest2 = dest.reshape(n, MOE_TOPK)
    return (jnp.take(y_sorted, dest2[:, 0], axis=0, mode='clip'),
            jnp.take(y_sorted, dest2[:, 1], axis=0, mode='clip'), wt)


SAMPLE_T_PAD = 8


def kernel(x_prompt, x_sample, state_mlstm_c, state_mlstm_n, state_mlstm_m, state_rwkv_s,
           state_rwkv_shift, state_gdn_s, state_gdn_conv, p_prompt, p_sample,
           w_in_ab, ml_b_i, ml_b_f, ml_norm_w, rw_mu, rw_w0, rw_w2, rw_a0, rw_a2, rw_g2,
           rw_k_k, rw_k_a, rw_r_k, rw_ln_w, rw_ln_b, w_out_ab,
           gd_w_in, gd_conv_w, gd_a_log, gd_dt_bias, gd_norm_w, gd_w_out,
           ln_mix_g, ln_mix_b, moe_w_group, moe_w_expert, moe_w_gate, moe_w_up, moe_w_down,
           ln_ffn_g, ln_ffn_b, ple_w_gate, ple_w_proj):
    pb, pt, d = x_prompt.shape
    sb, st, _ = x_sample.shape
    n_p = pb * pt
    n_s = sb * SAMPLE_T_PAD
    n = n_p + n_s
    pad_t = lambda a: jnp.pad(a, ((0, 0), (0, SAMPLE_T_PAD - st), (0, 0)))
    merge = lambda a_p, a_s: jnp.concatenate(
        [a_p.reshape(n_p, a_p.shape[-1]), pad_t(a_s).reshape(n_s, a_s.shape[-1])], axis=0)
    x = merge(x_prompt, x_sample)
    xb = x.astype(BF16)
    segs = (dict(row0=0, bsz=pb, t_pad=pt, t_valid=pt, rw_hb=16, gd_hpg=16),
            dict(row0=n_p, bsz=sb, t_pad=SAMPLE_T_PAD, t_valid=st, rw_hb=16, gd_hpg=16))
    zeros = lambda *shape: jnp.zeros(shape, F32)

    def last_rows(a, sg, count):
        assert sg['t_valid'] >= count
        idx = (sg['row0'] + jnp.arange(sg['bsz'])[:, None] * sg['t_pad']
               + (sg['t_valid'] - count + jnp.arange(count))[None, :])
        return jnp.take(a, idx.reshape(-1), axis=0, mode='clip').reshape(sg['bsz'], count, a.shape[-1])

    new_states = [[[] for _ in range(7)] for _ in segs]
    for layer in range(DEPTH):
        li = layer // 2
        if layer % 2 == 0:
            w_in = w_in_ab[li]
            proj_ml = matmul(xb, w_in, 3200, 640)
            proj_rw = matmul(xb, w_in[:, ML_COLS:].astype(BF16), RW_COLS, RW_COLS // 2)
            rw_prm = dict(mu=rw_mu[li], w0=rw_w0[li], w2=rw_w2[li], a0=rw_a0[li], a2=rw_a2[li], g2=rw_g2[li],
                          k_k=rw_k_k[li], k_a=rw_k_a[li], r_k=rw_r_k[li])
            heads_out = jnp.zeros((n, ML_V + RW_WIDTH), BF16)
            for si, sg in enumerate(segs):
                bsz, t_pad, t_valid, row0 = sg['bsz'], sg['t_pad'], sg['t_valid'], sg['row0']
                if si == 0:
                    c0, n0, m0 = zeros(bsz, ML_HEADS, ML_DK, ML_DV), zeros(bsz, ML_HEADS, ML_DK), zeros(bsz, ML_HEADS)
                    s0, sh0 = zeros(bsz, RW_HEADS, RW_DH, RW_DH), zeros(bsz, RW_COLS)
                else:
                    c0, n0, m0 = state_mlstm_c[li], state_mlstm_n[li], state_mlstm_m[li]
                    s0, sh0 = state_rwkv_s[li], state_rwkv_shift[li]
                heads_out, c, nn, m = mlstm_group(proj_ml, heads_out, row0, bsz, t_pad, t_valid,
                                                  min(ML_CHUNK, t_pad), c0, n0, m0,
                                                  ml_b_i[li], ml_b_f[li], ml_norm_w[li])
                prep = rwkv_prep(proj_rw, row0, bsz, t_pad, sh0, rw_prm, 256)
                heads_out, rs = rwkv_chunk(prep, heads_out, ML_V, row0, bsz, t_pad, t_valid,
                                           min(RW_CHUNK, t_pad), s0, rw_ln_w[li], rw_ln_b[li], sg['rw_hb'])
                new_shift = last_rows(proj_rw, sg, 1)[:, 0]
                for slot, val in zip(range(5), (c, nn, m, rs, new_shift)):
                    new_states[si][slot].append(val)
            mix = matmul(heads_out, w_out_ab[li], d, 512)
        else:
            w_in = gd_w_in[li]
            n_qkvz = GD_CONV_DIM + GD_V_WIDTH
            proj = matmul(xb, w_in, n_qkvz, 1024)
            w_gates = jnp.pad(w_in[:, n_qkvz:], ((0, 0), (0, LANES - 2 * GD_V_HEADS)))
            gates = matmul(xb, w_gates, LANES, LANES)
            gd_out = jnp.zeros((n, GD_V_WIDTH), BF16)
            for si, sg in enumerate(segs):
                bsz, t_pad, t_valid, row0 = sg['bsz'], sg['t_pad'], sg['t_valid'], sg['row0']
                if si == 0:
                    s0, buf0 = zeros(bsz, GD_V_HEADS, GD_DK, GD_DV), zeros(bsz, GD_CONV - 1, GD_CONV_DIM)
                else:
                    s0, buf0 = state_gdn_s[li], state_gdn_conv[li]
                qkv = gdn_prep(proj, row0, bsz, t_pad, buf0, gd_conv_w[li], 256)
                gd_out, gs = gdn_chunk(qkv, proj, gates, gd_out, row0, bsz, t_pad, t_valid,
                                       min(GD_CHUNK, t_pad), s0, gd_a_log[li], gd_dt_bias[li],
                                       gd_norm_w[li], sg['gd_hpg'])
                new_states[si][5].append(gs)
                new_states[si][6].append(last_rows(proj, sg, GD_CONV - 1)[:, :, :GD_CONV_DIM])
            mix = matmul(gd_out, gd_w_out[li], d, 512)
        x, xb = ln_residual(x, mix, ln_mix_g[layer], ln_mix_b[layer])
        ya, yb, wt = hier_moe(x, xb, layer, moe_w_group[layer], moe_w_expert[layer],
                              moe_w_gate, moe_w_up, moe_w_down)
        x, xb = ln_residual_moe(x, ya, yb, wt, ln_ffn_g[layer], ln_ffn_b[layer])
        x, xb = ple(xb, merge(p_prompt[layer], p_sample[layer]), x, layer, ple_w_gate, ple_w_proj)
    y_prompt = x[:n_p].reshape(pb, pt, d)
    y_sample = x[n_p:].reshape(sb, SAMPLE_T_PAD, d)[:, :st]
    stack = lambda vals: jnp.stack(vals)
    return (y_prompt, y_sample) + tuple(stack(v) for v in new_states[0]) + tuple(stack(v) for v in new_states[1])
```

```python
import functools

import jax
import jax.numpy as jnp
import numpy as np
from jax import lax
from jax.experimental import pallas as pl
from jax.experimental.pallas import tpu as pltpu

F32 = jnp.float32
BF16 = jnp.bfloat16

D_MODEL = 2048
DEPTH = 2
DEEPNORM_ALPHA = (2 * DEPTH) ** 0.25
LN_EPS = 1e-5
MIX_HALF = D_MODEL // 2
ML_HEADS = 4
ML_DV = MIX_HALF // ML_HEADS
ML_DK = ML_DV // 2
ML_CHUNK = 64
ML_GATE_CAP = 15.0
ML_NORM_EPS = 1e-6
ML_QK = ML_HEADS * ML_DK
ML_V = ML_HEADS * ML_DV
ML_COLS = 2 * ML_QK + 2 * ML_V + 2 * ML_HEADS
RW_DH = 64
RW_HEADS = MIX_HALF // RW_DH
RW_WIDTH = RW_HEADS * RW_DH
RW_DECAY_LORA = 64
RW_A_LORA = 64
RW_GATE_LORA = 128
RW_GN_EPS = 64e-5
RW_COLS = 3 * RW_WIDTH + RW_DECAY_LORA + RW_A_LORA + RW_GATE_LORA
RW_CHUNK = 64
GD_DK = 128
GD_DV = 128
GD_K_HEADS = D_MODEL // GD_DK
GD_V_HEADS = 2 * GD_K_HEADS
GD_CONV = 4
GD_CHUNK = 64
GD_EPS = 1e-6
GD_QK = GD_K_HEADS * GD_DK
GD_V_WIDTH = GD_V_HEADS * GD_DV
GD_CONV_DIM = 2 * GD_QK + GD_V_WIDTH
MOE_GROUPS = 4
MOE_PER_GROUP = 8
MOE_EXPERTS = MOE_GROUPS * MOE_PER_GROUP
MOE_TOPK = 2
MOE_FF = D_MODEL // 4
PLE_DIM = 256

LANES = 128
SUBLANES = 8
NEG = -1e30
VMEM_LIMIT = 56 * 1024 * 1024


def _cparams(sem):
    return pltpu.CompilerParams(dimension_semantics=sem, vmem_limit_bytes=VMEM_LIMIT)


def _dot(a, b):
    return jnp.dot(a.astype(BF16), b.astype(BF16), preferred_element_type=F32)


def _dot_nt(a, b):
    return lax.dot_general(a.astype(BF16), b.astype(BF16), (((1,), (1,)), ((), ())),
                           preferred_element_type=F32)


def _dot_tn(a, b):
    return lax.dot_general(a.astype(BF16), b.astype(BF16), (((0,), (0,)), ((), ())),
                           preferred_element_type=F32)


def _split3(x):
    hi = x.astype(BF16)
    r1 = x - hi.astype(F32)
    mid = r1.astype(BF16)
    lo = (r1 - mid.astype(F32)).astype(BF16)
    return hi, mid, lo


def _dot_exact_rhs(a, b01):
    hi, mid, lo = _split3(a)
    b = b01.astype(BF16)
    return (jnp.dot(hi, b, preferred_element_type=F32) + jnp.dot(mid, b, preferred_element_type=F32)
            + jnp.dot(lo, b, preferred_element_type=F32))


def _dot_exact_lhs(a01, b):
    hi, mid, lo = _split3(b)
    a = a01.astype(BF16)
    return (jnp.dot(a, hi, preferred_element_type=F32) + jnp.dot(a, mid, preferred_element_type=F32)
            + jnp.dot(a, lo, preferred_element_type=F32))


def _iota2(shape, dim):
    return lax.broadcasted_iota(jnp.int32, shape, dim)


def _col_to_row(col, eye):
    return jnp.sum(jnp.where(eye, col, 0.0), axis=0, keepdims=True)


def _sigmoid(x):
    return 1.0 / (1.0 + jnp.exp(-x))


def _softplus(x):
    return jnp.maximum(x, 0.0) + jnp.log1p(jnp.exp(-jnp.abs(x)))


def _neumann_inverse(n_mat, size):
    eye = (_iota2((size, size), 0) == _iota2((size, size), 1)).astype(F32)
    t = eye + n_mat
    x = n_mat
    steps = max(int(np.ceil(np.log2(size))) - 1, 0)
    for _ in range(steps):
        x = _dot(x, x)
        t = t + _dot(t, x)
    return t


def _mm_kernel(x_ref, w_ref, o_ref, wb_ref, *, n_valid):
    @pl.when(pl.program_id(1) == 0)
    def _():
        wb_ref[...] = w_ref[...].astype(BF16)

    acc = jnp.dot(x_ref[...].astype(BF16), wb_ref[...], preferred_element_type=F32)
    if n_valid is not None:
        acc = jnp.where(_iota2(acc.shape, 1) < n_valid, acc, 0.0)
    o_ref[...] = acc.astype(o_ref.dtype)


def _row_tile(m, want):
    while m % want:
        want //= 2
    return want


def matmul(x, w, n_out, tn, tm=1024, out_dtype=F32, col0=0):
    m, k = x.shape
    tm = _row_tile(m, tm)
    assert w.shape[0] == k and n_out % tn == 0 and m % tm == 0 and tn % LANES == 0 and col0 % tn == 0
    n_valid = None
    if col0 + n_out > w.shape[1]:
        assert n_out == tn
        n_valid = w.shape[1] - col0
    blk0 = col0 // tn
    return pl.pallas_call(
        functools.partial(_mm_kernel, n_valid=n_valid),
        grid=(n_out // tn, m // tm),
        in_specs=[pl.BlockSpec((tm, k), lambda j, i: (i, 0)),
                  pl.BlockSpec((k, tn), lambda j, i: (0, blk0 + j))],
        out_specs=pl.BlockSpec((tm, tn), lambda j, i: (i, j)),
        out_shape=jax.ShapeDtypeStruct((m, n_out), out_dtype),
        scratch_shapes=[pltpu.VMEM((k, tn), BF16)],
        compiler_params=_cparams(("arbitrary", "arbitrary")),
        name="matmul",
    )(x, w)


def _ln_res_kernel(x_ref, y_ref, g_ref, b_ref, o_ref, ob_ref, orow_ref):
    z = DEEPNORM_ALPHA * x_ref[...] + y_ref[...]
    zc = z - jnp.mean(z, axis=-1, keepdims=True)
    var = jnp.mean(zc * zc, axis=-1, keepdims=True)
    out = zc * lax.rsqrt(var + LN_EPS) * g_ref[...] + b_ref[...]
    o_ref[...] = out
    ob_ref[...] = out.astype(BF16)
    tm, d = out.shape
    pieces = d // LANES
    for j in range(pieces):
        orow_ref[pl.ds(j, tm, stride=pieces), :] = out[:, j * LANES:(j + 1) * LANES]


def ln_residual(x, y, g, b, tm=256):
    m, d = x.shape
    pieces = d // LANES
    row = pl.BlockSpec((tm, d), lambda i: (i, 0))
    vec = pl.BlockSpec((1, d), lambda i: (0, 0))
    return pl.pallas_call(
        _ln_res_kernel,
        grid=(m // tm,),
        in_specs=[row, row, vec, vec],
        out_specs=[row, row, pl.BlockSpec((tm * pieces, LANES), lambda i: (i, 0))],
        out_shape=[jax.ShapeDtypeStruct((m, d), F32), jax.ShapeDtypeStruct((m, d), BF16),
                   jax.ShapeDtypeStruct((m * pieces, LANES), F32)],
        compiler_params=_cparams(("arbitrary",)),
        name="ln_residual",
    )(x, y, g.reshape(1, d), b.reshape(1, d))


def _ple_kernel(xb_ref, p_ref, x_ref, wg_ref, wp_ref, o_ref, ob_ref, wgb_ref, wpb_ref):
    @pl.when(pl.program_id(1) == 0)
    def _():
        wgb_ref[...] = wg_ref[0].astype(BF16)
        wpb_ref[...] = wp_ref[0].astype(BF16)

    gate = _sigmoid(jnp.dot(xb_ref[...], wgb_ref[...], preferred_element_type=F32))
    emb = jnp.dot(p_ref[...].astype(BF16), wpb_ref[...], preferred_element_type=F32)
    out = x_ref[...] + gate * emb
    o_ref[...] = out
    ob_ref[...] = out.astype(BF16)


def ple(xb, p, x, layer, wg, wp, tm=1024, tn=512):
    m, d = x.shape
    pd = p.shape[1]
    tm = _row_tile(m, tm)
    return pl.pallas_call(
        _ple_kernel,
        grid=(d // tn, m // tm),
        in_specs=[pl.BlockSpec((tm, d), lambda j, i: (i, 0)),
                  pl.BlockSpec((tm, pd), lambda j, i: (i, 0)),
                  pl.BlockSpec((tm, tn), lambda j, i: (i, j)),
                  pl.BlockSpec((1, d, tn), lambda j, i: (layer, 0, j)),
                  pl.BlockSpec((1, pd, tn), lambda j, i: (layer, 0, j))],
        out_specs=[pl.BlockSpec((tm, tn), lambda j, i: (i, j))] * 2,
        out_shape=[jax.ShapeDtypeStruct((m, d), F32), jax.ShapeDtypeStruct((m, d), BF16)],
        scratch_shapes=[pltpu.VMEM((d, tn), BF16), pltpu.VMEM((pd, tn), BF16)],
        compiler_params=_cparams(("arbitrary", "arbitrary")),
        name="ple",
    )(xb, p, x, wg, wp)


def _mlstm_kernel(q_ref, k_ref, v_ref, o_ref, g_ref, c0_ref, n0_ref, m0_ref, nw_ref, gb_ref, _dst_ref,
                  h_ref, c_ref, n_ref, m_ref, *, chunk, t_valid):
    ci = pl.program_id(1)

    @pl.when(ci == 0)
    def _():
        c_ref[...] = c0_ref[...]
        n_ref[...] = n0_ref[...]
        m_ref[...] = m0_ref[...]

    size = chunk
    rows = _iota2((size, size), 0)
    cols = _iota2((size, size), 1)
    eye = rows == cols
    causal = cols <= rows
    gates = g_ref[:, 0:2 * ML_HEADS] + gb_ref[...]
    capped = ML_GATE_CAP * jnp.tanh(gates / ML_GATE_CAP)
    t_idx = ci * size + _iota2((size, 1), 0)
    valid = t_idx < t_valid
    ipre_all = jnp.where(valid, capped[:, 0:ML_HEADS], NEG)
    logf_all = jnp.where(valid, -_softplus(-capped[:, ML_HEADS:2 * ML_HEADS]), 0.0)
    heads = range(ML_HEADS)
    q = [q_ref[:, h * ML_DK:(h + 1) * ML_DK] for h in heads]
    k = [k_ref[:, h * ML_DK:(h + 1) * ML_DK] * (ML_DK ** -0.5) for h in heads]
    v = [v_ref[:, h * ML_DV:(h + 1) * ML_DV] for h in heads]
    qk = [_dot_nt(q[h], k[h]) for h in heads]
    c_prev = [c_ref[0, h] for h in heads]
    n_prev = [n_ref[0, h:h + 1, :] for h in heads]
    qc = [_dot(q[h], c_prev[h]) for h in heads]
    s, w_inter, m_t, kw, decay = [], [], [], [], []
    for h in heads:
        ig_col = ipre_all[:, h:h + 1]
        lf_col = logf_all[:, h:h + 1]
        ig_row = _col_to_row(ig_col, eye)
        lf_row = _col_to_row(lf_col, eye)
        b_col = jnp.sum(jnp.where(causal, lf_row, 0.0), axis=1, keepdims=True)
        b_row = jnp.sum(jnp.where(rows <= cols, lf_col, 0.0), axis=0, keepdims=True)
        d = jnp.where(causal, b_col - b_row + ig_row, NEG)
        inter = b_col + m_ref[0, :, h:h + 1]
        m_h = jnp.maximum(inter, jnp.max(d, axis=1, keepdims=True))
        w_h = jnp.exp(inter - m_h)
        s.append(qk[h] * jnp.exp(d - m_h))
        b_last = b_col[size - 1:size, :]
        m_last = m_h[size - 1:size, :]
        w_last = jnp.exp(b_last - b_col + ig_col - m_last)
        kw.append(k[h] * w_last)
        w_inter.append(w_h)
        m_t.append(m_h)
        decay.append(w_h[size - 1:size, :])
        m_ref[0, :, h:h + 1] = m_last
    sv = [_dot(s[h], v[h]) for h in heads]
    kv = [_dot_tn(kw[h], v[h]) for h in heads]
    for h in heads:
        num = w_inter[h] * qc[h] + sv[h]
        den = (w_inter[h] * jnp.sum(q[h] * n_prev[h], axis=1, keepdims=True)
               + jnp.sum(s[h], axis=1, keepdims=True))
        hid = num / jnp.maximum(jnp.abs(den), jnp.exp(-m_t[h]))
        c_ref[0, h] = decay[h] * c_prev[h] + kv[h]
        n_ref[0, h:h + 1, :] = decay[h] * n_prev[h] + jnp.sum(kw[h], axis=0, keepdims=True)
        hid = hid * lax.rsqrt(jnp.mean(hid * hid, axis=-1, keepdims=True) + ML_NORM_EPS)
        hid = hid * nw_ref[:, h * ML_DV:(h + 1) * ML_DV]
        hid = hid * _sigmoid(o_ref[:, h * ML_DV:(h + 1) * ML_DV])
        h_ref[:, h * ML_DV:(h + 1) * ML_DV] = hid.astype(h_ref.dtype)


def mlstm_group(proj, dst, row0, bsz, t_pad, t_valid, chunk, c0, n0, m0, b_i, b_f, norm_w):
    nc = t_pad // chunk
    blk0 = row0 // chunk
    assert row0 % chunk == 0 and t_pad % chunk == 0
    rmap = lambda c0_: (lambda b, c: (blk0 + b * nc + c, c0_))
    qk_w, v_w = ML_QK, ML_V
    in_specs = [
        pl.BlockSpec((chunk, qk_w), rmap(0)),
        pl.BlockSpec((chunk, qk_w), rmap(1)),
        pl.BlockSpec((chunk, v_w), rmap(1)),
        pl.BlockSpec((chunk, v_w), rmap(2)),
        pl.BlockSpec((chunk, LANES), rmap((2 * qk_w + 2 * v_w) // LANES)),
        pl.BlockSpec((1, ML_HEADS, ML_DK, ML_DV), lambda b, c: (b, 0, 0, 0)),
        pl.BlockSpec((1, ML_HEADS, ML_DK), lambda b, c: (b, 0, 0)),
        pl.BlockSpec((1, 1, ML_HEADS), lambda b, c: (b, 0, 0)),
        pl.BlockSpec((1, v_w), lambda b, c: (0, 0)),
        pl.BlockSpec((1, 2 * ML_HEADS), lambda b, c: (0, 0)),
        pl.BlockSpec(memory_space=pl.ANY),
    ]
    out_specs = [
        pl.BlockSpec((chunk, v_w), lambda b, c: (blk0 + b * nc + c, 0)),
        pl.BlockSpec((1, ML_HEADS, ML_DK, ML_DV), lambda b, c: (b, 0, 0, 0)),
        pl.BlockSpec((1, ML_HEADS, ML_DK), lambda b, c: (b, 0, 0)),
        pl.BlockSpec((1, 1, ML_HEADS), lambda b, c: (b, 0, 0)),
    ]
    out_shape = [
        jax.ShapeDtypeStruct(dst.shape, dst.dtype),
        jax.ShapeDtypeStruct((bsz, ML_HEADS, ML_DK, ML_DV), F32),
        jax.ShapeDtypeStruct((bsz, ML_HEADS, ML_DK), F32),
        jax.ShapeDtypeStruct((bsz, 1, ML_HEADS), F32),
    ]
    gate_bias = jnp.concatenate([b_i, b_f]).reshape(1, 2 * ML_HEADS)
    h, c, n, m = pl.pallas_call(
        functools.partial(_mlstm_kernel, chunk=chunk, t_valid=t_valid),
        grid=(bsz, nc),
        in_specs=in_specs,
        out_specs=out_specs,
        out_shape=out_shape,
        input_output_aliases={10: 0},
        compiler_params=_cparams(("arbitrary", "arbitrary")),
        name="mlstm",
    )(proj, proj, proj, proj, proj, c0, n0, m0.reshape(bsz, 1, ML_HEADS), norm_w.reshape(1, v_w), gate_bias, dst)
    return h, c, n, m.reshape(bsz, ML_HEADS)


def _head_block_ones(width, head):
    idx = np.arange(width) // head
    return jnp.asarray((idx[:, None] == idx[None, :]).astype(np.float32), dtype=BF16)


def _rwkv_prep_kernel(cur_ref, tail_ref, first_ref, mu_ref, w0_ref, a0_ref, kk_ref, ka_ref, rk_ref,
                      w2_ref, a2_ref, g2_ref, blk_ref,
                      r_ref, wl_ref, k_ref, v_ref, an_ref, bb_ref, bonus_ref, g_ref,
                      *, tt, t_pad, multi):
    cur = cur_ref[...]
    rolled = pltpu.roll(cur, 1, 0)
    row = _iota2((tt, 1), 0)
    if multi:
        prev = jnp.where((row & (t_pad - 1)) == 0, first_ref[...], rolled)
    else:
        starts_seq = (pl.program_id(0) % (t_pad // tt)) == 0
        head = jnp.where(starts_seq, first_ref[0], tail_ref[SUBLANES - 1:SUBLANES, :])
        prev = jnp.where(row == 0, head, rolled)
    mixed = cur + (prev - cur) * mu_ref[...]
    w = RW_WIDTH
    r = mixed[:, 0:w]
    k = mixed[:, w:2 * w]
    v = mixed[:, 2 * w:3 * w]
    c0 = 3 * w
    wl = mixed[:, c0:c0 + RW_DECAY_LORA]
    al = mixed[:, c0 + RW_DECAY_LORA:c0 + RW_DECAY_LORA + RW_A_LORA]
    gl = mixed[:, c0 + RW_DECAY_LORA + RW_A_LORA:]
    w_log = -jnp.exp(-_softplus(-(w0_ref[...] + _dot(jnp.tanh(wl), w2_ref[...]))) - 0.5)
    a = _sigmoid(a0_ref[...] + _dot(al, a2_ref[...]))
    g = _dot(_sigmoid(gl), g2_ref[...])
    kk = k * kk_ref[...]
    blk = blk_ref[...]
    kkn = kk / jnp.maximum(jnp.sqrt(_dot_exact_rhs(kk * kk, blk)), 1e-12)
    k2 = k * (1.0 + (a - 1.0) * ka_ref[...])
    r_ref[...] = r
    wl_ref[...] = w_log
    k_ref[...] = k2
    v_ref[...] = v
    an_ref[...] = -kkn
    bb_ref[...] = kkn * a
    bonus_ref[...] = _dot_exact_rhs(r * k2 * rk_ref[...], blk) * v
    g_ref[...] = g


def rwkv_prep(proj, row0, bsz, t_pad, shift0, prm, tt):
    n = bsz * t_pad
    multi = tt > t_pad
    assert row0 % tt == 0 and n % tt == 0 and (tt % t_pad == 0 if multi else t_pad % tt == 0)
    blk0 = row0 // tt
    c = RW_COLS
    if multi:
        first = jnp.zeros((bsz, t_pad, c), F32).at[:, 0].set(shift0).reshape(n, c)
        first_spec = pl.BlockSpec((tt, c), lambda i: (i, 0))
    else:
        first = shift0.reshape(bsz, 1, c)
        first_spec = pl.BlockSpec((1, 1, c), lambda i: (i // (t_pad // tt), 0, 0))
    tail_blk = tt // SUBLANES
    vec = lambda width: pl.BlockSpec((1, width), lambda i: (0, 0))
    full = lambda a, b: pl.BlockSpec((a, b), lambda i: (0, 0))
    w = RW_WIDTH
    out_spec = pl.BlockSpec((tt, w), lambda i: (i, 0))
    return pl.pallas_call(
        functools.partial(_rwkv_prep_kernel, tt=tt, t_pad=t_pad, multi=multi),
        grid=(n // tt,),
        in_specs=[pl.BlockSpec((tt, c), lambda i: (blk0 + i, 0)),
                  pl.BlockSpec((SUBLANES, c), lambda i: (jnp.maximum((blk0 + i) * tail_blk - 1, 0), 0)),
                  first_spec, vec(c), vec(w), vec(w), vec(w), vec(w), vec(w),
                  full(RW_DECAY_LORA, w), full(RW_A_LORA, w), full(RW_GATE_LORA, w), full(w, w)],
        out_specs=[out_spec] * 8,
        out_shape=[jax.ShapeDtypeStruct((n, w), F32)] * 8,
        compiler_params=_cparams(("arbitrary",)),
        name="rwkv_prep",
    )(proj, proj, first, prm['mu'].reshape(1, c), prm['w0'].reshape(1, w), prm['a0'].reshape(1, w),
      prm['k_k'].reshape(1, w), prm['k_a'].reshape(1, w), prm['r_k'].reshape(1, w),
      prm['w2'], prm['a2'], prm['g2'], _head_block_ones(w, RW_DH))


def _neumann_inverse_many(n_mats, size):
    eye = (_iota2((size, size), 0) == _iota2((size, size), 1)).astype(F32)
    ts = [eye + n_mat for n_mat in n_mats]
    xs = list(n_mats)
    steps = max(int(np.ceil(np.log2(size))) - 1, 0)
    for _ in range(steps):
        xs = [_dot(x, x) for x in xs]
        ts = [t + _dot(t, x) for t, x in zip(ts, xs)]
    return ts


def _rwkv_chunk_kernel(r_ref, w_ref, k_ref, v_ref, a_ref, b_ref, bonus_ref, g_ref, s0_ref,
                       lnw_ref, lnb_ref, _dst_ref, o_ref, s_ref, acc_ref, *, chunk, t_valid, hb):
    ci = pl.program_id(2)

    @pl.when(ci == 0)
    def _():
        s_ref[...] = s0_ref[...]

    size = chunk
    rows = _iota2((size, size), 0)
    cols = _iota2((size, size), 1)
    strict = cols < rows
    rows2 = _iota2((size, 2 * size), 0)
    cols2 = _iota2((size, 2 * size), 1)
    mask_ak = (cols2 >= size) & (cols2 - size < rows2)
    mask_o = jnp.where(cols2 >= size, cols2 - size, cols2) <= rows2
    valid = (ci * size + _iota2((size, 1), 0)) < t_valid
    w = jnp.where(valid, w_ref[...], 0.0)
    a = jnp.where(valid, a_ref[...], 0.0)
    b = jnp.where(valid, b_ref[...], 0.0)
    k = jnp.where(valid, k_ref[...], 0.0)
    v = v_ref[...]
    lam = _dot_exact_lhs((cols <= rows).astype(F32), w)
    e_pos = jnp.exp(lam)
    e_neg = jnp.exp(-lam)
    at = a * jnp.exp(lam - w)
    bt = b * e_neg
    kt = k * e_neg
    rt = r_ref[...] * e_pos
    heads = range(hb)
    sls = [slice(h * RW_DH, (h + 1) * RW_DH) for h in heads]
    at_h = [at[:, sl] for sl in sls]
    rt_h = [rt[:, sl] for sl in sls]
    v_h = [v[:, sl] for sl in sls]
    bk = [jnp.concatenate([bt[:, sl], kt[:, sl]], axis=0) for sl in sls]
    s0 = [s_ref[0, h] for h in heads]
    pa = [_dot_nt(at_h[h], bk[h]) for h in heads]
    pr = [_dot_nt(rt_h[h], bk[h]) for h in heads]
    as0 = [_dot_nt(at_h[h], s0[h]) for h in heads]
    rs0 = [_dot_nt(rt_h[h], s0[h]) for h in heads]
    zv = [jnp.concatenate([jnp.zeros_like(v_h[h]), v_h[h]], axis=0) for h in heads]
    rhs = [as0[h] + _dot(jnp.where(mask_ak, pa[h], 0.0), zv[h]) for h in heads]
    t_inv = _neumann_inverse_many([jnp.where(strict, pa[h][:, :size], 0.0) for h in heads], size)
    u = [_dot(t_inv[h], rhs[h]) for h in heads]
    uv = [jnp.concatenate([u[h], v_h[h]], axis=0) for h in heads]
    o = [rs0[h] + _dot(jnp.where(mask_o, pr[h], 0.0), uv[h]) for h in heads]
    ds = [_dot_tn(uv[h], bk[h]) for h in heads]
    for h in heads:
        s_ref[0, h] = (s0[h] + ds[h]) * e_pos[size - 1:size, sls[h]]
        oc = o[h] - jnp.mean(o[h], axis=-1, keepdims=True)
        acc_ref[:, sls[h]] = oc * lax.rsqrt(jnp.mean(oc * oc, axis=-1, keepdims=True) + RW_GN_EPS)
    out = (acc_ref[...] * lnw_ref[...] + lnb_ref[...] + bonus_ref[...]) * g_ref[...]
    o_ref[...] = out.astype(o_ref.dtype)


def rwkv_chunk(prep, dst, dst_col0, row0, bsz, t_pad, t_valid, chunk, s0, ln_w, ln_b, hb):
    nc = t_pad // chunk
    hg = RW_HEADS // hb
    wb = hb * RW_DH
    blk0 = row0 // chunk
    assert dst_col0 % wb == 0 and row0 % chunk == 0
    tok = pl.BlockSpec((chunk, wb), lambda b, g, c: (b * nc + c, g))
    st = pl.BlockSpec((1, hb, RW_DH, RW_DH), lambda b, g, c: (b, g, 0, 0))
    vec = pl.BlockSpec((1, wb), lambda b, g, c: (0, g))
    return pl.pallas_call(
        functools.partial(_rwkv_chunk_kernel, chunk=chunk, t_valid=t_valid, hb=hb),
        grid=(bsz, hg, nc),
        in_specs=[tok] * 8 + [st, vec, vec, pl.BlockSpec(memory_space=pl.ANY)],
        out_specs=[pl.BlockSpec((chunk, wb), lambda b, g, c: (blk0 + b * nc + c, dst_col0 // wb + g)), st],
        out_shape=[jax.ShapeDtypeStruct(dst.shape, dst.dtype),
                   jax.ShapeDtypeStruct((bsz, RW_HEADS, RW_DH, RW_DH), F32)],
        scratch_shapes=[pltpu.VMEM((chunk, wb), F32)],
        input_output_aliases={11: 0},
        compiler_params=_cparams(("arbitrary", "arbitrary", "arbitrary")),
        name="rwkv_chunk",
    )(*prep, s0, ln_w.reshape(1, RW_WIDTH), ln_b.reshape(1, RW_WIDTH), dst)


def _gdn_prep_kernel(cur_ref, tail_ref, hist_ref, cw_ref, o_ref, *, tt, t_pad, multi, heads_per_tile):
    cur = cur_ref[...]
    cw = cw_ref[...]
    row = _iota2((tt, 1), 0)
    acc = cur * cw[GD_CONV - 1:GD_CONV, :]
    if multi:
        hexp = hist_ref[...]
        pos = row & (t_pad - 1)
        for j in range(1, GD_CONV):
            back = GD_CONV - 1 - j
            hj = pltpu.roll(hexp, tt - back, 0) if back else hexp
            prev = jnp.where(pos >= j, pltpu.roll(cur, j, 0), hj)
            acc = acc + prev * cw[back:back + 1, :]
    else:
        starts_seq = (pl.program_id(0) % (t_pad // tt)) == 0
        hist = jnp.where(starts_seq, hist_ref[0], tail_ref[...])
        row8 = _iota2((SUBLANES, 1), 0)
        for j in range(1, GD_CONV):
            back = GD_CONV - 1 - j
            rolled = pltpu.roll(cur, j, 0)
            head = jnp.where(row8 < j, pltpu.roll(hist, j, 0), rolled[:SUBLANES])
            prev = jnp.concatenate([head, rolled[SUBLANES:]], axis=0)
            acc = acc + prev * cw[back:back + 1, :]
    act = acc * _sigmoid(acc)
    kind = pl.program_id(1)
    scale = jnp.where(kind == 0, GD_DK ** -0.5, 1.0)
    for h in range(heads_per_tile):
        sl = slice(h * GD_DK, (h + 1) * GD_DK)
        x = act[:, sl]
        normed = x * lax.rsqrt(jnp.sum(x * x, axis=-1, keepdims=True) + GD_EPS) * scale
        o_ref[:, sl] = jnp.where(kind < 2, normed, x)


def gdn_prep(proj, row0, bsz, t_pad, conv_buf, conv_w, tt):
    n = bsz * t_pad
    multi = tt > t_pad
    assert row0 % tt == 0 and n % tt == 0 and (tt % t_pad == 0 if multi else t_pad % tt == 0)
    blk0 = row0 // tt
    c = GD_CONV_DIM
    tc = GD_QK
    nbuf = GD_CONV - 1
    if multi:
        hist = jnp.pad(conv_buf, ((0, 0), (0, t_pad - nbuf), (0, 0))).reshape(n, c)
        hist_spec = pl.BlockSpec((tt, tc), lambda i, j: (i, j))
    else:
        hist = jnp.pad(conv_buf, ((0, 0), (SUBLANES - nbuf, 0), (0, 0)))
        hist_spec = pl.BlockSpec((1, SUBLANES, tc), lambda i, j: (i // (t_pad // tt), 0, j))
    tail_blk = tt // SUBLANES
    return pl.pallas_call(
        functools.partial(_gdn_prep_kernel, tt=tt, t_pad=t_pad, multi=multi, heads_per_tile=tc // GD_DK),
        grid=(n // tt, c // tc),
        in_specs=[pl.BlockSpec((tt, tc), lambda i, j: (blk0 + i, j)),
                  pl.BlockSpec((SUBLANES, tc), lambda i, j: (jnp.maximum((blk0 + i) * tail_blk - 1, 0), j)),
                  hist_spec,
                  pl.BlockSpec((GD_CONV, tc), lambda i, j: (0, j))],
        out_specs=pl.BlockSpec((tt, tc), lambda i, j: (i, j)),
        out_shape=jax.ShapeDtypeStruct((n, c), F32),
        compiler_params=_cparams(("arbitrary", "arbitrary")),
        name="gdn_prep",
    )(proj, proj, hist, conv_w)


def _gdn_chunk_kernel(q_ref, k_ref, v_ref, z_ref, gl_ref, al_ref, dt_ref, nw_ref, s0_ref, _dst_ref,
                      o_ref, s_ref, *, chunk, t_valid, hpg):
    ci = pl.program_id(2)
    hg = pl.program_id(1)

    @pl.when(ci == 0)
    def _():
        s_ref[...] = s0_ref[...]

    size = chunk
    rows = _iota2((size, size), 0)
    cols = _iota2((size, size), 1)
    eye = rows == cols
    incl = cols <= rows
    strict = cols < rows
    valid = (ci * size + _iota2((size, 1), 0)) < t_valid
    logits = gl_ref[...]
    beta_all = _sigmoid(logits)
    g_all = -jnp.exp(al_ref[...]) * _softplus(logits + dt_ref[...])
    lane = _iota2((size, LANES), 1)
    rep = GD_V_HEADS // GD_K_HEADS
    kheads = range(hpg // rep)
    heads = range(hpg)
    q = [q_ref[:, kh * GD_DK:(kh + 1) * GD_DK] for kh in kheads]
    k = [k_ref[:, kh * GD_DK:(kh + 1) * GD_DK] for kh in kheads]
    qk = [_dot_nt(q[kh], k[kh]) for kh in kheads]
    kk = [_dot_nt(k[kh], k[kh]) for kh in kheads]
    vsl = [slice(hl * GD_DV, (hl + 1) * GD_DV) for hl in heads]
    s = [s_ref[0, hl] for hl in heads]
    beta, gc, decay = [], [], []
    for hl in heads:
        head = hg * hpg + hl
        beta_h = jnp.sum(jnp.where(lane == head, beta_all, 0.0), axis=1, keepdims=True)
        g = jnp.sum(jnp.where(lane == head + GD_V_HEADS, g_all, 0.0), axis=1, keepdims=True)
        beta_h = jnp.where(valid, beta_h, 0.0)
        g = jnp.where(valid, g, 0.0)
        g_row = _col_to_row(g, eye)
        gc_h = jnp.sum(jnp.where(incl, g_row, 0.0), axis=1, keepdims=True)
        gc_row = jnp.sum(jnp.where(rows <= cols, g, 0.0), axis=0, keepdims=True)
        beta.append(beta_h)
        gc.append(gc_h)
        decay.append(jnp.where(incl, jnp.exp(jnp.where(incl, gc_h - gc_row, 0.0)), 0.0))
    qs = [_dot(q[hl // rep] * jnp.exp(gc[hl]), s[hl]) for hl in heads]
    t_inv = _neumann_inverse_many(
        [-jnp.where(strict, kk[hl // rep] * beta[hl] * decay[hl], 0.0) for hl in heads], size)
    uw = [_dot(t_inv[hl], jnp.concatenate([v_ref[:, vsl[hl]] * beta[hl],
                                           k[hl // rep] * (beta[hl] * jnp.exp(gc[hl]))], axis=1))
          for hl in heads]
    ws = [_dot(uw[hl][:, GD_DV:], s[hl]) for hl in heads]
    v_new = [uw[hl][:, :GD_DV] - ws[hl] for hl in heads]
    av = [_dot(qk[hl // rep] * decay[hl], v_new[hl]) for hl in heads]
    g_last = [gc[hl][size - 1:size, :] for hl in heads]
    kv = [_dot_tn(k[hl // rep] * jnp.exp(g_last[hl] - gc[hl]), v_new[hl]) for hl in heads]
    for hl in heads:
        s_ref[0, hl] = s[hl] * jnp.exp(g_last[hl]) + kv[hl]
        o = qs[hl] + av[hl]
        o = o * lax.rsqrt(jnp.mean(o * o, axis=-1, keepdims=True) + GD_EPS) * nw_ref[...]
        z = z_ref[:, vsl[hl]]
        o_ref[:, vsl[hl]] = (o * (z * _sigmoid(z))).astype(o_ref.dtype)


def gdn_chunk(qkv, proj, gates, dst, row0, bsz, t_pad, t_valid, chunk, s0, a_log, dt_bias, norm_w, hpg):
    nc = t_pad // chunk
    hgs = GD_V_HEADS // hpg
    rep = GD_V_HEADS // GD_K_HEADS
    kw = (hpg // rep) * GD_DK
    vw = hpg * GD_DV
    blk0 = row0 // chunk
    pad_row = lambda x: jnp.pad(x, (GD_V_HEADS, LANES - 2 * GD_V_HEADS)).reshape(1, LANES)
    st = pl.BlockSpec((1, hpg, GD_DK, GD_DV), lambda b, g, c: (b, g, 0, 0))
    vec = pl.BlockSpec((1, LANES), lambda b, g, c: (0, 0))
    return pl.pallas_call(
        functools.partial(_gdn_chunk_kernel, chunk=chunk, t_valid=t_valid, hpg=hpg),
        grid=(bsz, hgs, nc),
        in_specs=[pl.BlockSpec((chunk, kw), lambda b, g, c: (b * nc + c, g)),
                  pl.BlockSpec((chunk, kw), lambda b, g, c: (b * nc + c, GD_QK // kw + g)),
                  pl.BlockSpec((chunk, vw), lambda b, g, c: (b * nc + c, 2 * GD_QK // vw + g)),
                  pl.BlockSpec((chunk, vw), lambda b, g, c: (blk0 + b * nc + c, GD_CONV_DIM // vw + g)),
                  pl.BlockSpec((chunk, LANES), lambda b, g, c: (blk0 + b * nc + c, 0)),
                  vec, vec, vec, st, pl.BlockSpec(memory_space=pl.ANY)],
        out_specs=[pl.BlockSpec((chunk, vw), lambda b, g, c: (blk0 + b * nc + c, g)), st],
        out_shape=[jax.ShapeDtypeStruct(dst.shape, dst.dtype),
                   jax.ShapeDtypeStruct((bsz, GD_V_HEADS, GD_DK, GD_DV), F32)],
        input_output_aliases={9: 0},
        compiler_params=_cparams(("arbitrary", "arbitrary", "arbitrary")),
        name="gdn_chunk",
    )(qkv, qkv, qkv, proj, gates, pad_row(a_log), pad_row(dt_bias), norm_w.reshape(1, GD_DV), s0, dst)


MOE_TM = 256


def _router_kernel(x_ref, w_ref, wt_ref, id_ref):
    logits = jnp.dot(x_ref[...], w_ref[...].astype(BF16), preferred_element_type=F32)
    lane = _iota2(logits.shape, 1).astype(F32)
    first_of = lambda hit: jnp.min(jnp.where(hit, lane, float(LANES)), axis=1, keepdims=True)
    gl = jnp.where(lane < MOE_GROUPS, logits, NEG)
    gmax = jnp.max(gl, axis=1, keepdims=True)
    g_val = 1.0 / jnp.sum(jnp.exp(gl - gmax), axis=1, keepdims=True)
    lo = MOE_GROUPS + first_of(gl == gmax) * MOE_PER_GROUP
    vals = jnp.where((lane >= lo) & (lane < lo + MOE_PER_GROUP), logits, NEG)
    top1 = jnp.max(vals, axis=1, keepdims=True)
    i1 = first_of(vals == top1)
    vals2 = jnp.where(lane == i1, NEG, vals)
    top2 = jnp.max(vals2, axis=1, keepdims=True)
    i2 = first_of(vals2 == top2)
    e2 = jnp.exp(top2 - top1)
    w1 = (1.0 / (1.0 + e2)) * g_val
    w2 = (e2 / (1.0 + e2)) * g_val
    wt_ref[...] = jnp.where(lane == 0, w1, jnp.where(lane == 1, w2, 0.0))
    ids = jnp.where(lane == 0, i1 - MOE_GROUPS, jnp.where(lane == 1, i2 - MOE_GROUPS, 0.0))
    id_ref[...] = ids.astype(jnp.int32)


def moe_router(xb, w_router, tm=512):
    n, d = xb.shape
    out = pl.BlockSpec((tm, LANES), lambda i: (i, 0))
    return pl.pallas_call(
        _router_kernel,
        grid=(n // tm,),
        in_specs=[pl.BlockSpec((tm, d), lambda i: (i, 0)), pl.BlockSpec((d, LANES), lambda i: (0, 0))],
        out_specs=[out, out],
        out_shape=[jax.ShapeDtypeStruct((n, LANES), F32), jax.ShapeDtypeStruct((n, LANES), jnp.int32)],
        compiler_params=_cparams(("arbitrary",)),
        name="moe_router",
    )(xb, w_router)


def _moe_ffn_kernel(te_ref, tv_ref, idx_ref, idx_next_ref, x_hbm, wg_ref, wu_ref, wd_ref, o_ref,
                    xbuf, xcat, wgb, wub, wdb, sem):
    i = pl.program_id(0)
    n_tiles = pl.num_programs(0)
    slot = i % 2
    pieces = xcat.shape[1] // LANES

    def gather(rows_ref, into):
        for r in range(MOE_TM):
            pltpu.make_async_copy(x_hbm.at[pl.ds(rows_ref[0, 0, r], pieces)],
                                  xbuf.at[into, pl.ds(r * pieces, pieces)], sem.at[into]).start()

    @pl.when((i == 0) & (tv_ref[0] == 1))
    def _():
        gather(idx_ref, 0)

    nxt = jnp.minimum(i + 1, n_tiles - 1)

    @pl.when((i + 1 < n_tiles) & (tv_ref[nxt] == 1))
    def _():
        gather(idx_next_ref, 1 - slot)

    @pl.when(tv_ref[i] == 1)
    def _():
        pltpu.make_async_copy(xbuf.at[slot], xbuf.at[slot], sem.at[slot]).wait()

        @pl.when((i == 0) | (te_ref[i] != te_ref[jnp.maximum(i - 1, 0)]))
        def _():
            wgb[...] = wg_ref[0, 0].astype(BF16)
            wub[...] = wu_ref[0, 0].astype(BF16)
            wdb[...] = wd_ref[0, 0].astype(BF16)

        for j in range(pieces):
            xcat[:, j * LANES:(j + 1) * LANES] = xbuf[slot, pl.ds(j, MOE_TM, stride=pieces), :].astype(BF16)
        x = xcat[...]
        gate = jnp.dot(x, wgb[...], preferred_element_type=F32)
        up = jnp.dot(x, wub[...], preferred_element_type=F32)
        hid = gate * _sigmoid(gate) * up
        o_ref[...] = jnp.dot(hid.astype(BF16), wdb[...], preferred_element_type=F32)

    @pl.when(tv_ref[i] == 0)
    def _():
        o_ref[...] = jnp.zeros_like(o_ref)


def moe_ffn(x_rows, row_start, tile_expert, tile_valid, layer, w_gate, w_up, w_down):
    n_tiles = row_start.shape[0]
    d = w_gate.shape[2]
    ff = w_gate.shape[3]
    pieces = d // LANES
    grid_spec = pltpu.PrefetchScalarGridSpec(
        num_scalar_prefetch=2,
        grid=(n_tiles,),
        in_specs=[pl.BlockSpec((1, 1, MOE_TM), lambda i, te, tv: (i, 0, 0), memory_space=pltpu.SMEM),
                  pl.BlockSpec((1, 1, MOE_TM), lambda i, te, tv: (jnp.minimum(i + 1, n_tiles - 1), 0, 0),
                               memory_space=pltpu.SMEM),
                  pl.BlockSpec(memory_space=pl.ANY),
                  pl.BlockSpec((1, 1, d, ff), lambda i, te, tv: (layer, te[i], 0, 0)),
                  pl.BlockSpec((1, 1, d, ff), lambda i, te, tv: (layer, te[i], 0, 0)),
                  pl.BlockSpec((1, 1, ff, d), lambda i, te, tv: (layer, te[i], 0, 0))],
        out_specs=pl.BlockSpec((MOE_TM, d), lambda i, te, tv: (i, 0)),
        scratch_shapes=[pltpu.VMEM((2, MOE_TM * pieces, LANES), F32), pltpu.VMEM((MOE_TM, d), BF16),
                        pltpu.VMEM((d, ff), BF16), pltpu.VMEM((d, ff), BF16), pltpu.VMEM((ff, d), BF16),
                        pltpu.SemaphoreType.DMA((2,))],
    )
    return pl.pallas_call(
        _moe_ffn_kernel,
        grid_spec=grid_spec,
        out_shape=jax.ShapeDtypeStruct((n_tiles * MOE_TM, d), F32),
        compiler_params=_cparams(("arbitrary",)),
        name="moe_ffn",
    )(tile_expert, tile_valid, row_start, row_start, x_rows, w_gate, w_up, w_down)


def _ln_moe_kernel(x_ref, ya_ref, yb_ref, wt_ref, g_ref, b_ref, o_ref, ob_ref):
    wt = wt_ref[...]
    z = DEEPNORM_ALPHA * x_ref[...] + (wt[:, 0:1] * ya_ref[...] + wt[:, 1:2] * yb_ref[...])
    zc = z - jnp.mean(z, axis=-1, keepdims=True)
    var = jnp.mean(zc * zc, axis=-1, keepdims=True)
    out = zc * lax.rsqrt(var + LN_EPS) * g_ref[...] + b_ref[...]
    o_ref[...] = out
    ob_ref[...] = out.astype(BF16)


def ln_residual_moe(x, ya, yb, wt, g, b, tm=256):
    m, d = x.shape
    row = pl.BlockSpec((tm, d), lambda i: (i, 0))
    vec = pl.BlockSpec((1, d), lambda i: (0, 0))
    return pl.pallas_call(
        _ln_moe_kernel,
        grid=(m // tm,),
        in_specs=[row, row, row, pl.BlockSpec((tm, LANES), lambda i: (i, 0)), vec, vec],
        out_specs=[row, row],
        out_shape=[jax.ShapeDtypeStruct((m, d), F32), jax.ShapeDtypeStruct((m, d), BF16)],
        compiler_params=_cparams(("arbitrary",)),
        name="ln_residual_moe",
    )(x, ya, yb, wt, g.reshape(1, d), b.reshape(1, d))


def _table_lookup(table, idx):
    hit = idx[:, None] == jnp.arange(table.shape[0], dtype=idx.dtype)[None, :]
    return jnp.sum(jnp.where(hit, table[None, :], 0), axis=1)


def hier_moe(x_rows, xb, layer, w_group, w_expert, w_gate, w_up, w_down):
    n = xb.shape[0]
    assert (n * MOE_TOPK) % MOE_TM == 0
    w_router = jnp.pad(jnp.concatenate([w_group, w_expert], axis=1),
                       ((0, 0), (0, LANES - MOE_GROUPS - MOE_EXPERTS)))
    wt, ids = moe_router(xb, w_router)

    n_pairs = n * MOE_TOPK
    n_tiles = -(-n_pairs // MOE_TM) + MOE_EXPERTS
    n_rows = n_tiles * MOE_TM
    i32 = jnp.int32
    eid = ids[:, :MOE_TOPK].reshape(n_pairs)
    order = jnp.argsort(eid, stable=True).astype(i32)
    rank = jnp.argsort(order).astype(i32)
    experts = jnp.arange(MOE_EXPERTS, dtype=i32)
    counts = jnp.sum((eid[:, None] == experts[None, :]).astype(i32), axis=0)
    starts = jnp.cumsum(counts) - counts
    tiles_per = (counts + MOE_TM - 1) // MOE_TM
    tile_starts = jnp.cumsum(tiles_per) - tiles_per
    dest = _table_lookup(tile_starts * MOE_TM - starts, eid) + rank
    tile_idx = jnp.arange(n_tiles, dtype=i32)
    tile_expert = jnp.sum((tile_idx[:, None] >= (tile_starts + tiles_per)[None, :]).astype(i32), axis=1)
    tile_valid = (tile_expert < MOE_EXPERTS).astype(i32)
    last_used = jnp.max(jnp.where(tiles_per > 0, experts, 0))
    tile_expert = jnp.where(tile_valid == 1, tile_expert, last_used)
    off = (tile_idx - _table_lookup(tile_starts, tile_expert))[:, None] * MOE_TM + jnp.arange(MOE_TM, dtype=i32)[None, :]
    cnt = _table_lookup(counts, tile_expert)[:, None]
    src = jnp.clip(_table_lookup(starts, tile_expert)[:, None] + off, 0, n_pairs - 1)
    row_ok = (off < cnt) & (tile_valid[:, None] == 1)
    picked = jnp.take(order, src.reshape(n_rows), mode='clip').reshape(n_tiles, MOE_TM)
    pieces = xb.shape[1] // LANES
    row_start = (jnp.where(row_ok, picked // MOE_TOPK, 0) * pieces).reshape(n_tiles, 1, MOE_TM)

    y_sorted = moe_ffn(x_rows, row_start, tile_expert, tile_valid, layer, w_gate, w_up, w_down)
    dest2 = dest.reshape(n, MOE_TOPK)
    return (jnp.take(y_sorted, dest2[:, 0], axis=0, mode='clip'),
            jnp.take(y_sorted, dest2[:, 1], axis=0, mode='clip'), wt)


SAMPLE_T_PAD = 8


def kernel(x_prompt, x_sample, state_mlstm_c, state_mlstm_n, state_mlstm_m, state_rwkv_s,
           state_rwkv_shift, state_gdn_s, state_gdn_conv, p_prompt, p_sample,
           w_in_ab, ml_b_i, ml_b_f, ml_norm_w, rw_mu, rw_w0, rw_w2, rw_a0, rw_a2, rw_g2,
           rw_k_k, rw_k_a, rw_r_k, rw_ln_w, rw_ln_b, w_out_ab,
           gd_w_in, gd_conv_w, gd_a_log, gd_dt_bias, gd_norm_w, gd_w_out,
           ln_mix_g, ln_mix_b, moe_w_group, moe_w_expert, moe_w_gate, moe_w_up, moe_w_down,
           ln_ffn_g, ln_ffn_b, ple_w_gate, ple_w_proj):
    pb, pt, d = x_prompt.shape
    sb, st, _ = x_sample.shape
    n_p = pb * pt
    n_s = sb * SAMPLE_T_PAD
    n = n_p + n_s
    pad_t = lambda a: jnp.pad(a, ((0, 0), (0, SAMPLE_T_PAD - st), (0, 0)))
    merge = lambda a_p, a_s: jnp.concatenate(
        [a_p.reshape(n_p, a_p.shape[-1]), pad_t(a_s).reshape(n_s, a_s.shape[-1])], axis=0)
    x = merge(x_prompt, x_sample)
    xb = x.astype(BF16)
    segs = (dict(row0=0, bsz=pb, t_pad=pt, t_valid=pt, rw_hb=16, gd_hpg=16),
            dict(row0=n_p, bsz=sb, t_pad=SAMPLE_T_PAD, t_valid=st, rw_hb=16, gd_hpg=16))
    zeros = lambda *shape: jnp.zeros(shape, F32)

    def last_rows(a, sg, count):
        assert sg['t_valid'] >= count
        idx = (sg['row0'] + jnp.arange(sg['bsz'])[:, None] * sg['t_pad']
               + (sg['t_valid'] - count + jnp.arange(count))[None, :])
        return jnp.take(a, idx.reshape(-1), axis=0, mode='clip').reshape(sg['bsz'], count, a.shape[-1])

    new_states = [[[] for _ in range(7)] for _ in segs]
    for layer in range(DEPTH):
        li = layer // 2
        if layer % 2 == 0:
            w_in = w_in_ab[li]
            proj_ml = matmul(xb, w_in, 3584, 896)
            proj_rw = matmul(xb, w_in[:, ML_COLS:].astype(BF16), RW_COLS, RW_COLS // 2)
            rw_prm = dict(mu=rw_mu[li], w0=rw_w0[li], w2=rw_w2[li], a0=rw_a0[li], a2=rw_a2[li], g2=rw_g2[li],
                          k_k=rw_k_k[li], k_a=rw_k_a[li], r_k=rw_r_k[li])
            heads_out = jnp.zeros((n, ML_V + RW_WIDTH), BF16)
            for si, sg in enumerate(segs):
                bsz, t_pad, t_valid, row0 = sg['bsz'], sg['t_pad'], sg['t_valid'], sg['row0']
                if si == 0:
                    c0, n0, m0 = zeros(bsz, ML_HEADS, ML_DK, ML_DV), zeros(bsz, ML_HEADS, ML_DK), zeros(bsz, ML_HEADS)
                    s0, sh0 = zeros(bsz, RW_HEADS, RW_DH, RW_DH), zeros(bsz, RW_COLS)
                else:
                    c0, n0, m0 = state_mlstm_c[li], state_mlstm_n[li], state_mlstm_m[li]
                    s0, sh0 = state_rwkv_s[li], state_rwkv_shift[li]
                heads_out, c, nn, m = mlstm_group(proj_ml, heads_out, row0, bsz, t_pad, t_valid,
                                                  min(ML_CHUNK, t_pad), c0, n0, m0,
                                                  ml_b_i[li], ml_b_f[li], ml_norm_w[li])
                prep = rwkv_prep(proj_rw, row0, bsz, t_pad, sh0, rw_prm, 256)
                heads_out, rs = rwkv_chunk(prep, heads_out, ML_V, row0, bsz, t_pad, t_valid,
                                           min(RW_CHUNK, t_pad), s0, rw_ln_w[li], rw_ln_b[li], sg['rw_hb'])
                new_shift = last_rows(proj_rw, sg, 1)[:, 0]
                for slot, val in zip(range(5), (c, nn, m, rs, new_shift)):
                    new_states[si][slot].append(val)
            mix = matmul(heads_out, w_out_ab[li], d, 1024)
        else:
            w_in = gd_w_in[li]
            n_qkvz = GD_CONV_DIM + GD_V_WIDTH
            proj = matmul(xb, w_in, n_qkvz, 1024)
            gates = matmul(xb, w_in, LANES, LANES, col0=n_qkvz)
            gd_out = jnp.zeros((n, GD_V_WIDTH), BF16)
            for si, sg in enumerate(segs):
                bsz, t_pad, t_valid, row0 = sg['bsz'], sg['t_pad'], sg['t_valid'], sg['row0']
                if si == 0:
                    s0, buf0 = zeros(bsz, GD_V_HEADS, GD_DK, GD_DV), zeros(bsz, GD_CONV - 1, GD_CONV_DIM)
                else:
                    s0, buf0 = state_gdn_s[li], state_gdn_conv[li]
                qkv = gdn_prep(proj, row0, bsz, t_pad, buf0, gd_conv_w[li], 256)
                gd_out, gs = gdn_chunk(qkv, proj, gates, gd_out, row0, bsz, t_pad, t_valid,
                                       min(GD_CHUNK, t_pad), s0, gd_a_log[li], gd_dt_bias[li],
                                       gd_norm_w[li], sg['gd_hpg'])
                new_states[si][5].append(gs)
                new_states[si][6].append(last_rows(proj, sg, GD_CONV - 1)[:, :, :GD_CONV_DIM])
            mix = matmul(gd_out, gd_w_out[li], d, 512)
        x, xb, x_rows = ln_residual(x, mix, ln_mix_g[layer], ln_mix_b[layer])
        ya, yb, wt = hier_moe(x_rows, xb, layer, moe_w_group[layer], moe_w_expert[layer],
                              moe_w_gate, moe_w_up, moe_w_down)
        x, xb = ln_residual_moe(x, ya, yb, wt, ln_ffn_g[layer], ln_ffn_b[layer])
        x, xb = ple(xb, merge(p_prompt[layer], p_sample[layer]), x, layer, ple_w_gate, ple_w_proj)
    y_prompt = x[:n_p].reshape(pb, pt, d)
    y_sample = x[n_p:].reshape(sb, SAMPLE_T_PAD, d)[:, :st]
    stack = lambda vals: jnp.stack(vals)
    return (y_prompt, y_sample) + tuple(stack(v) for v in new_states[0]) + tuple(stack(v) for v in new_states[1])
```

```python
import functools

import jax
import jax.numpy as jnp
import numpy as np
from jax import lax
from jax.experimental import pallas as pl
from jax.experimental.pallas import tpu as pltpu

F32 = jnp.float32
BF16 = jnp.bfloat16

D_MODEL = 2048
DEPTH = 2
DEEPNORM_ALPHA = (2 * DEPTH) ** 0.25
LN_EPS = 1e-5
MIX_HALF = D_MODEL // 2
ML_HEADS = 4
ML_DV = MIX_HALF // ML_HEADS
ML_DK = ML_DV // 2
ML_CHUNK = 64
ML_GATE_CAP = 15.0
ML_NORM_EPS = 1e-6
ML_QK = ML_HEADS * ML_DK
ML_V = ML_HEADS * ML_DV
ML_COLS = 2 * ML_QK + 2 * ML_V + 2 * ML_HEADS
RW_DH = 64
RW_HEADS = MIX_HALF // RW_DH
RW_WIDTH = RW_HEADS * RW_DH
RW_DECAY_LORA = 64
RW_A_LORA = 64
RW_GATE_LORA = 128
RW_GN_EPS = 64e-5
RW_COLS = 3 * RW_WIDTH + RW_DECAY_LORA + RW_A_LORA + RW_GATE_LORA
RW_CHUNK = 64
GD_DK = 128
GD_DV = 128
GD_K_HEADS = D_MODEL // GD_DK
GD_V_HEADS = 2 * GD_K_HEADS
GD_CONV = 4
GD_CHUNK = 64
GD_EPS = 1e-6
GD_QK = GD_K_HEADS * GD_DK
GD_V_WIDTH = GD_V_HEADS * GD_DV
GD_CONV_DIM = 2 * GD_QK + GD_V_WIDTH
MOE_GROUPS = 4
MOE_PER_GROUP = 8
MOE_EXPERTS = MOE_GROUPS * MOE_PER_GROUP
MOE_TOPK = 2
MOE_FF = D_MODEL // 4
PLE_DIM = 256

LANES = 128
SUBLANES = 8
NEG = -1e30
VMEM_LIMIT = 56 * 1024 * 1024


def _cparams(sem):
    return pltpu.CompilerParams(dimension_semantics=sem, vmem_limit_bytes=VMEM_LIMIT)


def _dot(a, b):
    return jnp.dot(a.astype(BF16), b.astype(BF16), preferred_element_type=F32)


def _dot_nt(a, b):
    return lax.dot_general(a.astype(BF16), b.astype(BF16), (((1,), (1,)), ((), ())),
                           preferred_element_type=F32)


def _dot_tn(a, b):
    return lax.dot_general(a.astype(BF16), b.astype(BF16), (((0,), (0,)), ((), ())),
                           preferred_element_type=F32)


def _split3(x):
    hi = x.astype(BF16)
    r1 = x - hi.astype(F32)
    mid = r1.astype(BF16)
    lo = (r1 - mid.astype(F32)).astype(BF16)
    return hi, mid, lo


def _dot_exact_rhs(a, b01):
    hi, mid, lo = _split3(a)
    b = b01.astype(BF16)
    return (jnp.dot(hi, b, preferred_element_type=F32) + jnp.dot(mid, b, preferred_element_type=F32)
            + jnp.dot(lo, b, preferred_element_type=F32))


def _dot_exact_lhs(a01, b):
    hi, mid, lo = _split3(b)
    a = a01.astype(BF16)
    return (jnp.dot(a, hi, preferred_element_type=F32) + jnp.dot(a, mid, preferred_element_type=F32)
            + jnp.dot(a, lo, preferred_element_type=F32))


def _iota2(shape, dim):
    return lax.broadcasted_iota(jnp.int32, shape, dim)


def _col_to_row(col, eye):
    return jnp.sum(jnp.where(eye, col, 0.0), axis=0, keepdims=True)


def _sigmoid(x):
    return 1.0 / (1.0 + jnp.exp(-x))


def _softplus(x):
    return jnp.maximum(x, 0.0) + jnp.log1p(jnp.exp(-jnp.abs(x)))


def _neumann_inverse(n_mat, size):
    eye = (_iota2((size, size), 0) == _iota2((size, size), 1)).astype(F32)
    t = eye + n_mat
    x = n_mat
    steps = max(int(np.ceil(np.log2(size))) - 1, 0)
    for _ in range(steps):
        x = _dot(x, x)
        t = t + _dot(t, x)
    return t


def _mm_kernel(x_ref, w_ref, o_ref, wb_ref, *, n_valid):
    @pl.when(pl.program_id(1) == 0)
    def _():
        wb_ref[...] = w_ref[...].astype(BF16)

    acc = jnp.dot(x_ref[...].astype(BF16), wb_ref[...], preferred_element_type=F32)
    if n_valid is not None:
        acc = jnp.where(_iota2(acc.shape, 1) < n_valid, acc, 0.0)
    o_ref[...] = acc.astype(o_ref.dtype)


def _row_tile(m, want):
    while m % want:
        want //= 2
    return want


def matmul(x, w, n_out, tn, tm=1024, out_dtype=F32, col0=0):
    m, k = x.shape
    tm = _row_tile(m, tm)
    assert w.shape[0] == k and n_out % tn == 0 and m % tm == 0 and tn % LANES == 0 and col0 % tn == 0
    n_valid = None
    if col0 + n_out > w.shape[1]:
        assert n_out == tn
        n_valid = w.shape[1] - col0
    blk0 = col0 // tn
    return pl.pallas_call(
        functools.partial(_mm_kernel, n_valid=n_valid),
        grid=(n_out // tn, m // tm),
        in_specs=[pl.BlockSpec((tm, k), lambda j, i: (i, 0)),
                  pl.BlockSpec((k, tn), lambda j, i: (0, blk0 + j))],
        out_specs=pl.BlockSpec((tm, tn), lambda j, i: (i, j)),
        out_shape=jax.ShapeDtypeStruct((m, n_out), out_dtype),
        scratch_shapes=[pltpu.VMEM((k, tn), BF16)],
        compiler_params=_cparams(("arbitrary", "arbitrary")),
        name="matmul",
    )(x, w)


def _mm_wt_kernel(x_ref, wt_hbm, o_ref, wf_ref, wb_ref, sem, *, row0, tn, n_rows):
    j = pl.program_id(0)

    @pl.when(pl.program_id(1) == 0)
    def _():
        if n_rows < tn:
            wf_ref[...] = jnp.zeros_like(wf_ref)
        start = pl.multiple_of(row0 + j * tn, SUBLANES)
        copy = pltpu.make_async_copy(wt_hbm.at[pl.ds(start, n_rows)], wf_ref.at[pl.ds(0, n_rows)], sem)
        copy.start()
        copy.wait()
        k = wf_ref.shape[1]
        step = 2 * LANES
        for c in range(k // step):
            wb_ref[c * step:(c + 1) * step, :] = wf_ref[:, c * step:(c + 1) * step].T.astype(BF16)

    o_ref[...] = jnp.dot(x_ref[...].astype(BF16), wb_ref[...], preferred_element_type=F32)


def matmul_wt(x, wt, row0, n_out, tn, tm=1024):
    m, k = x.shape
    tm = _row_tile(m, tm)
    assert wt.shape[1] == k and n_out % tn == 0 and tn % LANES == 0 and row0 % SUBLANES == 0
    n_rows = tn
    if row0 + n_out > wt.shape[0]:
        assert n_out == tn
        n_rows = wt.shape[0] - row0
    assert n_rows % SUBLANES == 0
    return pl.pallas_call(
        functools.partial(_mm_wt_kernel, row0=row0, tn=tn, n_rows=n_rows),
        grid=(n_out // tn, m // tm),
        in_specs=[pl.BlockSpec((tm, k), lambda j, i: (i, 0)), pl.BlockSpec(memory_space=pl.ANY)],
        out_specs=pl.BlockSpec((tm, tn), lambda j, i: (i, j)),
        out_shape=jax.ShapeDtypeStruct((m, n_out), F32),
        scratch_shapes=[pltpu.VMEM((tn, k), F32), pltpu.VMEM((k, tn), BF16), pltpu.SemaphoreType.DMA(())],
        compiler_params=_cparams(("arbitrary", "arbitrary")),
        name="matmul_wt",
    )(x, wt)


def _ln_res_kernel(x_ref, y_ref, g_ref, b_ref, o_ref, ob_ref, orow_ref):
    z = DEEPNORM_ALPHA * x_ref[...] + y_ref[...]
    zc = z - jnp.mean(z, axis=-1, keepdims=True)
    var = jnp.mean(zc * zc, axis=-1, keepdims=True)
    out = zc * lax.rsqrt(var + LN_EPS) * g_ref[...] + b_ref[...]
    o_ref[...] = out
    ob_ref[...] = out.astype(BF16)
    tm, d = out.shape
    pieces = d // LANES
    for j in range(pieces):
        orow_ref[pl.ds(j, tm, stride=pieces), :] = out[:, j * LANES:(j + 1) * LANES]


def ln_residual(x, y, g, b, tm=256):
    m, d = x.shape
    pieces = d // LANES
    row = pl.BlockSpec((tm, d), lambda i: (i, 0))
    vec = pl.BlockSpec((1, d), lambda i: (0, 0))
    return pl.pallas_call(
        _ln_res_kernel,
        grid=(m // tm,),
        in_specs=[row, row, vec, vec],
        out_specs=[row, row, pl.BlockSpec((tm * pieces, LANES), lambda i: (i, 0))],
        out_shape=[jax.ShapeDtypeStruct((m, d), F32), jax.ShapeDtypeStruct((m, d), BF16),
                   jax.ShapeDtypeStruct((m * pieces, LANES), F32)],
        compiler_params=_cparams(("arbitrary",)),
        name="ln_residual",
    )(x, y, g.reshape(1, d), b.reshape(1, d))


def _ple_kernel(xb_ref, p_ref, x_ref, wg_ref, wp_ref, o_ref, ob_ref, wgb_ref, wpb_ref):
    @pl.when(pl.program_id(1) == 0)
    def _():
        wgb_ref[...] = wg_ref[0].astype(BF16)
        wpb_ref[...] = wp_ref[0].astype(BF16)

    gate = _sigmoid(jnp.dot(xb_ref[...], wgb_ref[...], preferred_element_type=F32))
    emb = jnp.dot(p_ref[...].astype(BF16), wpb_ref[...], preferred_element_type=F32)
    out = x_ref[...] + gate * emb
    o_ref[...] = out
    ob_ref[...] = out.astype(BF16)


def ple(xb, p, x, layer, wg, wp, tm=1024, tn=512):
    m, d = x.shape
    pd = p.shape[1]
    tm = _row_tile(m, tm)
    return pl.pallas_call(
        _ple_kernel,
        grid=(d // tn, m // tm),
        in_specs=[pl.BlockSpec((tm, d), lambda j, i: (i, 0)),
                  pl.BlockSpec((tm, pd), lambda j, i: (i, 0)),
                  pl.BlockSpec((tm, tn), lambda j, i: (i, j)),
                  pl.BlockSpec((1, d, tn), lambda j, i: (layer, 0, j)),
                  pl.BlockSpec((1, pd, tn), lambda j, i: (layer, 0, j))],
        out_specs=[pl.BlockSpec((tm, tn), lambda j, i: (i, j))] * 2,
        out_shape=[jax.ShapeDtypeStruct((m, d), F32), jax.ShapeDtypeStruct((m, d), BF16)],
        scratch_shapes=[pltpu.VMEM((d, tn), BF16), pltpu.VMEM((pd, tn), BF16)],
        compiler_params=_cparams(("arbitrary", "arbitrary")),
        name="ple",
    )(xb, p, x, wg, wp)


def _mlstm_kernel(q_ref, k_ref, v_ref, o_ref, g_ref, c0_ref, n0_ref, m0_ref, nw_ref, gb_ref, _dst_ref,
                  h_ref, c_ref, n_ref, m_ref, *, chunk, t_valid):
    ci = pl.program_id(1)

    @pl.when(ci == 0)
    def _():
        c_ref[...] = c0_ref[...]
        n_ref[...] = n0_ref[...]
        m_ref[...] = m0_ref[...]

    size = chunk
    rows = _iota2((size, size), 0)
    cols = _iota2((size, size), 1)
    eye = rows == cols
    causal = cols <= rows
    gates = g_ref[:, 0:2 * ML_HEADS] + gb_ref[...]
    capped = ML_GATE_CAP * jnp.tanh(gates / ML_GATE_CAP)
    t_idx = ci * size + _iota2((size, 1), 0)
    valid = t_idx < t_valid
    ipre_all = jnp.where(valid, capped[:, 0:ML_HEADS], NEG)
    logf_all = jnp.where(valid, -_softplus(-capped[:, ML_HEADS:2 * ML_HEADS]), 0.0)
    heads = range(ML_HEADS)
    q = [q_ref[:, h * ML_DK:(h + 1) * ML_DK] for h in heads]
    k = [k_ref[:, h * ML_DK:(h + 1) * ML_DK] * (ML_DK ** -0.5) for h in heads]
    v = [v_ref[:, h * ML_DV:(h + 1) * ML_DV] for h in heads]
    qk = [_dot_nt(q[h], k[h]) for h in heads]
    c_prev = [c_ref[0, h] for h in heads]
    n_prev = [n_ref[0, h:h + 1, :] for h in heads]
    qc = [_dot(q[h], c_prev[h]) for h in heads]
    s, w_inter, m_t, kw, decay = [], [], [], [], []
    for h in heads:
        ig_col = ipre_all[:, h:h + 1]
        lf_col = logf_all[:, h:h + 1]
        ig_row = _col_to_row(ig_col, eye)
        lf_row = _col_to_row(lf_col, eye)
        b_col = jnp.sum(jnp.where(causal, lf_row, 0.0), axis=1, keepdims=True)
        b_row = jnp.sum(jnp.where(rows <= cols, lf_col, 0.0), axis=0, keepdims=True)
        d = jnp.where(causal, b_col - b_row + ig_row, NEG)
        inter = b_col + m_ref[0, :, h:h + 1]
        m_h = jnp.maximum(inter, jnp.max(d, axis=1, keepdims=True))
        w_h = jnp.exp(inter - m_h)
        s.append(qk[h] * jnp.exp(d - m_h))
        b_last = b_col[size - 1:size, :]
        m_last = m_h[size - 1:size, :]
        w_last = jnp.exp(b_last - b_col + ig_col - m_last)
        kw.append(k[h] * w_last)
        w_inter.append(w_h)
        m_t.append(m_h)
        decay.append(w_h[size - 1:size, :])
        m_ref[0, :, h:h + 1] = m_last
    sv = [_dot(s[h], v[h]) for h in heads]
    kv = [_dot_tn(kw[h], v[h]) for h in heads]
    for h in heads:
        num = w_inter[h] * qc[h] + sv[h]
        den = (w_inter[h] * jnp.sum(q[h] * n_prev[h], axis=1, keepdims=True)
               + jnp.sum(s[h], axis=1, keepdims=True))
        hid = num / jnp.maximum(jnp.abs(den), jnp.exp(-m_t[h]))
        c_ref[0, h] = decay[h] * c_prev[h] + kv[h]
        n_ref[0, h:h + 1, :] = decay[h] * n_prev[h] + jnp.sum(kw[h], axis=0, keepdims=True)
        hid = hid * lax.rsqrt(jnp.mean(hid * hid, axis=-1, keepdims=True) + ML_NORM_EPS)
        hid = hid * nw_ref[:, h * ML_DV:(h + 1) * ML_DV]
        hid = hid * _sigmoid(o_ref[:, h * ML_DV:(h + 1) * ML_DV])
        h_ref[:, h * ML_DV:(h + 1) * ML_DV] = hid.astype(h_ref.dtype)


def mlstm_group(proj, dst, row0, bsz, t_pad, t_valid, chunk, c0, n0, m0, b_i, b_f, norm_w):
    nc = t_pad // chunk
    blk0 = row0 // chunk
    assert row0 % chunk == 0 and t_pad % chunk == 0
    rmap = lambda c0_: (lambda b, c: (blk0 + b * nc + c, c0_))
    qk_w, v_w = ML_QK, ML_V
    in_specs = [
        pl.BlockSpec((chunk, qk_w), rmap(0)),
        pl.BlockSpec((chunk, qk_w), rmap(1)),
        pl.BlockSpec((chunk, v_w), rmap(1)),
        pl.BlockSpec((chunk, v_w), rmap(2)),
        pl.BlockSpec((chunk, LANES), rmap((2 * qk_w + 2 * v_w) // LANES)),
        pl.BlockSpec((1, ML_HEADS, ML_DK, ML_DV), lambda b, c: (b, 0, 0, 0)),
        pl.BlockSpec((1, ML_HEADS, ML_DK), lambda b, c: (b, 0, 0)),
        pl.BlockSpec((1, 1, ML_HEADS), lambda b, c: (b, 0, 0)),
        pl.BlockSpec((1, v_w), lambda b, c: (0, 0)),
        pl.BlockSpec((1, 2 * ML_HEADS), lambda b, c: (0, 0)),
        pl.BlockSpec(memory_space=pl.ANY),
    ]
    out_specs = [
        pl.BlockSpec((chunk, v_w), lambda b, c: (blk0 + b * nc + c, 0)),
        pl.BlockSpec((1, ML_HEADS, ML_DK, ML_DV), lambda b, c: (b, 0, 0, 0)),
        pl.BlockSpec((1, ML_HEADS, ML_DK), lambda b, c: (b, 0, 0)),
        pl.BlockSpec((1, 1, ML_HEADS), lambda b, c: (b, 0, 0)),
    ]
    out_shape = [
        jax.ShapeDtypeStruct(dst.shape, dst.dtype),
        jax.ShapeDtypeStruct((bsz, ML_HEADS, ML_DK, ML_DV), F32),
        jax.ShapeDtypeStruct((bsz, ML_HEADS, ML_DK), F32),
        jax.ShapeDtypeStruct((bsz, 1, ML_HEADS), F32),
    ]
    gate_bias = jnp.concatenate([b_i, b_f]).reshape(1, 2 * ML_HEADS)
    h, c, n, m = pl.pallas_call(
        functools.partial(_mlstm_kernel, chunk=chunk, t_valid=t_valid),
        grid=(bsz, nc),
        in_specs=in_specs,
        out_specs=out_specs,
        out_shape=out_shape,
        input_output_aliases={10: 0},
        compiler_params=_cparams(("arbitrary", "arbitrary")),
        name="mlstm",
    )(proj, proj, proj, proj, proj, c0, n0, m0.reshape(bsz, 1, ML_HEADS), norm_w.reshape(1, v_w), gate_bias, dst)
    return h, c, n, m.reshape(bsz, ML_HEADS)


def _head_block_ones(width, head):
    idx = np.arange(width) // head
    return jnp.asarray((idx[:, None] == idx[None, :]).astype(np.float32), dtype=BF16)


def _rwkv_prep_kernel(cur_ref, tail_ref, first_ref, mu_ref, w0_ref, a0_ref, kk_ref, ka_ref, rk_ref,
                      w2_ref, a2_ref, g2_ref, blk_ref,
                      r_ref, wl_ref, k_ref, v_ref, an_ref, bb_ref, bonus_ref, g_ref,
                      *, tt, t_pad, multi):
    cur = cur_ref[...]
    rolled = pltpu.roll(cur, 1, 0)
    row = _iota2((tt, 1), 0)
    if multi:
        prev = jnp.where((row & (t_pad - 1)) == 0, first_ref[...], rolled)
    else:
        starts_seq = (pl.program_id(0) % (t_pad // tt)) == 0
        head = jnp.where(starts_seq, first_ref[0], tail_ref[SUBLANES - 1:SUBLANES, :])
        prev = jnp.where(row == 0, head, rolled)
    mixed = cur + (prev - cur) * mu_ref[...]
    w = RW_WIDTH
    r = mixed[:, 0:w]
    k = mixed[:, w:2 * w]
    v = mixed[:, 2 * w:3 * w]
    c0 = 3 * w
    wl = mixed[:, c0:c0 + RW_DECAY_LORA]
    al = mixed[:, c0 + RW_DECAY_LORA:c0 + RW_DECAY_LORA + RW_A_LORA]
    gl = mixed[:, c0 + RW_DECAY_LORA + RW_A_LORA:]
    w_log = -jnp.exp(-_softplus(-(w0_ref[...] + _dot(jnp.tanh(wl), w2_ref[...]))) - 0.5)
    a = _sigmoid(a0_ref[...] + _dot(al, a2_ref[...]))
    g = _dot(_sigmoid(gl), g2_ref[...])
    kk = k * kk_ref[...]
    blk = blk_ref[...]
    kkn = kk / jnp.maximum(jnp.sqrt(_dot_exact_rhs(kk * kk, blk)), 1e-12)
    k2 = k * (1.0 + (a - 1.0) * ka_ref[...])
    r_ref[...] = r
    wl_ref[...] = w_log
    k_ref[...] = k2
    v_ref[...] = v
    an_ref[...] = -kkn
    bb_ref[...] = kkn * a
    bonus_ref[...] = _dot_exact_rhs(r * k2 * rk_ref[...], blk) * v
    g_ref[...] = g


def rwkv_prep(proj, row0, bsz, t_pad, shift0, prm, tt):
    n = bsz * t_pad
    multi = tt > t_pad
    assert row0 % tt == 0 and n % tt == 0 and (tt % t_pad == 0 if multi else t_pad % tt == 0)
    blk0 = row0 // tt
    c = RW_COLS
    if multi:
        first = jnp.zeros((bsz, t_pad, c), F32).at[:, 0].set(shift0).reshape(n, c)
        first_spec = pl.BlockSpec((tt, c), lambda i: (i, 0))
    else:
        first = shift0.reshape(bsz, 1, c)
        first_spec = pl.BlockSpec((1, 1, c), lambda i: (i // (t_pad // tt), 0, 0))
    tail_blk = tt // SUBLANES
    vec = lambda width: pl.BlockSpec((1, width), lambda i: (0, 0))
    full = lambda a, b: pl.BlockSpec((a, b), lambda i: (0, 0))
    w = RW_WIDTH
    out_spec = pl.BlockSpec((tt, w), lambda i: (i, 0))
    return pl.pallas_call(
        functools.partial(_rwkv_prep_kernel, tt=tt, t_pad=t_pad, multi=multi),
        grid=(n // tt,),
        in_specs=[pl.BlockSpec((tt, c), lambda i: (blk0 + i, 0)),
                  pl.BlockSpec((SUBLANES, c), lambda i: (jnp.maximum((blk0 + i) * tail_blk - 1, 0), 0)),
                  first_spec, vec(c), vec(w), vec(w), vec(w), vec(w), vec(w),
                  full(RW_DECAY_LORA, w), full(RW_A_LORA, w), full(RW_GATE_LORA, w), full(w, w)],
        out_specs=[out_spec] * 8,
        out_shape=[jax.ShapeDtypeStruct((n, w), F32)] * 8,
        compiler_params=_cparams(("arbitrary",)),
        name="rwkv_prep",
    )(proj, proj, first, prm['mu'].reshape(1, c), prm['w0'].reshape(1, w), prm['a0'].reshape(1, w),
      prm['k_k'].reshape(1, w), prm['k_a'].reshape(1, w), prm['r_k'].reshape(1, w),
      prm['w2'], prm['a2'], prm['g2'], _head_block_ones(w, RW_DH))


def _neumann_inverse_many(n_mats, size):
    eye = (_iota2((size, size), 0) == _iota2((size, size), 1)).astype(F32)
    ts = [eye + n_mat for n_mat in n_mats]
    xs = list(n_mats)
    steps = max(int(np.ceil(np.log2(size))) - 1, 0)
    for _ in range(steps):
        xs = [_dot(x, x) for x in xs]
        ts = [t + _dot(t, x) for t, x in zip(ts, xs)]
    return ts


def _rwkv_chunk_kernel(r_ref, w_ref, k_ref, v_ref, a_ref, b_ref, bonus_ref, g_ref, s0_ref,
                       lnw_ref, lnb_ref, _dst_ref, o_ref, s_ref, acc_ref, *, chunk, t_valid, hb):
    ci = pl.program_id(2)

    @pl.when(ci == 0)
    def _():
        s_ref[...] = s0_ref[...]

    size = chunk
    rows = _iota2((size, size), 0)
    cols = _iota2((size, size), 1)
    strict = cols < rows
    rows2 = _iota2((size, 2 * size), 0)
    cols2 = _iota2((size, 2 * size), 1)
    mask_ak = (cols2 >= size) & (cols2 - size < rows2)
    mask_o = jnp.where(cols2 >= size, cols2 - size, cols2) <= rows2
    valid = (ci * size + _iota2((size, 1), 0)) < t_valid
    w = jnp.where(valid, w_ref[...], 0.0)
    a = jnp.where(valid, a_ref[...], 0.0)
    b = jnp.where(valid, b_ref[...], 0.0)
    k = jnp.where(valid, k_ref[...], 0.0)
    v = v_ref[...]
    lam = _dot_exact_lhs((cols <= rows).astype(F32), w)
    e_pos = jnp.exp(lam)
    e_neg = jnp.exp(-lam)
    at = a * jnp.exp(lam - w)
    bt = b * e_neg
    kt = k * e_neg
    rt = r_ref[...] * e_pos
    heads = range(hb)
    sls = [slice(h * RW_DH, (h + 1) * RW_DH) for h in heads]
    at_h = [at[:, sl] for sl in sls]
    rt_h = [rt[:, sl] for sl in sls]
    v_h = [v[:, sl] for sl in sls]
    bk = [jnp.concatenate([bt[:, sl], kt[:, sl]], axis=0) for sl in sls]
    s0 = [s_ref[0, h] for h in heads]
    pa = [_dot_nt(at_h[h], bk[h]) for h in heads]
    pr = [_dot_nt(rt_h[h], bk[h]) for h in heads]
    as0 = [_dot_nt(at_h[h], s0[h]) for h in heads]
    rs0 = [_dot_nt(rt_h[h], s0[h]) for h in heads]
    zv = [jnp.concatenate([jnp.zeros_like(v_h[h]), v_h[h]], axis=0) for h in heads]
    rhs = [as0[h] + _dot(jnp.where(mask_ak, pa[h], 0.0), zv[h]) for h in heads]
    t_inv = _neumann_inverse_many([jnp.where(strict, pa[h][:, :size], 0.0) for h in heads], size)
    u = [_dot(t_inv[h], rhs[h]) for h in heads]
    uv = [jnp.concatenate([u[h], v_h[h]], axis=0) for h in heads]
    o = [rs0[h] + _dot(jnp.where(mask_o, pr[h], 0.0), uv[h]) for h in heads]
    ds = [_dot_tn(uv[h], bk[h]) for h in heads]
    for h in heads:
        s_ref[0, h] = (s0[h] + ds[h]) * e_pos[size - 1:size, sls[h]]
        oc = o[h] - jnp.mean(o[h], axis=-1, keepdims=True)
        acc_ref[:, sls[h]] = oc * lax.rsqrt(jnp.mean(oc * oc, axis=-1, keepdims=True) + RW_GN_EPS)
    out = (acc_ref[...] * lnw_ref[...] + lnb_ref[...] + bonus_ref[...]) * g_ref[...]
    o_ref[...] = out.astype(o_ref.dtype)


def rwkv_chunk(prep, dst, dst_col0, row0, bsz, t_pad, t_valid, chunk, s0, ln_w, ln_b, hb):
    nc = t_pad // chunk
    hg = RW_HEADS // hb
    wb = hb * RW_DH
    blk0 = row0 // chunk
    assert dst_col0 % wb == 0 and row0 % chunk == 0
    tok = pl.BlockSpec((chunk, wb), lambda b, g, c: (b * nc + c, g))
    st = pl.BlockSpec((1, hb, RW_DH, RW_DH), lambda b, g, c: (b, g, 0, 0))
    vec = pl.BlockSpec((1, wb), lambda b, g, c: (0, g))
    return pl.pallas_call(
        functools.partial(_rwkv_chunk_kernel, chunk=chunk, t_valid=t_valid, hb=hb),
        grid=(bsz, hg, nc),
        in_specs=[tok] * 8 + [st, vec, vec, pl.BlockSpec(memory_space=pl.ANY)],
        out_specs=[pl.BlockSpec((chunk, wb), lambda b, g, c: (blk0 + b * nc + c, dst_col0 // wb + g)), st],
        out_shape=[jax.ShapeDtypeStruct(dst.shape, dst.dtype),
                   jax.ShapeDtypeStruct((bsz, RW_HEADS, RW_DH, RW_DH), F32)],
        scratch_shapes=[pltpu.VMEM((chunk, wb), F32)],
        input_output_aliases={11: 0},
        compiler_params=_cparams(("arbitrary", "arbitrary", "arbitrary")),
        name="rwkv_chunk",
    )(*prep, s0, ln_w.reshape(1, RW_WIDTH), ln_b.reshape(1, RW_WIDTH), dst)


def _gdn_prep_kernel(cur_ref, tail_ref, hist_ref, cw_ref, o_ref, *, tt, t_pad, multi, heads_per_tile):
    cur = cur_ref[...]
    cw = cw_ref[...]
    row = _iota2((tt, 1), 0)
    acc = cur * cw[GD_CONV - 1:GD_CONV, :]
    if multi:
        hexp = hist_ref[...]
        pos = row & (t_pad - 1)
        for j in range(1, GD_CONV):
            back = GD_CONV - 1 - j
            hj = pltpu.roll(hexp, tt - back, 0) if back else hexp
            prev = jnp.where(pos >= j, pltpu.roll(cur, j, 0), hj)
            acc = acc + prev * cw[back:back + 1, :]
    else:
        starts_seq = (pl.program_id(0) % (t_pad // tt)) == 0
        hist = jnp.where(starts_seq, hist_ref[0], tail_ref[...])
        row8 = _iota2((SUBLANES, 1), 0)
        for j in range(1, GD_CONV):
            back = GD_CONV - 1 - j
            rolled = pltpu.roll(cur, j, 0)
            head = jnp.where(row8 < j, pltpu.roll(hist, j, 0), rolled[:SUBLANES])
            prev = jnp.concatenate([head, rolled[SUBLANES:]], axis=0)
            acc = acc + prev * cw[back:back + 1, :]
    act = acc * _sigmoid(acc)
    kind = pl.program_id(1)
    scale = jnp.where(kind == 0, GD_DK ** -0.5, 1.0)
    for h in range(heads_per_tile):
        sl = slice(h * GD_DK, (h + 1) * GD_DK)
        x = act[:, sl]
        normed = x * lax.rsqrt(jnp.sum(x * x, axis=-1, keepdims=True) + GD_EPS) * scale
        o_ref[:, sl] = jnp.where(kind < 2, normed, x)


def gdn_prep(proj, row0, bsz, t_pad, conv_buf, conv_w, tt):
    n = bsz * t_pad
    multi = tt > t_pad
    assert row0 % tt == 0 and n % tt == 0 and (tt % t_pad == 0 if multi else t_pad % tt == 0)
    blk0 = row0 // tt
    c = GD_CONV_DIM
    tc = GD_QK
    nbuf = GD_CONV - 1
    if multi:
        hist = jnp.pad(conv_buf, ((0, 0), (0, t_pad - nbuf), (0, 0))).reshape(n, c)
        hist_spec = pl.BlockSpec((tt, tc), lambda i, j: (i, j))
    else:
        hist = jnp.pad(conv_buf, ((0, 0), (SUBLANES - nbuf, 0), (0, 0)))
        hist_spec = pl.BlockSpec((1, SUBLANES, tc), lambda i, j: (i // (t_pad // tt), 0, j))
    tail_blk = tt // SUBLANES
    return pl.pallas_call(
        functools.partial(_gdn_prep_kernel, tt=tt, t_pad=t_pad, multi=multi, heads_per_tile=tc // GD_DK),
        grid=(n // tt, c // tc),
        in_specs=[pl.BlockSpec((tt, tc), lambda i, j: (blk0 + i, j)),
                  pl.BlockSpec((SUBLANES, tc), lambda i, j: (jnp.maximum((blk0 + i) * tail_blk - 1, 0), j)),
                  hist_spec,
                  pl.BlockSpec((GD_CONV, tc), lambda i, j: (0, j))],
        out_specs=pl.BlockSpec((tt, tc), lambda i, j: (i, j)),
        out_shape=jax.ShapeDtypeStruct((n, c), F32),
        compiler_params=_cparams(("arbitrary", "arbitrary")),
        name="gdn_prep",
    )(proj, proj, hist, conv_w)


def _gdn_chunk_kernel(q_ref, k_ref, v_ref, z_ref, gl_ref, al_ref, dt_ref, nw_ref, s0_ref, _dst_ref,
                      o_ref, s_ref, *, chunk, t_valid, hpg):
    ci = pl.program_id(2)
    hg = pl.program_id(1)

    @pl.when(ci == 0)
    def _():
        s_ref[...] = s0_ref[...]

    size = chunk
    rows = _iota2((size, size), 0)
    cols = _iota2((size, size), 1)
    eye = rows == cols
    incl = cols <= rows
    strict = cols < rows
    valid = (ci * size + _iota2((size, 1), 0)) < t_valid
    logits = gl_ref[...]
    beta_all = _sigmoid(logits)
    g_all = -jnp.exp(al_ref[...]) * _softplus(logits + dt_ref[...])
    lane = _iota2((size, LANES), 1)
    rep = GD_V_HEADS // GD_K_HEADS
    kheads = range(hpg // rep)
    heads = range(hpg)
    q = [q_ref[:, kh * GD_DK:(kh + 1) * GD_DK] for kh in kheads]
    k = [k_ref[:, kh * GD_DK:(kh + 1) * GD_DK] for kh in kheads]
    qk = [_dot_nt(q[kh], k[kh]) for kh in kheads]
    kk = [_dot_nt(k[kh], k[kh]) for kh in kheads]
    vsl = [slice(hl * GD_DV, (hl + 1) * GD_DV) for hl in heads]
    s = [s_ref[0, hl] for hl in heads]
    beta, gc, decay = [], [], []
    for hl in heads:
        head = hg * hpg + hl
        beta_h = jnp.sum(jnp.where(lane == head, beta_all, 0.0), axis=1, keepdims=True)
        g = jnp.sum(jnp.where(lane == head + GD_V_HEADS, g_all, 0.0), axis=1, keepdims=True)
        beta_h = jnp.where(valid, beta_h, 0.0)
        g = jnp.where(valid, g, 0.0)
        g_row = _col_to_row(g, eye)
        gc_h = jnp.sum(jnp.where(incl, g_row, 0.0), axis=1, keepdims=True)
        gc_row = jnp.sum(jnp.where(rows <= cols, g, 0.0), axis=0, keepdims=True)
        beta.append(beta_h)
        gc.append(gc_h)
        decay.append(jnp.where(incl, jnp.exp(jnp.where(incl, gc_h - gc_row, 0.0)), 0.0))
    qs = [_dot(q[hl // rep] * jnp.exp(gc[hl]), s[hl]) for hl in heads]
    t_inv = _neumann_inverse_many(
        [-jnp.where(strict, kk[hl // rep] * beta[hl] * decay[hl], 0.0) for hl in heads], size)
    uw = [_dot(t_inv[hl], jnp.concatenate([v_ref[:, vsl[hl]] * beta[hl],
                                           k[hl // rep] * (beta[hl] * jnp.exp(gc[hl]))], axis=1))
          for hl in heads]
    ws = [_dot(uw[hl][:, GD_DV:], s[hl]) for hl in heads]
    v_new = [uw[hl][:, :GD_DV] - ws[hl] for hl in heads]
    av = [_dot(qk[hl // rep] * decay[hl], v_new[hl]) for hl in heads]
    g_last = [gc[hl][size - 1:size, :] for hl in heads]
    kv = [_dot_tn(k[hl // rep] * jnp.exp(g_last[hl] - gc[hl]), v_new[hl]) for hl in heads]
    for hl in heads:
        s_ref[0, hl] = s[hl] * jnp.exp(g_last[hl]) + kv[hl]
        o = qs[hl] + av[hl]
        o = o * lax.rsqrt(jnp.mean(o * o, axis=-1, keepdims=True) + GD_EPS) * nw_ref[...]
        z = z_ref[:, vsl[hl]]
        o_ref[:, vsl[hl]] = (o * (z * _sigmoid(z))).astype(o_ref.dtype)


def gdn_chunk(qkv, proj, gates, dst, row0, bsz, t_pad, t_valid, chunk, s0, a_log, dt_bias, norm_w, hpg):
    nc = t_pad // chunk
    hgs = GD_V_HEADS // hpg
    rep = GD_V_HEADS // GD_K_HEADS
    kw = (hpg // rep) * GD_DK
    vw = hpg * GD_DV
    blk0 = row0 // chunk
    pad_row = lambda x: jnp.pad(x, (GD_V_HEADS, LANES - 2 * GD_V_HEADS)).reshape(1, LANES)
    st = pl.BlockSpec((1, hpg, GD_DK, GD_DV), lambda b, g, c: (b, g, 0, 0))
    vec = pl.BlockSpec((1, LANES), lambda b, g, c: (0, 0))
    return pl.pallas_call(
        functools.partial(_gdn_chunk_kernel, chunk=chunk, t_valid=t_valid, hpg=hpg),
        grid=(bsz, hgs, nc),
        in_specs=[pl.BlockSpec((chunk, kw), lambda b, g, c: (b * nc + c, g)),
                  pl.BlockSpec((chunk, kw), lambda b, g, c: (b * nc + c, GD_QK // kw + g)),
                  pl.BlockSpec((chunk, vw), lambda b, g, c: (b * nc + c, 2 * GD_QK // vw + g)),
                  pl.BlockSpec((chunk, vw), lambda b, g, c: (blk0 + b * nc + c, GD_CONV_DIM // vw + g)),
                  pl.BlockSpec((chunk, LANES), lambda b, g, c: (blk0 + b * nc + c, 0)),
                  vec, vec, vec, st, pl.BlockSpec(memory_space=pl.ANY)],
        out_specs=[pl.BlockSpec((chunk, vw), lambda b, g, c: (blk0 + b * nc + c, g)), st],
        out_shape=[jax.ShapeDtypeStruct(dst.shape, dst.dtype),
                   jax.ShapeDtypeStruct((bsz, GD_V_HEADS, GD_DK, GD_DV), F32)],
        input_output_aliases={9: 0},
        compiler_params=_cparams(("arbitrary", "arbitrary", "arbitrary")),
        name="gdn_chunk",
    )(qkv, qkv, qkv, proj, gates, pad_row(a_log), pad_row(dt_bias), norm_w.reshape(1, GD_DV), s0, dst)


MOE_TM = 256


def _router_kernel(x_ref, w_ref, wt_ref, id_ref):
    logits = jnp.dot(x_ref[...], w_ref[...].astype(BF16), preferred_element_type=F32)
    lane = _iota2(logits.shape, 1).astype(F32)
    first_of = lambda hit: jnp.min(jnp.where(hit, lane, float(LANES)), axis=1, keepdims=True)
    gl = jnp.where(lane < MOE_GROUPS, logits, NEG)
    gmax = jnp.max(gl, axis=1, keepdims=True)
    g_val = 1.0 / jnp.sum(jnp.exp(gl - gmax), axis=1, keepdims=True)
    lo = MOE_GROUPS + first_of(gl == gmax) * MOE_PER_GROUP
    vals = jnp.where((lane >= lo) & (lane < lo + MOE_PER_GROUP), logits, NEG)
    top1 = jnp.max(vals, axis=1, keepdims=True)
    i1 = first_of(vals == top1)
    vals2 = jnp.where(lane == i1, NEG, vals)
    top2 = jnp.max(vals2, axis=1, keepdims=True)
    i2 = first_of(vals2 == top2)
    e2 = jnp.exp(top2 - top1)
    w1 = (1.0 / (1.0 + e2)) * g_val
    w2 = (e2 / (1.0 + e2)) * g_val
    wt_ref[...] = jnp.where(lane == 0, w1, jnp.where(lane == 1, w2, 0.0))
    ids = jnp.where(lane == 0, i1 - MOE_GROUPS, jnp.where(lane == 1, i2 - MOE_GROUPS, 0.0))
    id_ref[...] = ids.astype(jnp.int32)


def moe_router(xb, w_router, tm=512):
    n, d = xb.shape
    out = pl.BlockSpec((tm, LANES), lambda i: (i, 0))
    return pl.pallas_call(
        _router_kernel,
        grid=(n // tm,),
        in_specs=[pl.BlockSpec((tm, d), lambda i: (i, 0)), pl.BlockSpec((d, LANES), lambda i: (0, 0))],
        out_specs=[out, out],
        out_shape=[jax.ShapeDtypeStruct((n, LANES), F32), jax.ShapeDtypeStruct((n, LANES), jnp.int32)],
        compiler_params=_cparams(("arbitrary",)),
        name="moe_router",
    )(xb, w_router)


def _moe_ffn_kernel(te_ref, tv_ref, idx_ref, idx_next_ref, x_hbm, wg_ref, wu_ref, wd_ref, o_ref,
                    xbuf, xcat, wgb, wub, wdb, sem):
    i = pl.program_id(0)
    n_tiles = pl.num_programs(0)
    slot = i % 2
    pieces = xcat.shape[1] // LANES

    def gather(rows_ref, into):
        for r in range(MOE_TM):
            pltpu.make_async_copy(x_hbm.at[pl.ds(rows_ref[0, 0, r], pieces)],
                                  xbuf.at[into, pl.ds(r * pieces, pieces)], sem.at[into]).start(priority=r % 2)

    @pl.when((i == 0) & (tv_ref[0] == 1))
    def _():
        gather(idx_ref, 0)

    nxt = jnp.minimum(i + 1, n_tiles - 1)

    @pl.when((i + 1 < n_tiles) & (tv_ref[nxt] == 1))
    def _():
        gather(idx_next_ref, 1 - slot)

    @pl.when(tv_ref[i] == 1)
    def _():
        pltpu.make_async_copy(xbuf.at[slot], xbuf.at[slot], sem.at[slot]).wait()

        @pl.when((i == 0) | (te_ref[i] != te_ref[jnp.maximum(i - 1, 0)]))
        def _():
            wgb[...] = wg_ref[0, 0].astype(BF16)
            wub[...] = wu_ref[0, 0].astype(BF16)
            wdb[...] = wd_ref[0, 0].astype(BF16)

        for j in range(pieces):
            xcat[:, j * LANES:(j + 1) * LANES] = xbuf[slot, pl.ds(j, MOE_TM, stride=pieces), :].astype(BF16)
        x = xcat[...]
        gate = jnp.dot(x, wgb[...], preferred_element_type=F32)
        up = jnp.dot(x, wub[...], preferred_element_type=F32)
        hid = gate * _sigmoid(gate) * up
        o_ref[...] = jnp.dot(hid.astype(BF16), wdb[...], preferred_element_type=F32)

    @pl.when(tv_ref[i] == 0)
    def _():
        o_ref[...] = jnp.zeros_like(o_ref)


def moe_ffn(x_rows, row_start, tile_expert, tile_valid, layer, w_gate, w_up, w_down):
    n_tiles = row_start.shape[0]
    d = w_gate.shape[2]
    ff = w_gate.shape[3]
    pieces = d // LANES
    grid_spec = pltpu.PrefetchScalarGridSpec(
        num_scalar_prefetch=2,
        grid=(n_tiles,),
        in_specs=[pl.BlockSpec((1, 1, MOE_TM), lambda i, te, tv: (i, 0, 0), memory_space=pltpu.SMEM),
                  pl.BlockSpec((1, 1, MOE_TM), lambda i, te, tv: (jnp.minimum(i + 1, n_tiles - 1), 0, 0),
                               memory_space=pltpu.SMEM),
                  pl.BlockSpec(memory_space=pl.ANY),
                  pl.BlockSpec((1, 1, d, ff), lambda i, te, tv: (layer, te[i], 0, 0)),
                  pl.BlockSpec((1, 1, d, ff), lambda i, te, tv: (layer, te[i], 0, 0)),
                  pl.BlockSpec((1, 1, ff, d), lambda i, te, tv: (layer, te[i], 0, 0))],
        out_specs=pl.BlockSpec((MOE_TM, d), lambda i, te, tv: (i, 0)),
        scratch_shapes=[pltpu.VMEM((2, MOE_TM * pieces, LANES), F32), pltpu.VMEM((MOE_TM, d), BF16),
                        pltpu.VMEM((d, ff), BF16), pltpu.VMEM((d, ff), BF16), pltpu.VMEM((ff, d), BF16),
                        pltpu.SemaphoreType.DMA((2,))],
    )
    return pl.pallas_call(
        _moe_ffn_kernel,
        grid_spec=grid_spec,
        out_shape=jax.ShapeDtypeStruct((n_tiles * MOE_TM, d), F32),
        compiler_params=_cparams(("arbitrary",)),
        name="moe_ffn",
    )(tile_expert, tile_valid, row_start, row_start, x_rows, w_gate, w_up, w_down)


def _ln_moe_kernel(x_ref, ya_ref, yb_ref, wt_ref, g_ref, b_ref, o_ref, ob_ref):
    wt = wt_ref[...]
    z = DEEPNORM_ALPHA * x_ref[...] + (wt[:, 0:1] * ya_ref[...] + wt[:, 1:2] * yb_ref[...])
    zc = z - jnp.mean(z, axis=-1, keepdims=True)
    var = jnp.mean(zc * zc, axis=-1, keepdims=True)
    out = zc * lax.rsqrt(var + LN_EPS) * g_ref[...] + b_ref[...]
    o_ref[...] = out
    ob_ref[...] = out.astype(BF16)


def ln_residual_moe(x, ya, yb, wt, g, b, tm=256):
    m, d = x.shape
    row = pl.BlockSpec((tm, d), lambda i: (i, 0))
    vec = pl.BlockSpec((1, d), lambda i: (0, 0))
    return pl.pallas_call(
        _ln_moe_kernel,
        grid=(m // tm,),
        in_specs=[row, row, row, pl.BlockSpec((tm, LANES), lambda i: (i, 0)), vec, vec],
        out_specs=[row, row],
        out_shape=[jax.ShapeDtypeStruct((m, d), F32), jax.ShapeDtypeStruct((m, d), BF16)],
        compiler_params=_cparams(("arbitrary",)),
        name="ln_residual_moe",
    )(x, ya, yb, wt, g.reshape(1, d), b.reshape(1, d))


def _table_lookup(table, idx):
    hit = idx[:, None] == jnp.arange(table.shape[0], dtype=idx.dtype)[None, :]
    return jnp.sum(jnp.where(hit, table[None, :], 0), axis=1)


def hier_moe(x_rows, xb, layer, w_group, w_expert, w_gate, w_up, w_down):
    n = xb.shape[0]
    assert (n * MOE_TOPK) % MOE_TM == 0
    w_router = jnp.pad(jnp.concatenate([w_group, w_expert], axis=1),
                       ((0, 0), (0, LANES - MOE_GROUPS - MOE_EXPERTS)))
    wt, ids = moe_router(xb, w_router)

    n_pairs = n * MOE_TOPK
    n_tiles = -(-n_pairs // MOE_TM) + MOE_EXPERTS
    n_rows = n_tiles * MOE_TM
    i32 = jnp.int32
    eid = ids[:, :MOE_TOPK].reshape(n_pairs)
    order = jnp.argsort(eid, stable=True).astype(i32)
    rank = jnp.argsort(order).astype(i32)
    experts = jnp.arange(MOE_EXPERTS, dtype=i32)
    counts = jnp.sum((eid[:, None] == experts[None, :]).astype(i32), axis=0)
    starts = jnp.cumsum(counts) - counts
    tiles_per = (counts + MOE_TM - 1) // MOE_TM
    tile_starts = jnp.cumsum(tiles_per) - tiles_per
    dest = _table_lookup(tile_starts * MOE_TM - starts, eid) + rank
    tile_idx = jnp.arange(n_tiles, dtype=i32)
    tile_expert = jnp.sum((tile_idx[:, None] >= (tile_starts + tiles_per)[None, :]).astype(i32), axis=1)
    tile_valid = (tile_expert < MOE_EXPERTS).astype(i32)
    last_used = jnp.max(jnp.where(tiles_per > 0, experts, 0))
    tile_expert = jnp.where(tile_valid == 1, tile_expert, last_used)
    off = (tile_idx - _table_lookup(tile_starts, tile_expert))[:, None] * MOE_TM + jnp.arange(MOE_TM, dtype=i32)[None, :]
    cnt = _table_lookup(counts, tile_expert)[:, None]
    src = jnp.clip(_table_lookup(starts, tile_expert)[:, None] + off, 0, n_pairs - 1)
    row_ok = (off < cnt) & (tile_valid[:, None] == 1)
    picked = jnp.take(order, src.reshape(n_rows), mode='clip').reshape(n_tiles, MOE_TM)
    pieces = xb.shape[1] // LANES
    row_start = (jnp.where(row_ok, picked // MOE_TOPK, 0) * pieces).reshape(n_tiles, 1, MOE_TM)

    y_sorted = moe_ffn(x_rows, row_start, tile_expert, tile_valid, layer, w_gate, w_up, w_down)
    dest2 = dest.reshape(n, MOE_TOPK)
    return (jnp.take(y_sorted, dest2[:, 0], axis=0, mode='clip'),
            jnp.take(y_sorted, dest2[:, 1], axis=0, mode='clip'), wt)


SAMPLE_T_PAD = 8


def kernel(x_prompt, x_sample, state_mlstm_c, state_mlstm_n, state_mlstm_m, state_rwkv_s,
           state_rwkv_shift, state_gdn_s, state_gdn_conv, p_prompt, p_sample,
           w_in_ab, ml_b_i, ml_b_f, ml_norm_w, rw_mu, rw_w0, rw_w2, rw_a0, rw_a2, rw_g2,
           rw_k_k, rw_k_a, rw_r_k, rw_ln_w, rw_ln_b, w_out_ab,
           gd_w_in, gd_conv_w, gd_a_log, gd_dt_bias, gd_norm_w, gd_w_out,
           ln_mix_g, ln_mix_b, moe_w_group, moe_w_expert, moe_w_gate, moe_w_up, moe_w_down,
           ln_ffn_g, ln_ffn_b, ple_w_gate, ple_w_proj):
    pb, pt, d = x_prompt.shape
    sb, st, _ = x_sample.shape
    n_p = pb * pt
    n_s = sb * SAMPLE_T_PAD
    n = n_p + n_s
    pad_t = lambda a: jnp.pad(a, ((0, 0), (0, SAMPLE_T_PAD - st), (0, 0)))
    merge = lambda a_p, a_s: jnp.concatenate(
        [a_p.reshape(n_p, a_p.shape[-1]), pad_t(a_s).reshape(n_s, a_s.shape[-1])], axis=0)
    x = merge(x_prompt, x_sample)
    xb = x.astype(BF16)
    segs = (dict(row0=0, bsz=pb, t_pad=pt, t_valid=pt, rw_hb=16, gd_hpg=16),
            dict(row0=n_p, bsz=sb, t_pad=SAMPLE_T_PAD, t_valid=st, rw_hb=16, gd_hpg=16))
    zeros = lambda *shape: jnp.zeros(shape, F32)

    def last_rows(a, sg, count):
        assert sg['t_valid'] >= count
        seqs = a[sg['row0']:sg['row0'] + sg['bsz'] * sg['t_pad']].reshape(sg['bsz'], sg['t_pad'], a.shape[-1])
        return seqs[:, sg['t_valid'] - count:sg['t_valid']]

    new_states = [[[] for _ in range(7)] for _ in segs]
    for layer in range(DEPTH):
        li = layer // 2
        if layer % 2 == 0:
            w_in_t = jnp.transpose(w_in_ab[li])
            proj_ml = matmul_wt(xb, w_in_t, 0, 3584, 896)
            proj_rw = matmul_wt(xb, w_in_t, ML_COLS, RW_COLS, RW_COLS // 2)
            rw_prm = dict(mu=rw_mu[li], w0=rw_w0[li], w2=rw_w2[li], a0=rw_a0[li], a2=rw_a2[li], g2=rw_g2[li],
                          k_k=rw_k_k[li], k_a=rw_k_a[li], r_k=rw_r_k[li])
            heads_out = jnp.zeros((n, ML_V + RW_WIDTH), BF16)
            for si, sg in enumerate(segs):
                bsz, t_pad, t_valid, row0 = sg['bsz'], sg['t_pad'], sg['t_valid'], sg['row0']
                if si == 0:
                    c0, n0, m0 = zeros(bsz, ML_HEADS, ML_DK, ML_DV), zeros(bsz, ML_HEADS, ML_DK), zeros(bsz, ML_HEADS)
                    s0, sh0 = zeros(bsz, RW_HEADS, RW_DH, RW_DH), zeros(bsz, RW_COLS)
                else:
                    c0, n0, m0 = state_mlstm_c[li], state_mlstm_n[li], state_mlstm_m[li]
                    s0, sh0 = state_rwkv_s[li], state_rwkv_shift[li]
                heads_out, c, nn, m = mlstm_group(proj_ml, heads_out, row0, bsz, t_pad, t_valid,
                                                  min(ML_CHUNK, t_pad), c0, n0, m0,
                                                  ml_b_i[li], ml_b_f[li], ml_norm_w[li])
                prep = rwkv_prep(proj_rw, row0, bsz, t_pad, sh0, rw_prm, 256)
                heads_out, rs = rwkv_chunk(prep, heads_out, ML_V, row0, bsz, t_pad, t_valid,
                                           min(RW_CHUNK, t_pad), s0, rw_ln_w[li], rw_ln_b[li], sg['rw_hb'])
                new_shift = last_rows(proj_rw, sg, 1)[:, 0]
                for slot, val in zip(range(5), (c, nn, m, rs, new_shift)):
                    new_states[si][slot].append(val)
            mix = matmul(heads_out, w_out_ab[li], d, 1024)
        else:
            w_in_t = jnp.transpose(gd_w_in[li])
            n_qkvz = GD_CONV_DIM + GD_V_WIDTH
            proj = matmul_wt(xb, w_in_t, 0, n_qkvz, 1024)
            gates = matmul_wt(xb, w_in_t, n_qkvz, LANES, LANES)
            gd_out = jnp.zeros((n, GD_V_WIDTH), BF16)
            for si, sg in enumerate(segs):
                bsz, t_pad, t_valid, row0 = sg['bsz'], sg['t_pad'], sg['t_valid'], sg['row0']
                if si == 0:
                    s0, buf0 = zeros(bsz, GD_V_HEADS, GD_DK, GD_DV), zeros(bsz, GD_CONV - 1, GD_CONV_DIM)
                else:
                    s0, buf0 = state_gdn_s[li], state_gdn_conv[li]
                qkv = gdn_prep(proj, row0, bsz, t_pad, buf0, gd_conv_w[li], 256)
                gd_out, gs = gdn_chunk(qkv, proj, gates, gd_out, row0, bsz, t_pad, t_valid,
                                       min(GD_CHUNK, t_pad), s0, gd_a_log[li], gd_dt_bias[li],
                                       gd_norm_w[li], sg['gd_hpg'])
                new_states[si][5].append(gs)
                new_states[si][6].append(last_rows(proj, sg, GD_CONV - 1)[:, :, :GD_CONV_DIM])
            mix = matmul(gd_out, gd_w_out[li], d, 512)
        x, xb, x_rows = ln_residual(x, mix, ln_mix_g[layer], ln_mix_b[layer])
        ya, yb, wt = hier_moe(x_rows, xb, layer, moe_w_group[layer], moe_w_expert[layer],
                              moe_w_gate, moe_w_up, moe_w_down)
        x, xb = ln_residual_moe(x, ya, yb, wt, ln_ffn_g[layer], ln_ffn_b[layer])
        x, xb = ple(xb, merge(p_prompt[layer], p_sample[layer]), x, layer, ple_w_gate, ple_w_proj)
    y_prompt = x[:n_p].reshape(pb, pt, d)
    y_sample = x[n_p:].reshape(sb, SAMPLE_T_PAD, d)[:, :st]
    stack = lambda vals: jnp.stack(vals)
    return (y_prompt, y_sample) + tuple(stack(v) for v in new_states[0]) + tuple(stack(v) for v in new_states[1])
```

```python
import functools

import jax
import jax.numpy as jnp
import numpy as np
from jax import lax
from jax.experimental import pallas as pl
from jax.experimental.pallas import tpu as pltpu

F32 = jnp.float32
BF16 = jnp.bfloat16

D_MODEL = 2048
DEPTH = 2
DEEPNORM_ALPHA = (2 * DEPTH) ** 0.25
LN_EPS = 1e-5
MIX_HALF = D_MODEL // 2
ML_HEADS = 4
ML_DV = MIX_HALF // ML_HEADS
ML_DK = ML_DV // 2
ML_CHUNK = 64
ML_GATE_CAP = 15.0
ML_NORM_EPS = 1e-6
ML_QK = ML_HEADS * ML_DK
ML_V = ML_HEADS * ML_DV
ML_COLS = 2 * ML_QK + 2 * ML_V + 2 * ML_HEADS
RW_DH = 64
RW_HEADS = MIX_HALF // RW_DH
RW_WIDTH = RW_HEADS * RW_DH
RW_DECAY_LORA = 64
RW_A_LORA = 64
RW_GATE_LORA = 128
RW_GN_EPS = 64e-5
RW_COLS = 3 * RW_WIDTH + RW_DECAY_LORA + RW_A_LORA + RW_GATE_LORA
RW_CHUNK = 64
GD_DK = 128
GD_DV = 128
GD_K_HEADS = D_MODEL // GD_DK
GD_V_HEADS = 2 * GD_K_HEADS
GD_CONV = 4
GD_CHUNK = 64
GD_EPS = 1e-6
GD_QK = GD_K_HEADS * GD_DK
GD_V_WIDTH = GD_V_HEADS * GD_DV
GD_CONV_DIM = 2 * GD_QK + GD_V_WIDTH
MOE_GROUPS = 4
MOE_PER_GROUP = 8
MOE_EXPERTS = MOE_GROUPS * MOE_PER_GROUP
MOE_TOPK = 2
MOE_FF = D_MODEL // 4
PLE_DIM = 256

LANES = 128
SUBLANES = 8
NEG = -1e30
VMEM_LIMIT = 56 * 1024 * 1024


def _cparams(sem):
    return pltpu.CompilerParams(dimension_semantics=sem, vmem_limit_bytes=VMEM_LIMIT)


def _dot(a, b):
    return jnp.dot(a.astype(BF16), b.astype(BF16), preferred_element_type=F32)


def _dot_nt(a, b):
    return lax.dot_general(a.astype(BF16), b.astype(BF16), (((1,), (1,)), ((), ())),
                           preferred_element_type=F32)


def _dot_tn(a, b):
    return lax.dot_general(a.astype(BF16), b.astype(BF16), (((0,), (0,)), ((), ())),
                           preferred_element_type=F32)


def _split3(x):
    hi = x.astype(BF16)
    r1 = x - hi.astype(F32)
    mid = r1.astype(BF16)
    lo = (r1 - mid.astype(F32)).astype(BF16)
    return hi, mid, lo


def _dot_exact_rhs(a, b01):
    hi, mid, lo = _split3(a)
    b = b01.astype(BF16)
    return (jnp.dot(hi, b, preferred_element_type=F32) + jnp.dot(mid, b, preferred_element_type=F32)
            + jnp.dot(lo, b, preferred_element_type=F32))


def _dot_exact_lhs(a01, b):
    hi, mid, lo = _split3(b)
    a = a01.astype(BF16)
    return (jnp.dot(a, hi, preferred_element_type=F32) + jnp.dot(a, mid, preferred_element_type=F32)
            + jnp.dot(a, lo, preferred_element_type=F32))


def _iota2(shape, dim):
    return lax.broadcasted_iota(jnp.int32, shape, dim)


def _col_to_row(col, eye):
    return jnp.sum(jnp.where(eye, col, 0.0), axis=0, keepdims=True)


def _sigmoid(x):
    return 1.0 / (1.0 + jnp.exp(-x))


def _softplus(x):
    return jnp.maximum(x, 0.0) + jnp.log1p(jnp.exp(-jnp.abs(x)))


def _neumann_inverse(n_mat, size):
    eye = (_iota2((size, size), 0) == _iota2((size, size), 1)).astype(F32)
    t = eye + n_mat
    x = n_mat
    steps = max(int(np.ceil(np.log2(size))) - 1, 0)
    for _ in range(steps):
        x = _dot(x, x)
        t = t + _dot(t, x)
    return t


def _mm_kernel(x_ref, w_ref, o_ref, wb_ref, *, n_valid):
    @pl.when(pl.program_id(1) == 0)
    def _():
        wb_ref[...] = w_ref[...].astype(BF16)

    acc = jnp.dot(x_ref[...].astype(BF16), wb_ref[...], preferred_element_type=F32)
    if n_valid is not None:
        acc = jnp.where(_iota2(acc.shape, 1) < n_valid, acc, 0.0)
    o_ref[...] = acc.astype(o_ref.dtype)


def _row_tile(m, want):
    while m % want:
        want //= 2
    return want


def matmul(x, w, n_out, tn, tm=1024, out_dtype=F32, col0=0):
    m, k = x.shape
    tm = _row_tile(m, tm)
    assert w.shape[0] == k and n_out % tn == 0 and m % tm == 0 and tn % LANES == 0 and col0 % tn == 0
    n_valid = None
    if col0 + n_out > w.shape[1]:
        assert n_out == tn
        n_valid = w.shape[1] - col0
    blk0 = col0 // tn
    return pl.pallas_call(
        functools.partial(_mm_kernel, n_valid=n_valid),
        grid=(n_out // tn, m // tm),
        in_specs=[pl.BlockSpec((tm, k), lambda j, i: (i, 0)),
                  pl.BlockSpec((k, tn), lambda j, i: (0, blk0 + j))],
        out_specs=pl.BlockSpec((tm, tn), lambda j, i: (i, j)),
        out_shape=jax.ShapeDtypeStruct((m, n_out), out_dtype),
        scratch_shapes=[pltpu.VMEM((k, tn), BF16)],
        compiler_params=_cparams(("arbitrary", "arbitrary")),
        name="matmul",
    )(x, w)


def _mm_wt_kernel(x_ref, wt_hbm, o_ref, wf_ref, wb_ref, sem, *, row0, tn, n_rows):
    j = pl.program_id(0)

    @pl.when(pl.program_id(1) == 0)
    def _():
        if n_rows < tn:
            wf_ref[...] = jnp.zeros_like(wf_ref)
        start = pl.multiple_of(row0 + j * tn, SUBLANES)
        copy = pltpu.make_async_copy(wt_hbm.at[pl.ds(start, n_rows)], wf_ref.at[pl.ds(0, n_rows)], sem)
        copy.start()
        copy.wait()
        k = wf_ref.shape[1]
        step = 2 * LANES
        for c in range(k // step):
            wb_ref[c * step:(c + 1) * step, :] = wf_ref[:, c * step:(c + 1) * step].T.astype(BF16)

    o_ref[...] = jnp.dot(x_ref[...].astype(BF16), wb_ref[...], preferred_element_type=F32)


def matmul_wt(x, wt, row0, n_out, tn, tm=1024):
    m, k = x.shape
    tm = _row_tile(m, tm)
    assert wt.shape[1] == k and n_out % tn == 0 and tn % LANES == 0 and row0 % SUBLANES == 0
    n_rows = tn
    if row0 + n_out > wt.shape[0]:
        assert n_out == tn
        n_rows = wt.shape[0] - row0
    assert n_rows % SUBLANES == 0
    return pl.pallas_call(
        functools.partial(_mm_wt_kernel, row0=row0, tn=tn, n_rows=n_rows),
        grid=(n_out // tn, m // tm),
        in_specs=[pl.BlockSpec((tm, k), lambda j, i: (i, 0)), pl.BlockSpec(memory_space=pl.ANY)],
        out_specs=pl.BlockSpec((tm, tn), lambda j, i: (i, j)),
        out_shape=jax.ShapeDtypeStruct((m, n_out), F32),
        scratch_shapes=[pltpu.VMEM((tn, k), F32), pltpu.VMEM((k, tn), BF16), pltpu.SemaphoreType.DMA(())],
        compiler_params=_cparams(("arbitrary", "arbitrary")),
        name="matmul_wt",
    )(x, wt)


def _ln_res_kernel(x_ref, y_ref, g_ref, b_ref, o_ref, ob_ref, orow_ref):
    z = DEEPNORM_ALPHA * x_ref[...] + y_ref[...]
    zc = z - jnp.mean(z, axis=-1, keepdims=True)
    var = jnp.mean(zc * zc, axis=-1, keepdims=True)
    out = zc * lax.rsqrt(var + LN_EPS) * g_ref[...] + b_ref[...]
    o_ref[...] = out
    ob_ref[...] = out.astype(BF16)
    tm, d = out.shape
    pieces = d // LANES
    for j in range(pieces):
        orow_ref[pl.ds(j, tm, stride=pieces), :] = out[:, j * LANES:(j + 1) * LANES]


def ln_residual(x, y, g, b, tm=256):
    m, d = x.shape
    pieces = d // LANES
    row = pl.BlockSpec((tm, d), lambda i: (i, 0))
    vec = pl.BlockSpec((1, d), lambda i: (0, 0))
    return pl.pallas_call(
        _ln_res_kernel,
        grid=(m // tm,),
        in_specs=[row, row, vec, vec],
        out_specs=[row, row, pl.BlockSpec((tm * pieces, LANES), lambda i: (i, 0))],
        out_shape=[jax.ShapeDtypeStruct((m, d), F32), jax.ShapeDtypeStruct((m, d), BF16),
                   jax.ShapeDtypeStruct((m * pieces, LANES), F32)],
        compiler_params=_cparams(("arbitrary",)),
        name="ln_residual",
    )(x, y, g.reshape(1, d), b.reshape(1, d))


def _ple_kernel(xb_ref, p_ref, x_ref, wg_ref, wp_ref, o_ref, ob_ref, wgb_ref, wpb_ref):
    @pl.when(pl.program_id(1) == 0)
    def _():
        wgb_ref[...] = wg_ref[0].astype(BF16)
        wpb_ref[...] = wp_ref[0].astype(BF16)

    gate = _sigmoid(jnp.dot(xb_ref[...], wgb_ref[...], preferred_element_type=F32))
    emb = jnp.dot(p_ref[...].astype(BF16), wpb_ref[...], preferred_element_type=F32)
    out = x_ref[...] + gate * emb
    o_ref[...] = out
    ob_ref[...] = out.astype(BF16)


def ple(xb, p, x, layer, wg, wp, tm=1024, tn=512):
    m, d = x.shape
    pd = p.shape[1]
    tm = _row_tile(m, tm)
    return pl.pallas_call(
        _ple_kernel,
        grid=(d // tn, m // tm),
        in_specs=[pl.BlockSpec((tm, d), lambda j, i: (i, 0)),
                  pl.BlockSpec((tm, pd), lambda j, i: (i, 0)),
                  pl.BlockSpec((tm, tn), lambda j, i: (i, j)),
                  pl.BlockSpec((1, d, tn), lambda j, i: (layer, 0, j)),
                  pl.BlockSpec((1, pd, tn), lambda j, i: (layer, 0, j))],
        out_specs=[pl.BlockSpec((tm, tn), lambda j, i: (i, j))] * 2,
        out_shape=[jax.ShapeDtypeStruct((m, d), F32), jax.ShapeDtypeStruct((m, d), BF16)],
        scratch_shapes=[pltpu.VMEM((d, tn), BF16), pltpu.VMEM((pd, tn), BF16)],
        compiler_params=_cparams(("arbitrary", "arbitrary")),
        name="ple",
    )(xb, p, x, wg, wp)


def _mlstm_kernel(q_ref, k_ref, v_ref, o_ref, g_ref, c0_ref, n0_ref, m0_ref, nw_ref, gb_ref, _dst_ref,
                  h_ref, c_ref, n_ref, m_ref, *, chunk, t_valid, nb):
    ci = pl.program_id(1)

    @pl.when(ci == 0)
    def _():
        c_ref[...] = c0_ref[...]
        n_ref[...] = n0_ref[...]
        m_ref[...] = m0_ref[...]

    size = chunk
    rows = _iota2((size, size), 0)
    cols = _iota2((size, size), 1)
    eye = rows == cols
    causal = cols <= rows
    gates = g_ref[:, 0:2 * ML_HEADS] + gb_ref[...]
    capped = ML_GATE_CAP * jnp.tanh(gates / ML_GATE_CAP)
    valid = (ci * size + (_iota2((nb * size, 1), 0) & (size - 1))) < t_valid
    ipre_all = jnp.where(valid, capped[:, 0:ML_HEADS], NEG)
    logf_all = jnp.where(valid, -_softplus(-capped[:, ML_HEADS:2 * ML_HEADS]), 0.0)
    units = [(sq, h) for sq in range(nb) for h in range(ML_HEADS)]
    rs = [slice(sq * size, (sq + 1) * size) for sq, _ in units]
    ksl = [slice(h * ML_DK, (h + 1) * ML_DK) for _, h in units]
    vsl = [slice(h * ML_DV, (h + 1) * ML_DV) for _, h in units]
    ids = range(len(units))
    q = [q_ref[rs[u], ksl[u]] for u in ids]
    k = [k_ref[rs[u], ksl[u]] * (ML_DK ** -0.5) for u in ids]
    v = [v_ref[rs[u], vsl[u]] for u in ids]
    qk = [_dot_nt(q[u], k[u]) for u in ids]
    c_prev = [c_ref[sq, h] for sq, h in units]
    n_prev = [n_ref[sq, h:h + 1, :] for sq, h in units]
    qc = [_dot(q[u], c_prev[u]) for u in ids]
    s, w_inter, m_t, kw, decay = [], [], [], [], []
    for u, (sq, h) in enumerate(units):
        ig_col = ipre_all[rs[u], h:h + 1]
        lf_col = logf_all[rs[u], h:h + 1]
        ig_row = _col_to_row(ig_col, eye)
        lf_row = _col_to_row(lf_col, eye)
        b_col = jnp.sum(jnp.where(causal, lf_row, 0.0), axis=1, keepdims=True)
        b_row = jnp.sum(jnp.where(rows <= cols, lf_col, 0.0), axis=0, keepdims=True)
        d = jnp.where(causal, b_col - b_row + ig_row, NEG)
        inter = b_col + m_ref[sq, :, h:h + 1]
        m_h = jnp.maximum(inter, jnp.max(d, axis=1, keepdims=True))
        w_h = jnp.exp(inter - m_h)
        s.append(qk[u] * jnp.exp(d - m_h))
        b_last = b_col[size - 1:size, :]
        m_last = m_h[size - 1:size, :]
        w_last = jnp.exp(b_last - b_col + ig_col - m_last)
        kw.append(k[u] * w_last)
        w_inter.append(w_h)
        m_t.append(m_h)
        decay.append(w_h[size - 1:size, :])
        m_ref[sq, :, h:h + 1] = m_last
    sv = [_dot(s[u], v[u]) for u in ids]
    kv = [_dot_tn(kw[u], v[u]) for u in ids]
    for u, (sq, h) in enumerate(units):
        num = w_inter[u] * qc[u] + sv[u]
        den = (w_inter[u] * jnp.sum(q[u] * n_prev[u], axis=1, keepdims=True)
               + jnp.sum(s[u], axis=1, keepdims=True))
        hid = num / jnp.maximum(jnp.abs(den), jnp.exp(-m_t[u]))
        c_ref[sq, h] = decay[u] * c_prev[u] + kv[u]
        n_ref[sq, h:h + 1, :] = decay[u] * n_prev[u] + jnp.sum(kw[u], axis=0, keepdims=True)
        hid = hid * lax.rsqrt(jnp.mean(hid * hid, axis=-1, keepdims=True) + ML_NORM_EPS)
        hid = hid * nw_ref[:, vsl[u]]
        hid = hid * _sigmoid(o_ref[rs[u], vsl[u]])
        h_ref[rs[u], vsl[u]] = hid.astype(h_ref.dtype)


def mlstm_group(proj, dst, row0, bsz, t_pad, t_valid, chunk, c0, n0, m0, b_i, b_f, norm_w, nb=1):
    nc = t_pad // chunk
    rows_blk = nb * chunk
    blk0 = row0 // rows_blk
    assert row0 % rows_blk == 0 and t_pad % chunk == 0 and bsz % nb == 0 and (nb == 1 or nc == 1)
    assert chunk & (chunk - 1) == 0
    rmap = lambda c0_: (lambda b, c: (blk0 + b * nc + c, c0_))
    qk_w, v_w = ML_QK, ML_V
    in_specs = [
        pl.BlockSpec((rows_blk, qk_w), rmap(0)),
        pl.BlockSpec((rows_blk, qk_w), rmap(1)),
        pl.BlockSpec((rows_blk, v_w), rmap(1)),
        pl.BlockSpec((rows_blk, v_w), rmap(2)),
        pl.BlockSpec((rows_blk, LANES), rmap((2 * qk_w + 2 * v_w) // LANES)),
        pl.BlockSpec((nb, ML_HEADS, ML_DK, ML_DV), lambda b, c: (b, 0, 0, 0)),
        pl.BlockSpec((nb, ML_HEADS, ML_DK), lambda b, c: (b, 0, 0)),
        pl.BlockSpec((nb, 1, ML_HEADS), lambda b, c: (b, 0, 0)),
        pl.BlockSpec((1, v_w), lambda b, c: (0, 0)),
        pl.BlockSpec((1, 2 * ML_HEADS), lambda b, c: (0, 0)),
        pl.BlockSpec(memory_space=pl.ANY),
    ]
    out_specs = [
        pl.BlockSpec((rows_blk, v_w), lambda b, c: (blk0 + b * nc + c, 0)),
        pl.BlockSpec((nb, ML_HEADS, ML_DK, ML_DV), lambda b, c: (b, 0, 0, 0)),
        pl.BlockSpec((nb, ML_HEADS, ML_DK), lambda b, c: (b, 0, 0)),
        pl.BlockSpec((nb, 1, ML_HEADS), lambda b, c: (b, 0, 0)),
    ]
    out_shape = [
        jax.ShapeDtypeStruct(dst.shape, dst.dtype),
        jax.ShapeDtypeStruct((bsz, ML_HEADS, ML_DK, ML_DV), F32),
        jax.ShapeDtypeStruct((bsz, ML_HEADS, ML_DK), F32),
        jax.ShapeDtypeStruct((bsz, 1, ML_HEADS), F32),
    ]
    gate_bias = jnp.concatenate([b_i, b_f]).reshape(1, 2 * ML_HEADS)
    h, c, n, m = pl.pallas_call(
        functools.partial(_mlstm_kernel, chunk=chunk, t_valid=t_valid, nb=nb),
        grid=(bsz // nb, nc),
        in_specs=in_specs,
        out_specs=out_specs,
        out_shape=out_shape,
        input_output_aliases={10: 0},
        compiler_params=_cparams(("arbitrary", "arbitrary")),
        name="mlstm",
    )(proj, proj, proj, proj, proj, c0, n0, m0.reshape(bsz, 1, ML_HEADS), norm_w.reshape(1, v_w), gate_bias, dst)
    return h, c, n, m.reshape(bsz, ML_HEADS)


def _head_block_ones(width, head):
    idx = np.arange(width) // head
    return jnp.asarray((idx[:, None] == idx[None, :]).astype(np.float32), dtype=BF16)


def _rwkv_prep_kernel(cur_ref, tail_ref, first_ref, mu_ref, w0_ref, a0_ref, kk_ref, ka_ref, rk_ref,
                      w2_ref, a2_ref, g2_ref, blk_ref,
                      r_ref, wl_ref, k_ref, v_ref, an_ref, bb_ref, bonus_ref, g_ref,
                      *, tt, t_pad, multi):
    cur = cur_ref[...]
    rolled = pltpu.roll(cur, 1, 0)
    row = _iota2((tt, 1), 0)
    if multi:
        prev = jnp.where((row & (t_pad - 1)) == 0, first_ref[...], rolled)
    else:
        starts_seq = (pl.program_id(0) % (t_pad // tt)) == 0
        head = jnp.where(starts_seq, first_ref[0], tail_ref[SUBLANES - 1:SUBLANES, :])
        prev = jnp.where(row == 0, head, rolled)
    mixed = cur + (prev - cur) * mu_ref[...]
    w = RW_WIDTH
    r = mixed[:, 0:w]
    k = mixed[:, w:2 * w]
    v = mixed[:, 2 * w:3 * w]
    c0 = 3 * w
    wl = mixed[:, c0:c0 + RW_DECAY_LORA]
    al = mixed[:, c0 + RW_DECAY_LORA:c0 + RW_DECAY_LORA + RW_A_LORA]
    gl = mixed[:, c0 + RW_DECAY_LORA + RW_A_LORA:]
    w_log = -jnp.exp(-_softplus(-(w0_ref[...] + _dot(jnp.tanh(wl), w2_ref[...]))) - 0.5)
    a = _sigmoid(a0_ref[...] + _dot(al, a2_ref[...]))
    g = _dot(_sigmoid(gl), g2_ref[...])
    kk = k * kk_ref[...]
    blk = blk_ref[...]
    kkn = kk / jnp.maximum(jnp.sqrt(_dot_exact_rhs(kk * kk, blk)), 1e-12)
    k2 = k * (1.0 + (a - 1.0) * ka_ref[...])
    r_ref[...] = r
    wl_ref[...] = w_log
    k_ref[...] = k2
    v_ref[...] = v
    an_ref[...] = -kkn
    bb_ref[...] = kkn * a
    bonus_ref[...] = _dot_exact_rhs(r * k2 * rk_ref[...], blk) * v
    g_ref[...] = g


def rwkv_prep(proj, row0, bsz, t_pad, shift0, prm, tt):
    n = bsz * t_pad
    multi = tt > t_pad
    assert row0 % tt == 0 and n % tt == 0 and (tt % t_pad == 0 if multi else t_pad % tt == 0)
    blk0 = row0 // tt
    c = RW_COLS
    if multi:
        first = jnp.zeros((bsz, t_pad, c), F32).at[:, 0].set(shift0).reshape(n, c)
        first_spec = pl.BlockSpec((tt, c), lambda i: (i, 0))
    else:
        first = shift0.reshape(bsz, 1, c)
        first_spec = pl.BlockSpec((1, 1, c), lambda i: (i // (t_pad // tt), 0, 0))
    tail_blk = tt // SUBLANES
    vec = lambda width: pl.BlockSpec((1, width), lambda i: (0, 0))
    full = lambda a, b: pl.BlockSpec((a, b), lambda i: (0, 0))
    w = RW_WIDTH
    out_spec = pl.BlockSpec((tt, w), lambda i: (i, 0))
    return pl.pallas_call(
        functools.partial(_rwkv_prep_kernel, tt=tt, t_pad=t_pad, multi=multi),
        grid=(n // tt,),
        in_specs=[pl.BlockSpec((tt, c), lambda i: (blk0 + i, 0)),
                  pl.BlockSpec((SUBLANES, c), lambda i: (jnp.maximum((blk0 + i) * tail_blk - 1, 0), 0)),
                  first_spec, vec(c), vec(w), vec(w), vec(w), vec(w), vec(w),
                  full(RW_DECAY_LORA, w), full(RW_A_LORA, w), full(RW_GATE_LORA, w), full(w, w)],
        out_specs=[out_spec] * 8,
        out_shape=[jax.ShapeDtypeStruct((n, w), F32)] * 8,
        compiler_params=_cparams(("arbitrary",)),
        name="rwkv_prep",
    )(proj, proj, first, prm['mu'].reshape(1, c), prm['w0'].reshape(1, w), prm['a0'].reshape(1, w),
      prm['k_k'].reshape(1, w), prm['k_a'].reshape(1, w), prm['r_k'].reshape(1, w),
      prm['w2'], prm['a2'], prm['g2'], _head_block_ones(w, RW_DH))


def _neumann_inverse_many(n_mats, size):
    eye = (_iota2((size, size), 0) == _iota2((size, size), 1)).astype(F32)
    ts = [eye + n_mat for n_mat in n_mats]
    xs = list(n_mats)
    steps = max(int(np.ceil(np.log2(size))) - 1, 0)
    for _ in range(steps):
        xs = [_dot(x, x) for x in xs]
        ts = [t + _dot(t, x) for t, x in zip(ts, xs)]
    return ts


def _rwkv_chunk_kernel(r_ref, w_ref, k_ref, v_ref, a_ref, b_ref, bonus_ref, g_ref, s0_ref,
                       lnw_ref, lnb_ref, _dst_ref, o_ref, s_ref, acc_ref, *, chunk, t_valid, hb, nb):
    ci = pl.program_id(2)

    @pl.when(ci == 0)
    def _():
        s_ref[...] = s0_ref[...]

    size = chunk
    rows = _iota2((size, size), 0)
    cols = _iota2((size, size), 1)
    strict = cols < rows
    rows2 = _iota2((size, 2 * size), 0)
    cols2 = _iota2((size, 2 * size), 1)
    mask_ak = (cols2 >= size) & (cols2 - size < rows2)
    mask_o = jnp.where(cols2 >= size, cols2 - size, cols2) <= rows2
    slab = nb * size
    rows_s = _iota2((slab, slab), 0)
    cols_s = _iota2((slab, slab), 1)
    shift = size.bit_length() - 1
    same_seq_tril = ((rows_s >> shift) == (cols_s >> shift)) & (cols_s <= rows_s)
    valid = (ci * size + (_iota2((slab, 1), 0) & (size - 1))) < t_valid
    w = jnp.where(valid, w_ref[...], 0.0)
    a = jnp.where(valid, a_ref[...], 0.0)
    b = jnp.where(valid, b_ref[...], 0.0)
    k = jnp.where(valid, k_ref[...], 0.0)
    v = v_ref[...]
    lam = _dot_exact_lhs(same_seq_tril.astype(F32), w)
    e_pos = jnp.exp(lam)
    e_neg = jnp.exp(-lam)
    at = a * jnp.exp(lam - w)
    bt = b * e_neg
    kt = k * e_neg
    rt = r_ref[...] * e_pos
    units = [(sq, h) for sq in range(nb) for h in range(hb)]
    ids = range(len(units))
    rs = [slice(sq * size, (sq + 1) * size) for sq, _ in units]
    sls = [slice(h * RW_DH, (h + 1) * RW_DH) for _, h in units]
    at_h = [at[rs[u], sls[u]] for u in ids]
    rt_h = [rt[rs[u], sls[u]] for u in ids]
    v_h = [v[rs[u], sls[u]] for u in ids]
    bk = [jnp.concatenate([bt[rs[u], sls[u]], kt[rs[u], sls[u]]], axis=0) for u in ids]
    s0 = [s_ref[sq, h] for sq, h in units]
    pa = [_dot_nt(at_h[u], bk[u]) for u in ids]
    pr = [_dot_nt(rt_h[u], bk[u]) for u in ids]
    as0 = [_dot_nt(at_h[u], s0[u]) for u in ids]
    rs0 = [_dot_nt(rt_h[u], s0[u]) for u in ids]
    zv = [jnp.concatenate([jnp.zeros_like(v_h[u]), v_h[u]], axis=0) for u in ids]
    rhs = [as0[u] + _dot(jnp.where(mask_ak, pa[u], 0.0), zv[u]) for u in ids]
    t_inv = _neumann_inverse_many([jnp.where(strict, pa[u][:, :size], 0.0) for u in ids], size)
    uu = [_dot(t_inv[u], rhs[u]) for u in ids]
    uv = [jnp.concatenate([uu[u], v_h[u]], axis=0) for u in ids]
    o = [rs0[u] + _dot(jnp.where(mask_o, pr[u], 0.0), uv[u]) for u in ids]
    ds = [_dot_tn(uv[u], bk[u]) for u in ids]
    for u, (sq, h) in enumerate(units):
        last = (sq + 1) * size - 1
        s_ref[sq, h] = (s0[u] + ds[u]) * e_pos[last:last + 1, sls[u]]
        oc = o[u] - jnp.mean(o[u], axis=-1, keepdims=True)
        acc_ref[rs[u], sls[u]] = oc * lax.rsqrt(jnp.mean(oc * oc, axis=-1, keepdims=True) + RW_GN_EPS)
    out = (acc_ref[...] * lnw_ref[...] + lnb_ref[...] + bonus_ref[...]) * g_ref[...]
    o_ref[...] = out.astype(o_ref.dtype)


def rwkv_chunk(prep, dst, dst_col0, row0, bsz, t_pad, t_valid, chunk, s0, ln_w, ln_b, hb, nb=1):
    nc = t_pad // chunk
    hg = RW_HEADS // hb
    wb = hb * RW_DH
    rows_blk = nb * chunk
    blk0 = row0 // rows_blk
    assert dst_col0 % wb == 0 and row0 % rows_blk == 0 and bsz % nb == 0 and (nb == 1 or nc == 1)
    assert chunk & (chunk - 1) == 0
    tok = pl.BlockSpec((rows_blk, wb), lambda b, g, c: (b * nc + c, g))
    st = pl.BlockSpec((nb, hb, RW_DH, RW_DH), lambda b, g, c: (b, g, 0, 0))
    vec = pl.BlockSpec((1, wb), lambda b, g, c: (0, g))
    return pl.pallas_call(
        functools.partial(_rwkv_chunk_kernel, chunk=chunk, t_valid=t_valid, hb=hb, nb=nb),
        grid=(bsz // nb, hg, nc),
        in_specs=[tok] * 8 + [st, vec, vec, pl.BlockSpec(memory_space=pl.ANY)],
        out_specs=[pl.BlockSpec((rows_blk, wb), lambda b, g, c: (blk0 + b * nc + c, dst_col0 // wb + g)), st],
        out_shape=[jax.ShapeDtypeStruct(dst.shape, dst.dtype),
                   jax.ShapeDtypeStruct((bsz, RW_HEADS, RW_DH, RW_DH), F32)],
        scratch_shapes=[pltpu.VMEM((rows_blk, wb), F32)],
        input_output_aliases={11: 0},
        compiler_params=_cparams(("arbitrary", "arbitrary", "arbitrary")),
        name="rwkv_chunk",
    )(*prep, s0, ln_w.reshape(1, RW_WIDTH), ln_b.reshape(1, RW_WIDTH), dst)


def _conv_silu(cur_ref, hist_ref, cw_ref, size):
    cur = cur_ref[...]
    hist = hist_ref[...]
    cw = cw_ref[...]
    row8 = _iota2((SUBLANES, 1), 0)
    acc = cur * cw[GD_CONV - 1:GD_CONV, :]
    for j in range(1, GD_CONV):
        back = GD_CONV - 1 - j
        rolled = pltpu.roll(cur, j, 0)
        head = jnp.where(row8 < j, pltpu.roll(hist, j, 0), rolled[:SUBLANES])
        prev = head if size == SUBLANES else jnp.concatenate([head, rolled[SUBLANES:]], axis=0)
        acc = acc + prev * cw[back:back + 1, :]
    hist_ref[...] = cur[size - SUBLANES:size]
    return acc * _sigmoid(acc)


def _l2_normalize(x, scale):
    return x * lax.rsqrt(jnp.sum(x * x, axis=-1, keepdims=True) + GD_EPS) * scale


def _gdn_chunk_kernel(q_ref, k_ref, v_ref, z_ref, gl_ref, bq_ref, bk_ref, bv_ref, cq_ref, ck_ref, cv_ref,
                      al_ref, dt_ref, nw_ref, s0_ref, _dst_ref,
                      o_ref, s_ref, hq_ref, hk_ref, hv_ref, *, chunk, t_valid, hpg):
    ci = pl.program_id(2)
    hg = pl.program_id(1)

    @pl.when(ci == 0)
    def _():
        s_ref[...] = s0_ref[...]
        hq_ref[...] = bq_ref[0]
        hk_ref[...] = bk_ref[0]
        hv_ref[...] = bv_ref[0]

    size = chunk
    q_act = _conv_silu(q_ref, hq_ref, cq_ref, size)
    k_act = _conv_silu(k_ref, hk_ref, ck_ref, size)
    v_act = _conv_silu(v_ref, hv_ref, cv_ref, size)
    rows = _iota2((size, size), 0)
    cols = _iota2((size, size), 1)
    eye = rows == cols
    incl = cols <= rows
    strict = cols < rows
    valid = (ci * size + _iota2((size, 1), 0)) < t_valid
    logits = gl_ref[...]
    beta_all = _sigmoid(logits)
    g_all = -jnp.exp(al_ref[...]) * _softplus(logits + dt_ref[...])
    lane = _iota2((size, LANES), 1)
    rep = GD_V_HEADS // GD_K_HEADS
    kheads = range(hpg // rep)
    heads = range(hpg)
    q = [_l2_normalize(q_act[:, kh * GD_DK:(kh + 1) * GD_DK], GD_DK ** -0.5) for kh in kheads]
    k = [_l2_normalize(k_act[:, kh * GD_DK:(kh + 1) * GD_DK], 1.0) for kh in kheads]
    qk = [_dot_nt(q[kh], k[kh]) for kh in kheads]
    kk = [_dot_nt(k[kh], k[kh]) for kh in kheads]
    vsl = [slice(hl * GD_DV, (hl + 1) * GD_DV) for hl in heads]
    s = [s_ref[0, hl] for hl in heads]
    beta, gc, decay = [], [], []
    for hl in heads:
        head = hg * hpg + hl
        beta_h = jnp.sum(jnp.where(lane == head, beta_all, 0.0), axis=1, keepdims=True)
        g = jnp.sum(jnp.where(lane == head + GD_V_HEADS, g_all, 0.0), axis=1, keepdims=True)
        beta_h = jnp.where(valid, beta_h, 0.0)
        g = jnp.where(valid, g, 0.0)
        g_row = _col_to_row(g, eye)
        gc_h = jnp.sum(jnp.where(incl, g_row, 0.0), axis=1, keepdims=True)
        gc_row = jnp.sum(jnp.where(rows <= cols, g, 0.0), axis=0, keepdims=True)
        beta.append(beta_h)
        gc.append(gc_h)
        decay.append(jnp.where(incl, jnp.exp(jnp.where(incl, gc_h - gc_row, 0.0)), 0.0))
    qs = [_dot(q[hl // rep] * jnp.exp(gc[hl]), s[hl]) for hl in heads]
    t_inv = _neumann_inverse_many(
        [-jnp.where(strict, kk[hl // rep] * beta[hl] * decay[hl], 0.0) for hl in heads], size)
    uw = [_dot(t_inv[hl], jnp.concatenate([v_act[:, vsl[hl]] * beta[hl],
                                           k[hl // rep] * (beta[hl] * jnp.exp(gc[hl]))], axis=1))
          for hl in heads]
    ws = [_dot(uw[hl][:, GD_DV:], s[hl]) for hl in heads]
    v_new = [uw[hl][:, :GD_DV] - ws[hl] for hl in heads]
    av = [_dot(qk[hl // rep] * decay[hl], v_new[hl]) for hl in heads]
    g_last = [gc[hl][size - 1:size, :] for hl in heads]
    kv = [_dot_tn(k[hl // rep] * jnp.exp(g_last[hl] - gc[hl]), v_new[hl]) for hl in heads]
    for hl in heads:
        s_ref[0, hl] = s[hl] * jnp.exp(g_last[hl]) + kv[hl]
        o = qs[hl] + av[hl]
        o = o * lax.rsqrt(jnp.mean(o * o, axis=-1, keepdims=True) + GD_EPS) * nw_ref[...]
        z = z_ref[:, vsl[hl]]
        o_ref[:, vsl[hl]] = (o * (z * _sigmoid(z))).astype(o_ref.dtype)


def gdn_chunk(proj, gates, conv_buf, conv_w, dst, row0, bsz, t_pad, t_valid, chunk, s0, a_log, dt_bias,
              norm_w, hpg):
    nc = t_pad // chunk
    hgs = GD_V_HEADS // hpg
    rep = GD_V_HEADS // GD_K_HEADS
    kw = (hpg // rep) * GD_DK
    vw = hpg * GD_DV
    blk0 = row0 // chunk
    assert row0 % chunk == 0 and chunk % SUBLANES == 0
    pad_row = lambda x: jnp.pad(x, (GD_V_HEADS, LANES - 2 * GD_V_HEADS)).reshape(1, LANES)
    hist = jnp.pad(conv_buf, ((0, 0), (SUBLANES - (GD_CONV - 1), 0), (0, 0)))
    st = pl.BlockSpec((1, hpg, GD_DK, GD_DV), lambda b, g, c: (b, g, 0, 0))
    vec = pl.BlockSpec((1, LANES), lambda b, g, c: (0, 0))
    q_col, k_col, v_col = (lambda g: g), (lambda g: GD_QK // kw + g), (lambda g: 2 * GD_QK // vw + g)
    tok = lambda width, col: pl.BlockSpec((chunk, width), lambda b, g, c: (blk0 + b * nc + c, col(g)))
    buf = lambda width, col: pl.BlockSpec((1, SUBLANES, width), lambda b, g, c: (b, 0, col(g)))
    taps = lambda width, col: pl.BlockSpec((GD_CONV, width), lambda b, g, c: (0, col(g)))
    return pl.pallas_call(
        functools.partial(_gdn_chunk_kernel, chunk=chunk, t_valid=t_valid, hpg=hpg),
        grid=(bsz, hgs, nc),
        in_specs=[tok(kw, q_col), tok(kw, k_col), tok(vw, v_col),
                  tok(vw, lambda g: GD_CONV_DIM // vw + g),
                  pl.BlockSpec((chunk, LANES), lambda b, g, c: (blk0 + b * nc + c, 0)),
                  buf(kw, q_col), buf(kw, k_col), buf(vw, v_col),
                  taps(kw, q_col), taps(kw, k_col), taps(vw, v_col),
                  vec, vec, vec, st, pl.BlockSpec(memory_space=pl.ANY)],
        out_specs=[pl.BlockSpec((chunk, vw), lambda b, g, c: (blk0 + b * nc + c, g)), st],
        out_shape=[jax.ShapeDtypeStruct(dst.shape, dst.dtype),
                   jax.ShapeDtypeStruct((bsz, GD_V_HEADS, GD_DK, GD_DV), F32)],
        scratch_shapes=[pltpu.VMEM((SUBLANES, kw), F32), pltpu.VMEM((SUBLANES, kw), F32),
                        pltpu.VMEM((SUBLANES, vw), F32)],
        input_output_aliases={15: 0},
        compiler_params=_cparams(("arbitrary", "arbitrary", "arbitrary")),
        name="gdn_chunk",
    )(proj, proj, proj, proj, gates, hist, hist, hist, conv_w, conv_w, conv_w,
      pad_row(a_log), pad_row(dt_bias), norm_w.reshape(1, GD_DV), s0, dst)


MOE_TM = 256


def _router_kernel(x_ref, w_ref, wt_ref, id_ref):
    logits = jnp.dot(x_ref[...], w_ref[...].astype(BF16), preferred_element_type=F32)
    lane = _iota2(logits.shape, 1).astype(F32)
    first_of = lambda hit: jnp.min(jnp.where(hit, lane, float(LANES)), axis=1, keepdims=True)
    gl = jnp.where(lane < MOE_GROUPS, logits, NEG)
    gmax = jnp.max(gl, axis=1, keepdims=True)
    g_val = 1.0 / jnp.sum(jnp.exp(gl - gmax), axis=1, keepdims=True)
    lo = MOE_GROUPS + first_of(gl == gmax) * MOE_PER_GROUP
    vals = jnp.where((lane >= lo) & (lane < lo + MOE_PER_GROUP), logits, NEG)
    top1 = jnp.max(vals, axis=1, keepdims=True)
    i1 = first_of(vals == top1)
    vals2 = jnp.where(lane == i1, NEG, vals)
    top2 = jnp.max(vals2, axis=1, keepdims=True)
    i2 = first_of(vals2 == top2)
    e2 = jnp.exp(top2 - top1)
    w1 = (1.0 / (1.0 + e2)) * g_val
    w2 = (e2 / (1.0 + e2)) * g_val
    wt_ref[...] = jnp.where(lane == 0, w1, jnp.where(lane == 1, w2, 0.0))
    ids = jnp.where(lane == 0, i1 - MOE_GROUPS, jnp.where(lane == 1, i2 - MOE_GROUPS, 0.0))
    id_ref[...] = ids.astype(jnp.int32)


def moe_router(xb, w_router, tm=512):
    n, d = xb.shape
    out = pl.BlockSpec((tm, LANES), lambda i: (i, 0))
    return pl.pallas_call(
        _router_kernel,
        grid=(n // tm,),
        in_specs=[pl.BlockSpec((tm, d), lambda i: (i, 0)), pl.BlockSpec((d, LANES), lambda i: (0, 0))],
        out_specs=[out, out],
        out_shape=[jax.ShapeDtypeStruct((n, LANES), F32), jax.ShapeDtypeStruct((n, LANES), jnp.int32)],
        compiler_params=_cparams(("arbitrary",)),
        name="moe_router",
    )(xb, w_router)


def _moe_ffn_kernel(te_ref, tv_ref, idx_ref, idx_next_ref, x_hbm, wg_ref, wu_ref, wd_ref, o_ref,
                    xbuf, xcat, wgb, wub, wdb, sem):
    i = pl.program_id(0)
    n_tiles = pl.num_programs(0)
    slot = i % 2
    pieces = xcat.shape[1] // LANES

    def gather(rows_ref, into):
        for r in range(MOE_TM):
            pltpu.make_async_copy(x_hbm.at[pl.ds(rows_ref[0, 0, r], pieces)],
                                  xbuf.at[into, pl.ds(r * pieces, pieces)], sem.at[into]).start(priority=r % 2)

    @pl.when((i == 0) & (tv_ref[0] == 1))
    def _():
        gather(idx_ref, 0)

    nxt = jnp.minimum(i + 1, n_tiles - 1)

    @pl.when((i + 1 < n_tiles) & (tv_ref[nxt] == 1))
    def _():
        gather(idx_next_ref, 1 - slot)

    @pl.when(tv_ref[i] == 1)
    def _():
        pltpu.make_async_copy(xbuf.at[slot], xbuf.at[slot], sem.at[slot]).wait()

        @pl.when((i == 0) | (te_ref[i] != te_ref[jnp.maximum(i - 1, 0)]))
        def _():
            wgb[...] = wg_ref[0, 0].astype(BF16)
            wub[...] = wu_ref[0, 0].astype(BF16)
            wdb[...] = wd_ref[0, 0].astype(BF16)

        for j in range(pieces):
            xcat[:, j * LANES:(j + 1) * LANES] = xbuf[slot, pl.ds(j, MOE_TM, stride=pieces), :].astype(BF16)
        x = xcat[...]
        gate = jnp.dot(x, wgb[...], preferred_element_type=F32)
        up = jnp.dot(x, wub[...], preferred_element_type=F32)
        hid = gate * _sigmoid(gate) * up
        o_ref[...] = jnp.dot(hid.astype(BF16), wdb[...], preferred_element_type=F32)

    @pl.when(tv_ref[i] == 0)
    def _():
        o_ref[...] = jnp.zeros_like(o_ref)


def moe_ffn(x_rows, row_start, tile_expert, tile_valid, layer, w_gate, w_up, w_down):
    n_tiles = row_start.shape[0]
    d = w_gate.shape[2]
    ff = w_gate.shape[3]
    pieces = d // LANES
    grid_spec = pltpu.PrefetchScalarGridSpec(
        num_scalar_prefetch=2,
        grid=(n_tiles,),
        in_specs=[pl.BlockSpec((1, 1, MOE_TM), lambda i, te, tv: (i, 0, 0), memory_space=pltpu.SMEM),
                  pl.BlockSpec((1, 1, MOE_TM), lambda i, te, tv: (jnp.minimum(i + 1, n_tiles - 1), 0, 0),
                               memory_space=pltpu.SMEM),
                  pl.BlockSpec(memory_space=pl.ANY),
                  pl.BlockSpec((1, 1, d, ff), lambda i, te, tv: (layer, te[i], 0, 0)),
                  pl.BlockSpec((1, 1, d, ff), lambda i, te, tv: (layer, te[i], 0, 0)),
                  pl.BlockSpec((1, 1, ff, d), lambda i, te, tv: (layer, te[i], 0, 0))],
        out_specs=pl.BlockSpec((MOE_TM, d), lambda i, te, tv: (i, 0)),
        scratch_shapes=[pltpu.VMEM((2, MOE_TM * pieces, LANES), F32), pltpu.VMEM((MOE_TM, d), BF16),
                        pltpu.VMEM((d, ff), BF16), pltpu.VMEM((d, ff), BF16), pltpu.VMEM((ff, d), BF16),
                        pltpu.SemaphoreType.DMA((2,))],
    )
    return pl.pallas_call(
        _moe_ffn_kernel,
        grid_spec=grid_spec,
        out_shape=jax.ShapeDtypeStruct((n_tiles * MOE_TM, d), F32),
        compiler_params=_cparams(("arbitrary",)),
        name="moe_ffn",
    )(tile_expert, tile_valid, row_start, row_start, x_rows, w_gate, w_up, w_down)


def _ln_moe_kernel(x_ref, ya_ref, yb_ref, wt_ref, g_ref, b_ref, o_ref, ob_ref):
    wt = wt_ref[...]
    z = DEEPNORM_ALPHA * x_ref[...] + (wt[:, 0:1] * ya_ref[...] + wt[:, 1:2] * yb_ref[...])
    zc = z - jnp.mean(z, axis=-1, keepdims=True)
    var = jnp.mean(zc * zc, axis=-1, keepdims=True)
    out = zc * lax.rsqrt(var + LN_EPS) * g_ref[...] + b_ref[...]
    o_ref[...] = out
    ob_ref[...] = out.astype(BF16)


def ln_residual_moe(x, ya, yb, wt, g, b, tm=256):
    m, d = x.shape
    row = pl.BlockSpec((tm, d), lambda i: (i, 0))
    vec = pl.BlockSpec((1, d), lambda i: (0, 0))
    return pl.pallas_call(
        _ln_moe_kernel,
        grid=(m // tm,),
        in_specs=[row, row, row, pl.BlockSpec((tm, LANES), lambda i: (i, 0)), vec, vec],
        out_specs=[row, row],
        out_shape=[jax.ShapeDtypeStruct((m, d), F32), jax.ShapeDtypeStruct((m, d), BF16)],
        compiler_params=_cparams(("arbitrary",)),
        name="ln_residual_moe",
    )(x, ya, yb, wt, g.reshape(1, d), b.reshape(1, d))


def _table_lookup(table, idx):
    hit = idx[:, None] == jnp.arange(table.shape[0], dtype=idx.dtype)[None, :]
    return jnp.sum(jnp.where(hit, table[None, :], 0), axis=1)


def hier_moe(x_rows, xb, layer, w_group, w_expert, w_gate, w_up, w_down):
    n = xb.shape[0]
    assert (n * MOE_TOPK) % MOE_TM == 0
    w_router = jnp.pad(jnp.concatenate([w_group, w_expert], axis=1),
                       ((0, 0), (0, LANES - MOE_GROUPS - MOE_EXPERTS)))
    wt, ids = moe_router(xb, w_router)

    n_pairs = n * MOE_TOPK
    n_tiles = -(-n_pairs // MOE_TM) + MOE_EXPERTS
    n_rows = n_tiles * MOE_TM
    i32 = jnp.int32
    eid = ids[:, :MOE_TOPK].reshape(n_pairs)
    order = jnp.argsort(eid, stable=True).astype(i32)
    rank = jnp.argsort(order).astype(i32)
    experts = jnp.arange(MOE_EXPERTS, dtype=i32)
    counts = jnp.sum((eid[:, None] == experts[None, :]).astype(i32), axis=0)
    starts = jnp.cumsum(counts) - counts
    tiles_per = (counts + MOE_TM - 1) // MOE_TM
    tile_starts = jnp.cumsum(tiles_per) - tiles_per
    dest = _table_lookup(tile_starts * MOE_TM - starts, eid) + rank
    tile_idx = jnp.arange(n_tiles, dtype=i32)
    tile_expert = jnp.sum((tile_idx[:, None] >= (tile_starts + tiles_per)[None, :]).astype(i32), axis=1)
    tile_valid = (tile_expert < MOE_EXPERTS).astype(i32)
    last_used = jnp.max(jnp.where(tiles_per > 0, experts, 0))
    tile_expert = jnp.where(tile_valid == 1, tile_expert, last_used)
    off = (tile_idx - _table_lookup(tile_starts, tile_expert))[:, None] * MOE_TM + jnp.arange(MOE_TM, dtype=i32)[None, :]
    cnt = _table_lookup(counts, tile_expert)[:, None]
    src = jnp.clip(_table_lookup(starts, tile_expert)[:, None] + off, 0, n_pairs - 1)
    row_ok = (off < cnt) & (tile_valid[:, None] == 1)
    picked = jnp.take(order, src.reshape(n_rows), mode='clip').reshape(n_tiles, MOE_TM)
    pieces = xb.shape[1] // LANES
    row_start = (jnp.where(row_ok, picked // MOE_TOPK, 0) * pieces).reshape(n_tiles, 1, MOE_TM)

    y_sorted = moe_ffn(x_rows, row_start, tile_expert, tile_valid, layer, w_gate, w_up, w_down)
    dest2 = dest.reshape(n, MOE_TOPK)
    return (jnp.take(y_sorted, dest2[:, 0], axis=0, mode='clip'),
            jnp.take(y_sorted, dest2[:, 1], axis=0, mode='clip'), wt)


SAMPLE_T_PAD = 8


def kernel(x_prompt, x_sample, state_mlstm_c, state_mlstm_n, state_mlstm_m, state_rwkv_s,
           state_rwkv_shift, state_gdn_s, state_gdn_conv, p_prompt, p_sample,
           w_in_ab, ml_b_i, ml_b_f, ml_norm_w, rw_mu, rw_w0, rw_w2, rw_a0, rw_a2, rw_g2,
           rw_k_k, rw_k_a, rw_r_k, rw_ln_w, rw_ln_b, w_out_ab,
           gd_w_in, gd_conv_w, gd_a_log, gd_dt_bias, gd_norm_w, gd_w_out,
           ln_mix_g, ln_mix_b, moe_w_group, moe_w_expert, moe_w_gate, moe_w_up, moe_w_down,
           ln_ffn_g, ln_ffn_b, ple_w_gate, ple_w_proj):
    pb, pt, d = x_prompt.shape
    sb, st, _ = x_sample.shape
    n_p = pb * pt
    n_s = sb * SAMPLE_T_PAD
    n = n_p + n_s
    pad_t = lambda a: jnp.pad(a, ((0, 0), (0, SAMPLE_T_PAD - st), (0, 0)))
    merge = lambda a_p, a_s: jnp.concatenate(
        [a_p.reshape(n_p, a_p.shape[-1]), pad_t(a_s).reshape(n_s, a_s.shape[-1])], axis=0)
    x = merge(x_prompt, x_sample)
    xb = x.astype(BF16)
    segs = (dict(row0=0, bsz=pb, t_pad=pt, t_valid=pt, ml_nb=1, rw_hb=16, rw_nb=1, gd_hpg=16),
            dict(row0=n_p, bsz=sb, t_pad=SAMPLE_T_PAD, t_valid=st, ml_nb=2, rw_hb=16, rw_nb=2, gd_hpg=32))
    zeros = lambda *shape: jnp.zeros(shape, F32)

    def last_rows(a, sg, count):
        assert sg['t_valid'] >= count
        idx = (sg['row0'] + jnp.arange(sg['bsz'])[:, None] * sg['t_pad']
               + (sg['t_valid'] - count + jnp.arange(count))[None, :])
        return jnp.take(a, idx.reshape(-1), axis=0, mode='clip').reshape(sg['bsz'], count, a.shape[-1])

    new_states = [[[] for _ in range(7)] for _ in segs]
    for layer in range(DEPTH):
        li = layer // 2
        if layer % 2 == 0:
            w_in_t = jnp.transpose(w_in_ab[li])
            proj_ml = matmul_wt(xb, w_in_t, 0, 3584, 896)
            proj_rw = matmul_wt(xb, w_in_t, ML_COLS, RW_COLS, RW_COLS // 2)
            rw_prm = dict(mu=rw_mu[li], w0=rw_w0[li], w2=rw_w2[li], a0=rw_a0[li], a2=rw_a2[li], g2=rw_g2[li],
                          k_k=rw_k_k[li], k_a=rw_k_a[li], r_k=rw_r_k[li])
            heads_out = jnp.zeros((n, ML_V + RW_WIDTH), BF16)
            for si, sg in enumerate(segs):
                bsz, t_pad, t_valid, row0 = sg['bsz'], sg['t_pad'], sg['t_valid'], sg['row0']
                if si == 0:
                    c0, n0, m0 = zeros(bsz, ML_HEADS, ML_DK, ML_DV), zeros(bsz, ML_HEADS, ML_DK), zeros(bsz, ML_HEADS)
                    s0, sh0 = zeros(bsz, RW_HEADS, RW_DH, RW_DH), zeros(bsz, RW_COLS)
                else:
                    c0, n0, m0 = state_mlstm_c[li], state_mlstm_n[li], state_mlstm_m[li]
                    s0, sh0 = state_rwkv_s[li], state_rwkv_shift[li]
                heads_out, c, nn, m = mlstm_group(proj_ml, heads_out, row0, bsz, t_pad, t_valid,
                                                  min(ML_CHUNK, t_pad), c0, n0, m0,
                                                  ml_b_i[li], ml_b_f[li], ml_norm_w[li], sg['ml_nb'])
                prep = rwkv_prep(proj_rw, row0, bsz, t_pad, sh0, rw_prm, 256)
                heads_out, rs = rwkv_chunk(prep, heads_out, ML_V, row0, bsz, t_pad, t_valid,
                                           min(RW_CHUNK, t_pad), s0, rw_ln_w[li], rw_ln_b[li],
                                           sg['rw_hb'], sg['rw_nb'])
                new_shift = last_rows(proj_rw, sg, 1)[:, 0]
                for slot, val in zip(range(5), (c, nn, m, rs, new_shift)):
                    new_states[si][slot].append(val)
            mix = matmul(heads_out, w_out_ab[li], d, 1024)
        else:
            w_in_t = jnp.transpose(gd_w_in[li])
            n_qkvz = GD_CONV_DIM + GD_V_WIDTH
            proj = matmul_wt(xb, w_in_t, 0, n_qkvz, 1024)
            gates = matmul_wt(xb, w_in_t, n_qkvz, LANES, LANES)
            gd_out = jnp.zeros((n, GD_V_WIDTH), BF16)
            for si, sg in enumerate(segs):
                bsz, t_pad, t_valid, row0 = sg['bsz'], sg['t_pad'], sg['t_valid'], sg['row0']
                if si == 0:
                    s0, buf0 = zeros(bsz, GD_V_HEADS, GD_DK, GD_DV), zeros(bsz, GD_CONV - 1, GD_CONV_DIM)
                else:
                    s0, buf0 = state_gdn_s[li], state_gdn_conv[li]
                gd_out, gs = gdn_chunk(proj, gates, buf0, gd_conv_w[li], gd_out, row0, bsz, t_pad, t_valid,
                                       min(GD_CHUNK, t_pad), s0, gd_a_log[li], gd_dt_bias[li],
                                       gd_norm_w[li], sg['gd_hpg'])
                new_states[si][5].append(gs)
                new_states[si][6].append(last_rows(proj, sg, GD_CONV - 1)[:, :, :GD_CONV_DIM])
            mix = matmul(gd_out, gd_w_out[li], d, 512)
        x, xb, x_rows = ln_residual(x, mix, ln_mix_g[layer], ln_mix_b[layer])
        ya, yb, wt = hier_moe(x_rows, xb, layer, moe_w_group[layer], moe_w_expert[layer],
                              moe_w_gate, moe_w_up, moe_w_down)
        x, xb = ln_residual_moe(x, ya, yb, wt, ln_ffn_g[layer], ln_ffn_b[layer])
        x, xb = ple(xb, merge(p_prompt[layer], p_sample[layer]), x, layer, ple_w_gate, ple_w_proj)
    y_prompt = x[:n_p].reshape(pb, pt, d)
    y_sample = x[n_p:].reshape(sb, SAMPLE_T_PAD, d)[:, :st]
    stack = lambda vals: jnp.stack(vals)
    return (y_prompt, y_sample) + tuple(stack(v) for v in new_states[0]) + tuple(stack(v) for v in new_states[1])
```

```python
import functools

import jax
import jax.numpy as jnp
import numpy as np
from jax import lax
from jax.experimental import pallas as pl
from jax.experimental.pallas import tpu as pltpu

F32 = jnp.float32
BF16 = jnp.bfloat16

D_MODEL = 2048
DEPTH = 2
DEEPNORM_ALPHA = (2 * DEPTH) ** 0.25
LN_EPS = 1e-5
MIX_HALF = D_MODEL // 2
ML_HEADS = 4
ML_DV = MIX_HALF // ML_HEADS
ML_DK = ML_DV // 2
ML_CHUNK = 64
ML_GATE_CAP = 15.0
ML_NORM_EPS = 1e-6
ML_QK = ML_HEADS * ML_DK
ML_V = ML_HEADS * ML_DV
ML_COLS = 2 * ML_QK + 2 * ML_V + 2 * ML_HEADS
RW_DH = 64
RW_HEADS = MIX_HALF // RW_DH
RW_WIDTH = RW_HEADS * RW_DH
RW_DECAY_LORA = 64
RW_A_LORA = 64
RW_GATE_LORA = 128
RW_GN_EPS = 64e-5
RW_COLS = 3 * RW_WIDTH + RW_DECAY_LORA + RW_A_LORA + RW_GATE_LORA
RW_CHUNK = 64
GD_DK = 128
GD_DV = 128
GD_K_HEADS = D_MODEL // GD_DK
GD_V_HEADS = 2 * GD_K_HEADS
GD_CONV = 4
GD_CHUNK = 64
GD_EPS = 1e-6
GD_QK = GD_K_HEADS * GD_DK
GD_V_WIDTH = GD_V_HEADS * GD_DV
GD_CONV_DIM = 2 * GD_QK + GD_V_WIDTH
MOE_GROUPS = 4
MOE_PER_GROUP = 8
MOE_EXPERTS = MOE_GROUPS * MOE_PER_GROUP
MOE_TOPK = 2
MOE_FF = D_MODEL // 4
PLE_DIM = 256

LANES = 128
SUBLANES = 8
NEG = -1e30
VMEM_LIMIT = 56 * 1024 * 1024


def _cparams(sem):
    return pltpu.CompilerParams(dimension_semantics=sem, vmem_limit_bytes=VMEM_LIMIT)


def _dot(a, b):
    return jnp.dot(a.astype(BF16), b.astype(BF16), preferred_element_type=F32)


def _dot_nt(a, b):
    return lax.dot_general(a.astype(BF16), b.astype(BF16), (((1,), (1,)), ((), ())),
                           preferred_element_type=F32)


def _dot_tn(a, b):
    return lax.dot_general(a.astype(BF16), b.astype(BF16), (((0,), (0,)), ((), ())),
                           preferred_element_type=F32)


def _split3(x):
    hi = x.astype(BF16)
    r1 = x - hi.astype(F32)
    mid = r1.astype(BF16)
    lo = (r1 - mid.astype(F32)).astype(BF16)
    return hi, mid, lo


def _dot_exact_rhs(a, b01):
    hi, mid, lo = _split3(a)
    b = b01.astype(BF16)
    return (jnp.dot(hi, b, preferred_element_type=F32) + jnp.dot(mid, b, preferred_element_type=F32)
            + jnp.dot(lo, b, preferred_element_type=F32))


def _dot_exact_lhs(a01, b):
    hi, mid, lo = _split3(b)
    a = a01.astype(BF16)
    return (jnp.dot(a, hi, preferred_element_type=F32) + jnp.dot(a, mid, preferred_element_type=F32)
            + jnp.dot(a, lo, preferred_element_type=F32))


def _iota2(shape, dim):
    return lax.broadcasted_iota(jnp.int32, shape, dim)


def _col_to_row(col, eye):
    return jnp.sum(jnp.where(eye, col, 0.0), axis=0, keepdims=True)


def _sigmoid(x):
    return 1.0 / (1.0 + jnp.exp(-x))


def _softplus(x):
    return jnp.maximum(x, 0.0) + jnp.log1p(jnp.exp(-jnp.abs(x)))


def _neumann_inverse(n_mat, size):
    eye = (_iota2((size, size), 0) == _iota2((size, size), 1)).astype(F32)
    t = eye + n_mat
    x = n_mat
    steps = max(int(np.ceil(np.log2(size))) - 1, 0)
    for _ in range(steps):
        x = _dot(x, x)
        t = t + _dot(t, x)
    return t


def _mm_kernel(x_ref, w_ref, o_ref, wb_ref, *, n_valid):
    @pl.when(pl.program_id(1) == 0)
    def _():
        wb_ref[...] = w_ref[...].astype(BF16)

    acc = jnp.dot(x_ref[...].astype(BF16), wb_ref[...], preferred_element_type=F32)
    if n_valid is not None:
        acc = jnp.where(_iota2(acc.shape, 1) < n_valid, acc, 0.0)
    o_ref[...] = acc.astype(o_ref.dtype)


def _row_tile(m, want):
    while m % want:
        want //= 2
    return want


def matmul(x, w, n_out, tn, tm=1024, out_dtype=F32, col0=0):
    m, k = x.shape
    tm = _row_tile(m, tm)
    assert w.shape[0] == k and n_out % tn == 0 and m % tm == 0 and tn % LANES == 0 and col0 % tn == 0
    n_valid = None
    if col0 + n_out > w.shape[1]:
        assert n_out == tn
        n_valid = w.shape[1] - col0
    blk0 = col0 // tn
    return pl.pallas_call(
        functools.partial(_mm_kernel, n_valid=n_valid),
        grid=(n_out // tn, m // tm),
        in_specs=[pl.BlockSpec((tm, k), lambda j, i: (i, 0)),
                  pl.BlockSpec((k, tn), lambda j, i: (0, blk0 + j))],
        out_specs=pl.BlockSpec((tm, tn), lambda j, i: (i, j)),
        out_shape=jax.ShapeDtypeStruct((m, n_out), out_dtype),
        scratch_shapes=[pltpu.VMEM((k, tn), BF16)],
        compiler_params=_cparams(("arbitrary", "arbitrary")),
        name="matmul",
    )(x, w)


def _mm_wt_kernel(x_ref, wt_hbm, o_ref, wf_ref, wb_ref, sem, *, row0, tn, n_rows):
    j = pl.program_id(0)

    @pl.when(pl.program_id(1) == 0)
    def _():
        if n_rows < tn:
            wf_ref[...] = jnp.zeros_like(wf_ref)
        start = pl.multiple_of(row0 + j * tn, SUBLANES)
        copy = pltpu.make_async_copy(wt_hbm.at[pl.ds(start, n_rows)], wf_ref.at[pl.ds(0, n_rows)], sem)
        copy.start()
        copy.wait()
        k = wf_ref.shape[1]
        step = 2 * LANES
        for c in range(k // step):
            wb_ref[c * step:(c + 1) * step, :] = wf_ref[:, c * step:(c + 1) * step].T.astype(BF16)

    o_ref[...] = jnp.dot(x_ref[...].astype(BF16), wb_ref[...], preferred_element_type=F32)


def matmul_wt(x, wt, row0, n_out, tn, tm=1024):
    m, k = x.shape
    tm = _row_tile(m, tm)
    assert wt.shape[1] == k and n_out % tn == 0 and tn % LANES == 0 and row0 % SUBLANES == 0
    n_rows = tn
    if row0 + n_out > wt.shape[0]:
        assert n_out == tn
        n_rows = wt.shape[0] - row0
    assert n_rows % SUBLANES == 0
    return pl.pallas_call(
        functools.partial(_mm_wt_kernel, row0=row0, tn=tn, n_rows=n_rows),
        grid=(n_out // tn, m // tm),
        in_specs=[pl.BlockSpec((tm, k), lambda j, i: (i, 0)), pl.BlockSpec(memory_space=pl.ANY)],
        out_specs=pl.BlockSpec((tm, tn), lambda j, i: (i, j)),
        out_shape=jax.ShapeDtypeStruct((m, n_out), F32),
        scratch_shapes=[pltpu.VMEM((tn, k), F32), pltpu.VMEM((k, tn), BF16), pltpu.SemaphoreType.DMA(())],
        compiler_params=_cparams(("arbitrary", "arbitrary")),
        name="matmul_wt",
    )(x, wt)


def _ln_res_kernel(x_ref, y_ref, g_ref, b_ref, o_ref, ob_ref):
    z = DEEPNORM_ALPHA * x_ref[...] + y_ref[...]
    zc = z - jnp.mean(z, axis=-1, keepdims=True)
    var = jnp.mean(zc * zc, axis=-1, keepdims=True)
    out = zc * lax.rsqrt(var + LN_EPS) * g_ref[...] + b_ref[...]
    o_ref[...] = out
    ob_ref[...] = out.astype(BF16)


def ln_residual(x, y, g, b, tm=256):
    m, d = x.shape
    row = pl.BlockSpec((tm, d), lambda i: (i, 0))
    vec = pl.BlockSpec((1, d), lambda i: (0, 0))
    return pl.pallas_call(
        _ln_res_kernel,
        grid=(m // tm,),
        in_specs=[row, row, vec, vec],
        out_specs=[row, row],
        out_shape=[jax.ShapeDtypeStruct((m, d), F32), jax.ShapeDtypeStruct((m, d), BF16)],
        compiler_params=_cparams(("arbitrary",)),
        name="ln_residual",
    )(x, y, g.reshape(1, d), b.reshape(1, d))


def _mlstm_kernel(q_ref, k_ref, v_ref, o_ref, g_ref, c0_ref, n0_ref, m0_ref, nw_ref, gb_ref, _dst_ref,
                  h_ref, c_ref, n_ref, m_ref, *, chunk, t_valid, nb):
    ci = pl.program_id(1)

    @pl.when(ci == 0)
    def _():
        c_ref[...] = c0_ref[...]
        n_ref[...] = n0_ref[...]
        m_ref[...] = m0_ref[...]

    size = chunk
    rows = _iota2((size, size), 0)
    cols = _iota2((size, size), 1)
    eye = rows == cols
    causal = cols <= rows
    gates = g_ref[:, 0:2 * ML_HEADS] + gb_ref[...]
    capped = ML_GATE_CAP * jnp.tanh(gates / ML_GATE_CAP)
    valid = (ci * size + (_iota2((nb * size, 1), 0) & (size - 1))) < t_valid
    ipre_all = jnp.where(valid, capped[:, 0:ML_HEADS], NEG)
    logf_all = jnp.where(valid, -_softplus(-capped[:, ML_HEADS:2 * ML_HEADS]), 0.0)
    units = [(sq, h) for sq in range(nb) for h in range(ML_HEADS)]
    rs = [slice(sq * size, (sq + 1) * size) for sq, _ in units]
    ksl = [slice(h * ML_DK, (h + 1) * ML_DK) for _, h in units]
    vsl = [slice(h * ML_DV, (h + 1) * ML_DV) for _, h in units]
    ids = range(len(units))
    q = [q_ref[rs[u], ksl[u]] for u in ids]
    k = [k_ref[rs[u], ksl[u]] * (ML_DK ** -0.5) for u in ids]
    v = [v_ref[rs[u], vsl[u]] for u in ids]
    qk = [_dot_nt(q[u], k[u]) for u in ids]
    c_prev = [c_ref[sq, h] for sq, h in units]
    n_prev = [n_ref[sq, h:h + 1, :] for sq, h in units]
    qc = [_dot(q[u], c_prev[u]) for u in ids]
    s, w_inter, m_t, kw, decay = [], [], [], [], []
    for u, (sq, h) in enumerate(units):
        ig_col = ipre_all[rs[u], h:h + 1]
        lf_col = logf_all[rs[u], h:h + 1]
        ig_row = _col_to_row(ig_col, eye)
        lf_row = _col_to_row(lf_col, eye)
        b_col = jnp.sum(jnp.where(causal, lf_row, 0.0), axis=1, keepdims=True)
        b_row = jnp.sum(jnp.where(rows <= cols, lf_col, 0.0), axis=0, keepdims=True)
        d = jnp.where(causal, b_col - b_row + ig_row, NEG)
        inter = b_col + m_ref[sq, :, h:h + 1]
        m_h = jnp.maximum(inter, jnp.max(d, axis=1, keepdims=True))
        w_h = jnp.exp(inter - m_h)
        s.append(qk[u] * jnp.exp(d - m_h))
        b_last = b_col[size - 1:size, :]
        m_last = m_h[size - 1:size, :]
        w_last = jnp.exp(b_last - b_col + ig_col - m_last)
        kw.append(k[u] * w_last)
        w_inter.append(w_h)
        m_t.append(m_h)
        decay.append(w_h[size - 1:size, :])
        m_ref[sq, :, h:h + 1] = m_last
    sv = [_dot(s[u], v[u]) for u in ids]
    kv = [_dot_tn(kw[u], v[u]) for u in ids]
    for u, (sq, h) in enumerate(units):
        num = w_inter[u] * qc[u] + sv[u]
        den = (w_inter[u] * jnp.sum(q[u] * n_prev[u], axis=1, keepdims=True)
               + jnp.sum(s[u], axis=1, keepdims=True))
        hid = num / jnp.maximum(jnp.abs(den), jnp.exp(-m_t[u]))
        c_ref[sq, h] = decay[u] * c_prev[u] + kv[u]
        n_ref[sq, h:h + 1, :] = decay[u] * n_prev[u] + jnp.sum(kw[u], axis=0, keepdims=True)
        hid = hid * lax.rsqrt(jnp.mean(hid * hid, axis=-1, keepdims=True) + ML_NORM_EPS)
        hid = hid * nw_ref[:, vsl[u]]
        hid = hid * _sigmoid(o_ref[rs[u], vsl[u]])
        h_ref[rs[u], vsl[u]] = hid.astype(h_ref.dtype)


def mlstm_group(proj, dst, row0, bsz, t_pad, t_valid, chunk, c0, n0, m0, b_i, b_f, norm_w, nb=1):
    nc = t_pad // chunk
    rows_blk = nb * chunk
    blk0 = row0 // rows_blk
    assert row0 % rows_blk == 0 and t_pad % chunk == 0 and bsz % nb == 0 and (nb == 1 or nc == 1)
    assert chunk & (chunk - 1) == 0
    rmap = lambda c0_: (lambda b, c: (blk0 + b * nc + c, c0_))
    qk_w, v_w = ML_QK, ML_V
    in_specs = [
        pl.BlockSpec((rows_blk, qk_w), rmap(0)),
        pl.BlockSpec((rows_blk, qk_w), rmap(1)),
        pl.BlockSpec((rows_blk, v_w), rmap(1)),
        pl.BlockSpec((rows_blk, v_w), rmap(2)),
        pl.BlockSpec((rows_blk, LANES), rmap((2 * qk_w + 2 * v_w) // LANES)),
        pl.BlockSpec((nb, ML_HEADS, ML_DK, ML_DV), lambda b, c: (b, 0, 0, 0)),
        pl.BlockSpec((nb, ML_HEADS, ML_DK), lambda b, c: (b, 0, 0)),
        pl.BlockSpec((nb, 1, ML_HEADS), lambda b, c: (b, 0, 0)),
        pl.BlockSpec((1, v_w), lambda b, c: (0, 0)),
        pl.BlockSpec((1, 2 * ML_HEADS), lambda b, c: (0, 0)),
        pl.BlockSpec(memory_space=pl.ANY),
    ]
    out_specs = [
        pl.BlockSpec((rows_blk, v_w), lambda b, c: (blk0 + b * nc + c, 0)),
        pl.BlockSpec((nb, ML_HEADS, ML_DK, ML_DV), lambda b, c: (b, 0, 0, 0)),
        pl.BlockSpec((nb, ML_HEADS, ML_DK), lambda b, c: (b, 0, 0)),
        pl.BlockSpec((nb, 1, ML_HEADS), lambda b, c: (b, 0, 0)),
    ]
    out_shape = [
        jax.ShapeDtypeStruct(dst.shape, dst.dtype),
        jax.ShapeDtypeStruct((bsz, ML_HEADS, ML_DK, ML_DV), F32),
        jax.ShapeDtypeStruct((bsz, ML_HEADS, ML_DK), F32),
        jax.ShapeDtypeStruct((bsz, 1, ML_HEADS), F32),
    ]
    gate_bias = jnp.concatenate([b_i, b_f]).reshape(1, 2 * ML_HEADS)
    h, c, n, m = pl.pallas_call(
        functools.partial(_mlstm_kernel, chunk=chunk, t_valid=t_valid, nb=nb),
        grid=(bsz // nb, nc),
        in_specs=in_specs,
        out_specs=out_specs,
        out_shape=out_shape,
        input_output_aliases={10: 0},
        compiler_params=_cparams(("arbitrary", "arbitrary")),
        name="mlstm",
    )(proj, proj, proj, proj, proj, c0, n0, m0.reshape(bsz, 1, ML_HEADS), norm_w.reshape(1, v_w), gate_bias, dst)
    return h, c, n, m.reshape(bsz, ML_HEADS)


def _head_block_ones(width, head):
    idx = np.arange(width) // head
    return jnp.asarray((idx[:, None] == idx[None, :]).astype(np.float32), dtype=BF16)


def _rwkv_prep_kernel(cur_ref, tail_ref, first_ref, mu_ref, w0_ref, a0_ref, kk_ref, ka_ref, rk_ref,
                      w2_ref, a2_ref, g2_ref, blk_ref,
                      r_ref, wl_ref, k_ref, v_ref, an_ref, bb_ref, bonus_ref, g_ref,
                      *, tt, t_pad, multi):
    cur = cur_ref[...]
    rolled = pltpu.roll(cur, 1, 0)
    row = _iota2((tt, 1), 0)
    if multi:
        prev = jnp.where((row & (t_pad - 1)) == 0, first_ref[...], rolled)
    else:
        starts_seq = (pl.program_id(0) % (t_pad // tt)) == 0
        head = jnp.where(starts_seq, first_ref[0], tail_ref[SUBLANES - 1:SUBLANES, :])
        prev = jnp.where(row == 0, head, rolled)
    mixed = cur + (prev - cur) * mu_ref[...]
    w = RW_WIDTH
    r = mixed[:, 0:w]
    k = mixed[:, w:2 * w]
    v = mixed[:, 2 * w:3 * w]
    c0 = 3 * w
    wl = mixed[:, c0:c0 + RW_DECAY_LORA]
    al = mixed[:, c0 + RW_DECAY_LORA:c0 + RW_DECAY_LORA + RW_A_LORA]
    gl = mixed[:, c0 + RW_DECAY_LORA + RW_A_LORA:]
    w_log = -jnp.exp(-_softplus(-(w0_ref[...] + _dot(jnp.tanh(wl), w2_ref[...]))) - 0.5)
    a = _sigmoid(a0_ref[...] + _dot(al, a2_ref[...]))
    g = _dot(_sigmoid(gl), g2_ref[...])
    kk = k * kk_ref[...]
    blk = blk_ref[...]
    kkn = kk / jnp.maximum(jnp.sqrt(_dot_exact_rhs(kk * kk, blk)), 1e-12)
    k2 = k * (1.0 + (a - 1.0) * ka_ref[...])
    r_ref[...] = r
    wl_ref[...] = w_log
    k_ref[...] = k2
    v_ref[...] = v
    an_ref[...] = -kkn
    bb_ref[...] = kkn * a
    bonus_ref[...] = _dot_exact_rhs(r * k2 * rk_ref[...], blk) * v
    g_ref[...] = g


def rwkv_prep(proj, row0, bsz, t_pad, shift0, prm, tt):
    n = bsz * t_pad
    multi = tt > t_pad
    assert row0 % tt == 0 and n % tt == 0 and (tt % t_pad == 0 if multi else t_pad % tt == 0)
    blk0 = row0 // tt
    c = RW_COLS
    if multi:
        first = jnp.zeros((bsz, t_pad, c), F32).at[:, 0].set(shift0).reshape(n, c)
        first_spec = pl.BlockSpec((tt, c), lambda i: (i, 0))
    else:
        first = shift0.reshape(bsz, 1, c)
        first_spec = pl.BlockSpec((1, 1, c), lambda i: (i // (t_pad // tt), 0, 0))
    tail_blk = tt // SUBLANES
    vec = lambda width: pl.BlockSpec((1, width), lambda i: (0, 0))
    full = lambda a, b: pl.BlockSpec((a, b), lambda i: (0, 0))
    w = RW_WIDTH
    out_spec = pl.BlockSpec((tt, w), lambda i: (i, 0))
    return pl.pallas_call(
        functools.partial(_rwkv_prep_kernel, tt=tt, t_pad=t_pad, multi=multi),
        grid=(n // tt,),
        in_specs=[pl.BlockSpec((tt, c), lambda i: (blk0 + i, 0)),
                  pl.BlockSpec((SUBLANES, c), lambda i: (jnp.maximum((blk0 + i) * tail_blk - 1, 0), 0)),
                  first_spec, vec(c), vec(w), vec(w), vec(w), vec(w), vec(w),
                  full(RW_DECAY_LORA, w), full(RW_A_LORA, w), full(RW_GATE_LORA, w), full(w, w)],
        out_specs=[out_spec] * 8,
        out_shape=[jax.ShapeDtypeStruct((n, w), F32)] * 8,
        compiler_params=_cparams(("arbitrary",)),
        name="rwkv_prep",
    )(proj, proj, first, prm['mu'].reshape(1, c), prm['w0'].reshape(1, w), prm['a0'].reshape(1, w),
      prm['k_k'].reshape(1, w), prm['k_a'].reshape(1, w), prm['r_k'].reshape(1, w),
      prm['w2'], prm['a2'], prm['g2'], _head_block_ones(w, RW_DH))


def _neumann_inverse_many(n_mats, size):
    eye = (_iota2((size, size), 0) == _iota2((size, size), 1)).astype(F32)
    ts = [eye + n_mat for n_mat in n_mats]
    xs = list(n_mats)
    steps = max(int(np.ceil(np.log2(size))) - 1, 0)
    for _ in range(steps):
        xs = [_dot(x, x) for x in xs]
        ts = [t + _dot(t, x) for t, x in zip(ts, xs)]
    return ts


def _rwkv_chunk_kernel(r_ref, w_ref, k_ref, v_ref, a_ref, b_ref, bonus_ref, g_ref, s0_ref,
                       lnw_ref, lnb_ref, _dst_ref, o_ref, s_ref, acc_ref, *, chunk, t_valid, hb, nb):
    ci = pl.program_id(2)

    @pl.when(ci == 0)
    def _():
        s_ref[...] = s0_ref[...]

    size = chunk
    rows = _iota2((size, size), 0)
    cols = _iota2((size, size), 1)
    strict = cols < rows
    rows2 = _iota2((size, 2 * size), 0)
    cols2 = _iota2((size, 2 * size), 1)
    mask_ak = (cols2 >= size) & (cols2 - size < rows2)
    mask_o = jnp.where(cols2 >= size, cols2 - size, cols2) <= rows2
    slab = nb * size
    rows_s = _iota2((slab, slab), 0)
    cols_s = _iota2((slab, slab), 1)
    shift = size.bit_length() - 1
    same_seq_tril = ((rows_s >> shift) == (cols_s >> shift)) & (cols_s <= rows_s)
    valid = (ci * size + (_iota2((slab, 1), 0) & (size - 1))) < t_valid
    w = jnp.where(valid, w_ref[...], 0.0)
    a = jnp.where(valid, a_ref[...], 0.0)
    b = jnp.where(valid, b_ref[...], 0.0)
    k = jnp.where(valid, k_ref[...], 0.0)
    v = v_ref[...]
    lam = _dot_exact_lhs(same_seq_tril.astype(F32), w)
    e_pos = jnp.exp(lam)
    e_neg = jnp.exp(-lam)
    at = a * jnp.exp(lam - w)
    bt = b * e_neg
    kt = k * e_neg
    rt = r_ref[...] * e_pos
    units = [(sq, h) for sq in range(nb) for h in range(hb)]
    ids = range(len(units))
    rs = [slice(sq * size, (sq + 1) * size) for sq, _ in units]
    sls = [slice(h * RW_DH, (h + 1) * RW_DH) for _, h in units]
    at_h = [at[rs[u], sls[u]] for u in ids]
    rt_h = [rt[rs[u], sls[u]] for u in ids]
    v_h = [v[rs[u], sls[u]] for u in ids]
    bk = [jnp.concatenate([bt[rs[u], sls[u]], kt[rs[u], sls[u]]], axis=0) for u in ids]
    s0 = [s_ref[sq, h] for sq, h in units]
    pa = [_dot_nt(at_h[u], bk[u]) for u in ids]
    pr = [_dot_nt(rt_h[u], bk[u]) for u in ids]
    as0 = [_dot_nt(at_h[u], s0[u]) for u in ids]
    rs0 = [_dot_nt(rt_h[u], s0[u]) for u in ids]
    zv = [jnp.concatenate([jnp.zeros_like(v_h[u]), v_h[u]], axis=0) for u in ids]
    rhs = [as0[u] + _dot(jnp.where(mask_ak, pa[u], 0.0), zv[u]) for u in ids]
    t_inv = _neumann_inverse_many([jnp.where(strict, pa[u][:, :size], 0.0) for u in ids], size)
    uu = [_dot(t_inv[u], rhs[u]) for u in ids]
    uv = [jnp.concatenate([uu[u], v_h[u]], axis=0) for u in ids]
    o = [rs0[u] + _dot(jnp.where(mask_o, pr[u], 0.0), uv[u]) for u in ids]
    ds = [_dot_tn(uv[u], bk[u]) for u in ids]
    for u, (sq, h) in enumerate(units):
        last = (sq + 1) * size - 1
        s_ref[sq, h] = (s0[u] + ds[u]) * e_pos[last:last + 1, sls[u]]
        oc = o[u] - jnp.mean(o[u], axis=-1, keepdims=True)
        acc_ref[rs[u], sls[u]] = oc * lax.rsqrt(jnp.mean(oc * oc, axis=-1, keepdims=True) + RW_GN_EPS)
    out = (acc_ref[...] * lnw_ref[...] + lnb_ref[...] + bonus_ref[...]) * g_ref[...]
    o_ref[...] = out.astype(o_ref.dtype)


def rwkv_chunk(prep, dst, dst_col0, row0, bsz, t_pad, t_valid, chunk, s0, ln_w, ln_b, hb, nb=1):
    nc = t_pad // chunk
    hg = RW_HEADS // hb
    wb = hb * RW_DH
    rows_blk = nb * chunk
    blk0 = row0 // rows_blk
    assert dst_col0 % wb == 0 and row0 % rows_blk == 0 and bsz % nb == 0 and (nb == 1 or nc == 1)
    assert chunk & (chunk - 1) == 0
    tok = pl.BlockSpec((rows_blk, wb), lambda b, g, c: (b * nc + c, g))
    st = pl.BlockSpec((nb, hb, RW_DH, RW_DH), lambda b, g, c: (b, g, 0, 0))
    vec = pl.BlockSpec((1, wb), lambda b, g, c: (0, g))
    return pl.pallas_call(
        functools.partial(_rwkv_chunk_kernel, chunk=chunk, t_valid=t_valid, hb=hb, nb=nb),
        grid=(bsz // nb, hg, nc),
        in_specs=[tok] * 8 + [st, vec, vec, pl.BlockSpec(memory_space=pl.ANY)],
        out_specs=[pl.BlockSpec((rows_blk, wb), lambda b, g, c: (blk0 + b * nc + c, dst_col0 // wb + g)), st],
        out_shape=[jax.ShapeDtypeStruct(dst.shape, dst.dtype),
                   jax.ShapeDtypeStruct((bsz, RW_HEADS, RW_DH, RW_DH), F32)],
        scratch_shapes=[pltpu.VMEM((rows_blk, wb), F32)],
        input_output_aliases={11: 0},
        compiler_params=_cparams(("arbitrary", "arbitrary", "arbitrary")),
        name="rwkv_chunk",
    )(*prep, s0, ln_w.reshape(1, RW_WIDTH), ln_b.reshape(1, RW_WIDTH), dst)


def _conv_silu(cur_ref, hist_ref, cw_ref, size):
    cur = cur_ref[...]
    hist = hist_ref[...]
    cw = cw_ref[...]
    row8 = _iota2((SUBLANES, 1), 0)
    acc = cur * cw[GD_CONV - 1:GD_CONV, :]
    for j in range(1, GD_CONV):
        back = GD_CONV - 1 - j
        rolled = pltpu.roll(cur, j, 0)
        head = jnp.where(row8 < j, pltpu.roll(hist, j, 0), rolled[:SUBLANES])
        prev = head if size == SUBLANES else jnp.concatenate([head, rolled[SUBLANES:]], axis=0)
        acc = acc + prev * cw[back:back + 1, :]
    hist_ref[...] = cur[size - SUBLANES:size]
    return acc * _sigmoid(acc)


def _l2_normalize(x, scale):
    return x * lax.rsqrt(jnp.sum(x * x, axis=-1, keepdims=True) + GD_EPS) * scale


def _gdn_chunk_kernel(q_ref, k_ref, v_ref, z_ref, gl_ref, bq_ref, bk_ref, bv_ref, cq_ref, ck_ref, cv_ref,
                      al_ref, dt_ref, nw_ref, s0_ref, _dst_ref,
                      o_ref, s_ref, nq_ref, nk_ref, nv_ref, hq_ref, hk_ref, hv_ref,
                      *, chunk, t_valid, hpg, tail_chunk, tail_row0):
    ci = pl.program_id(2)
    hg = pl.program_id(1)

    @pl.when(ci == 0)
    def _():
        s_ref[...] = s0_ref[...]
        hq_ref[...] = bq_ref[0]
        hk_ref[...] = bk_ref[0]
        hv_ref[...] = bv_ref[0]

    @pl.when(ci == tail_chunk)
    def _():
        tail = slice(tail_row0, tail_row0 + SUBLANES)
        nq_ref[0] = q_ref[tail, :]
        nk_ref[0] = k_ref[tail, :]
        nv_ref[0] = v_ref[tail, :]

    size = chunk
    q_act = _conv_silu(q_ref, hq_ref, cq_ref, size)
    k_act = _conv_silu(k_ref, hk_ref, ck_ref, size)
    v_act = _conv_silu(v_ref, hv_ref, cv_ref, size)
    rows = _iota2((size, size), 0)
    cols = _iota2((size, size), 1)
    eye = rows == cols
    incl = cols <= rows
    strict = cols < rows
    valid = (ci * size + _iota2((size, 1), 0)) < t_valid
    logits = gl_ref[...]
    beta_all = _sigmoid(logits)
    g_all = -jnp.exp(al_ref[...]) * _softplus(logits + dt_ref[...])
    lane = _iota2((size, LANES), 1)
    rep = GD_V_HEADS // GD_K_HEADS
    kheads = range(hpg // rep)
    heads = range(hpg)
    q = [_l2_normalize(q_act[:, kh * GD_DK:(kh + 1) * GD_DK], GD_DK ** -0.5) for kh in kheads]
    k = [_l2_normalize(k_act[:, kh * GD_DK:(kh + 1) * GD_DK], 1.0) for kh in kheads]
    qk = [_dot_nt(q[kh], k[kh]) for kh in kheads]
    kk = [_dot_nt(k[kh], k[kh]) for kh in kheads]
    vsl = [slice(hl * GD_DV, (hl + 1) * GD_DV) for hl in heads]
    s = [s_ref[0, hl] for hl in heads]
    beta, gc, decay = [], [], []
    for hl in heads:
        head = hg * hpg + hl
        beta_h = jnp.sum(jnp.where(lane == head, beta_all, 0.0), axis=1, keepdims=True)
        g = jnp.sum(jnp.where(lane == head + GD_V_HEADS, g_all, 0.0), axis=1, keepdims=True)
        beta_h = jnp.where(valid, beta_h, 0.0)
        g = jnp.where(valid, g, 0.0)
        g_row = _col_to_row(g, eye)
        gc_h = jnp.sum(jnp.where(incl, g_row, 0.0), axis=1, keepdims=True)
        gc_row = jnp.sum(jnp.where(rows <= cols, g, 0.0), axis=0, keepdims=True)
        beta.append(beta_h)
        gc.append(gc_h)
        decay.append(jnp.where(incl, jnp.exp(jnp.where(incl, gc_h - gc_row, 0.0)), 0.0))
    qs = [_dot(q[hl // rep] * jnp.exp(gc[hl]), s[hl]) for hl in heads]
    t_inv = _neumann_inverse_many(
        [-jnp.where(strict, kk[hl // rep] * beta[hl] * decay[hl], 0.0) for hl in heads], size)
    uw = [_dot(t_inv[hl], jnp.concatenate([v_act[:, vsl[hl]] * beta[hl],
                                           k[hl // rep] * (beta[hl] * jnp.exp(gc[hl]))], axis=1))
          for hl in heads]
    ws = [_dot(uw[hl][:, GD_DV:], s[hl]) for hl in heads]
    v_new = [uw[hl][:, :GD_DV] - ws[hl] for hl in heads]
    av = [_dot(qk[hl // rep] * decay[hl], v_new[hl]) for hl in heads]
    g_last = [gc[hl][size - 1:size, :] for hl in heads]
    kv = [_dot_tn(k[hl // rep] * jnp.exp(g_last[hl] - gc[hl]), v_new[hl]) for hl in heads]
    for hl in heads:
        s_ref[0, hl] = s[hl] * jnp.exp(g_last[hl]) + kv[hl]
        o = qs[hl] + av[hl]
        o = o * lax.rsqrt(jnp.mean(o * o, axis=-1, keepdims=True) + GD_EPS) * nw_ref[...]
        z = z_ref[:, vsl[hl]]
        o_ref[:, vsl[hl]] = (o * (z * _sigmoid(z))).astype(o_ref.dtype)


def gdn_chunk(proj, gates, conv_buf, conv_w, dst, row0, bsz, t_pad, t_valid, chunk, s0, a_log, dt_bias,
              norm_w, hpg):
    nc = t_pad // chunk
    hgs = GD_V_HEADS // hpg
    rep = GD_V_HEADS // GD_K_HEADS
    kw = (hpg // rep) * GD_DK
    vw = hpg * GD_DV
    blk0 = row0 // chunk
    assert row0 % chunk == 0 and chunk % SUBLANES == 0
    pad_row = lambda x: jnp.pad(x, (GD_V_HEADS, LANES - 2 * GD_V_HEADS)).reshape(1, LANES)
    hist = jnp.pad(conv_buf, ((0, 0), (SUBLANES - (GD_CONV - 1), 0), (0, 0)))
    st = pl.BlockSpec((1, hpg, GD_DK, GD_DV), lambda b, g, c: (b, g, 0, 0))
    vec = pl.BlockSpec((1, LANES), lambda b, g, c: (0, 0))
    q_col, k_col, v_col = (lambda g: g), (lambda g: GD_QK // kw + g), (lambda g: 2 * GD_QK // vw + g)
    tok = lambda width, col: pl.BlockSpec((chunk, width), lambda b, g, c: (blk0 + b * nc + c, col(g)))
    buf = lambda width, col: pl.BlockSpec((1, SUBLANES, width), lambda b, g, c: (b, 0, col(g)))
    taps = lambda width, col: pl.BlockSpec((GD_CONV, width), lambda b, g, c: (0, col(g)))
    last = (t_valid - 1) % chunk
    tail_row0 = (last // SUBLANES) * SUBLANES
    first = last - (GD_CONV - 2) - tail_row0
    assert first >= 0
    new_buf = lambda width: jax.ShapeDtypeStruct((bsz, SUBLANES, width), F32)
    out, state, nq, nk, nv = pl.pallas_call(
        functools.partial(_gdn_chunk_kernel, chunk=chunk, t_valid=t_valid, hpg=hpg,
                          tail_chunk=(t_valid - 1) // chunk, tail_row0=tail_row0),
        grid=(bsz, hgs, nc),
        in_specs=[tok(kw, q_col), tok(kw, k_col), tok(vw, v_col),
                  tok(vw, lambda g: GD_CONV_DIM // vw + g),
                  pl.BlockSpec((chunk, LANES), lambda b, g, c: (blk0 + b * nc + c, 0)),
                  buf(kw, q_col), buf(kw, k_col), buf(vw, v_col),
                  taps(kw, q_col), taps(kw, k_col), taps(vw, v_col),
                  vec, vec, vec, st, pl.BlockSpec(memory_space=pl.ANY)],
        out_specs=[pl.BlockSpec((chunk, vw), lambda b, g, c: (blk0 + b * nc + c, g)), st,
                   pl.BlockSpec((1, SUBLANES, kw), lambda b, g, c: (b, 0, g)),
                   pl.BlockSpec((1, SUBLANES, kw), lambda b, g, c: (b, 0, g)),
                   pl.BlockSpec((1, SUBLANES, vw), lambda b, g, c: (b, 0, g))],
        out_shape=[jax.ShapeDtypeStruct(dst.shape, dst.dtype),
                   jax.ShapeDtypeStruct((bsz, GD_V_HEADS, GD_DK, GD_DV), F32),
                   new_buf(GD_QK), new_buf(GD_QK), new_buf(GD_V_WIDTH)],
        scratch_shapes=[pltpu.VMEM((SUBLANES, kw), F32), pltpu.VMEM((SUBLANES, kw), F32),
                        pltpu.VMEM((SUBLANES, vw), F32)],
        input_output_aliases={15: 0},
        compiler_params=_cparams(("arbitrary", "arbitrary", "arbitrary")),
        name="gdn_chunk",
    )(proj, proj, proj, proj, gates, hist, hist, hist, conv_w, conv_w, conv_w,
      pad_row(a_log), pad_row(dt_bias), norm_w.reshape(1, GD_DV), s0, dst)
    conv_state = jnp.concatenate([nq, nk, nv], axis=-1)[:, first:first + GD_CONV - 1]
    return out, state, conv_state


MOE_TM = 256


def _router_kernel(x_ref, w_ref, wt_ref, id_ref):
    logits = jnp.dot(x_ref[...], w_ref[...].astype(BF16), preferred_element_type=F32)
    lane = _iota2(logits.shape, 1).astype(F32)
    first_of = lambda hit: jnp.min(jnp.where(hit, lane, float(LANES)), axis=1, keepdims=True)
    gl = jnp.where(lane < MOE_GROUPS, logits, NEG)
    gmax = jnp.max(gl, axis=1, keepdims=True)
    g_val = 1.0 / jnp.sum(jnp.exp(gl - gmax), axis=1, keepdims=True)
    lo = MOE_GROUPS + first_of(gl == gmax) * MOE_PER_GROUP
    vals = jnp.where((lane >= lo) & (lane < lo + MOE_PER_GROUP), logits, NEG)
    top1 = jnp.max(vals, axis=1, keepdims=True)
    i1 = first_of(vals == top1)
    vals2 = jnp.where(lane == i1, NEG, vals)
    top2 = jnp.max(vals2, axis=1, keepdims=True)
    i2 = first_of(vals2 == top2)
    e2 = jnp.exp(top2 - top1)
    w1 = (1.0 / (1.0 + e2)) * g_val
    w2 = (e2 / (1.0 + e2)) * g_val
    wt_ref[...] = jnp.where(lane == 0, w1, jnp.where(lane == 1, w2, 0.0))
    ids = jnp.where(lane == 0, i1 - MOE_GROUPS, jnp.where(lane == 1, i2 - MOE_GROUPS, 0.0))
    id_ref[...] = ids.astype(jnp.int32)


def moe_router(xb, w_router, tm=512):
    n, d = xb.shape
    out = pl.BlockSpec((tm, LANES), lambda i: (i, 0))
    return pl.pallas_call(
        _router_kernel,
        grid=(n // tm,),
        in_specs=[pl.BlockSpec((tm, d), lambda i: (i, 0)), pl.BlockSpec((d, LANES), lambda i: (0, 0))],
        out_specs=[out, out],
        out_shape=[jax.ShapeDtypeStruct((n, LANES), F32), jax.ShapeDtypeStruct((n, LANES), jnp.int32)],
        compiler_params=_cparams(("arbitrary",)),
        name="moe_router",
    )(xb, w_router)


def _moe_ffn_kernel(te_ref, tv_ref, idx_ref, idx_next_ref, x_hbm, wg_ref, wu_ref, wd_ref, o_ref,
                    xbuf, wgb, wub, wdb, sem):
    i = pl.program_id(0)
    n_tiles = pl.num_programs(0)
    slot = i % 2

    def gather(rows_ref, into):
        for r in range(MOE_TM):
            pltpu.make_async_copy(x_hbm.at[pl.ds(rows_ref[0, 0, r], 1)],
                                  xbuf.at[into, pl.ds(r, 1)], sem.at[into]).start()

    @pl.when((i == 0) & (tv_ref[0] == 1))
    def _():
        gather(idx_ref, 0)

    nxt = jnp.minimum(i + 1, n_tiles - 1)

    @pl.when((i + 1 < n_tiles) & (tv_ref[nxt] == 1))
    def _():
        gather(idx_next_ref, 1 - slot)

    @pl.when(tv_ref[i] == 1)
    def _():
        pltpu.make_async_copy(xbuf.at[slot], xbuf.at[slot], sem.at[slot]).wait()

        @pl.when((i == 0) | (te_ref[i] != te_ref[jnp.maximum(i - 1, 0)]))
        def _():
            wgb[...] = wg_ref[0, 0].astype(BF16)
            wub[...] = wu_ref[0, 0].astype(BF16)
            wdb[...] = wd_ref[0, 0].astype(BF16)

        x = xbuf[slot].astype(BF16)
        gate = jnp.dot(x, wgb[...], preferred_element_type=F32)
        up = jnp.dot(x, wub[...], preferred_element_type=F32)
        hid = gate * _sigmoid(gate) * up
        o_ref[...] = jnp.dot(hid.astype(BF16), wdb[...], preferred_element_type=F32)

    @pl.when(tv_ref[i] == 0)
    def _():
        o_ref[...] = jnp.zeros_like(o_ref)


def moe_ffn(x, row_token, tile_expert, tile_valid, layer, w_gate, w_up, w_down):
    n_tiles = row_token.shape[0]
    d = x.shape[1]
    ff = w_gate.shape[3]
    grid_spec = pltpu.PrefetchScalarGridSpec(
        num_scalar_prefetch=2,
        grid=(n_tiles,),
        in_specs=[pl.BlockSpec((1, 1, MOE_TM), lambda i, te, tv: (i, 0, 0), memory_space=pltpu.SMEM),
                  pl.BlockSpec((1, 1, MOE_TM), lambda i, te, tv: (jnp.minimum(i + 1, n_tiles - 1), 0, 0),
                               memory_space=pltpu.SMEM),
                  pl.BlockSpec(memory_space=pl.ANY),
                  pl.BlockSpec((1, 1, d, ff), lambda i, te, tv: (layer, te[i], 0, 0)),
                  pl.BlockSpec((1, 1, d, ff), lambda i, te, tv: (layer, te[i], 0, 0)),
                  pl.BlockSpec((1, 1, ff, d), lambda i, te, tv: (layer, te[i], 0, 0))],
        out_specs=pl.BlockSpec((MOE_TM, d), lambda i, te, tv: (i, 0)),
        scratch_shapes=[pltpu.VMEM((2, MOE_TM, d), F32),
                        pltpu.VMEM((d, ff), BF16), pltpu.VMEM((d, ff), BF16), pltpu.VMEM((ff, d), BF16),
                        pltpu.SemaphoreType.DMA((2,))],
    )
    return pl.pallas_call(
        _moe_ffn_kernel,
        grid_spec=grid_spec,
        out_shape=jax.ShapeDtypeStruct((n_tiles * MOE_TM, d), F32),
        compiler_params=_cparams(("arbitrary",)),
        name="moe_ffn",
    )(tile_expert, tile_valid, row_token, row_token, x, w_gate, w_up, w_down)


def _ln_moe_ple_kernel(x_ref, ya_ref, yb_ref, wt_ref, g_ref, b_ref, p_ref, wg_ref, wp_ref, o_ref, ob_ref):
    wt = wt_ref[...]
    z = DEEPNORM_ALPHA * x_ref[...] + (wt[:, 0:1] * ya_ref[...] + wt[:, 1:2] * yb_ref[...])
    zc = z - jnp.mean(z, axis=-1, keepdims=True)
    var = jnp.mean(zc * zc, axis=-1, keepdims=True)
    hid = zc * lax.rsqrt(var + LN_EPS) * g_ref[...] + b_ref[...]
    gate = _sigmoid(jnp.dot(hid.astype(BF16), wg_ref[...], preferred_element_type=F32))
    emb = jnp.dot(p_ref[...].astype(BF16), wp_ref[...], preferred_element_type=F32)
    out = hid + gate * emb
    o_ref[...] = out
    ob_ref[...] = out.astype(BF16)


def ln_moe_ple(x, ya, yb, wt, g, b, p, wg, wp, tm=512):
    m, d = x.shape
    pd = p.shape[1]
    tm = _row_tile(m, tm)
    row = pl.BlockSpec((tm, d), lambda i: (i, 0))
    vec = pl.BlockSpec((1, d), lambda i: (0, 0))
    once = pl.Buffered(1)
    return pl.pallas_call(
        _ln_moe_ple_kernel,
        grid=(m // tm,),
        in_specs=[row, row, row, pl.BlockSpec((tm, LANES), lambda i: (i, 0)), vec, vec,
                  pl.BlockSpec((tm, pd), lambda i: (i, 0)),
                  pl.BlockSpec((d, d), lambda i: (0, 0), pipeline_mode=once),
                  pl.BlockSpec((pd, d), lambda i: (0, 0), pipeline_mode=once)],
        out_specs=[row, row],
        out_shape=[jax.ShapeDtypeStruct((m, d), F32), jax.ShapeDtypeStruct((m, d), BF16)],
        compiler_params=_cparams(("arbitrary",)),
        name="ln_moe_ple",
    )(x, ya, yb, wt, g.reshape(1, d), b.reshape(1, d), p, wg, wp)


def _table_lookup(table, idx):
    hit = idx[:, None] == jnp.arange(table.shape[0], dtype=idx.dtype)[None, :]
    return jnp.sum(jnp.where(hit, table[None, :], 0), axis=1)


def hier_moe(x, xb, layer, w_group, w_expert, w_gate, w_up, w_down):
    n = xb.shape[0]
    assert (n * MOE_TOPK) % MOE_TM == 0
    w_router = jnp.pad(jnp.concatenate([w_group, w_expert], axis=1),
                       ((0, 0), (0, LANES - MOE_GROUPS - MOE_EXPERTS)))
    wt, ids = moe_router(xb, w_router)

    n_pairs = n * MOE_TOPK
    n_tiles = -(-n_pairs // MOE_TM) + MOE_EXPERTS
    n_rows = n_tiles * MOE_TM
    i32 = jnp.int32
    eid = ids[:, :MOE_TOPK].reshape(n_pairs)
    order = jnp.argsort(eid, stable=True).astype(i32)
    rank = jnp.argsort(order).astype(i32)
    experts = jnp.arange(MOE_EXPERTS, dtype=i32)
    counts = jnp.sum((eid[:, None] == experts[None, :]).astype(i32), axis=0)
    starts = jnp.cumsum(counts) - counts
    tiles_per = (counts + MOE_TM - 1) // MOE_TM
    tile_starts = jnp.cumsum(tiles_per) - tiles_per
    dest = _table_lookup(tile_starts * MOE_TM - starts, eid) + rank
    tile_idx = jnp.arange(n_tiles, dtype=i32)
    tile_expert = jnp.sum((tile_idx[:, None] >= (tile_starts + tiles_per)[None, :]).astype(i32), axis=1)
    tile_valid = (tile_expert < MOE_EXPERTS).astype(i32)
    last_used = jnp.max(jnp.where(tiles_per > 0, experts, 0))
    tile_expert = jnp.where(tile_valid == 1, tile_expert, last_used)
    off = (tile_idx - _table_lookup(tile_starts, tile_expert))[:, None] * MOE_TM + jnp.arange(MOE_TM, dtype=i32)[None, :]
    cnt = _table_lookup(counts, tile_expert)[:, None]
    src = jnp.clip(_table_lookup(starts, tile_expert)[:, None] + off, 0, n_pairs - 1)
    row_ok = (off < cnt) & (tile_valid[:, None] == 1)
    picked = jnp.take(order, src.reshape(n_rows), mode='clip').reshape(n_tiles, MOE_TM)
    row_token = jnp.where(row_ok, picked // MOE_TOPK, 0).reshape(n_tiles, 1, MOE_TM)

    y_sorted = moe_ffn(x, row_token, tile_expert, tile_valid, layer, w_gate, w_up, w_down)
    dest2 = dest.reshape(n, MOE_TOPK)
    return (jnp.take(y_sorted, dest2[:, 0], axis=0, mode='clip'),
            jnp.take(y_sorted, dest2[:, 1], axis=0, mode='clip'), wt)


SAMPLE_T_PAD = 8


def kernel(x_prompt, x_sample, state_mlstm_c, state_mlstm_n, state_mlstm_m, state_rwkv_s,
           state_rwkv_shift, state_gdn_s, state_gdn_conv, p_prompt, p_sample,
           w_in_ab, ml_b_i, ml_b_f, ml_norm_w, rw_mu, rw_w0, rw_w2, rw_a0, rw_a2, rw_g2,
           rw_k_k, rw_k_a, rw_r_k, rw_ln_w, rw_ln_b, w_out_ab,
           gd_w_in, gd_conv_w, gd_a_log, gd_dt_bias, gd_norm_w, gd_w_out,
           ln_mix_g, ln_mix_b, moe_w_group, moe_w_expert, moe_w_gate, moe_w_up, moe_w_down,
           ln_ffn_g, ln_ffn_b, ple_w_gate, ple_w_proj):
    pb, pt, d = x_prompt.shape
    sb, st, _ = x_sample.shape
    n_p = pb * pt
    n_s = sb * SAMPLE_T_PAD
    n = n_p + n_s
    pad_t = lambda a: jnp.pad(a, ((0, 0), (0, SAMPLE_T_PAD - st), (0, 0)))
    merge = lambda a_p, a_s: jnp.concatenate(
        [a_p.reshape(n_p, a_p.shape[-1]), pad_t(a_s).reshape(n_s, a_s.shape[-1])], axis=0)
    x = merge(x_prompt, x_sample)
    xb = x.astype(BF16)
    segs = (dict(row0=0, bsz=pb, t_pad=pt, t_valid=pt, ml_nb=1, rw_hb=16, rw_nb=1, gd_hpg=16),
            dict(row0=n_p, bsz=sb, t_pad=SAMPLE_T_PAD, t_valid=st, ml_nb=2, rw_hb=16, rw_nb=2, gd_hpg=32))
    zeros = lambda *shape: jnp.zeros(shape, F32)

    def last_rows(a, sg, count):
        assert sg['t_valid'] >= count
        idx = (sg['row0'] + jnp.arange(sg['bsz'])[:, None] * sg['t_pad']
               + (sg['t_valid'] - count + jnp.arange(count))[None, :])
        return jnp.take(a, idx.reshape(-1), axis=0, mode='clip').reshape(sg['bsz'], count, a.shape[-1])

    new_states = [[[] for _ in range(7)] for _ in segs]
    for layer in range(DEPTH):
        li = layer // 2
        if layer % 2 == 0:
            w_in_t = jnp.transpose(w_in_ab[li])
            proj_ml = matmul_wt(xb, w_in_t, 0, 3584, 896)
            proj_rw = matmul_wt(xb, w_in_t, ML_COLS, RW_COLS, RW_COLS // 2)
            rw_prm = dict(mu=rw_mu[li], w0=rw_w0[li], w2=rw_w2[li], a0=rw_a0[li], a2=rw_a2[li], g2=rw_g2[li],
                          k_k=rw_k_k[li], k_a=rw_k_a[li], r_k=rw_r_k[li])
            heads_out = jnp.zeros((n, ML_V + RW_WIDTH), BF16)
            for si, sg in enumerate(segs):
                bsz, t_pad, t_valid, row0 = sg['bsz'], sg['t_pad'], sg['t_valid'], sg['row0']
                if si == 0:
                    c0, n0, m0 = zeros(bsz, ML_HEADS, ML_DK, ML_DV), zeros(bsz, ML_HEADS, ML_DK), zeros(bsz, ML_HEADS)
                    s0, sh0 = zeros(bsz, RW_HEADS, RW_DH, RW_DH), zeros(bsz, RW_COLS)
                else:
                    c0, n0, m0 = state_mlstm_c[li], state_mlstm_n[li], state_mlstm_m[li]
                    s0, sh0 = state_rwkv_s[li], state_rwkv_shift[li]
                heads_out, c, nn, m = mlstm_group(proj_ml, heads_out, row0, bsz, t_pad, t_valid,
                                                  min(ML_CHUNK, t_pad), c0, n0, m0,
                                                  ml_b_i[li], ml_b_f[li], ml_norm_w[li], sg['ml_nb'])
                prep = rwkv_prep(proj_rw, row0, bsz, t_pad, sh0, rw_prm, 256)
                heads_out, rs = rwkv_chunk(prep, heads_out, ML_V, row0, bsz, t_pad, t_valid,
                                           min(RW_CHUNK, t_pad), s0, rw_ln_w[li], rw_ln_b[li],
                                           sg['rw_hb'], sg['rw_nb'])
                new_shift = last_rows(proj_rw, sg, 1)[:, 0]
                for slot, val in zip(range(5), (c, nn, m, rs, new_shift)):
                    new_states[si][slot].append(val)
            mix = matmul(heads_out, w_out_ab[li], d, 1024)
        else:
            w_in_t = jnp.transpose(gd_w_in[li])
            n_qkvz = GD_CONV_DIM + GD_V_WIDTH
            proj = matmul_wt(xb, w_in_t, 0, n_qkvz, 1024)
            gates = matmul_wt(xb, w_in_t, n_qkvz, LANES, LANES)
            gd_out = jnp.zeros((n, GD_V_WIDTH), BF16)
            for si, sg in enumerate(segs):
                bsz, t_pad, t_valid, row0 = sg['bsz'], sg['t_pad'], sg['t_valid'], sg['row0']
                if si == 0:
                    s0, buf0 = zeros(bsz, GD_V_HEADS, GD_DK, GD_DV), zeros(bsz, GD_CONV - 1, GD_CONV_DIM)
                else:
                    s0, buf0 = state_gdn_s[li], state_gdn_conv[li]
                gd_out, gs, gcb = gdn_chunk(proj, gates, buf0, gd_conv_w[li], gd_out, row0, bsz, t_pad, t_valid,
                                            min(GD_CHUNK, t_pad), s0, gd_a_log[li], gd_dt_bias[li],
                                            gd_norm_w[li], sg['gd_hpg'])
                new_states[si][5].append(gs)
                new_states[si][6].append(gcb)
            mix = matmul(gd_out, gd_w_out[li], d, 512)
        x, xb = ln_residual(x, mix, ln_mix_g[layer], ln_mix_b[layer])
        ya, yb, wt = hier_moe(x, xb, layer, moe_w_group[layer], moe_w_expert[layer],
                              moe_w_gate, moe_w_up, moe_w_down)
        x, xb = ln_moe_ple(x, ya, yb, wt, ln_ffn_g[layer], ln_ffn_b[layer],
                           merge(p_prompt[layer], p_sample[layer]),
                           ple_w_gate[layer].astype(BF16), ple_w_proj[layer].astype(BF16))
    y_prompt = x[:n_p].reshape(pb, pt, d)
    y_sample = x[n_p:].reshape(sb, SAMPLE_T_PAD, d)[:, :st]
    stack = lambda vals: jnp.stack(vals)
    return (y_prompt, y_sample) + tuple(stack(v) for v in new_states[0]) + tuple(stack(v) for v in new_states[1])
```

```python
import functools

import jax
import jax.numpy as jnp
import numpy as np
from jax import lax
from jax.experimental import pallas as pl
from jax.experimental.pallas import tpu as pltpu

F32 = jnp.float32
BF16 = jnp.bfloat16

D_MODEL = 2048
DEPTH = 2
DEEPNORM_ALPHA = (2 * DEPTH) ** 0.25
LN_EPS = 1e-5
MIX_HALF = D_MODEL // 2
ML_HEADS = 4
ML_DV = MIX_HALF // ML_HEADS
ML_DK = ML_DV // 2
ML_CHUNK = 64
ML_GATE_CAP = 15.0
ML_NORM_EPS = 1e-6
ML_QK = ML_HEADS * ML_DK
ML_V = ML_HEADS * ML_DV
ML_COLS = 2 * ML_QK + 2 * ML_V + 2 * ML_HEADS
RW_DH = 64
RW_HEADS = MIX_HALF // RW_DH
RW_WIDTH = RW_HEADS * RW_DH
RW_DECAY_LORA = 64
RW_A_LORA = 64
RW_GATE_LORA = 128
RW_GN_EPS = 64e-5
RW_COLS = 3 * RW_WIDTH + RW_DECAY_LORA + RW_A_LORA + RW_GATE_LORA
RW_CHUNK = 64
GD_DK = 128
GD_DV = 128
GD_K_HEADS = D_MODEL // GD_DK
GD_V_HEADS = 2 * GD_K_HEADS
GD_CONV = 4
GD_CHUNK = 64
GD_EPS = 1e-6
GD_QK = GD_K_HEADS * GD_DK
GD_V_WIDTH = GD_V_HEADS * GD_DV
GD_CONV_DIM = 2 * GD_QK + GD_V_WIDTH
MOE_GROUPS = 4
MOE_PER_GROUP = 8
MOE_EXPERTS = MOE_GROUPS * MOE_PER_GROUP
MOE_TOPK = 2
MOE_FF = D_MODEL // 4
PLE_DIM = 256

LANES = 128
SUBLANES = 8
NEG = -1e30
VMEM_LIMIT = 56 * 1024 * 1024


def _cparams(sem):
    return pltpu.CompilerParams(dimension_semantics=sem, vmem_limit_bytes=VMEM_LIMIT)


def _dot(a, b):
    return jnp.dot(a.astype(BF16), b.astype(BF16), preferred_element_type=F32)


def _dot_nt(a, b):
    return lax.dot_general(a.astype(BF16), b.astype(BF16), (((1,), (1,)), ((), ())),
                           preferred_element_type=F32)


def _dot_tn(a, b):
    return lax.dot_general(a.astype(BF16), b.astype(BF16), (((0,), (0,)), ((), ())),
                           preferred_element_type=F32)


def _split3(x):
    hi = x.astype(BF16)
    r1 = x - hi.astype(F32)
    mid = r1.astype(BF16)
    lo = (r1 - mid.astype(F32)).astype(BF16)
    return hi, mid, lo


def _dot_exact_rhs(a, b01):
    hi, mid, lo = _split3(a)
    b = b01.astype(BF16)
    return (jnp.dot(hi, b, preferred_element_type=F32) + jnp.dot(mid, b, preferred_element_type=F32)
            + jnp.dot(lo, b, preferred_element_type=F32))


def _dot_exact_lhs(a01, b):
    hi, mid, lo = _split3(b)
    a = a01.astype(BF16)
    return (jnp.dot(a, hi, preferred_element_type=F32) + jnp.dot(a, mid, preferred_element_type=F32)
            + jnp.dot(a, lo, preferred_element_type=F32))


def _iota2(shape, dim):
    return lax.broadcasted_iota(jnp.int32, shape, dim)


def _col_to_row(col, eye):
    return jnp.sum(jnp.where(eye, col, 0.0), axis=0, keepdims=True)


def _sigmoid(x):
    return 1.0 / (1.0 + jnp.exp(-x))


def _softplus(x):
    return jnp.maximum(x, 0.0) + jnp.log1p(jnp.exp(-jnp.abs(x)))


def _neumann_inverse(n_mat, size):
    eye = (_iota2((size, size), 0) == _iota2((size, size), 1)).astype(F32)
    t = eye + n_mat
    x = n_mat
    steps = max(int(np.ceil(np.log2(size))) - 1, 0)
    for _ in range(steps):
        x = _dot(x, x)
        t = t + _dot(t, x)
    return t


def _mm_kernel(x_ref, w_ref, o_ref, wb_ref, *, n_valid):
    @pl.when(pl.program_id(1) == 0)
    def _():
        wb_ref[...] = w_ref[...].astype(BF16)

    acc = jnp.dot(x_ref[...].astype(BF16), wb_ref[...], preferred_element_type=F32)
    if n_valid is not None:
        acc = jnp.where(_iota2(acc.shape, 1) < n_valid, acc, 0.0)
    o_ref[...] = acc.astype(o_ref.dtype)


def _row_tile(m, want):
    while m % want:
        want //= 2
    return want


def matmul(x, w, n_out, tn, tm=1024, out_dtype=F32, col0=0):
    m, k = x.shape
    tm = _row_tile(m, tm)
    assert w.shape[0] == k and n_out % tn == 0 and m % tm == 0 and tn % LANES == 0 and col0 % tn == 0
    n_valid = None
    if col0 + n_out > w.shape[1]:
        assert n_out == tn
        n_valid = w.shape[1] - col0
    blk0 = col0 // tn
    return pl.pallas_call(
        functools.partial(_mm_kernel, n_valid=n_valid),
        grid=(n_out // tn, m // tm),
        in_specs=[pl.BlockSpec((tm, k), lambda j, i: (i, 0)),
                  pl.BlockSpec((k, tn), lambda j, i: (0, blk0 + j))],
        out_specs=pl.BlockSpec((tm, tn), lambda j, i: (i, j)),
        out_shape=jax.ShapeDtypeStruct((m, n_out), out_dtype),
        scratch_shapes=[pltpu.VMEM((k, tn), BF16)],
        compiler_params=_cparams(("arbitrary", "arbitrary")),
        name="matmul",
    )(x, w)


def _mm_wt_kernel(x_ref, wt_hbm, o_ref, wf_ref, wb_ref, sem, *, row0, tn, n_rows):
    j = pl.program_id(0)

    @pl.when(pl.program_id(1) == 0)
    def _():
        if n_rows < tn:
            wf_ref[...] = jnp.zeros_like(wf_ref)
        start = pl.multiple_of(row0 + j * tn, SUBLANES)
        copy = pltpu.make_async_copy(wt_hbm.at[pl.ds(start, n_rows)], wf_ref.at[pl.ds(0, n_rows)], sem)
        copy.start()
        copy.wait()
        k = wf_ref.shape[1]
        step = 2 * LANES
        for c in range(k // step):
            wb_ref[c * step:(c + 1) * step, :] = wf_ref[:, c * step:(c + 1) * step].T.astype(BF16)

    o_ref[...] = jnp.dot(x_ref[...].astype(BF16), wb_ref[...], preferred_element_type=F32)


def matmul_wt(x, wt, row0, n_out, tn, tm=1024):
    m, k = x.shape
    tm = _row_tile(m, tm)
    assert wt.shape[1] == k and n_out % tn == 0 and tn % LANES == 0 and row0 % SUBLANES == 0
    n_rows = tn
    if row0 + n_out > wt.shape[0]:
        assert n_out == tn
        n_rows = wt.shape[0] - row0
    assert n_rows % SUBLANES == 0
    return pl.pallas_call(
        functools.partial(_mm_wt_kernel, row0=row0, tn=tn, n_rows=n_rows),
        grid=(n_out // tn, m // tm),
        in_specs=[pl.BlockSpec((tm, k), lambda j, i: (i, 0)), pl.BlockSpec(memory_space=pl.ANY)],
        out_specs=pl.BlockSpec((tm, tn), lambda j, i: (i, j)),
        out_shape=jax.ShapeDtypeStruct((m, n_out), F32),
        scratch_shapes=[pltpu.VMEM((tn, k), F32), pltpu.VMEM((k, tn), BF16), pltpu.SemaphoreType.DMA(())],
        compiler_params=_cparams(("arbitrary", "arbitrary")),
        name="matmul_wt",
    )(x, wt)


def _ln_res_kernel(x_ref, y_ref, g_ref, b_ref, o_ref, ob_ref):
    z = DEEPNORM_ALPHA * x_ref[...] + y_ref[...]
    zc = z - jnp.mean(z, axis=-1, keepdims=True)
    var = jnp.mean(zc * zc, axis=-1, keepdims=True)
    out = zc * lax.rsqrt(var + LN_EPS) * g_ref[...] + b_ref[...]
    o_ref[...] = out
    ob_ref[...] = out.astype(BF16)


def ln_residual(x, y, g, b, tm=256):
    m, d = x.shape
    row = pl.BlockSpec((tm, d), lambda i: (i, 0))
    vec = pl.BlockSpec((1, d), lambda i: (0, 0))
    return pl.pallas_call(
        _ln_res_kernel,
        grid=(m // tm,),
        in_specs=[row, row, vec, vec],
        out_specs=[row, row],
        out_shape=[jax.ShapeDtypeStruct((m, d), F32), jax.ShapeDtypeStruct((m, d), BF16)],
        compiler_params=_cparams(("arbitrary",)),
        name="ln_residual",
    )(x, y, g.reshape(1, d), b.reshape(1, d))


def _mlstm_kernel(q_ref, k_ref, v_ref, o_ref, g_ref, c0_ref, n0_ref, m0_ref, nw_ref, gb_ref, _dst_ref,
                  h_ref, c_ref, n_ref, m_ref, *, chunk, t_valid, nb):
    ci = pl.program_id(1)

    @pl.when(ci == 0)
    def _():
        c_ref[...] = c0_ref[...]
        n_ref[...] = n0_ref[...]
        m_ref[...] = m0_ref[...]

    size = chunk
    rows = _iota2((size, size), 0)
    cols = _iota2((size, size), 1)
    eye = rows == cols
    causal = cols <= rows
    gates = g_ref[:, 0:2 * ML_HEADS] + gb_ref[...]
    capped = ML_GATE_CAP * jnp.tanh(gates / ML_GATE_CAP)
    valid = (ci * size + (_iota2((nb * size, 1), 0) & (size - 1))) < t_valid
    ipre_all = jnp.where(valid, capped[:, 0:ML_HEADS], NEG)
    logf_all = jnp.where(valid, -_softplus(-capped[:, ML_HEADS:2 * ML_HEADS]), 0.0)
    units = [(sq, h) for sq in range(nb) for h in range(ML_HEADS)]
    rs = [slice(sq * size, (sq + 1) * size) for sq, _ in units]
    ksl = [slice(h * ML_DK, (h + 1) * ML_DK) for _, h in units]
    vsl = [slice(h * ML_DV, (h + 1) * ML_DV) for _, h in units]
    ids = range(len(units))
    q = [q_ref[rs[u], ksl[u]] for u in ids]
    k = [k_ref[rs[u], ksl[u]] * (ML_DK ** -0.5) for u in ids]
    v = [v_ref[rs[u], vsl[u]] for u in ids]
    qk = [_dot_nt(q[u], k[u]) for u in ids]
    c_prev = [c_ref[sq, h] for sq, h in units]
    n_prev = [n_ref[sq, h:h + 1, :] for sq, h in units]
    qc = [_dot(q[u], c_prev[u]) for u in ids]
    s, w_inter, m_t, kw, decay = [], [], [], [], []
    for u, (sq, h) in enumerate(units):
        ig_col = ipre_all[rs[u], h:h + 1]
        lf_col = logf_all[rs[u], h:h + 1]
        ig_row = _col_to_row(ig_col, eye)
        lf_row = _col_to_row(lf_col, eye)
        b_col = jnp.sum(jnp.where(causal, lf_row, 0.0), axis=1, keepdims=True)
        b_row = jnp.sum(jnp.where(rows <= cols, lf_col, 0.0), axis=0, keepdims=True)
        d = jnp.where(causal, b_col - b_row + ig_row, NEG)
        inter = b_col + m_ref[sq, :, h:h + 1]
        m_h = jnp.maximum(inter, jnp.max(d, axis=1, keepdims=True))
        w_h = jnp.exp(inter - m_h)
        s.append(qk[u] * jnp.exp(d - m_h))
        b_last = b_col[size - 1:size, :]
        m_last = m_h[size - 1:size, :]
        w_last = jnp.exp(b_last - b_col + ig_col - m_last)
        kw.append(k[u] * w_last)
        w_inter.append(w_h)
        m_t.append(m_h)
        decay.append(w_h[size - 1:size, :])
        m_ref[sq, :, h:h + 1] = m_last
    sv = [_dot(s[u], v[u]) for u in ids]
    kv = [_dot_tn(kw[u], v[u]) for u in ids]
    for u, (sq, h) in enumerate(units):
        num = w_inter[u] * qc[u] + sv[u]
        den = (w_inter[u] * jnp.sum(q[u] * n_prev[u], axis=1, keepdims=True)
               + jnp.sum(s[u], axis=1, keepdims=True))
        hid = num / jnp.maximum(jnp.abs(den), jnp.exp(-m_t[u]))
        c_ref[sq, h] = decay[u] * c_prev[u] + kv[u]
        n_ref[sq, h:h + 1, :] = decay[u] * n_prev[u] + jnp.sum(kw[u], axis=0, keepdims=True)
        hid = hid * lax.rsqrt(jnp.mean(hid * hid, axis=-1, keepdims=True) + ML_NORM_EPS)
        hid = hid * nw_ref[:, vsl[u]]
        hid = hid * _sigmoid(o_ref[rs[u], vsl[u]])
        h_ref[rs[u], vsl[u]] = hid.astype(h_ref.dtype)


def mlstm_group(proj, dst, row0, bsz, t_pad, t_valid, chunk, c0, n0, m0, b_i, b_f, norm_w, nb=1):
    nc = t_pad // chunk
    rows_blk = nb * chunk
    blk0 = row0 // rows_blk
    assert row0 % rows_blk == 0 and t_pad % chunk == 0 and bsz % nb == 0 and (nb == 1 or nc == 1)
    assert chunk & (chunk - 1) == 0
    rmap = lambda c0_: (lambda b, c: (blk0 + b * nc + c, c0_))
    qk_w, v_w = ML_QK, ML_V
    in_specs = [
        pl.BlockSpec((rows_blk, qk_w), rmap(0)),
        pl.BlockSpec((rows_blk, qk_w), rmap(1)),
        pl.BlockSpec((rows_blk, v_w), rmap(1)),
        pl.BlockSpec((rows_blk, v_w), rmap(2)),
        pl.BlockSpec((rows_blk, LANES), rmap((2 * qk_w + 2 * v_w) // LANES)),
        pl.BlockSpec((nb, ML_HEADS, ML_DK, ML_DV), lambda b, c: (b, 0, 0, 0)),
        pl.BlockSpec((nb, ML_HEADS, ML_DK), lambda b, c: (b, 0, 0)),
        pl.BlockSpec((nb, 1, ML_HEADS), lambda b, c: (b, 0, 0)),
        pl.BlockSpec((1, v_w), lambda b, c: (0, 0)),
        pl.BlockSpec((1, 2 * ML_HEADS), lambda b, c: (0, 0)),
        pl.BlockSpec(memory_space=pl.ANY),
    ]
    out_specs = [
        pl.BlockSpec((rows_blk, v_w), lambda b, c: (blk0 + b * nc + c, 0)),
        pl.BlockSpec((nb, ML_HEADS, ML_DK, ML_DV), lambda b, c: (b, 0, 0, 0)),
        pl.BlockSpec((nb, ML_HEADS, ML_DK), lambda b, c: (b, 0, 0)),
        pl.BlockSpec((nb, 1, ML_HEADS), lambda b, c: (b, 0, 0)),
    ]
    out_shape = [
        jax.ShapeDtypeStruct(dst.shape, dst.dtype),
        jax.ShapeDtypeStruct((bsz, ML_HEADS, ML_DK, ML_DV), F32),
        jax.ShapeDtypeStruct((bsz, ML_HEADS, ML_DK), F32),
        jax.ShapeDtypeStruct((bsz, 1, ML_HEADS), F32),
    ]
    gate_bias = jnp.concatenate([b_i, b_f]).reshape(1, 2 * ML_HEADS)
    h, c, n, m = pl.pallas_call(
        functools.partial(_mlstm_kernel, chunk=chunk, t_valid=t_valid, nb=nb),
        grid=(bsz // nb, nc),
        in_specs=in_specs,
        out_specs=out_specs,
        out_shape=out_shape,
        input_output_aliases={10: 0},
        compiler_params=_cparams(("arbitrary", "arbitrary")),
        name="mlstm",
    )(proj, proj, proj, proj, proj, c0, n0, m0.reshape(bsz, 1, ML_HEADS), norm_w.reshape(1, v_w), gate_bias, dst)
    return h, c, n, m.reshape(bsz, ML_HEADS)


def _head_block_ones(width, head):
    idx = np.arange(width) // head
    return jnp.asarray((idx[:, None] == idx[None, :]).astype(np.float32), dtype=BF16)


def _rwkv_prep_kernel(cur_ref, tail_ref, first_ref, mu_ref, w0_ref, a0_ref, kk_ref, ka_ref, rk_ref,
                      w2_ref, a2_ref, g2_ref, blk_ref,
                      r_ref, wl_ref, k_ref, v_ref, an_ref, bb_ref, bonus_ref, g_ref,
                      *, tt, t_pad, multi):
    cur = cur_ref[...]
    rolled = pltpu.roll(cur, 1, 0)
    row = _iota2((tt, 1), 0)
    if multi:
        prev = jnp.where((row & (t_pad - 1)) == 0, first_ref[...], rolled)
    else:
        starts_seq = (pl.program_id(0) % (t_pad // tt)) == 0
        head = jnp.where(starts_seq, first_ref[0], tail_ref[SUBLANES - 1:SUBLANES, :])
        prev = jnp.where(row == 0, head, rolled)
    mixed = cur + (prev - cur) * mu_ref[...]
    w = RW_WIDTH
    r = mixed[:, 0:w]
    k = mixed[:, w:2 * w]
    v = mixed[:, 2 * w:3 * w]
    c0 = 3 * w
    wl = mixed[:, c0:c0 + RW_DECAY_LORA]
    al = mixed[:, c0 + RW_DECAY_LORA:c0 + RW_DECAY_LORA + RW_A_LORA]
    gl = mixed[:, c0 + RW_DECAY_LORA + RW_A_LORA:]
    w_log = -jnp.exp(-_softplus(-(w0_ref[...] + _dot(jnp.tanh(wl), w2_ref[...]))) - 0.5)
    a = _sigmoid(a0_ref[...] + _dot(al, a2_ref[...]))
    g = _dot(_sigmoid(gl), g2_ref[...])
    kk = k * kk_ref[...]
    blk = blk_ref[...]
    kkn = kk / jnp.maximum(jnp.sqrt(_dot_exact_rhs(kk * kk, blk)), 1e-12)
    k2 = k * (1.0 + (a - 1.0) * ka_ref[...])
    r_ref[...] = r
    wl_ref[...] = w_log
    k_ref[...] = k2
    v_ref[...] = v
    an_ref[...] = -kkn
    bb_ref[...] = kkn * a
    bonus_ref[...] = _dot_exact_rhs(r * k2 * rk_ref[...], blk) * v
    g_ref[...] = g


def rwkv_prep(proj, row0, bsz, t_pad, shift0, prm, tt):
    n = bsz * t_pad
    multi = tt > t_pad
    assert row0 % tt == 0 and n % tt == 0 and (tt % t_pad == 0 if multi else t_pad % tt == 0)
    blk0 = row0 // tt
    c = RW_COLS
    if multi:
        first = jnp.zeros((bsz, t_pad, c), F32).at[:, 0].set(shift0).reshape(n, c)
        first_spec = pl.BlockSpec((tt, c), lambda i: (i, 0))
    else:
        first = shift0.reshape(bsz, 1, c)
        first_spec = pl.BlockSpec((1, 1, c), lambda i: (i // (t_pad // tt), 0, 0))
    tail_blk = tt // SUBLANES
    vec = lambda width: pl.BlockSpec((1, width), lambda i: (0, 0))
    full = lambda a, b: pl.BlockSpec((a, b), lambda i: (0, 0))
    w = RW_WIDTH
    out_spec = pl.BlockSpec((tt, w), lambda i: (i, 0))
    return pl.pallas_call(
        functools.partial(_rwkv_prep_kernel, tt=tt, t_pad=t_pad, multi=multi),
        grid=(n // tt,),
        in_specs=[pl.BlockSpec((tt, c), lambda i: (blk0 + i, 0)),
                  pl.BlockSpec((SUBLANES, c), lambda i: (jnp.maximum((blk0 + i) * tail_blk - 1, 0), 0)),
                  first_spec, vec(c), vec(w), vec(w), vec(w), vec(w), vec(w),
                  full(RW_DECAY_LORA, w), full(RW_A_LORA, w), full(RW_GATE_LORA, w), full(w, w)],
        out_specs=[out_spec] * 8,
        out_shape=[jax.ShapeDtypeStruct((n, w), F32)] * 8,
        compiler_params=_cparams(("arbitrary",)),
        name="rwkv_prep",
    )(proj, proj, first, prm['mu'].reshape(1, c), prm['w0'].reshape(1, w), prm['a0'].reshape(1, w),
      prm['k_k'].reshape(1, w), prm['k_a'].reshape(1, w), prm['r_k'].reshape(1, w),
      prm['w2'], prm['a2'], prm['g2'], _head_block_ones(w, RW_DH))


def _neumann_inverse_many(n_mats, size):
    eye = (_iota2((size, size), 0) == _iota2((size, size), 1)).astype(F32)
    ts = [eye + n_mat for n_mat in n_mats]
    xs = list(n_mats)
    steps = max(int(np.ceil(np.log2(size))) - 1, 0)
    for _ in range(steps):
        xs = [_dot(x, x) for x in xs]
        ts = [t + _dot(t, x) for t, x in zip(ts, xs)]
    return ts


def _rwkv_chunk_kernel(r_ref, w_ref, k_ref, v_ref, a_ref, b_ref, bonus_ref, g_ref, s0_ref,
                       lnw_ref, lnb_ref, _dst_ref, o_ref, s_ref, acc_ref, *, chunk, t_valid, hb, nb):
    ci = pl.program_id(2)

    @pl.when(ci == 0)
    def _():
        s_ref[...] = s0_ref[...]

    size = chunk
    rows = _iota2((size, size), 0)
    cols = _iota2((size, size), 1)
    strict = cols < rows
    rows2 = _iota2((size, 2 * size), 0)
    cols2 = _iota2((size, 2 * size), 1)
    mask_ak = (cols2 >= size) & (cols2 - size < rows2)
    mask_o = jnp.where(cols2 >= size, cols2 - size, cols2) <= rows2
    slab = nb * size
    rows_s = _iota2((slab, slab), 0)
    cols_s = _iota2((slab, slab), 1)
    shift = size.bit_length() - 1
    same_seq_tril = ((rows_s >> shift) == (cols_s >> shift)) & (cols_s <= rows_s)
    valid = (ci * size + (_iota2((slab, 1), 0) & (size - 1))) < t_valid
    w = jnp.where(valid, w_ref[...], 0.0)
    a = jnp.where(valid, a_ref[...], 0.0)
    b = jnp.where(valid, b_ref[...], 0.0)
    k = jnp.where(valid, k_ref[...], 0.0)
    v = v_ref[...]
    lam = _dot_exact_lhs(same_seq_tril.astype(F32), w)
    e_pos = jnp.exp(lam)
    e_neg = jnp.exp(-lam)
    at = a * jnp.exp(lam - w)
    bt = b * e_neg
    kt = k * e_neg
    rt = r_ref[...] * e_pos
    units = [(sq, h) for sq in range(nb) for h in range(hb)]
    ids = range(len(units))
    rs = [slice(sq * size, (sq + 1) * size) for sq, _ in units]
    sls = [slice(h * RW_DH, (h + 1) * RW_DH) for _, h in units]
    at_h = [at[rs[u], sls[u]] for u in ids]
    rt_h = [rt[rs[u], sls[u]] for u in ids]
    v_h = [v[rs[u], sls[u]] for u in ids]
    bk = [jnp.concatenate([bt[rs[u], sls[u]], kt[rs[u], sls[u]]], axis=0) for u in ids]
    s0 = [s_ref[sq, h] for sq, h in units]
    pa = [_dot_nt(at_h[u], bk[u]) for u in ids]
    pr = [_dot_nt(rt_h[u], bk[u]) for u in ids]
    as0 = [_dot_nt(at_h[u], s0[u]) for u in ids]
    rs0 = [_dot_nt(rt_h[u], s0[u]) for u in ids]
    zv = [jnp.concatenate([jnp.zeros_like(v_h[u]), v_h[u]], axis=0) for u in ids]
    rhs = [as0[u] + _dot(jnp.where(mask_ak, pa[u], 0.0), zv[u]) for u in ids]
    t_inv = _neumann_inverse_many([jnp.where(strict, pa[u][:, :size], 0.0) for u in ids], size)
    uu = [_dot(t_inv[u], rhs[u]) for u in ids]
    uv = [jnp.concatenate([uu[u], v_h[u]], axis=0) for u in ids]
    o = [rs0[u] + _dot(jnp.where(mask_o, pr[u], 0.0), uv[u]) for u in ids]
    ds = [_dot_tn(uv[u], bk[u]) for u in ids]
    for u, (sq, h) in enumerate(units):
        last = (sq + 1) * size - 1
        s_ref[sq, h] = (s0[u] + ds[u]) * e_pos[last:last + 1, sls[u]]
        oc = o[u] - jnp.mean(o[u], axis=-1, keepdims=True)
        acc_ref[rs[u], sls[u]] = oc * lax.rsqrt(jnp.mean(oc * oc, axis=-1, keepdims=True) + RW_GN_EPS)
    out = (acc_ref[...] * lnw_ref[...] + lnb_ref[...] + bonus_ref[...]) * g_ref[...]
    o_ref[...] = out.astype(o_ref.dtype)


def rwkv_chunk(prep, dst, dst_col0, row0, bsz, t_pad, t_valid, chunk, s0, ln_w, ln_b, hb, nb=1):
    nc = t_pad // chunk
    hg = RW_HEADS // hb
    wb = hb * RW_DH
    rows_blk = nb * chunk
    blk0 = row0 // rows_blk
    assert dst_col0 % wb == 0 and row0 % rows_blk == 0 and bsz % nb == 0 and (nb == 1 or nc == 1)
    assert chunk & (chunk - 1) == 0
    tok = pl.BlockSpec((rows_blk, wb), lambda b, g, c: (b * nc + c, g))
    st = pl.BlockSpec((nb, hb, RW_DH, RW_DH), lambda b, g, c: (b, g, 0, 0))
    vec = pl.BlockSpec((1, wb), lambda b, g, c: (0, g))
    return pl.pallas_call(
        functools.partial(_rwkv_chunk_kernel, chunk=chunk, t_valid=t_valid, hb=hb, nb=nb),
        grid=(bsz // nb, hg, nc),
        in_specs=[tok] * 8 + [st, vec, vec, pl.BlockSpec(memory_space=pl.ANY)],
        out_specs=[pl.BlockSpec((rows_blk, wb), lambda b, g, c: (blk0 + b * nc + c, dst_col0 // wb + g)), st],
        out_shape=[jax.ShapeDtypeStruct(dst.shape, dst.dtype),
                   jax.ShapeDtypeStruct((bsz, RW_HEADS, RW_DH, RW_DH), F32)],
        scratch_shapes=[pltpu.VMEM((rows_blk, wb), F32)],
        input_output_aliases={11: 0},
        compiler_params=_cparams(("arbitrary", "arbitrary", "arbitrary")),
        name="rwkv_chunk",
    )(*prep, s0, ln_w.reshape(1, RW_WIDTH), ln_b.reshape(1, RW_WIDTH), dst)


def _conv_silu(cur_ref, hist_ref, cw_ref, size):
    cur = cur_ref[...]
    hist = hist_ref[...]
    cw = cw_ref[...]
    row8 = _iota2((SUBLANES, 1), 0)
    acc = cur * cw[GD_CONV - 1:GD_CONV, :]
    for j in range(1, GD_CONV):
        back = GD_CONV - 1 - j
        rolled = pltpu.roll(cur, j, 0)
        head = jnp.where(row8 < j, pltpu.roll(hist, j, 0), rolled[:SUBLANES])
        prev = head if size == SUBLANES else jnp.concatenate([head, rolled[SUBLANES:]], axis=0)
        acc = acc + prev * cw[back:back + 1, :]
    hist_ref[...] = cur[size - SUBLANES:size]
    return acc * _sigmoid(acc)


def _l2_normalize(x, scale):
    return x * lax.rsqrt(jnp.sum(x * x, axis=-1, keepdims=True) + GD_EPS) * scale


def _gdn_chunk_kernel(q_ref, k_ref, v_ref, z_ref, gl_ref, bq_ref, bk_ref, bv_ref, cq_ref, ck_ref, cv_ref,
                      al_ref, dt_ref, nw_ref, s0_ref, _dst_ref,
                      o_ref, s_ref, nq_ref, nk_ref, nv_ref, hq_ref, hk_ref, hv_ref,
                      *, chunk, t_valid, hpg, tail_chunk, tail_row0):
    ci = pl.program_id(2)
    hg = pl.program_id(1)

    @pl.when(ci == 0)
    def _():
        s_ref[...] = s0_ref[...]
        hq_ref[...] = bq_ref[0]
        hk_ref[...] = bk_ref[0]
        hv_ref[...] = bv_ref[0]

    @pl.when(ci == tail_chunk)
    def _():
        tail = slice(tail_row0, tail_row0 + SUBLANES)
        nq_ref[0] = q_ref[tail, :]
        nk_ref[0] = k_ref[tail, :]
        nv_ref[0] = v_ref[tail, :]

    size = chunk
    q_act = _conv_silu(q_ref, hq_ref, cq_ref, size)
    k_act = _conv_silu(k_ref, hk_ref, ck_ref, size)
    v_act = _conv_silu(v_ref, hv_ref, cv_ref, size)
    rows = _iota2((size, size), 0)
    cols = _iota2((size, size), 1)
    eye = rows == cols
    incl = cols <= rows
    strict = cols < rows
    valid = (ci * size + _iota2((size, 1), 0)) < t_valid
    logits = gl_ref[...]
    beta_all = _sigmoid(logits)
    g_all = -jnp.exp(al_ref[...]) * _softplus(logits + dt_ref[...])
    lane = _iota2((size, LANES), 1)
    rep = GD_V_HEADS // GD_K_HEADS
    kheads = range(hpg // rep)
    heads = range(hpg)
    q = [_l2_normalize(q_act[:, kh * GD_DK:(kh + 1) * GD_DK], GD_DK ** -0.5) for kh in kheads]
    k = [_l2_normalize(k_act[:, kh * GD_DK:(kh + 1) * GD_DK], 1.0) for kh in kheads]
    qk = [_dot_nt(q[kh], k[kh]) for kh in kheads]
    kk = [_dot_nt(k[kh], k[kh]) for kh in kheads]
    vsl = [slice(hl * GD_DV, (hl + 1) * GD_DV) for hl in heads]
    s = [s_ref[0, hl] for hl in heads]
    beta, gc, decay = [], [], []
    for hl in heads:
        head = hg * hpg + hl
        beta_h = jnp.sum(jnp.where(lane == head, beta_all, 0.0), axis=1, keepdims=True)
        g = jnp.sum(jnp.where(lane == head + GD_V_HEADS, g_all, 0.0), axis=1, keepdims=True)
        beta_h = jnp.where(valid, beta_h, 0.0)
        g = jnp.where(valid, g, 0.0)
        g_row = _col_to_row(g, eye)
        gc_h = jnp.sum(jnp.where(incl, g_row, 0.0), axis=1, keepdims=True)
        gc_row = jnp.sum(jnp.where(rows <= cols, g, 0.0), axis=0, keepdims=True)
        beta.append(beta_h)
        gc.append(gc_h)
        decay.append(jnp.where(incl, jnp.exp(jnp.where(incl, gc_h - gc_row, 0.0)), 0.0))
    qs = [_dot(q[hl // rep] * jnp.exp(gc[hl]), s[hl]) for hl in heads]
    t_inv = _neumann_inverse_many(
        [-jnp.where(strict, kk[hl // rep] * beta[hl] * decay[hl], 0.0) for hl in heads], size)
    uw = [_dot(t_inv[hl], jnp.concatenate([v_act[:, vsl[hl]] * beta[hl],
                                           k[hl // rep] * (beta[hl] * jnp.exp(gc[hl]))], axis=1))
          for hl in heads]
    ws = [_dot(uw[hl][:, GD_DV:], s[hl]) for hl in heads]
    v_new = [uw[hl][:, :GD_DV] - ws[hl] for hl in heads]
    av = [_dot(qk[hl // rep] * decay[hl], v_new[hl]) for hl in heads]
    g_last = [gc[hl][size - 1:size, :] for hl in heads]
    kv = [_dot_tn(k[hl // rep] * jnp.exp(g_last[hl] - gc[hl]), v_new[hl]) for hl in heads]
    for hl in heads:
        s_ref[0, hl] = s[hl] * jnp.exp(g_last[hl]) + kv[hl]
        o = qs[hl] + av[hl]
        o = o * lax.rsqrt(jnp.mean(o * o, axis=-1, keepdims=True) + GD_EPS) * nw_ref[...]
        z = z_ref[:, vsl[hl]]
        o_ref[:, vsl[hl]] = (o * (z * _sigmoid(z))).astype(o_ref.dtype)


def gdn_chunk(proj, gates, conv_buf, conv_w, dst, row0, bsz, t_pad, t_valid, chunk, s0, a_log, dt_bias,
              norm_w, hpg):
    nc = t_pad // chunk
    hgs = GD_V_HEADS // hpg
    rep = GD_V_HEADS // GD_K_HEADS
    kw = (hpg // rep) * GD_DK
    vw = hpg * GD_DV
    blk0 = row0 // chunk
    assert row0 % chunk == 0 and chunk % SUBLANES == 0
    pad_row = lambda x: jnp.pad(x, (GD_V_HEADS, LANES - 2 * GD_V_HEADS)).reshape(1, LANES)
    hist = jnp.pad(conv_buf, ((0, 0), (SUBLANES - (GD_CONV - 1), 0), (0, 0)))
    st = pl.BlockSpec((1, hpg, GD_DK, GD_DV), lambda b, g, c: (b, g, 0, 0))
    vec = pl.BlockSpec((1, LANES), lambda b, g, c: (0, 0))
    q_col, k_col, v_col = (lambda g: g), (lambda g: GD_QK // kw + g), (lambda g: 2 * GD_QK // vw + g)
    tok = lambda width, col: pl.BlockSpec((chunk, width), lambda b, g, c: (blk0 + b * nc + c, col(g)))
    buf = lambda width, col: pl.BlockSpec((1, SUBLANES, width), lambda b, g, c: (b, 0, col(g)))
    taps = lambda width, col: pl.BlockSpec((GD_CONV, width), lambda b, g, c: (0, col(g)))
    last = (t_valid - 1) % chunk
    tail_row0 = (last // SUBLANES) * SUBLANES
    first = last - (GD_CONV - 2) - tail_row0
    assert first >= 0
    new_buf = lambda width: jax.ShapeDtypeStruct((bsz, SUBLANES, width), F32)
    out, state, nq, nk, nv = pl.pallas_call(
        functools.partial(_gdn_chunk_kernel, chunk=chunk, t_valid=t_valid, hpg=hpg,
                          tail_chunk=(t_valid - 1) // chunk, tail_row0=tail_row0),
        grid=(bsz, hgs, nc),
        in_specs=[tok(kw, q_col), tok(kw, k_col), tok(vw, v_col),
                  tok(vw, lambda g: GD_CONV_DIM // vw + g),
                  pl.BlockSpec((chunk, LANES), lambda b, g, c: (blk0 + b * nc + c, 0)),
                  buf(kw, q_col), buf(kw, k_col), buf(vw, v_col),
                  taps(kw, q_col), taps(kw, k_col), taps(vw, v_col),
                  vec, vec, vec, st, pl.BlockSpec(memory_space=pl.ANY)],
        out_specs=[pl.BlockSpec((chunk, vw), lambda b, g, c: (blk0 + b * nc + c, g)), st,
                   pl.BlockSpec((1, SUBLANES, kw), lambda b, g, c: (b, 0, g)),
                   pl.BlockSpec((1, SUBLANES, kw), lambda b, g, c: (b, 0, g)),
                   pl.BlockSpec((1, SUBLANES, vw), lambda b, g, c: (b, 0, g))],
        out_shape=[jax.ShapeDtypeStruct(dst.shape, dst.dtype),
                   jax.ShapeDtypeStruct((bsz, GD_V_HEADS, GD_DK, GD_DV), F32),
                   new_buf(GD_QK), new_buf(GD_QK), new_buf(GD_V_WIDTH)],
        scratch_shapes=[pltpu.VMEM((SUBLANES, kw), F32), pltpu.VMEM((SUBLANES, kw), F32),
                        pltpu.VMEM((SUBLANES, vw), F32)],
        input_output_aliases={15: 0},
        compiler_params=_cparams(("arbitrary", "arbitrary", "arbitrary")),
        name="gdn_chunk",
    )(proj, proj, proj, proj, gates, hist, hist, hist, conv_w, conv_w, conv_w,
      pad_row(a_log), pad_row(dt_bias), norm_w.reshape(1, GD_DV), s0, dst)
    conv_state = jnp.concatenate([nq, nk, nv], axis=-1)[:, first:first + GD_CONV - 1]
    return out, state, conv_state


MOE_TM = 256


def _router_kernel(x_ref, w_ref, wt_ref, id_ref):
    logits = jnp.dot(x_ref[...], w_ref[...].astype(BF16), preferred_element_type=F32)
    lane = _iota2(logits.shape, 1).astype(F32)
    first_of = lambda hit: jnp.min(jnp.where(hit, lane, float(LANES)), axis=1, keepdims=True)
    gl = jnp.where(lane < MOE_GROUPS, logits, NEG)
    gmax = jnp.max(gl, axis=1, keepdims=True)
    g_val = 1.0 / jnp.sum(jnp.exp(gl - gmax), axis=1, keepdims=True)
    lo = MOE_GROUPS + first_of(gl == gmax) * MOE_PER_GROUP
    vals = jnp.where((lane >= lo) & (lane < lo + MOE_PER_GROUP), logits, NEG)
    top1 = jnp.max(vals, axis=1, keepdims=True)
    i1 = first_of(vals == top1)
    vals2 = jnp.where(lane == i1, NEG, vals)
    top2 = jnp.max(vals2, axis=1, keepdims=True)
    i2 = first_of(vals2 == top2)
    e2 = jnp.exp(top2 - top1)
    w1 = (1.0 / (1.0 + e2)) * g_val
    w2 = (e2 / (1.0 + e2)) * g_val
    wt_ref[...] = jnp.where(lane == 0, w1, jnp.where(lane == 1, w2, 0.0))
    ids = jnp.where(lane == 0, i1 - MOE_GROUPS, jnp.where(lane == 1, i2 - MOE_GROUPS, 0.0))
    id_ref[...] = ids.astype(jnp.int32)


def moe_router(xb, w_router, tm=512):
    n, d = xb.shape
    out = pl.BlockSpec((tm, LANES), lambda i: (i, 0))
    return pl.pallas_call(
        _router_kernel,
        grid=(n // tm,),
        in_specs=[pl.BlockSpec((tm, d), lambda i: (i, 0)), pl.BlockSpec((d, LANES), lambda i: (0, 0))],
        out_specs=[out, out],
        out_shape=[jax.ShapeDtypeStruct((n, LANES), F32), jax.ShapeDtypeStruct((n, LANES), jnp.int32)],
        compiler_params=_cparams(("arbitrary",)),
        name="moe_router",
    )(xb, w_router)


def _moe_ffn_kernel(te_ref, cnt_ref, first_ref, next_ref, wslot_ref,
                    idx_ref, idx_next_ref, x_hbm, wg_hbm, wu_hbm, wd_hbm, o_ref,
                    xbuf, wgf, wuf, wdf, wgb, wub, wdb, gsem, wsem, *, layer):
    i = pl.program_id(0)
    n_tiles = pl.num_programs(0)
    slot = i % 2

    def gather(rows_ref, into, count):
        def start_row(r, carry):
            pltpu.make_async_copy(x_hbm.at[pl.ds(rows_ref[0, 0, r], 1)],
                                  xbuf.at[into, pl.ds(r, 1)], gsem.at[into]).start()
            return carry
        lax.fori_loop(0, count, start_row, 0)

    def gather_wait(into, count):
        def wait_row(r, carry):
            pltpu.make_async_copy(x_hbm.at[pl.ds(0, 1)], xbuf.at[into, pl.ds(0, 1)], gsem.at[into]).wait()
            return carry
        lax.fori_loop(0, count, wait_row, 0)

    def weight_copies(expert, ws):
        return [pltpu.make_async_copy(src.at[layer, expert], dst.at[ws], wsem.at[ws])
                for src, dst in ((wg_hbm, wgf), (wu_hbm, wuf), (wd_hbm, wdf))]

    @pl.when(i == 0)
    def _():
        xbuf[...] = jnp.zeros_like(xbuf)
        gather(idx_ref, 0, cnt_ref[0])
        for copy in weight_copies(te_ref[0], 0):
            copy.start()

    nxt = jnp.minimum(i + 1, n_tiles - 1)

    @pl.when(i + 1 < n_tiles)
    def _():
        gather(idx_next_ref, 1 - slot, cnt_ref[nxt])

    @pl.when(cnt_ref[i] > 0)
    def _():
        gather_wait(slot, cnt_ref[i])

        @pl.when(first_ref[i] == 1)
        def _():
            ws = wslot_ref[i]
            for copy in weight_copies(te_ref[i], ws):
                copy.wait()
            wgb[...] = wgf[ws].astype(BF16)
            wub[...] = wuf[ws].astype(BF16)
            wdb[...] = wdf[ws].astype(BF16)

            @pl.when(next_ref[i] >= 0)
            def _():
                for copy in weight_copies(next_ref[i], 1 - ws):
                    copy.start()

        x = xbuf[slot].astype(BF16)
        gate = jnp.dot(x, wgb[...], preferred_element_type=F32)
        up = jnp.dot(x, wub[...], preferred_element_type=F32)
        hid = gate * _sigmoid(gate) * up
        o_ref[...] = jnp.dot(hid.astype(BF16), wdb[...], preferred_element_type=F32)

    @pl.when(cnt_ref[i] == 0)
    def _():
        o_ref[...] = jnp.zeros_like(o_ref)


def moe_ffn(x, row_token, tables, layer, w_gate, w_up, w_down):
    n_tiles = row_token.shape[0]
    d = x.shape[1]
    ff = w_gate.shape[3]
    any_space = pl.BlockSpec(memory_space=pl.ANY)
    grid_spec = pltpu.PrefetchScalarGridSpec(
        num_scalar_prefetch=5,
        grid=(n_tiles,),
        in_specs=[pl.BlockSpec((1, 1, MOE_TM), lambda i, *_: (i, 0, 0), memory_space=pltpu.SMEM),
                  pl.BlockSpec((1, 1, MOE_TM), lambda i, *_: (jnp.minimum(i + 1, n_tiles - 1), 0, 0),
                               memory_space=pltpu.SMEM),
                  any_space, any_space, any_space, any_space],
        out_specs=pl.BlockSpec((MOE_TM, d), lambda i, *_: (i, 0)),
        scratch_shapes=[pltpu.VMEM((2, MOE_TM, d), F32),
                        pltpu.VMEM((2, d, ff), F32), pltpu.VMEM((2, d, ff), F32), pltpu.VMEM((2, ff, d), F32),
                        pltpu.VMEM((d, ff), BF16), pltpu.VMEM((d, ff), BF16), pltpu.VMEM((ff, d), BF16),
                        pltpu.SemaphoreType.DMA((2,)), pltpu.SemaphoreType.DMA((2,))],
    )
    return pl.pallas_call(
        functools.partial(_moe_ffn_kernel, layer=layer),
        grid_spec=grid_spec,
        out_shape=jax.ShapeDtypeStruct((n_tiles * MOE_TM, d), F32),
        compiler_params=_cparams(("arbitrary",)),
        name="moe_ffn",
    )(*tables, row_token, row_token, x, w_gate, w_up, w_down)


def _ln_moe_ple_kernel(x_ref, ya_ref, yb_ref, wt_ref, g_ref, b_ref, p_ref, wg_ref, wp_ref, o_ref, ob_ref):
    wt = wt_ref[...]
    z = DEEPNORM_ALPHA * x_ref[...] + (wt[:, 0:1] * ya_ref[...] + wt[:, 1:2] * yb_ref[...])
    zc = z - jnp.mean(z, axis=-1, keepdims=True)
    var = jnp.mean(zc * zc, axis=-1, keepdims=True)
    hid = zc * lax.rsqrt(var + LN_EPS) * g_ref[...] + b_ref[...]
    gate = _sigmoid(jnp.dot(hid.astype(BF16), wg_ref[...], preferred_element_type=F32))
    emb = jnp.dot(p_ref[...].astype(BF16), wp_ref[...], preferred_element_type=F32)
    out = hid + gate * emb
    o_ref[...] = out
    ob_ref[...] = out.astype(BF16)


def ln_moe_ple(x, ya, yb, wt, g, b, p, wg, wp, tm=512):
    m, d = x.shape
    pd = p.shape[1]
    tm = _row_tile(m, tm)
    row = pl.BlockSpec((tm, d), lambda i: (i, 0))
    vec = pl.BlockSpec((1, d), lambda i: (0, 0))
    once = pl.Buffered(1)
    return pl.pallas_call(
        _ln_moe_ple_kernel,
        grid=(m // tm,),
        in_specs=[row, row, row, pl.BlockSpec((tm, LANES), lambda i: (i, 0)), vec, vec,
                  pl.BlockSpec((tm, pd), lambda i: (i, 0)),
                  pl.BlockSpec((d, d), lambda i: (0, 0), pipeline_mode=once),
                  pl.BlockSpec((pd, d), lambda i: (0, 0), pipeline_mode=once)],
        out_specs=[row, row],
        out_shape=[jax.ShapeDtypeStruct((m, d), F32), jax.ShapeDtypeStruct((m, d), BF16)],
        compiler_params=_cparams(("arbitrary",)),
        name="ln_moe_ple",
    )(x, ya, yb, wt, g.reshape(1, d), b.reshape(1, d), p, wg, wp)


def _table_lookup(table, idx):
    hit = idx[:, None] == jnp.arange(table.shape[0], dtype=idx.dtype)[None, :]
    return jnp.sum(jnp.where(hit, table[None, :], 0), axis=1)


def hier_moe(x, xb, layer, w_group, w_expert, w_gate, w_up, w_down):
    n = xb.shape[0]
    w_router = jnp.pad(jnp.concatenate([w_group, w_expert], axis=1),
                       ((0, 0), (0, LANES - MOE_GROUPS - MOE_EXPERTS)))
    wt, ids = moe_router(xb, w_router)

    n_pairs = n * MOE_TOPK
    n_tiles = -(-n_pairs // MOE_TM) + MOE_EXPERTS
    n_rows = n_tiles * MOE_TM
    i32 = jnp.int32
    eid = ids[:, :MOE_TOPK].reshape(n_pairs)
    order = jnp.argsort(eid, stable=True).astype(i32)
    rank = jnp.argsort(order).astype(i32)
    experts = jnp.arange(MOE_EXPERTS, dtype=i32)
    counts = jnp.sum((eid[:, None] == experts[None, :]).astype(i32), axis=0)
    starts = jnp.cumsum(counts) - counts
    tiles_per = (counts + MOE_TM - 1) // MOE_TM
    tile_starts = jnp.cumsum(tiles_per) - tiles_per
    dest = _table_lookup(tile_starts * MOE_TM - starts, eid) + rank
    tile_idx = jnp.arange(n_tiles, dtype=i32)
    tile_expert = jnp.sum((tile_idx[:, None] >= (tile_starts + tiles_per)[None, :]).astype(i32), axis=1)
    tile_valid = (tile_expert < MOE_EXPERTS).astype(i32)
    used = tiles_per > 0
    last_used = jnp.max(jnp.where(used, experts, 0))
    tile_expert = jnp.where(tile_valid == 1, tile_expert, last_used)
    tile_in_run = tile_idx - _table_lookup(tile_starts, tile_expert)
    off = tile_in_run[:, None] * MOE_TM + jnp.arange(MOE_TM, dtype=i32)[None, :]
    cnt = _table_lookup(counts, tile_expert)
    src = jnp.clip(_table_lookup(starts, tile_expert)[:, None] + off, 0, n_pairs - 1)
    row_ok = (off < cnt[:, None]) & (tile_valid[:, None] == 1)
    picked = jnp.take(order, src.reshape(n_rows), mode='clip').reshape(n_tiles, MOE_TM)
    row_token = jnp.where(row_ok, picked // MOE_TOPK, 0).reshape(n_tiles, 1, MOE_TM)
    tile_rows = jnp.clip(cnt - tile_in_run * MOE_TM, 0, MOE_TM) * tile_valid
    tile_first = tile_valid * (tile_in_run == 0).astype(i32)
    later_used = (experts[None, :] > experts[:, None]) & used[None, :]
    next_used = jnp.min(jnp.where(later_used, experts[None, :], MOE_EXPERTS), axis=1)
    next_used = jnp.where(next_used < MOE_EXPERTS, next_used, -1)
    run_index = jnp.cumsum(used.astype(i32)) - 1
    tables = (tile_expert, tile_rows, tile_first, _table_lookup(next_used, tile_expert),
              _table_lookup(run_index, tile_expert) % 2)

    y_sorted = moe_ffn(x, row_token, tables, layer, w_gate, w_up, w_down)
    dest2 = dest.reshape(n, MOE_TOPK)
    return (jnp.take(y_sorted, dest2[:, 0], axis=0, mode='clip'),
            jnp.take(y_sorted, dest2[:, 1], axis=0, mode='clip'), wt)


SAMPLE_T_PAD = 8


def kernel(x_prompt, x_sample, state_mlstm_c, state_mlstm_n, state_mlstm_m, state_rwkv_s,
           state_rwkv_shift, state_gdn_s, state_gdn_conv, p_prompt, p_sample,
           w_in_ab, ml_b_i, ml_b_f, ml_norm_w, rw_mu, rw_w0, rw_w2, rw_a0, rw_a2, rw_g2,
           rw_k_k, rw_k_a, rw_r_k, rw_ln_w, rw_ln_b, w_out_ab,
           gd_w_in, gd_conv_w, gd_a_log, gd_dt_bias, gd_norm_w, gd_w_out,
           ln_mix_g, ln_mix_b, moe_w_group, moe_w_expert, moe_w_gate, moe_w_up, moe_w_down,
           ln_ffn_g, ln_ffn_b, ple_w_gate, ple_w_proj):
    pb, pt, d = x_prompt.shape
    sb, st, _ = x_sample.shape
    n_p = pb * pt
    n_s = sb * SAMPLE_T_PAD
    n = n_p + n_s
    pad_t = lambda a: jnp.pad(a, ((0, 0), (0, SAMPLE_T_PAD - st), (0, 0)))
    merge = lambda a_p, a_s: jnp.concatenate(
        [a_p.reshape(n_p, a_p.shape[-1]), pad_t(a_s).reshape(n_s, a_s.shape[-1])], axis=0)
    x = merge(x_prompt, x_sample)
    xb = x.astype(BF16)
    segs = (dict(row0=0, bsz=pb, t_pad=pt, t_valid=pt, ml_nb=1, rw_hb=16, rw_nb=1, gd_hpg=16),
            dict(row0=n_p, bsz=sb, t_pad=SAMPLE_T_PAD, t_valid=st, ml_nb=2, rw_hb=16, rw_nb=2, gd_hpg=32))
    zeros = lambda *shape: jnp.zeros(shape, F32)

    def last_rows(a, sg, count):
        assert sg['t_valid'] >= count
        idx = (sg['row0'] + jnp.arange(sg['bsz'])[:, None] * sg['t_pad']
               + (sg['t_valid'] - count + jnp.arange(count))[None, :])
        return jnp.take(a, idx.reshape(-1), axis=0, mode='clip').reshape(sg['bsz'], count, a.shape[-1])

    new_states = [[[] for _ in range(7)] for _ in segs]
    for layer in range(DEPTH):
        li = layer // 2
        if layer % 2 == 0:
            w_in_t = jnp.transpose(w_in_ab[li])
            proj_ml = matmul_wt(xb, w_in_t, 0, 3584, 896)
            proj_rw = matmul_wt(xb, w_in_t, ML_COLS, RW_COLS, RW_COLS // 2)
            rw_prm = dict(mu=rw_mu[li], w0=rw_w0[li], w2=rw_w2[li], a0=rw_a0[li], a2=rw_a2[li], g2=rw_g2[li],
                          k_k=rw_k_k[li], k_a=rw_k_a[li], r_k=rw_r_k[li])
            heads_out = jnp.zeros((n, ML_V + RW_WIDTH), BF16)
            for si, sg in enumerate(segs):
                bsz, t_pad, t_valid, row0 = sg['bsz'], sg['t_pad'], sg['t_valid'], sg['row0']
                if si == 0:
                    c0, n0, m0 = zeros(bsz, ML_HEADS, ML_DK, ML_DV), zeros(bsz, ML_HEADS, ML_DK), zeros(bsz, ML_HEADS)
                    s0, sh0 = zeros(bsz, RW_HEADS, RW_DH, RW_DH), zeros(bsz, RW_COLS)
                else:
                    c0, n0, m0 = state_mlstm_c[li], state_mlstm_n[li], state_mlstm_m[li]
                    s0, sh0 = state_rwkv_s[li], state_rwkv_shift[li]
                heads_out, c, nn, m = mlstm_group(proj_ml, heads_out, row0, bsz, t_pad, t_valid,
                                                  min(ML_CHUNK, t_pad), c0, n0, m0,
                                                  ml_b_i[li], ml_b_f[li], ml_norm_w[li], sg['ml_nb'])
                prep = rwkv_prep(proj_rw, row0, bsz, t_pad, sh0, rw_prm, 256)
                heads_out, rs = rwkv_chunk(prep, heads_out, ML_V, row0, bsz, t_pad, t_valid,
                                           min(RW_CHUNK, t_pad), s0, rw_ln_w[li], rw_ln_b[li],
                                           sg['rw_hb'], sg['rw_nb'])
                new_shift = last_rows(proj_rw, sg, 1)[:, 0]
                for slot, val in zip(range(5), (c, nn, m, rs, new_shift)):
                    new_states[si][slot].append(val)
            mix = matmul(heads_out, w_out_ab[li], d, 1024)
        else:
            w_in_t = jnp.transpose(gd_w_in[li])
            n_qkvz = GD_CONV_DIM + GD_V_WIDTH
            proj = matmul_wt(xb, w_in_t, 0, n_qkvz, 1024)
            gates = matmul_wt(xb, w_in_t, n_qkvz, LANES, LANES)
            gd_out = jnp.zeros((n, GD_V_WIDTH), BF16)
            for si, sg in enumerate(segs):
                bsz, t_pad, t_valid, row0 = sg['bsz'], sg['t_pad'], sg['t_valid'], sg['row0']
                if si == 0:
                    s0, buf0 = zeros(bsz, GD_V_HEADS, GD_DK, GD_DV), zeros(bsz, GD_CONV - 1, GD_CONV_DIM)
                else:
                    s0, buf0 = state_gdn_s[li], state_gdn_conv[li]
                gd_out, gs, gcb = gdn_chunk(proj, gates, buf0, gd_conv_w[li], gd_out, row0, bsz, t_pad, t_valid,
                                            min(GD_CHUNK, t_pad), s0, gd_a_log[li], gd_dt_bias[li],
                                            gd_norm_w[li], sg['gd_hpg'])
                new_states[si][5].append(gs)
                new_states[si][6].append(gcb)
            mix = matmul(gd_out, gd_w_out[li], d, 512)
        x, xb = ln_residual(x, mix, ln_mix_g[layer], ln_mix_b[layer])
        ya, yb, wt = hier_moe(x, xb, layer, moe_w_group[layer], moe_w_expert[layer],
                              moe_w_gate, moe_w_up, moe_w_down)
        x, xb = ln_moe_ple(x, ya, yb, wt, ln_ffn_g[layer], ln_ffn_b[layer],
                           merge(p_prompt[layer], p_sample[layer]),
                           ple_w_gate[layer].astype(BF16), ple_w_proj[layer].astype(BF16))
    y_prompt = x[:n_p].reshape(pb, pt, d)
    y_sample = x[n_p:].reshape(sb, SAMPLE_T_PAD, d)[:, :st]
    stack = lambda vals: jnp.stack(vals)
    return (y_prompt, y_sample) + tuple(stack(v) for v in new_states[0]) + tuple(stack(v) for v in new_states[1])
```

```python
import functools

import jax
import jax.numpy as jnp
import numpy as np
from jax import lax
from jax.experimental import pallas as pl
from jax.experimental.pallas import tpu as pltpu

F32 = jnp.float32
BF16 = jnp.bfloat16

D_MODEL = 2048
DEPTH = 2
DEEPNORM_ALPHA = (2 * DEPTH) ** 0.25
LN_EPS = 1e-5
MIX_HALF = D_MODEL // 2
ML_HEADS = 4
ML_DV = MIX_HALF // ML_HEADS
ML_DK = ML_DV // 2
ML_CHUNK = 64
ML_GATE_CAP = 15.0
ML_NORM_EPS = 1e-6
ML_QK = ML_HEADS * ML_DK
ML_V = ML_HEADS * ML_DV
ML_COLS = 2 * ML_QK + 2 * ML_V + 2 * ML_HEADS
RW_DH = 64
RW_HEADS = MIX_HALF // RW_DH
RW_WIDTH = RW_HEADS * RW_DH
RW_DECAY_LORA = 64
RW_A_LORA = 64
RW_GATE_LORA = 128
RW_GN_EPS = 64e-5
RW_COLS = 3 * RW_WIDTH + RW_DECAY_LORA + RW_A_LORA + RW_GATE_LORA
RW_CHUNK = 64
GD_DK = 128
GD_DV = 128
GD_K_HEADS = D_MODEL // GD_DK
GD_V_HEADS = 2 * GD_K_HEADS
GD_CONV = 4
GD_CHUNK = 64
GD_EPS = 1e-6
GD_QK = GD_K_HEADS * GD_DK
GD_V_WIDTH = GD_V_HEADS * GD_DV
GD_CONV_DIM = 2 * GD_QK + GD_V_WIDTH
MOE_GROUPS = 4
MOE_PER_GROUP = 8
MOE_EXPERTS = MOE_GROUPS * MOE_PER_GROUP
MOE_TOPK = 2
MOE_FF = D_MODEL // 4
PLE_DIM = 256

LANES = 128
SUBLANES = 8
NEG = -1e30
VMEM_LIMIT = 56 * 1024 * 1024


def _cparams(sem):
    return pltpu.CompilerParams(dimension_semantics=sem, vmem_limit_bytes=VMEM_LIMIT)


def _dot(a, b):
    return jnp.dot(a.astype(BF16), b.astype(BF16), preferred_element_type=F32)


def _dot_nt(a, b):
    return lax.dot_general(a.astype(BF16), b.astype(BF16), (((1,), (1,)), ((), ())),
                           preferred_element_type=F32)


def _dot_tn(a, b):
    return lax.dot_general(a.astype(BF16), b.astype(BF16), (((0,), (0,)), ((), ())),
                           preferred_element_type=F32)


def _split3(x):
    hi = x.astype(BF16)
    r1 = x - hi.astype(F32)
    mid = r1.astype(BF16)
    lo = (r1 - mid.astype(F32)).astype(BF16)
    return hi, mid, lo


def _dot_exact_rhs(a, b01):
    hi, mid, lo = _split3(a)
    b = b01.astype(BF16)
    return (jnp.dot(hi, b, preferred_element_type=F32) + jnp.dot(mid, b, preferred_element_type=F32)
            + jnp.dot(lo, b, preferred_element_type=F32))


def _dot_exact_lhs(a01, b):
    hi, mid, lo = _split3(b)
    a = a01.astype(BF16)
    return (jnp.dot(a, hi, preferred_element_type=F32) + jnp.dot(a, mid, preferred_element_type=F32)
            + jnp.dot(a, lo, preferred_element_type=F32))


def _iota2(shape, dim):
    return lax.broadcasted_iota(jnp.int32, shape, dim)


def _col_to_row(col, eye):
    return jnp.sum(jnp.where(eye, col, 0.0), axis=0, keepdims=True)


def _sigmoid(x):
    return 1.0 / (1.0 + jnp.exp(-x))


def _softplus(x):
    return jnp.maximum(x, 0.0) + jnp.log1p(jnp.exp(-jnp.abs(x)))


def _neumann_inverse(n_mat, size):
    eye = (_iota2((size, size), 0) == _iota2((size, size), 1)).astype(F32)
    t = eye + n_mat
    x = n_mat
    steps = max(int(np.ceil(np.log2(size))) - 1, 0)
    for _ in range(steps):
        x = _dot(x, x)
        t = t + _dot(t, x)
    return t


def _mm_kernel(x_ref, w_ref, o_ref, wb_ref, *, n_valid):
    @pl.when(pl.program_id(1) == 0)
    def _():
        wb_ref[...] = w_ref[...].astype(BF16)

    acc = jnp.dot(x_ref[...].astype(BF16), wb_ref[...], preferred_element_type=F32)
    if n_valid is not None:
        acc = jnp.where(_iota2(acc.shape, 1) < n_valid, acc, 0.0)
    o_ref[...] = acc.astype(o_ref.dtype)


def _row_tile(m, want):
    while m % want:
        want //= 2
    return want


def matmul(x, w, n_out, tn, tm=1024, out_dtype=F32, col0=0):
    m, k = x.shape
    tm = _row_tile(m, tm)
    assert w.shape[0] == k and n_out % tn == 0 and m % tm == 0 and tn % LANES == 0 and col0 % tn == 0
    n_valid = None
    if col0 + n_out > w.shape[1]:
        assert n_out == tn
        n_valid = w.shape[1] - col0
    blk0 = col0 // tn
    return pl.pallas_call(
        functools.partial(_mm_kernel, n_valid=n_valid),
        grid=(n_out // tn, m // tm),
        in_specs=[pl.BlockSpec((tm, k), lambda j, i: (i, 0)),
                  pl.BlockSpec((k, tn), lambda j, i: (0, blk0 + j))],
        out_specs=pl.BlockSpec((tm, tn), lambda j, i: (i, j)),
        out_shape=jax.ShapeDtypeStruct((m, n_out), out_dtype),
        scratch_shapes=[pltpu.VMEM((k, tn), BF16)],
        compiler_params=_cparams(("arbitrary", "arbitrary")),
        name="matmul",
    )(x, w)


def _mm_wt_kernel(x_ref, wt_hbm, o_ref, wf_ref, wb_ref, sem, *, row0, tn, n_rows):
    j = pl.program_id(0)

    @pl.when(pl.program_id(1) == 0)
    def _():
        if n_rows < tn:
            wf_ref[...] = jnp.zeros_like(wf_ref)
        start = pl.multiple_of(row0 + j * tn, SUBLANES)
        copy = pltpu.make_async_copy(wt_hbm.at[pl.ds(start, n_rows)], wf_ref.at[pl.ds(0, n_rows)], sem)
        copy.start()
        copy.wait()
        k = wf_ref.shape[1]
        step = 2 * LANES
        for c in range(k // step):
            wb_ref[c * step:(c + 1) * step, :] = wf_ref[:, c * step:(c + 1) * step].T.astype(BF16)

    o_ref[...] = jnp.dot(x_ref[...].astype(BF16), wb_ref[...], preferred_element_type=F32)


def matmul_wt(x, wt, row0, n_out, tn, tm=1024):
    m, k = x.shape
    tm = _row_tile(m, tm)
    assert wt.shape[1] == k and n_out % tn == 0 and tn % LANES == 0 and row0 % SUBLANES == 0
    n_rows = tn
    if row0 + n_out > wt.shape[0]:
        assert n_out == tn
        n_rows = wt.shape[0] - row0
    assert n_rows % SUBLANES == 0
    return pl.pallas_call(
        functools.partial(_mm_wt_kernel, row0=row0, tn=tn, n_rows=n_rows),
        grid=(n_out // tn, m // tm),
        in_specs=[pl.BlockSpec((tm, k), lambda j, i: (i, 0)), pl.BlockSpec(memory_space=pl.ANY)],
        out_specs=pl.BlockSpec((tm, tn), lambda j, i: (i, j)),
        out_shape=jax.ShapeDtypeStruct((m, n_out), F32),
        scratch_shapes=[pltpu.VMEM((tn, k), F32), pltpu.VMEM((k, tn), BF16), pltpu.SemaphoreType.DMA(())],
        compiler_params=_cparams(("arbitrary", "arbitrary")),
        name="matmul_wt",
    )(x, wt)


def _ln_res_kernel(x_ref, y_ref, g_ref, b_ref, o_ref, ob_ref):
    z = DEEPNORM_ALPHA * x_ref[...] + y_ref[...]
    zc = z - jnp.mean(z, axis=-1, keepdims=True)
    var = jnp.mean(zc * zc, axis=-1, keepdims=True)
    out = zc * lax.rsqrt(var + LN_EPS) * g_ref[...] + b_ref[...]
    o_ref[...] = out
    ob_ref[...] = out.astype(BF16)


def ln_residual(x, y, g, b, tm=256):
    m, d = x.shape
    row = pl.BlockSpec((tm, d), lambda i: (i, 0))
    vec = pl.BlockSpec((1, d), lambda i: (0, 0))
    return pl.pallas_call(
        _ln_res_kernel,
        grid=(m // tm,),
        in_specs=[row, row, vec, vec],
        out_specs=[row, row],
        out_shape=[jax.ShapeDtypeStruct((m, d), F32), jax.ShapeDtypeStruct((m, d), BF16)],
        compiler_params=_cparams(("arbitrary",)),
        name="ln_residual",
    )(x, y, g.reshape(1, d), b.reshape(1, d))


def _mlstm_kernel(q_ref, k_ref, v_ref, o_ref, g_ref, c0_ref, n0_ref, m0_ref, nw_ref, gb_ref, _dst_ref,
                  h_ref, c_ref, n_ref, m_ref, *, chunk, t_valid, nb):
    ci = pl.program_id(1)

    @pl.when(ci == 0)
    def _():
        c_ref[...] = c0_ref[...]
        n_ref[...] = n0_ref[...]
        m_ref[...] = m0_ref[...]

    size = chunk
    rows = _iota2((size, size), 0)
    cols = _iota2((size, size), 1)
    eye = rows == cols
    causal = cols <= rows
    gates = g_ref[:, 0:2 * ML_HEADS] + gb_ref[...]
    capped = ML_GATE_CAP * jnp.tanh(gates / ML_GATE_CAP)
    valid = (ci * size + (_iota2((nb * size, 1), 0) & (size - 1))) < t_valid
    ipre_all = jnp.where(valid, capped[:, 0:ML_HEADS], NEG)
    logf_all = jnp.where(valid, -_softplus(-capped[:, ML_HEADS:2 * ML_HEADS]), 0.0)
    units = [(sq, h) for sq in range(nb) for h in range(ML_HEADS)]
    rs = [slice(sq * size, (sq + 1) * size) for sq, _ in units]
    ksl = [slice(h * ML_DK, (h + 1) * ML_DK) for _, h in units]
    vsl = [slice(h * ML_DV, (h + 1) * ML_DV) for _, h in units]
    ids = range(len(units))
    q = [q_ref[rs[u], ksl[u]] for u in ids]
    k = [k_ref[rs[u], ksl[u]] * (ML_DK ** -0.5) for u in ids]
    v = [v_ref[rs[u], vsl[u]] for u in ids]
    qk = [_dot_nt(q[u], k[u]) for u in ids]
    c_prev = [c_ref[sq, h] for sq, h in units]
    n_prev = [n_ref[sq, h:h + 1, :] for sq, h in units]
    qc = [_dot(q[u], c_prev[u]) for u in ids]
    s, w_inter, m_t, kw, decay = [], [], [], [], []
    for u, (sq, h) in enumerate(units):
        ig_col = ipre_all[rs[u], h:h + 1]
        lf_col = logf_all[rs[u], h:h + 1]
        ig_row = _col_to_row(ig_col, eye)
        lf_row = _col_to_row(lf_col, eye)
        b_col = jnp.sum(jnp.where(causal, lf_row, 0.0), axis=1, keepdims=True)
        b_row = jnp.sum(jnp.where(rows <= cols, lf_col, 0.0), axis=0, keepdims=True)
        d = jnp.where(causal, b_col - b_row + ig_row, NEG)
        inter = b_col + m_ref[sq, :, h:h + 1]
        m_h = jnp.maximum(inter, jnp.max(d, axis=1, keepdims=True))
        w_h = jnp.exp(inter - m_h)
        s.append(qk[u] * jnp.exp(d - m_h))
        b_last = b_col[size - 1:size, :]
        m_last = m_h[size - 1:size, :]
        w_last = jnp.exp(b_last - b_col + ig_col - m_last)
        kw.append(k[u] * w_last)
        w_inter.append(w_h)
        m_t.append(m_h)
        decay.append(w_h[size - 1:size, :])
        m_ref[sq, :, h:h + 1] = m_last
    sv = [_dot(s[u], v[u]) for u in ids]
    kv = [_dot_tn(kw[u], v[u]) for u in ids]
    for u, (sq, h) in enumerate(units):
        num = w_inter[u] * qc[u] + sv[u]
        den = (w_inter[u] * jnp.sum(q[u] * n_prev[u], axis=1, keepdims=True)
               + jnp.sum(s[u], axis=1, keepdims=True))
        hid = num / jnp.maximum(jnp.abs(den), jnp.exp(-m_t[u]))
        c_ref[sq, h] = decay[u] * c_prev[u] + kv[u]
        n_ref[sq, h:h + 1, :] = decay[u] * n_prev[u] + jnp.sum(kw[u], axis=0, keepdims=True)
        hid = hid * lax.rsqrt(jnp.mean(hid * hid, axis=-1, keepdims=True) + ML_NORM_EPS)
        hid = hid * nw_ref[:, vsl[u]]
        hid = hid * _sigmoid(o_ref[rs[u], vsl[u]])
        h_ref[rs[u], vsl[u]] = hid.astype(h_ref.dtype)


def mlstm_group(proj, dst, row0, bsz, t_pad, t_valid, chunk, c0, n0, m0, b_i, b_f, norm_w, nb=1):
    nc = t_pad // chunk
    rows_blk = nb * chunk
    blk0 = row0 // rows_blk
    assert row0 % rows_blk == 0 and t_pad % chunk == 0 and bsz % nb == 0 and (nb == 1 or nc == 1)
    assert chunk & (chunk - 1) == 0
    rmap = lambda c0_: (lambda b, c: (blk0 + b * nc + c, c0_))
    qk_w, v_w = ML_QK, ML_V
    in_specs = [
        pl.BlockSpec((rows_blk, qk_w), rmap(0)),
        pl.BlockSpec((rows_blk, qk_w), rmap(1)),
        pl.BlockSpec((rows_blk, v_w), rmap(1)),
        pl.BlockSpec((rows_blk, v_w), rmap(2)),
        pl.BlockSpec((rows_blk, LANES), rmap((2 * qk_w + 2 * v_w) // LANES)),
        pl.BlockSpec((nb, ML_HEADS, ML_DK, ML_DV), lambda b, c: (b, 0, 0, 0)),
        pl.BlockSpec((nb, ML_HEADS, ML_DK), lambda b, c: (b, 0, 0)),
        pl.BlockSpec((nb, 1, ML_HEADS), lambda b, c: (b, 0, 0)),
        pl.BlockSpec((1, v_w), lambda b, c: (0, 0)),
        pl.BlockSpec((1, 2 * ML_HEADS), lambda b, c: (0, 0)),
        pl.BlockSpec(memory_space=pl.ANY),
    ]
    out_specs = [
        pl.BlockSpec((rows_blk, v_w), lambda b, c: (blk0 + b * nc + c, 0)),
        pl.BlockSpec((nb, ML_HEADS, ML_DK, ML_DV), lambda b, c: (b, 0, 0, 0)),
        pl.BlockSpec((nb, ML_HEADS, ML_DK), lambda b, c: (b, 0, 0)),
        pl.BlockSpec((nb, 1, ML_HEADS), lambda b, c: (b, 0, 0)),
    ]
    out_shape = [
        jax.ShapeDtypeStruct(dst.shape, dst.dtype),
        jax.ShapeDtypeStruct((bsz, ML_HEADS, ML_DK, ML_DV), F32),
        jax.ShapeDtypeStruct((bsz, ML_HEADS, ML_DK), F32),
        jax.ShapeDtypeStruct((bsz, 1, ML_HEADS), F32),
    ]
    gate_bias = jnp.concatenate([b_i, b_f]).reshape(1, 2 * ML_HEADS)
    h, c, n, m = pl.pallas_call(
        functools.partial(_mlstm_kernel, chunk=chunk, t_valid=t_valid, nb=nb),
        grid=(bsz // nb, nc),
        in_specs=in_specs,
        out_specs=out_specs,
        out_shape=out_shape,
        input_output_aliases={10: 0},
        compiler_params=_cparams(("arbitrary", "arbitrary")),
        name="mlstm",
    )(proj, proj, proj, proj, proj, c0, n0, m0.reshape(bsz, 1, ML_HEADS), norm_w.reshape(1, v_w), gate_bias, dst)
    return h, c, n, m.reshape(bsz, ML_HEADS)


def _head_indicator(width, head):
    idx = np.arange(width) // head
    return jnp.asarray((idx[:, None] == np.arange(LANES)[None, :]).astype(np.float32), dtype=BF16)


def _head_sums(x, ind_ref, ind_t_ref):
    return _dot_exact_rhs(_dot_exact_rhs(x, ind_ref[...]), ind_t_ref[...])


def _rwkv_prep_kernel(cur_ref, tail_ref, first_ref, mu_ref, w0_ref, a0_ref, kk_ref, ka_ref, rk_ref,
                      w2_ref, a2_ref, g2_ref, ind_ref, ind_t_ref,
                      r_ref, wl_ref, k_ref, v_ref, an_ref, bb_ref, bonus_ref, g_ref,
                      *, tt, t_pad, multi):
    cur = cur_ref[...]
    rolled = pltpu.roll(cur, 1, 0)
    row = _iota2((tt, 1), 0)
    if multi:
        prev = jnp.where((row & (t_pad - 1)) == 0, first_ref[...], rolled)
    else:
        starts_seq = (pl.program_id(0) % (t_pad // tt)) == 0
        head = jnp.where(starts_seq, first_ref[0], tail_ref[SUBLANES - 1:SUBLANES, :])
        prev = jnp.where(row == 0, head, rolled)
    mixed = cur + (prev - cur) * mu_ref[...]
    w = RW_WIDTH
    r = mixed[:, 0:w]
    k = mixed[:, w:2 * w]
    v = mixed[:, 2 * w:3 * w]
    c0 = 3 * w
    wl = mixed[:, c0:c0 + RW_DECAY_LORA]
    al = mixed[:, c0 + RW_DECAY_LORA:c0 + RW_DECAY_LORA + RW_A_LORA]
    gl = mixed[:, c0 + RW_DECAY_LORA + RW_A_LORA:]
    w_log = -jnp.exp(-_softplus(-(w0_ref[...] + _dot(jnp.tanh(wl), w2_ref[...]))) - 0.5)
    a = _sigmoid(a0_ref[...] + _dot(al, a2_ref[...]))
    g = _dot(_sigmoid(gl), g2_ref[...])
    kk = k * kk_ref[...]
    kkn = kk / jnp.maximum(jnp.sqrt(_head_sums(kk * kk, ind_ref, ind_t_ref)), 1e-12)
    k2 = k * (1.0 + (a - 1.0) * ka_ref[...])
    r_ref[...] = r
    wl_ref[...] = w_log
    k_ref[...] = k2
    v_ref[...] = v
    an_ref[...] = -kkn
    bb_ref[...] = kkn * a
    bonus_ref[...] = _head_sums(r * k2 * rk_ref[...], ind_ref, ind_t_ref) * v
    g_ref[...] = g


def rwkv_prep(proj, row0, bsz, t_pad, shift0, prm, tt):
    n = bsz * t_pad
    multi = tt > t_pad
    assert row0 % tt == 0 and n % tt == 0 and (tt % t_pad == 0 if multi else t_pad % tt == 0)
    blk0 = row0 // tt
    c = RW_COLS
    if multi:
        first = jnp.zeros((bsz, t_pad, c), F32).at[:, 0].set(shift0).reshape(n, c)
        first_spec = pl.BlockSpec((tt, c), lambda i: (i, 0))
    else:
        first = shift0.reshape(bsz, 1, c)
        first_spec = pl.BlockSpec((1, 1, c), lambda i: (i // (t_pad // tt), 0, 0))
    tail_blk = tt // SUBLANES
    vec = lambda width: pl.BlockSpec((1, width), lambda i: (0, 0))
    full = lambda a, b: pl.BlockSpec((a, b), lambda i: (0, 0))
    w = RW_WIDTH
    out_spec = pl.BlockSpec((tt, w), lambda i: (i, 0))
    return pl.pallas_call(
        functools.partial(_rwkv_prep_kernel, tt=tt, t_pad=t_pad, multi=multi),
        grid=(n // tt,),
        in_specs=[pl.BlockSpec((tt, c), lambda i: (blk0 + i, 0)),
                  pl.BlockSpec((SUBLANES, c), lambda i: (jnp.maximum((blk0 + i) * tail_blk - 1, 0), 0)),
                  first_spec, vec(c), vec(w), vec(w), vec(w), vec(w), vec(w),
                  full(RW_DECAY_LORA, w), full(RW_A_LORA, w), full(RW_GATE_LORA, w),
                  full(w, LANES), full(LANES, w)],
        out_specs=[out_spec] * 8,
        out_shape=[jax.ShapeDtypeStruct((n, w), F32)] * 8,
        compiler_params=_cparams(("arbitrary",)),
        name="rwkv_prep",
    )(proj, proj, first, prm['mu'].reshape(1, c), prm['w0'].reshape(1, w), prm['a0'].reshape(1, w),
      prm['k_k'].reshape(1, w), prm['k_a'].reshape(1, w), prm['r_k'].reshape(1, w),
      prm['w2'], prm['a2'], prm['g2'], _head_indicator(w, RW_DH), _head_indicator(w, RW_DH).T)


def _neumann_inverse_many(n_mats, size):
    eye = (_iota2((size, size), 0) == _iota2((size, size), 1)).astype(F32)
    ts = [eye + n_mat for n_mat in n_mats]
    xs = list(n_mats)
    steps = max(int(np.ceil(np.log2(size))) - 1, 0)
    for _ in range(steps):
        xs = [_dot(x, x) for x in xs]
        ts = [t + _dot(t, x) for t, x in zip(ts, xs)]
    return ts


def _rwkv_chunk_kernel(r_ref, w_ref, k_ref, v_ref, a_ref, b_ref, bonus_ref, g_ref, s0_ref,
                       lnw_ref, lnb_ref, _dst_ref, o_ref, s_ref, acc_ref, *, chunk, t_valid, hb, nb):
    ci = pl.program_id(2)

    @pl.when(ci == 0)
    def _():
        s_ref[...] = s0_ref[...]

    size = chunk
    rows = _iota2((size, size), 0)
    cols = _iota2((size, size), 1)
    strict = cols < rows
    rows2 = _iota2((size, 2 * size), 0)
    cols2 = _iota2((size, 2 * size), 1)
    mask_ak = (cols2 >= size) & (cols2 - size < rows2)
    mask_o = jnp.where(cols2 >= size, cols2 - size, cols2) <= rows2
    slab = nb * size
    rows_s = _iota2((slab, slab), 0)
    cols_s = _iota2((slab, slab), 1)
    shift = size.bit_length() - 1
    same_seq_tril = ((rows_s >> shift) == (cols_s >> shift)) & (cols_s <= rows_s)
    valid = (ci * size + (_iota2((slab, 1), 0) & (size - 1))) < t_valid
    w = jnp.where(valid, w_ref[...], 0.0)
    a = jnp.where(valid, a_ref[...], 0.0)
    b = jnp.where(valid, b_ref[...], 0.0)
    k = jnp.where(valid, k_ref[...], 0.0)
    v = v_ref[...]
    lam = _dot_exact_lhs(same_seq_tril.astype(F32), w)
    e_pos = jnp.exp(lam)
    e_neg = jnp.exp(-lam)
    at = a * jnp.exp(lam - w)
    bt = b * e_neg
    kt = k * e_neg
    rt = r_ref[...] * e_pos
    units = [(sq, h) for sq in range(nb) for h in range(hb)]
    ids = range(len(units))
    rs = [slice(sq * size, (sq + 1) * size) for sq, _ in units]
    sls = [slice(h * RW_DH, (h + 1) * RW_DH) for _, h in units]
    at_h = [at[rs[u], sls[u]] for u in ids]
    rt_h = [rt[rs[u], sls[u]] for u in ids]
    v_h = [v[rs[u], sls[u]] for u in ids]
    bk = [jnp.concatenate([bt[rs[u], sls[u]], kt[rs[u], sls[u]]], axis=0) for u in ids]
    s0 = [s_ref[sq, h] for sq, h in units]
    pa = [_dot_nt(at_h[u], bk[u]) for u in ids]
    pr = [_dot_nt(rt_h[u], bk[u]) for u in ids]
    as0 = [_dot_nt(at_h[u], s0[u]) for u in ids]
    rs0 = [_dot_nt(rt_h[u], s0[u]) for u in ids]
    zv = [jnp.concatenate([jnp.zeros_like(v_h[u]), v_h[u]], axis=0) for u in ids]
    rhs = [as0[u] + _dot(jnp.where(mask_ak, pa[u], 0.0), zv[u]) for u in ids]
    t_inv = _neumann_inverse_many([jnp.where(strict, pa[u][:, :size], 0.0) for u in ids], size)
    uu = [_dot(t_inv[u], rhs[u]) for u in ids]
    uv = [jnp.concatenate([uu[u], v_h[u]], axis=0) for u in ids]
    o = [rs0[u] + _dot(jnp.where(mask_o, pr[u], 0.0), uv[u]) for u in ids]
    ds = [_dot_tn(uv[u], bk[u]) for u in ids]
    for u, (sq, h) in enumerate(units):
        last = (sq + 1) * size - 1
        s_ref[sq, h] = (s0[u] + ds[u]) * e_pos[last:last + 1, sls[u]]
        oc = o[u] - jnp.mean(o[u], axis=-1, keepdims=True)
        acc_ref[rs[u], sls[u]] = oc * lax.rsqrt(jnp.mean(oc * oc, axis=-1, keepdims=True) + RW_GN_EPS)
    out = (acc_ref[...] * lnw_ref[...] + lnb_ref[...] + bonus_ref[...]) * g_ref[...]
    o_ref[...] = out.astype(o_ref.dtype)


def rwkv_chunk(prep, dst, dst_col0, row0, bsz, t_pad, t_valid, chunk, s0, ln_w, ln_b, hb, nb=1):
    nc = t_pad // chunk
    hg = RW_HEADS // hb
    wb = hb * RW_DH
    rows_blk = nb * chunk
    blk0 = row0 // rows_blk
    assert dst_col0 % wb == 0 and row0 % rows_blk == 0 and bsz % nb == 0 and (nb == 1 or nc == 1)
    assert chunk & (chunk - 1) == 0
    tok = pl.BlockSpec((rows_blk, wb), lambda b, g, c: (b * nc + c, g))
    st = pl.BlockSpec((nb, hb, RW_DH, RW_DH), lambda b, g, c: (b, g, 0, 0))
    vec = pl.BlockSpec((1, wb), lambda b, g, c: (0, g))
    return pl.pallas_call(
        functools.partial(_rwkv_chunk_kernel, chunk=chunk, t_valid=t_valid, hb=hb, nb=nb),
        grid=(bsz // nb, hg, nc),
        in_specs=[tok] * 8 + [st, vec, vec, pl.BlockSpec(memory_space=pl.ANY)],
        out_specs=[pl.BlockSpec((rows_blk, wb), lambda b, g, c: (blk0 + b * nc + c, dst_col0 // wb + g)), st],
        out_shape=[jax.ShapeDtypeStruct(dst.shape, dst.dtype),
                   jax.ShapeDtypeStruct((bsz, RW_HEADS, RW_DH, RW_DH), F32)],
        scratch_shapes=[pltpu.VMEM((rows_blk, wb), F32)],
        input_output_aliases={11: 0},
        compiler_params=_cparams(("arbitrary", "arbitrary", "arbitrary")),
        name="rwkv_chunk",
    )(*prep, s0, ln_w.reshape(1, RW_WIDTH), ln_b.reshape(1, RW_WIDTH), dst)


def _conv_silu(cur_ref, hist_ref, cw_ref, size):
    cur = cur_ref[...]
    hist = hist_ref[...]
    cw = cw_ref[...]
    row8 = _iota2((SUBLANES, 1), 0)
    acc = cur * cw[GD_CONV - 1:GD_CONV, :]
    for j in range(1, GD_CONV):
        back = GD_CONV - 1 - j
        rolled = pltpu.roll(cur, j, 0)
        head = jnp.where(row8 < j, pltpu.roll(hist, j, 0), rolled[:SUBLANES])
        prev = head if size == SUBLANES else jnp.concatenate([head, rolled[SUBLANES:]], axis=0)
        acc = acc + prev * cw[back:back + 1, :]
    hist_ref[...] = cur[size - SUBLANES:size]
    return acc * _sigmoid(acc)


def _l2_normalize(x, scale):
    return x * lax.rsqrt(jnp.sum(x * x, axis=-1, keepdims=True) + GD_EPS) * scale


def _gdn_chunk_kernel(q_ref, k_ref, v_ref, z_ref, gl_ref, bq_ref, bk_ref, bv_ref, cq_ref, ck_ref, cv_ref,
                      al_ref, dt_ref, nw_ref, s0_ref, _dst_ref,
                      o_ref, s_ref, nq_ref, nk_ref, nv_ref, hq_ref, hk_ref, hv_ref,
                      *, chunk, t_valid, hpg, tail_chunk, tail_row0):
    ci = pl.program_id(2)
    hg = pl.program_id(1)

    @pl.when(ci == 0)
    def _():
        s_ref[...] = s0_ref[...]
        hq_ref[...] = bq_ref[0]
        hk_ref[...] = bk_ref[0]
        hv_ref[...] = bv_ref[0]

    @pl.when(ci == tail_chunk)
    def _():
        tail = slice(tail_row0, tail_row0 + SUBLANES)
        nq_ref[0] = q_ref[tail, :]
        nk_ref[0] = k_ref[tail, :]
        nv_ref[0] = v_ref[tail, :]

    size = chunk
    q_act = _conv_silu(q_ref, hq_ref, cq_ref, size)
    k_act = _conv_silu(k_ref, hk_ref, ck_ref, size)
    v_act = _conv_silu(v_ref, hv_ref, cv_ref, size)
    rows = _iota2((size, size), 0)
    cols = _iota2((size, size), 1)
    eye = rows == cols
    incl = cols <= rows
    strict = cols < rows
    valid = (ci * size + _iota2((size, 1), 0)) < t_valid
    logits = gl_ref[...]
    beta_all = _sigmoid(logits)
    g_all = -jnp.exp(al_ref[...]) * _softplus(logits + dt_ref[...])
    lane = _iota2((size, LANES), 1)
    rep = GD_V_HEADS // GD_K_HEADS
    kheads = range(hpg // rep)
    heads = range(hpg)
    q = [_l2_normalize(q_act[:, kh * GD_DK:(kh + 1) * GD_DK], GD_DK ** -0.5) for kh in kheads]
    k = [_l2_normalize(k_act[:, kh * GD_DK:(kh + 1) * GD_DK], 1.0) for kh in kheads]
    qk = [_dot_nt(q[kh], k[kh]) for kh in kheads]
    kk = [_dot_nt(k[kh], k[kh]) for kh in kheads]
    vsl = [slice(hl * GD_DV, (hl + 1) * GD_DV) for hl in heads]
    s = [s_ref[0, hl] for hl in heads]
    beta, gc, decay = [], [], []
    for hl in heads:
        head = hg * hpg + hl
        beta_h = jnp.sum(jnp.where(lane == head, beta_all, 0.0), axis=1, keepdims=True)
        g = jnp.sum(jnp.where(lane == head + GD_V_HEADS, g_all, 0.0), axis=1, keepdims=True)
        beta_h = jnp.where(valid, beta_h, 0.0)
        g = jnp.where(valid, g, 0.0)
        g_row = _col_to_row(g, eye)
        gc_h = jnp.sum(jnp.where(incl, g_row, 0.0), axis=1, keepdims=True)
        gc_row = jnp.sum(jnp.where(rows <= cols, g, 0.0), axis=0, keepdims=True)
        beta.append(beta_h)
        gc.append(gc_h)
        decay.append(jnp.where(incl, jnp.exp(jnp.where(incl, gc_h - gc_row, 0.0)), 0.0))
    qs = [_dot(q[hl // rep] * jnp.exp(gc[hl]), s[hl]) for hl in heads]
    t_inv = _neumann_inverse_many(
        [-jnp.where(strict, kk[hl // rep] * beta[hl] * decay[hl], 0.0) for hl in heads], size)
    uw = [_dot(t_inv[hl], jnp.concatenate([v_act[:, vsl[hl]] * beta[hl],
                                           k[hl // rep] * (beta[hl] * jnp.exp(gc[hl]))], axis=1))
          for hl in heads]
    ws = [_dot(uw[hl][:, GD_DV:], s[hl]) for hl in heads]
    v_new = [uw[hl][:, :GD_DV] - ws[hl] for hl in heads]
    av = [_dot(qk[hl // rep] * decay[hl], v_new[hl]) for hl in heads]
    g_last = [gc[hl][size - 1:size, :] for hl in heads]
    kv = [_dot_tn(k[hl // rep] * jnp.exp(g_last[hl] - gc[hl]), v_new[hl]) for hl in heads]
    for hl in heads:
        s_ref[0, hl] = s[hl] * jnp.exp(g_last[hl]) + kv[hl]
        o = qs[hl] + av[hl]
        o = o * lax.rsqrt(jnp.mean(o * o, axis=-1, keepdims=True) + GD_EPS) * nw_ref[...]
        z = z_ref[:, vsl[hl]]
        o_ref[:, vsl[hl]] = (o * (z * _sigmoid(z))).astype(o_ref.dtype)


def gdn_chunk(proj, gates, conv_buf, conv_w, dst, row0, bsz, t_pad, t_valid, chunk, s0, a_log, dt_bias,
              norm_w, hpg):
    nc = t_pad // chunk
    hgs = GD_V_HEADS // hpg
    rep = GD_V_HEADS // GD_K_HEADS
    kw = (hpg // rep) * GD_DK
    vw = hpg * GD_DV
    blk0 = row0 // chunk
    assert row0 % chunk == 0 and chunk % SUBLANES == 0
    pad_row = lambda x: jnp.pad(x, (GD_V_HEADS, LANES - 2 * GD_V_HEADS)).reshape(1, LANES)
    hist = jnp.pad(conv_buf, ((0, 0), (SUBLANES - (GD_CONV - 1), 0), (0, 0)))
    st = pl.BlockSpec((1, hpg, GD_DK, GD_DV), lambda b, g, c: (b, g, 0, 0))
    vec = pl.BlockSpec((1, LANES), lambda b, g, c: (0, 0))
    q_col, k_col, v_col = (lambda g: g), (lambda g: GD_QK // kw + g), (lambda g: 2 * GD_QK // vw + g)
    tok = lambda width, col: pl.BlockSpec((chunk, width), lambda b, g, c: (blk0 + b * nc + c, col(g)))
    buf = lambda width, col: pl.BlockSpec((1, SUBLANES, width), lambda b, g, c: (b, 0, col(g)))
    taps = lambda width, col: pl.BlockSpec((GD_CONV, width), lambda b, g, c: (0, col(g)))
    last = (t_valid - 1) % chunk
    tail_row0 = (last // SUBLANES) * SUBLANES
    first = last - (GD_CONV - 2) - tail_row0
    assert first >= 0
    new_buf = lambda width: jax.ShapeDtypeStruct((bsz, SUBLANES, width), F32)
    out, state, nq, nk, nv = pl.pallas_call(
        functools.partial(_gdn_chunk_kernel, chunk=chunk, t_valid=t_valid, hpg=hpg,
                          tail_chunk=(t_valid - 1) // chunk, tail_row0=tail_row0),
        grid=(bsz, hgs, nc),
        in_specs=[tok(kw, q_col), tok(kw, k_col), tok(vw, v_col),
                  tok(vw, lambda g: GD_CONV_DIM // vw + g),
                  pl.BlockSpec((chunk, LANES), lambda b, g, c: (blk0 + b * nc + c, 0)),
                  buf(kw, q_col), buf(kw, k_col), buf(vw, v_col),
                  taps(kw, q_col), taps(kw, k_col), taps(vw, v_col),
                  vec, vec, vec, st, pl.BlockSpec(memory_space=pl.ANY)],
        out_specs=[pl.BlockSpec((chunk, vw), lambda b, g, c: (blk0 + b * nc + c, g)), st,
                   pl.BlockSpec((1, SUBLANES, kw), lambda b, g, c: (b, 0, g)),
                   pl.BlockSpec((1, SUBLANES, kw), lambda b, g, c: (b, 0, g)),
                   pl.BlockSpec((1, SUBLANES, vw), lambda b, g, c: (b, 0, g))],
        out_shape=[jax.ShapeDtypeStruct(dst.shape, dst.dtype),
                   jax.ShapeDtypeStruct((bsz, GD_V_HEADS, GD_DK, GD_DV), F32),
                   new_buf(GD_QK), new_buf(GD_QK), new_buf(GD_V_WIDTH)],
        scratch_shapes=[pltpu.VMEM((SUBLANES, kw), F32), pltpu.VMEM((SUBLANES, kw), F32),
                        pltpu.VMEM((SUBLANES, vw), F32)],
        input_output_aliases={15: 0},
        compiler_params=_cparams(("arbitrary", "arbitrary", "arbitrary")),
        name="gdn_chunk",
    )(proj, proj, proj, proj, gates, hist, hist, hist, conv_w, conv_w, conv_w,
      pad_row(a_log), pad_row(dt_bias), norm_w.reshape(1, GD_DV), s0, dst)
    conv_state = jnp.concatenate([nq, nk, nv], axis=-1)[:, first:first + GD_CONV - 1]
    return out, state, conv_state


MOE_TM = 256


def _router_kernel(x_ref, w_ref, wt_ref, id_ref):
    logits = jnp.dot(x_ref[...], w_ref[...].astype(BF16), preferred_element_type=F32)
    lane = _iota2(logits.shape, 1).astype(F32)
    first_of = lambda hit: jnp.min(jnp.where(hit, lane, float(LANES)), axis=1, keepdims=True)
    gl = jnp.where(lane < MOE_GROUPS, logits, NEG)
    gmax = jnp.max(gl, axis=1, keepdims=True)
    g_val = 1.0 / jnp.sum(jnp.exp(gl - gmax), axis=1, keepdims=True)
    lo = MOE_GROUPS + first_of(gl == gmax) * MOE_PER_GROUP
    vals = jnp.where((lane >= lo) & (lane < lo + MOE_PER_GROUP), logits, NEG)
    top1 = jnp.max(vals, axis=1, keepdims=True)
    i1 = first_of(vals == top1)
    vals2 = jnp.where(lane == i1, NEG, vals)
    top2 = jnp.max(vals2, axis=1, keepdims=True)
    i2 = first_of(vals2 == top2)
    e2 = jnp.exp(top2 - top1)
    w1 = (1.0 / (1.0 + e2)) * g_val
    w2 = (e2 / (1.0 + e2)) * g_val
    wt_ref[...] = jnp.where(lane == 0, w1, jnp.where(lane == 1, w2, 0.0))
    ids = jnp.where(lane == 0, i1 - MOE_GROUPS, jnp.where(lane == 1, i2 - MOE_GROUPS, 0.0))
    id_ref[...] = ids.astype(jnp.int32)


def moe_router(xb, w_router, tm=512):
    n, d = xb.shape
    out = pl.BlockSpec((tm, LANES), lambda i: (i, 0))
    return pl.pallas_call(
        _router_kernel,
        grid=(n // tm,),
        in_specs=[pl.BlockSpec((tm, d), lambda i: (i, 0)), pl.BlockSpec((d, LANES), lambda i: (0, 0))],
        out_specs=[out, out],
        out_shape=[jax.ShapeDtypeStruct((n, LANES), F32), jax.ShapeDtypeStruct((n, LANES), jnp.int32)],
        compiler_params=_cparams(("arbitrary",)),
        name="moe_router",
    )(xb, w_router)


def _moe_ffn_kernel(te_ref, cnt_ref, first_ref, next_ref, wslot_ref,
                    idx_ref, idx_next_ref, x_hbm, wg_hbm, wu_hbm, wd_hbm, o_ref,
                    xbuf, wgf, wuf, wdf, wgb, wub, wdb, gsem, wsem, *, layer):
    i = pl.program_id(0)
    n_tiles = pl.num_programs(0)
    slot = i % 2

    def gather(rows_ref, into, count):
        def start_row(r, carry):
            pltpu.make_async_copy(x_hbm.at[pl.ds(rows_ref[0, 0, r], 1)],
                                  xbuf.at[into, pl.ds(r, 1)], gsem.at[into]).start()
            return carry
        lax.fori_loop(0, count, start_row, 0)

    def gather_wait(into, count):
        def wait_row(r, carry):
            pltpu.make_async_copy(x_hbm.at[pl.ds(0, 1)], xbuf.at[into, pl.ds(0, 1)], gsem.at[into]).wait()
            return carry
        lax.fori_loop(0, count, wait_row, 0)

    def weight_copies(expert, ws):
        return [pltpu.make_async_copy(src.at[layer, expert], dst.at[ws], wsem.at[ws])
                for src, dst in ((wg_hbm, wgf), (wu_hbm, wuf), (wd_hbm, wdf))]

    @pl.when(i == 0)
    def _():
        xbuf[...] = jnp.zeros_like(xbuf)
        gather(idx_ref, 0, cnt_ref[0])
        for copy in weight_copies(te_ref[0], 0):
            copy.start()

    nxt = jnp.minimum(i + 1, n_tiles - 1)

    @pl.when(i + 1 < n_tiles)
    def _():
        gather(idx_next_ref, 1 - slot, cnt_ref[nxt])

    @pl.when(cnt_ref[i] > 0)
    def _():
        gather_wait(slot, cnt_ref[i])

        @pl.when(first_ref[i] == 1)
        def _():
            ws = wslot_ref[i]
            for copy in weight_copies(te_ref[i], ws):
                copy.wait()
            wgb[...] = wgf[ws].astype(BF16)
            wub[...] = wuf[ws].astype(BF16)
            wdb[...] = wdf[ws].astype(BF16)

            @pl.when(next_ref[i] >= 0)
            def _():
                for copy in weight_copies(next_ref[i], 1 - ws):
                    copy.start()

        x = xbuf[slot].astype(BF16)
        gate = jnp.dot(x, wgb[...], preferred_element_type=F32)
        up = jnp.dot(x, wub[...], preferred_element_type=F32)
        hid = gate * _sigmoid(gate) * up
        o_ref[...] = jnp.dot(hid.astype(BF16), wdb[...], preferred_element_type=F32)

    @pl.when(cnt_ref[i] == 0)
    def _():
        o_ref[...] = jnp.zeros_like(o_ref)


def moe_ffn(x, row_token, tables, layer, w_gate, w_up, w_down):
    n_tiles = row_token.shape[0]
    d = x.shape[1]
    ff = w_gate.shape[3]
    any_space = pl.BlockSpec(memory_space=pl.ANY)
    grid_spec = pltpu.PrefetchScalarGridSpec(
        num_scalar_prefetch=5,
        grid=(n_tiles,),
        in_specs=[pl.BlockSpec((1, 1, MOE_TM), lambda i, *_: (i, 0, 0), memory_space=pltpu.SMEM),
                  pl.BlockSpec((1, 1, MOE_TM), lambda i, *_: (jnp.minimum(i + 1, n_tiles - 1), 0, 0),
                               memory_space=pltpu.SMEM),
                  any_space, any_space, any_space, any_space],
        out_specs=pl.BlockSpec((MOE_TM, d), lambda i, *_: (i, 0)),
        scratch_shapes=[pltpu.VMEM((2, MOE_TM, d), F32),
                        pltpu.VMEM((2, d, ff), F32), pltpu.VMEM((2, d, ff), F32), pltpu.VMEM((2, ff, d), F32),
                        pltpu.VMEM((d, ff), BF16), pltpu.VMEM((d, ff), BF16), pltpu.VMEM((ff, d), BF16),
                        pltpu.SemaphoreType.DMA((2,)), pltpu.SemaphoreType.DMA((2,))],
    )
    return pl.pallas_call(
        functools.partial(_moe_ffn_kernel, layer=layer),
        grid_spec=grid_spec,
        out_shape=jax.ShapeDtypeStruct((n_tiles * MOE_TM, d), F32),
        compiler_params=_cparams(("arbitrary",)),
        name="moe_ffn",
    )(*tables, row_token, row_token, x, w_gate, w_up, w_down)


def _ln_moe_ple_kernel(x_ref, ya_ref, yb_ref, wt_ref, g_ref, b_ref, p_ref, wg_ref, wp_ref, o_ref, ob_ref):
    wt = wt_ref[...]
    z = DEEPNORM_ALPHA * x_ref[...] + (wt[:, 0:1] * ya_ref[...] + wt[:, 1:2] * yb_ref[...])
    zc = z - jnp.mean(z, axis=-1, keepdims=True)
    var = jnp.mean(zc * zc, axis=-1, keepdims=True)
    hid = zc * lax.rsqrt(var + LN_EPS) * g_ref[...] + b_ref[...]
    gate = _sigmoid(jnp.dot(hid.astype(BF16), wg_ref[...], preferred_element_type=F32))
    emb = jnp.dot(p_ref[...].astype(BF16), wp_ref[...], preferred_element_type=F32)
    out = hid + gate * emb
    o_ref[...] = out
    ob_ref[...] = out.astype(BF16)


def ln_moe_ple(x, ya, yb, wt, g, b, p, wg, wp, tm=512):
    m, d = x.shape
    pd = p.shape[1]
    tm = _row_tile(m, tm)
    row = pl.BlockSpec((tm, d), lambda i: (i, 0))
    vec = pl.BlockSpec((1, d), lambda i: (0, 0))
    once = pl.Buffered(1)
    return pl.pallas_call(
        _ln_moe_ple_kernel,
        grid=(m // tm,),
        in_specs=[row, row, row, pl.BlockSpec((tm, LANES), lambda i: (i, 0)), vec, vec,
                  pl.BlockSpec((tm, pd), lambda i: (i, 0)),
                  pl.BlockSpec((d, d), lambda i: (0, 0), pipeline_mode=once),
                  pl.BlockSpec((pd, d), lambda i: (0, 0), pipeline_mode=once)],
        out_specs=[row, row],
        out_shape=[jax.ShapeDtypeStruct((m, d), F32), jax.ShapeDtypeStruct((m, d), BF16)],
        compiler_params=_cparams(("arbitrary",)),
        name="ln_moe_ple",
    )(x, ya, yb, wt, g.reshape(1, d), b.reshape(1, d), p, wg, wp)


def _table_lookup(table, idx):
    hit = idx[:, None] == jnp.arange(table.shape[0], dtype=idx.dtype)[None, :]
    return jnp.sum(jnp.where(hit, table[None, :], 0), axis=1)


def hier_moe(x, xb, layer, w_group, w_expert, w_gate, w_up, w_down):
    n = xb.shape[0]
    w_router = jnp.pad(jnp.concatenate([w_group, w_expert], axis=1),
                       ((0, 0), (0, LANES - MOE_GROUPS - MOE_EXPERTS)))
    wt, ids = moe_router(xb, w_router)

    n_pairs = n * MOE_TOPK
    n_tiles = -(-n_pairs // MOE_TM) + MOE_EXPERTS
    n_rows = n_tiles * MOE_TM
    i32 = jnp.int32
    eid = ids[:, :MOE_TOPK].reshape(n_pairs)
    order = jnp.argsort(eid, stable=True).astype(i32)
    rank = jnp.argsort(order).astype(i32)
    experts = jnp.arange(MOE_EXPERTS, dtype=i32)
    counts = jnp.sum((eid[:, None] == experts[None, :]).astype(i32), axis=0)
    starts = jnp.cumsum(counts) - counts
    tiles_per = (counts + MOE_TM - 1) // MOE_TM
    tile_starts = jnp.cumsum(tiles_per) - tiles_per
    dest = _table_lookup(tile_starts * MOE_TM - starts, eid) + rank
    tile_idx = jnp.arange(n_tiles, dtype=i32)
    tile_expert = jnp.sum((tile_idx[:, None] >= (tile_starts + tiles_per)[None, :]).astype(i32), axis=1)
    tile_valid = (tile_expert < MOE_EXPERTS).astype(i32)
    used = tiles_per > 0
    last_used = jnp.max(jnp.where(used, experts, 0))
    tile_expert = jnp.where(tile_valid == 1, tile_expert, last_used)
    tile_in_run = tile_idx - _table_lookup(tile_starts, tile_expert)
    off = tile_in_run[:, None] * MOE_TM + jnp.arange(MOE_TM, dtype=i32)[None, :]
    cnt = _table_lookup(counts, tile_expert)
    src = jnp.clip(_table_lookup(starts, tile_expert)[:, None] + off, 0, n_pairs - 1)
    row_ok = (off < cnt[:, None]) & (tile_valid[:, None] == 1)
    picked = jnp.take(order, src.reshape(n_rows), mode='clip').reshape(n_tiles, MOE_TM)
    row_token = jnp.where(row_ok, picked // MOE_TOPK, 0).reshape(n_tiles, 1, MOE_TM)
    tile_rows = jnp.clip(cnt - tile_in_run * MOE_TM, 0, MOE_TM) * tile_valid
    tile_first = tile_valid * (tile_in_run == 0).astype(i32)
    later_used = (experts[None, :] > experts[:, None]) & used[None, :]
    next_used = jnp.min(jnp.where(later_used, experts[None, :], MOE_EXPERTS), axis=1)
    next_used = jnp.where(next_used < MOE_EXPERTS, next_used, -1)
    run_index = jnp.cumsum(used.astype(i32)) - 1
    tables = (tile_expert, tile_rows, tile_first, _table_lookup(next_used, tile_expert),
              _table_lookup(run_index, tile_expert) % 2)

    y_sorted = moe_ffn(x, row_token, tables, layer, w_gate, w_up, w_down)
    dest2 = dest.reshape(n, MOE_TOPK)
    return (jnp.take(y_sorted, dest2[:, 0], axis=0, mode='clip'),
            jnp.take(y_sorted, dest2[:, 1], axis=0, mode='clip'), wt)


SAMPLE_T_PAD = 8


def kernel(x_prompt, x_sample, state_mlstm_c, state_mlstm_n, state_mlstm_m, state_rwkv_s,
           state_rwkv_shift, state_gdn_s, state_gdn_conv, p_prompt, p_sample,
           w_in_ab, ml_b_i, ml_b_f, ml_norm_w, rw_mu, rw_w0, rw_w2, rw_a0, rw_a2, rw_g2,
           rw_k_k, rw_k_a, rw_r_k, rw_ln_w, rw_ln_b, w_out_ab,
           gd_w_in, gd_conv_w, gd_a_log, gd_dt_bias, gd_norm_w, gd_w_out,
           ln_mix_g, ln_mix_b, moe_w_group, moe_w_expert, moe_w_gate, moe_w_up, moe_w_down,
           ln_ffn_g, ln_ffn_b, ple_w_gate, ple_w_proj):
    pb, pt, d = x_prompt.shape
    sb, st, _ = x_sample.shape
    n_p = pb * pt
    n_s = sb * SAMPLE_T_PAD
    n = n_p + n_s
    pad_t = lambda a: jnp.pad(a, ((0, 0), (0, SAMPLE_T_PAD - st), (0, 0)))
    merge = lambda a_p, a_s: jnp.concatenate(
        [a_p.reshape(n_p, a_p.shape[-1]), pad_t(a_s).reshape(n_s, a_s.shape[-1])], axis=0)
    x = merge(x_prompt, x_sample)
    xb = x.astype(BF16)
    segs = (dict(row0=0, bsz=pb, t_pad=pt, t_valid=pt, ml_nb=1, rw_hb=16, rw_nb=1, gd_hpg=16),
            dict(row0=n_p, bsz=sb, t_pad=SAMPLE_T_PAD, t_valid=st, ml_nb=2, rw_hb=16, rw_nb=2, gd_hpg=32))
    zeros = lambda *shape: jnp.zeros(shape, F32)

    def last_rows(a, sg, count):
        assert sg['t_valid'] >= count
        idx = (sg['row0'] + jnp.arange(sg['bsz'])[:, None] * sg['t_pad']
               + (sg['t_valid'] - count + jnp.arange(count))[None, :])
        return jnp.take(a, idx.reshape(-1), axis=0, mode='clip').reshape(sg['bsz'], count, a.shape[-1])

    new_states = [[[] for _ in range(7)] for _ in segs]
    for layer in range(DEPTH):
        li = layer // 2
        if layer % 2 == 0:
            w_in_t = jnp.transpose(w_in_ab[li])
            proj_ml = matmul_wt(xb, w_in_t, 0, 3584, 896)
            proj_rw = matmul_wt(xb, w_in_t, ML_COLS, RW_COLS, RW_COLS // 2)
            rw_prm = dict(mu=rw_mu[li], w0=rw_w0[li], w2=rw_w2[li], a0=rw_a0[li], a2=rw_a2[li], g2=rw_g2[li],
                          k_k=rw_k_k[li], k_a=rw_k_a[li], r_k=rw_r_k[li])
            heads_out = jnp.zeros((n, ML_V + RW_WIDTH), BF16)
            for si, sg in enumerate(segs):
                bsz, t_pad, t_valid, row0 = sg['bsz'], sg['t_pad'], sg['t_valid'], sg['row0']
                if si == 0:
                    c0, n0, m0 = zeros(bsz, ML_HEADS, ML_DK, ML_DV), zeros(bsz, ML_HEADS, ML_DK), zeros(bsz, ML_HEADS)
                    s0, sh0 = zeros(bsz, RW_HEADS, RW_DH, RW_DH), zeros(bsz, RW_COLS)
                else:
                    c0, n0, m0 = state_mlstm_c[li], state_mlstm_n[li], state_mlstm_m[li]
                    s0, sh0 = state_rwkv_s[li], state_rwkv_shift[li]
                heads_out, c, nn, m = mlstm_group(proj_ml, heads_out, row0, bsz, t_pad, t_valid,
                                                  min(ML_CHUNK, t_pad), c0, n0, m0,
                                                  ml_b_i[li], ml_b_f[li], ml_norm_w[li], sg['ml_nb'])
                prep = rwkv_prep(proj_rw, row0, bsz, t_pad, sh0, rw_prm, 256)
                heads_out, rs = rwkv_chunk(prep, heads_out, ML_V, row0, bsz, t_pad, t_valid,
                                           min(RW_CHUNK, t_pad), s0, rw_ln_w[li], rw_ln_b[li],
                                           sg['rw_hb'], sg['rw_nb'])
                new_shift = last_rows(proj_rw, sg, 1)[:, 0]
                for slot, val in zip(range(5), (c, nn, m, rs, new_shift)):
                    new_states[si][slot].append(val)
            mix = matmul(heads_out, w_out_ab[li], d, 1024)
        else:
            w_in_t = jnp.transpose(gd_w_in[li])
            n_qkvz = GD_CONV_DIM + GD_V_WIDTH
            proj = matmul_wt(xb, w_in_t, 0, n_qkvz, 1024)
            gates = matmul_wt(xb, w_in_t, n_qkvz, LANES, LANES)
            gd_out = jnp.zeros((n, GD_V_WIDTH), BF16)
            for si, sg in enumerate(segs):
                bsz, t_pad, t_valid, row0 = sg['bsz'], sg['t_pad'], sg['t_valid'], sg['row0']
                if si == 0:
                    s0, buf0 = zeros(bsz, GD_V_HEADS, GD_DK, GD_DV), zeros(bsz, GD_CONV - 1, GD_CONV_DIM)
                else:
                    s0, buf0 = state_gdn_s[li], state_gdn_conv[li]
                gd_out, gs, gcb = gdn_chunk(proj, gates, buf0, gd_conv_w[li], gd_out, row0, bsz, t_pad, t_valid,
                                            min(GD_CHUNK, t_pad), s0, gd_a_log[li], gd_dt_bias[li],
                                            gd_norm_w[li], sg['gd_hpg'])
                new_states[si][5].append(gs)
                new_states[si][6].append(gcb)
            mix = matmul(gd_out, gd_w_out[li], d, 512)
        x, xb = ln_residual(x, mix, ln_mix_g[layer], ln_mix_b[layer])
        ya, yb, wt = hier_moe(x, xb, layer, moe_w_group[layer], moe_w_expert[layer],
                              moe_w_gate, moe_w_up, moe_w_down)
        x, xb = ln_moe_ple(x, ya, yb, wt, ln_ffn_g[layer], ln_ffn_b[layer],
                           merge(p_prompt[layer], p_sample[layer]),
                           ple_w_gate[layer].astype(BF16), ple_w_proj[layer].astype(BF16))
    y_prompt = x[:n_p].reshape(pb, pt, d)
    y_sample = x[n_p:].reshape(sb, SAMPLE_T_PAD, d)[:, :st]
    stack = lambda vals: jnp.stack(vals)
    return (y_prompt, y_sample) + tuple(stack(v) for v in new_states[0]) + tuple(stack(v) for v in new_states[1])
```

```python
import functools

import jax
import jax.numpy as jnp
import numpy as np
from jax import lax
from jax.experimental import pallas as pl
from jax.experimental.pallas import tpu as pltpu

F32 = jnp.float32
BF16 = jnp.bfloat16

D_MODEL = 2048
DEPTH = 2
DEEPNORM_ALPHA = (2 * DEPTH) ** 0.25
LN_EPS = 1e-5
MIX_HALF = D_MODEL // 2
ML_HEADS = 4
ML_DV = MIX_HALF // ML_HEADS
ML_DK = ML_DV // 2
ML_CHUNK = 64
ML_GATE_CAP = 15.0
ML_NORM_EPS = 1e-6
ML_QK = ML_HEADS * ML_DK
ML_V = ML_HEADS * ML_DV
ML_COLS = 2 * ML_QK + 2 * ML_V + 2 * ML_HEADS
RW_DH = 64
RW_HEADS = MIX_HALF // RW_DH
RW_WIDTH = RW_HEADS * RW_DH
RW_DECAY_LORA = 64
RW_A_LORA = 64
RW_GATE_LORA = 128
RW_GN_EPS = 64e-5
RW_COLS = 3 * RW_WIDTH + RW_DECAY_LORA + RW_A_LORA + RW_GATE_LORA
RW_CHUNK = 64
GD_DK = 128
GD_DV = 128
GD_K_HEADS = D_MODEL // GD_DK
GD_V_HEADS = 2 * GD_K_HEADS
GD_CONV = 4
GD_CHUNK = 64
GD_EPS = 1e-6
GD_QK = GD_K_HEADS * GD_DK
GD_V_WIDTH = GD_V_HEADS * GD_DV
GD_CONV_DIM = 2 * GD_QK + GD_V_WIDTH
MOE_GROUPS = 4
MOE_PER_GROUP = 8
MOE_EXPERTS = MOE_GROUPS * MOE_PER_GROUP
MOE_TOPK = 2
MOE_FF = D_MODEL // 4
PLE_DIM = 256

LANES = 128
SUBLANES = 8
NEG = -1e30
VMEM_LIMIT = 56 * 1024 * 1024


def _cparams(sem):
    return pltpu.CompilerParams(dimension_semantics=sem, vmem_limit_bytes=VMEM_LIMIT)


def _dot(a, b):
    return jnp.dot(a.astype(BF16), b.astype(BF16), preferred_element_type=F32)


def _dot_nt(a, b):
    return lax.dot_general(a.astype(BF16), b.astype(BF16), (((1,), (1,)), ((), ())),
                           preferred_element_type=F32)


def _dot_tn(a, b):
    return lax.dot_general(a.astype(BF16), b.astype(BF16), (((0,), (0,)), ((), ())),
                           preferred_element_type=F32)


def _split3(x):
    hi = x.astype(BF16)
    r1 = x - hi.astype(F32)
    mid = r1.astype(BF16)
    lo = (r1 - mid.astype(F32)).astype(BF16)
    return hi, mid, lo


def _dot_exact_rhs(a, b01):
    hi, mid, lo = _split3(a)
    b = b01.astype(BF16)
    return (jnp.dot(hi, b, preferred_element_type=F32) + jnp.dot(mid, b, preferred_element_type=F32)
            + jnp.dot(lo, b, preferred_element_type=F32))


def _dot_exact_lhs(a01, b):
    hi, mid, lo = _split3(b)
    a = a01.astype(BF16)
    return (jnp.dot(a, hi, preferred_element_type=F32) + jnp.dot(a, mid, preferred_element_type=F32)
            + jnp.dot(a, lo, preferred_element_type=F32))


def _iota2(shape, dim):
    return lax.broadcasted_iota(jnp.int32, shape, dim)


def _col_to_row(col, eye):
    return jnp.sum(jnp.where(eye, col, 0.0), axis=0, keepdims=True)


def _sigmoid(x):
    return 1.0 / (1.0 + jnp.exp(-x))


def _softplus(x):
    return jnp.maximum(x, 0.0) + jnp.log1p(jnp.exp(-jnp.abs(x)))


def _neumann_inverse(n_mat, size):
    eye = (_iota2((size, size), 0) == _iota2((size, size), 1)).astype(F32)
    t = eye + n_mat
    x = n_mat
    steps = max(int(np.ceil(np.log2(size))) - 1, 0)
    for _ in range(steps):
        x = _dot(x, x)
        t = t + _dot(t, x)
    return t


def _mm_kernel(x_ref, w_ref, o_ref, wb_ref, *, n_valid):
    @pl.when(pl.program_id(1) == 0)
    def _():
        wb_ref[...] = w_ref[...].astype(BF16)

    acc = jnp.dot(x_ref[...].astype(BF16), wb_ref[...], preferred_element_type=F32)
    if n_valid is not None:
        acc = jnp.where(_iota2(acc.shape, 1) < n_valid, acc, 0.0)
    o_ref[...] = acc.astype(o_ref.dtype)


def _row_tile(m, want):
    while m % want:
        want //= 2
    return want


def matmul(x, w, n_out, tn, tm=1024, out_dtype=F32, col0=0):
    m, k = x.shape
    tm = _row_tile(m, tm)
    assert w.shape[0] == k and n_out % tn == 0 and m % tm == 0 and tn % LANES == 0 and col0 % tn == 0
    n_valid = None
    if col0 + n_out > w.shape[1]:
        assert n_out == tn
        n_valid = w.shape[1] - col0
    blk0 = col0 // tn
    return pl.pallas_call(
        functools.partial(_mm_kernel, n_valid=n_valid),
        grid=(n_out // tn, m // tm),
        in_specs=[pl.BlockSpec((tm, k), lambda j, i: (i, 0)),
                  pl.BlockSpec((k, tn), lambda j, i: (0, blk0 + j))],
        out_specs=pl.BlockSpec((tm, tn), lambda j, i: (i, j)),
        out_shape=jax.ShapeDtypeStruct((m, n_out), out_dtype),
        scratch_shapes=[pltpu.VMEM((k, tn), BF16)],
        compiler_params=_cparams(("arbitrary", "arbitrary")),
        name="matmul",
    )(x, w)


def _mm_wt_kernel(x_ref, wt_hbm, o_ref, wf_ref, wb_ref, sem, *, row0, tn, n_rows):
    j = pl.program_id(0)

    @pl.when(pl.program_id(1) == 0)
    def _():
        if n_rows < tn:
            wf_ref[...] = jnp.zeros_like(wf_ref)
        start = pl.multiple_of(row0 + j * tn, SUBLANES)
        copy = pltpu.make_async_copy(wt_hbm.at[pl.ds(start, n_rows)], wf_ref.at[pl.ds(0, n_rows)], sem)
        copy.start()
        copy.wait()
        k = wf_ref.shape[1]
        step = 2 * LANES
        for c in range(k // step):
            wb_ref[c * step:(c + 1) * step, :] = wf_ref[:, c * step:(c + 1) * step].T.astype(BF16)

    o_ref[...] = jnp.dot(x_ref[...].astype(BF16), wb_ref[...], preferred_element_type=F32)


def matmul_wt(x, wt, row0, n_out, tn, tm=1024):
    m, k = x.shape
    tm = _row_tile(m, tm)
    assert wt.shape[1] == k and n_out % tn == 0 and tn % LANES == 0 and row0 % SUBLANES == 0
    n_rows = tn
    if row0 + n_out > wt.shape[0]:
        assert n_out == tn
        n_rows = wt.shape[0] - row0
    assert n_rows % SUBLANES == 0
    return pl.pallas_call(
        functools.partial(_mm_wt_kernel, row0=row0, tn=tn, n_rows=n_rows),
        grid=(n_out // tn, m // tm),
        in_specs=[pl.BlockSpec((tm, k), lambda j, i: (i, 0)), pl.BlockSpec(memory_space=pl.ANY)],
        out_specs=pl.BlockSpec((tm, tn), lambda j, i: (i, j)),
        out_shape=jax.ShapeDtypeStruct((m, n_out), F32),
        scratch_shapes=[pltpu.VMEM((tn, k), F32), pltpu.VMEM((k, tn), BF16), pltpu.SemaphoreType.DMA(())],
        compiler_params=_cparams(("arbitrary", "arbitrary")),
        name="matmul_wt",
    )(x, wt)


def _mlstm_kernel(q_ref, k_ref, v_ref, o_ref, g_ref, c0_ref, n0_ref, m0_ref, nw_ref, gb_ref, _dst_ref,
                  h_ref, c_ref, n_ref, m_ref, *, chunk, t_valid, nb):
    ci = pl.program_id(1)

    @pl.when(ci == 0)
    def _():
        c_ref[...] = c0_ref[...]
        n_ref[...] = n0_ref[...]
        m_ref[...] = m0_ref[...]

    size = chunk
    rows = _iota2((size, size), 0)
    cols = _iota2((size, size), 1)
    eye = rows == cols
    causal = cols <= rows
    gates = g_ref[:, 0:2 * ML_HEADS] + gb_ref[...]
    capped = ML_GATE_CAP * jnp.tanh(gates / ML_GATE_CAP)
    valid = (ci * size + (_iota2((nb * size, 1), 0) & (size - 1))) < t_valid
    ipre_all = jnp.where(valid, capped[:, 0:ML_HEADS], NEG)
    logf_all = jnp.where(valid, -_softplus(-capped[:, ML_HEADS:2 * ML_HEADS]), 0.0)
    units = [(sq, h) for sq in range(nb) for h in range(ML_HEADS)]
    rs = [slice(sq * size, (sq + 1) * size) for sq, _ in units]
    ksl = [slice(h * ML_DK, (h + 1) * ML_DK) for _, h in units]
    vsl = [slice(h * ML_DV, (h + 1) * ML_DV) for _, h in units]
    ids = range(len(units))
    q = [q_ref[rs[u], ksl[u]] for u in ids]
    k = [k_ref[rs[u], ksl[u]] * (ML_DK ** -0.5) for u in ids]
    v = [v_ref[rs[u], vsl[u]] for u in ids]
    qk = [_dot_nt(q[u], k[u]) for u in ids]
    c_prev = [c_ref[sq, h] for sq, h in units]
    n_prev = [n_ref[sq, h:h + 1, :] for sq, h in units]
    qc = [_dot(q[u], c_prev[u]) for u in ids]
    s, w_inter, m_t, kw, decay = [], [], [], [], []
    for u, (sq, h) in enumerate(units):
        ig_col = ipre_all[rs[u], h:h + 1]
        lf_col = logf_all[rs[u], h:h + 1]
        ig_row = _col_to_row(ig_col, eye)
        lf_row = _col_to_row(lf_col, eye)
        b_col = jnp.sum(jnp.where(causal, lf_row, 0.0), axis=1, keepdims=True)
        b_row = jnp.sum(jnp.where(rows <= cols, lf_col, 0.0), axis=0, keepdims=True)
        d = jnp.where(causal, b_col - b_row + ig_row, NEG)
        inter = b_col + m_ref[sq, :, h:h + 1]
        m_h = jnp.maximum(inter, jnp.max(d, axis=1, keepdims=True))
        w_h = jnp.exp(inter - m_h)
        s.append(qk[u] * jnp.exp(d - m_h))
        b_last = b_col[size - 1:size, :]
        m_last = m_h[size - 1:size, :]
        w_last = jnp.exp(b_last - b_col + ig_col - m_last)
        kw.append(k[u] * w_last)
        w_inter.append(w_h)
        m_t.append(m_h)
        decay.append(w_h[size - 1:size, :])
        m_ref[sq, :, h:h + 1] = m_last
    sv = [_dot(s[u], v[u]) for u in ids]
    kv = [_dot_tn(kw[u], v[u]) for u in ids]
    for u, (sq, h) in enumerate(units):
        num = w_inter[u] * qc[u] + sv[u]
        den = (w_inter[u] * jnp.sum(q[u] * n_prev[u], axis=1, keepdims=True)
               + jnp.sum(s[u], axis=1, keepdims=True))
        hid = num / jnp.maximum(jnp.abs(den), jnp.exp(-m_t[u]))
        c_ref[sq, h] = decay[u] * c_prev[u] + kv[u]
        n_ref[sq, h:h + 1, :] = decay[u] * n_prev[u] + jnp.sum(kw[u], axis=0, keepdims=True)
        hid = hid * lax.rsqrt(jnp.mean(hid * hid, axis=-1, keepdims=True) + ML_NORM_EPS)
        hid = hid * nw_ref[:, vsl[u]]
        hid = hid * _sigmoid(o_ref[rs[u], vsl[u]])
        h_ref[rs[u], vsl[u]] = hid.astype(h_ref.dtype)


def mlstm_group(proj, dst, row0, bsz, t_pad, t_valid, chunk, c0, n0, m0, b_i, b_f, norm_w, nb=1):
    nc = t_pad // chunk
    rows_blk = nb * chunk
    blk0 = row0 // rows_blk
    assert row0 % rows_blk == 0 and t_pad % chunk == 0 and bsz % nb == 0 and (nb == 1 or nc == 1)
    assert chunk & (chunk - 1) == 0
    rmap = lambda c0_: (lambda b, c: (blk0 + b * nc + c, c0_))
    qk_w, v_w = ML_QK, ML_V
    in_specs = [
        pl.BlockSpec((rows_blk, qk_w), rmap(0)),
        pl.BlockSpec((rows_blk, qk_w), rmap(1)),
        pl.BlockSpec((rows_blk, v_w), rmap(1)),
        pl.BlockSpec((rows_blk, v_w), rmap(2)),
        pl.BlockSpec((rows_blk, LANES), rmap((2 * qk_w + 2 * v_w) // LANES)),
        pl.BlockSpec((nb, ML_HEADS, ML_DK, ML_DV), lambda b, c: (b, 0, 0, 0)),
        pl.BlockSpec((nb, ML_HEADS, ML_DK), lambda b, c: (b, 0, 0)),
        pl.BlockSpec((nb, 1, ML_HEADS), lambda b, c: (b, 0, 0)),
        pl.BlockSpec((1, v_w), lambda b, c: (0, 0)),
        pl.BlockSpec((1, 2 * ML_HEADS), lambda b, c: (0, 0)),
        pl.BlockSpec(memory_space=pl.ANY),
    ]
    out_specs = [
        pl.BlockSpec((rows_blk, v_w), lambda b, c: (blk0 + b * nc + c, 0)),
        pl.BlockSpec((nb, ML_HEADS, ML_DK, ML_DV), lambda b, c: (b, 0, 0, 0)),
        pl.BlockSpec((nb, ML_HEADS, ML_DK), lambda b, c: (b, 0, 0)),
        pl.BlockSpec((nb, 1, ML_HEADS), lambda b, c: (b, 0, 0)),
    ]
    out_shape = [
        jax.ShapeDtypeStruct(dst.shape, dst.dtype),
        jax.ShapeDtypeStruct((bsz, ML_HEADS, ML_DK, ML_DV), F32),
        jax.ShapeDtypeStruct((bsz, ML_HEADS, ML_DK), F32),
        jax.ShapeDtypeStruct((bsz, 1, ML_HEADS), F32),
    ]
    gate_bias = jnp.concatenate([b_i, b_f]).reshape(1, 2 * ML_HEADS)
    h, c, n, m = pl.pallas_call(
        functools.partial(_mlstm_kernel, chunk=chunk, t_valid=t_valid, nb=nb),
        grid=(bsz // nb, nc),
        in_specs=in_specs,
        out_specs=out_specs,
        out_shape=out_shape,
        input_output_aliases={10: 0},
        compiler_params=_cparams(("arbitrary", "arbitrary")),
        name="mlstm",
    )(proj, proj, proj, proj, proj, c0, n0, m0.reshape(bsz, 1, ML_HEADS), norm_w.reshape(1, v_w), gate_bias, dst)
    return h, c, n, m.reshape(bsz, ML_HEADS)


def _head_indicator(width, head):
    idx = np.arange(width) // head
    return jnp.asarray((idx[:, None] == np.arange(LANES)[None, :]).astype(np.float32), dtype=BF16)


def _head_sums(x, ind_ref, ind_t_ref):
    return _dot_exact_rhs(_dot_exact_rhs(x, ind_ref[...]), ind_t_ref[...])


def _rwkv_prep_kernel(cur_ref, tail_ref, first_ref, mu_ref, w0_ref, a0_ref, kk_ref, ka_ref, rk_ref,
                      w2_ref, a2_ref, g2_ref, ind_ref, ind_t_ref,
                      r_ref, wl_ref, k_ref, v_ref, an_ref, bb_ref, bonus_ref, g_ref,
                      *, tt, t_pad, multi):
    cur = cur_ref[...]
    rolled = pltpu.roll(cur, 1, 0)
    row = _iota2((tt, 1), 0)
    if multi:
        prev = jnp.where((row & (t_pad - 1)) == 0, first_ref[...], rolled)
    else:
        starts_seq = (pl.program_id(0) % (t_pad // tt)) == 0
        head = jnp.where(starts_seq, first_ref[0], tail_ref[SUBLANES - 1:SUBLANES, :])
        prev = jnp.where(row == 0, head, rolled)
    mixed = cur + (prev - cur) * mu_ref[...]
    w = RW_WIDTH
    r = mixed[:, 0:w]
    k = mixed[:, w:2 * w]
    v = mixed[:, 2 * w:3 * w]
    c0 = 3 * w
    wl = mixed[:, c0:c0 + RW_DECAY_LORA]
    al = mixed[:, c0 + RW_DECAY_LORA:c0 + RW_DECAY_LORA + RW_A_LORA]
    gl = mixed[:, c0 + RW_DECAY_LORA + RW_A_LORA:]
    w_log = -jnp.exp(-_softplus(-(w0_ref[...] + _dot(jnp.tanh(wl), w2_ref[...]))) - 0.5)
    a = _sigmoid(a0_ref[...] + _dot(al, a2_ref[...]))
    g = _dot(_sigmoid(gl), g2_ref[...])
    kk = k * kk_ref[...]
    kkn = kk / jnp.maximum(jnp.sqrt(_head_sums(kk * kk, ind_ref, ind_t_ref)), 1e-12)
    k2 = k * (1.0 + (a - 1.0) * ka_ref[...])
    r_ref[...] = r
    wl_ref[...] = w_log
    k_ref[...] = k2
    v_ref[...] = v
    an_ref[...] = -kkn
    bb_ref[...] = kkn * a
    bonus_ref[...] = _head_sums(r * k2 * rk_ref[...], ind_ref, ind_t_ref) * v
    g_ref[...] = g


def rwkv_prep(proj, row0, bsz, t_pad, shift0, prm, tt):
    n = bsz * t_pad
    multi = tt > t_pad
    assert row0 % tt == 0 and n % tt == 0 and (tt % t_pad == 0 if multi else t_pad % tt == 0)
    blk0 = row0 // tt
    c = RW_COLS
    if multi:
        first = jnp.zeros((bsz, t_pad, c), F32).at[:, 0].set(shift0).reshape(n, c)
        first_spec = pl.BlockSpec((tt, c), lambda i: (i, 0))
    else:
        first = shift0.reshape(bsz, 1, c)
        first_spec = pl.BlockSpec((1, 1, c), lambda i: (i // (t_pad // tt), 0, 0))
    tail_blk = tt // SUBLANES
    vec = lambda width: pl.BlockSpec((1, width), lambda i: (0, 0))
    full = lambda a, b: pl.BlockSpec((a, b), lambda i: (0, 0))
    w = RW_WIDTH
    out_spec = pl.BlockSpec((tt, w), lambda i: (i, 0))
    return pl.pallas_call(
        functools.partial(_rwkv_prep_kernel, tt=tt, t_pad=t_pad, multi=multi),
        grid=(n // tt,),
        in_specs=[pl.BlockSpec((tt, c), lambda i: (blk0 + i, 0)),
                  pl.BlockSpec((SUBLANES, c), lambda i: (jnp.maximum((blk0 + i) * tail_blk - 1, 0), 0)),
                  first_spec, vec(c), vec(w), vec(w), vec(w), vec(w), vec(w),
                  full(RW_DECAY_LORA, w), full(RW_A_LORA, w), full(RW_GATE_LORA, w),
                  full(w, LANES), full(LANES, w)],
        out_specs=[out_spec] * 8,
        out_shape=[jax.ShapeDtypeStruct((n, w), F32)] * 8,
        compiler_params=_cparams(("arbitrary",)),
        name="rwkv_prep",
    )(proj, proj, first, prm['mu'].reshape(1, c), prm['w0'].reshape(1, w), prm['a0'].reshape(1, w),
      prm['k_k'].reshape(1, w), prm['k_a'].reshape(1, w), prm['r_k'].reshape(1, w),
      prm['w2'], prm['a2'], prm['g2'], _head_indicator(w, RW_DH), _head_indicator(w, RW_DH).T)


def _neumann_inverse_many(n_mats, size):
    eye = (_iota2((size, size), 0) == _iota2((size, size), 1)).astype(F32)
    ts = [eye + n_mat for n_mat in n_mats]
    xs = list(n_mats)
    steps = max(int(np.ceil(np.log2(size))) - 1, 0)
    for _ in range(steps):
        xs = [_dot(x, x) for x in xs]
        ts = [t + _dot(t, x) for t, x in zip(ts, xs)]
    return ts


def _rwkv_chunk_kernel(r_ref, w_ref, k_ref, v_ref, a_ref, b_ref, bonus_ref, g_ref, s0_ref,
                       lnw_ref, lnb_ref, _dst_ref, o_ref, s_ref, acc_ref, *, chunk, t_valid, hb, nb):
    ci = pl.program_id(2)

    @pl.when(ci == 0)
    def _():
        s_ref[...] = s0_ref[...]

    size = chunk
    rows = _iota2((size, size), 0)
    cols = _iota2((size, size), 1)
    strict = cols < rows
    rows2 = _iota2((size, 2 * size), 0)
    cols2 = _iota2((size, 2 * size), 1)
    mask_ak = (cols2 >= size) & (cols2 - size < rows2)
    mask_o = jnp.where(cols2 >= size, cols2 - size, cols2) <= rows2
    slab = nb * size
    rows_s = _iota2((slab, slab), 0)
    cols_s = _iota2((slab, slab), 1)
    shift = size.bit_length() - 1
    same_seq_tril = ((rows_s >> shift) == (cols_s >> shift)) & (cols_s <= rows_s)
    valid = (ci * size + (_iota2((slab, 1), 0) & (size - 1))) < t_valid
    w = jnp.where(valid, w_ref[...], 0.0)
    a = jnp.where(valid, a_ref[...], 0.0)
    b = jnp.where(valid, b_ref[...], 0.0)
    k = jnp.where(valid, k_ref[...], 0.0)
    v = v_ref[...]
    lam = _dot_exact_lhs(same_seq_tril.astype(F32), w)
    e_pos = jnp.exp(lam)
    e_neg = jnp.exp(-lam)
    at = a * jnp.exp(lam - w)
    bt = b * e_neg
    kt = k * e_neg
    rt = r_ref[...] * e_pos
    units = [(sq, h) for sq in range(nb) for h in range(hb)]
    ids = range(len(units))
    rs = [slice(sq * size, (sq + 1) * size) for sq, _ in units]
    sls = [slice(h * RW_DH, (h + 1) * RW_DH) for _, h in units]
    at_h = [at[rs[u], sls[u]] for u in ids]
    rt_h = [rt[rs[u], sls[u]] for u in ids]
    v_h = [v[rs[u], sls[u]] for u in ids]
    bk = [jnp.concatenate([bt[rs[u], sls[u]], kt[rs[u], sls[u]]], axis=0) for u in ids]
    s0 = [s_ref[sq, h] for sq, h in units]
    pa = [_dot_nt(at_h[u], bk[u]) for u in ids]
    pr = [_dot_nt(rt_h[u], bk[u]) for u in ids]
    as0 = [_dot_nt(at_h[u], s0[u]) for u in ids]
    rs0 = [_dot_nt(rt_h[u], s0[u]) for u in ids]
    zv = [jnp.concatenate([jnp.zeros_like(v_h[u]), v_h[u]], axis=0) for u in ids]
    rhs = [as0[u] + _dot(jnp.where(mask_ak, pa[u], 0.0), zv[u]) for u in ids]
    t_inv = _neumann_inverse_many([jnp.where(strict, pa[u][:, :size], 0.0) for u in ids], size)
    uu = [_dot(t_inv[u], rhs[u]) for u in ids]
    uv = [jnp.concatenate([uu[u], v_h[u]], axis=0) for u in ids]
    o = [rs0[u] + _dot(jnp.where(mask_o, pr[u], 0.0), uv[u]) for u in ids]
    ds = [_dot_tn(uv[u], bk[u]) for u in ids]
    for u, (sq, h) in enumerate(units):
        last = (sq + 1) * size - 1
        s_ref[sq, h] = (s0[u] + ds[u]) * e_pos[last:last + 1, sls[u]]
        oc = o[u] - jnp.mean(o[u], axis=-1, keepdims=True)
        acc_ref[rs[u], sls[u]] = oc * lax.rsqrt(jnp.mean(oc * oc, axis=-1, keepdims=True) + RW_GN_EPS)
    out = (acc_ref[...] * lnw_ref[...] + lnb_ref[...] + bonus_ref[...]) * g_ref[...]
    o_ref[...] = out.astype(o_ref.dtype)


def rwkv_chunk(prep, dst, dst_col0, row0, bsz, t_pad, t_valid, chunk, s0, ln_w, ln_b, hb, nb=1):
    nc = t_pad // chunk
    hg = RW_HEADS // hb
    wb = hb * RW_DH
    rows_blk = nb * chunk
    blk0 = row0 // rows_blk
    assert dst_col0 % wb == 0 and row0 % rows_blk == 0 and bsz % nb == 0 and (nb == 1 or nc == 1)
    assert chunk & (chunk - 1) == 0
    tok = pl.BlockSpec((rows_blk, wb), lambda b, g, c: (b * nc + c, g))
    st = pl.BlockSpec((nb, hb, RW_DH, RW_DH), lambda b, g, c: (b, g, 0, 0))
    vec = pl.BlockSpec((1, wb), lambda b, g, c: (0, g))
    return pl.pallas_call(
        functools.partial(_rwkv_chunk_kernel, chunk=chunk, t_valid=t_valid, hb=hb, nb=nb),
        grid=(bsz // nb, hg, nc),
        in_specs=[tok] * 8 + [st, vec, vec, pl.BlockSpec(memory_space=pl.ANY)],
        out_specs=[pl.BlockSpec((rows_blk, wb), lambda b, g, c: (blk0 + b * nc + c, dst_col0 // wb + g)), st],
        out_shape=[jax.ShapeDtypeStruct(dst.shape, dst.dtype),
                   jax.ShapeDtypeStruct((bsz, RW_HEADS, RW_DH, RW_DH), F32)],
        scratch_shapes=[pltpu.VMEM((rows_blk, wb), F32)],
        input_output_aliases={11: 0},
        compiler_params=_cparams(("arbitrary", "arbitrary", "arbitrary")),
        name="rwkv_chunk",
    )(*prep, s0, ln_w.reshape(1, RW_WIDTH), ln_b.reshape(1, RW_WIDTH), dst)


def _conv_silu(cur_ref, hist_ref, cw_ref, size):
    cur = cur_ref[...]
    hist = hist_ref[...]
    cw = cw_ref[...]
    row8 = _iota2((SUBLANES, 1), 0)
    acc = cur * cw[GD_CONV - 1:GD_CONV, :]
    for j in range(1, GD_CONV):
        back = GD_CONV - 1 - j
        rolled = pltpu.roll(cur, j, 0)
        head = jnp.where(row8 < j, pltpu.roll(hist, j, 0), rolled[:SUBLANES])
        prev = head if size == SUBLANES else jnp.concatenate([head, rolled[SUBLANES:]], axis=0)
        acc = acc + prev * cw[back:back + 1, :]
    hist_ref[...] = cur[size - SUBLANES:size]
    return acc * _sigmoid(acc)


def _l2_normalize(x, scale):
    return x * lax.rsqrt(jnp.sum(x * x, axis=-1, keepdims=True) + GD_EPS) * scale


def _gdn_chunk_kernel(q_ref, k_ref, v_ref, z_ref, gl_ref, bq_ref, bk_ref, bv_ref, cq_ref, ck_ref, cv_ref,
                      al_ref, dt_ref, nw_ref, s0_ref, _dst_ref,
                      o_ref, s_ref, nq_ref, nk_ref, nv_ref, hq_ref, hk_ref, hv_ref,
                      *, chunk, t_valid, hpg, tail_chunk, tail_row0):
    ci = pl.program_id(2)
    hg = pl.program_id(1)

    @pl.when(ci == 0)
    def _():
        s_ref[...] = s0_ref[...]
        hq_ref[...] = bq_ref[0]
        hk_ref[...] = bk_ref[0]
        hv_ref[...] = bv_ref[0]

    @pl.when(ci == tail_chunk)
    def _():
        tail = slice(tail_row0, tail_row0 + SUBLANES)
        nq_ref[0] = q_ref[tail, :]
        nk_ref[0] = k_ref[tail, :]
        nv_ref[0] = v_ref[tail, :]

    size = chunk
    q_act = _conv_silu(q_ref, hq_ref, cq_ref, size)
    k_act = _conv_silu(k_ref, hk_ref, ck_ref, size)
    v_act = _conv_silu(v_ref, hv_ref, cv_ref, size)
    rows = _iota2((size, size), 0)
    cols = _iota2((size, size), 1)
    eye = rows == cols
    incl = cols <= rows
    strict = cols < rows
    valid = (ci * size + _iota2((size, 1), 0)) < t_valid
    logits = gl_ref[...]
    beta_all = _sigmoid(logits)
    g_all = -jnp.exp(al_ref[...]) * _softplus(logits + dt_ref[...])
    lane = _iota2((size, LANES), 1)
    rep = GD_V_HEADS // GD_K_HEADS
    kheads = range(hpg // rep)
    heads = range(hpg)
    q = [_l2_normalize(q_act[:, kh * GD_DK:(kh + 1) * GD_DK], GD_DK ** -0.5) for kh in kheads]
    k = [_l2_normalize(k_act[:, kh * GD_DK:(kh + 1) * GD_DK], 1.0) for kh in kheads]
    qk = [_dot_nt(q[kh], k[kh]) for kh in kheads]
    kk = [_dot_nt(k[kh], k[kh]) for kh in kheads]
    vsl = [slice(hl * GD_DV, (hl + 1) * GD_DV) for hl in heads]
    s = [s_ref[0, hl] for hl in heads]
    beta, gc, decay = [], [], []
    for hl in heads:
        head = hg * hpg + hl
        beta_h = jnp.sum(jnp.where(lane == head, beta_all, 0.0), axis=1, keepdims=True)
        g = jnp.sum(jnp.where(lane == head + GD_V_HEADS, g_all, 0.0), axis=1, keepdims=True)
        beta_h = jnp.where(valid, beta_h, 0.0)
        g = jnp.where(valid, g, 0.0)
        g_row = _col_to_row(g, eye)
        gc_h = jnp.sum(jnp.where(incl, g_row, 0.0), axis=1, keepdims=True)
        gc_row = jnp.sum(jnp.where(rows <= cols, g, 0.0), axis=0, keepdims=True)
        beta.append(beta_h)
        gc.append(gc_h)
        decay.append(jnp.where(incl, jnp.exp(jnp.where(incl, gc_h - gc_row, 0.0)), 0.0))
    qs = [_dot(q[hl // rep] * jnp.exp(gc[hl]), s[hl]) for hl in heads]
    t_inv = _neumann_inverse_many(
        [-jnp.where(strict, kk[hl // rep] * beta[hl] * decay[hl], 0.0) for hl in heads], size)
    uw = [_dot(t_inv[hl], jnp.concatenate([v_act[:, vsl[hl]] * beta[hl],
                                           k[hl // rep] * (beta[hl] * jnp.exp(gc[hl]))], axis=1))
          for hl in heads]
    ws = [_dot(uw[hl][:, GD_DV:], s[hl]) for hl in heads]
    v_new = [uw[hl][:, :GD_DV] - ws[hl] for hl in heads]
    av = [_dot(qk[hl // rep] * decay[hl], v_new[hl]) for hl in heads]
    g_last = [gc[hl][size - 1:size, :] for hl in heads]
    kv = [_dot_tn(k[hl // rep] * jnp.exp(g_last[hl] - gc[hl]), v_new[hl]) for hl in heads]
    for hl in heads:
        s_ref[0, hl] = s[hl] * jnp.exp(g_last[hl]) + kv[hl]
        o = qs[hl] + av[hl]
        o = o * lax.rsqrt(jnp.mean(o * o, axis=-1, keepdims=True) + GD_EPS) * nw_ref[...]
        z = z_ref[:, vsl[hl]]
        o_ref[:, vsl[hl]] = (o * (z * _sigmoid(z))).astype(o_ref.dtype)


def gdn_chunk(proj, gates, conv_buf, conv_w, dst, row0, bsz, t_pad, t_valid, chunk, s0, a_log, dt_bias,
              norm_w, hpg):
    nc = t_pad // chunk
    hgs = GD_V_HEADS // hpg
    rep = GD_V_HEADS // GD_K_HEADS
    kw = (hpg // rep) * GD_DK
    vw = hpg * GD_DV
    blk0 = row0 // chunk
    assert row0 % chunk == 0 and chunk % SUBLANES == 0
    pad_row = lambda x: jnp.pad(x, (GD_V_HEADS, LANES - 2 * GD_V_HEADS)).reshape(1, LANES)
    hist = jnp.pad(conv_buf, ((0, 0), (SUBLANES - (GD_CONV - 1), 0), (0, 0)))
    st = pl.BlockSpec((1, hpg, GD_DK, GD_DV), lambda b, g, c: (b, g, 0, 0))
    vec = pl.BlockSpec((1, LANES), lambda b, g, c: (0, 0))
    q_col, k_col, v_col = (lambda g: g), (lambda g: GD_QK // kw + g), (lambda g: 2 * GD_QK // vw + g)
    tok = lambda width, col: pl.BlockSpec((chunk, width), lambda b, g, c: (blk0 + b * nc + c, col(g)))
    buf = lambda width, col: pl.BlockSpec((1, SUBLANES, width), lambda b, g, c: (b, 0, col(g)))
    taps = lambda width, col: pl.BlockSpec((GD_CONV, width), lambda b, g, c: (0, col(g)))
    last = (t_valid - 1) % chunk
    tail_row0 = (last // SUBLANES) * SUBLANES
    first = last - (GD_CONV - 2) - tail_row0
    assert first >= 0
    new_buf = lambda width: jax.ShapeDtypeStruct((bsz, SUBLANES, width), F32)
    out, state, nq, nk, nv = pl.pallas_call(
        functools.partial(_gdn_chunk_kernel, chunk=chunk, t_valid=t_valid, hpg=hpg,
                          tail_chunk=(t_valid - 1) // chunk, tail_row0=tail_row0),
        grid=(bsz, hgs, nc),
        in_specs=[tok(kw, q_col), tok(kw, k_col), tok(vw, v_col),
                  tok(vw, lambda g: GD_CONV_DIM // vw + g),
                  pl.BlockSpec((chunk, LANES), lambda b, g, c: (blk0 + b * nc + c, 0)),
                  buf(kw, q_col), buf(kw, k_col), buf(vw, v_col),
                  taps(kw, q_col), taps(kw, k_col), taps(vw, v_col),
                  vec, vec, vec, st, pl.BlockSpec(memory_space=pl.ANY)],
        out_specs=[pl.BlockSpec((chunk, vw), lambda b, g, c: (blk0 + b * nc + c, g)), st,
                   pl.BlockSpec((1, SUBLANES, kw), lambda b, g, c: (b, 0, g)),
                   pl.BlockSpec((1, SUBLANES, kw), lambda b, g, c: (b, 0, g)),
                   pl.BlockSpec((1, SUBLANES, vw), lambda b, g, c: (b, 0, g))],
        out_shape=[jax.ShapeDtypeStruct(dst.shape, dst.dtype),
                   jax.ShapeDtypeStruct((bsz, GD_V_HEADS, GD_DK, GD_DV), F32),
                   new_buf(GD_QK), new_buf(GD_QK), new_buf(GD_V_WIDTH)],
        scratch_shapes=[pltpu.VMEM((SUBLANES, kw), F32), pltpu.VMEM((SUBLANES, kw), F32),
                        pltpu.VMEM((SUBLANES, vw), F32)],
        input_output_aliases={15: 0},
        compiler_params=_cparams(("arbitrary", "arbitrary", "arbitrary")),
        name="gdn_chunk",
    )(proj, proj, proj, proj, gates, hist, hist, hist, conv_w, conv_w, conv_w,
      pad_row(a_log), pad_row(dt_bias), norm_w.reshape(1, GD_DV), s0, dst)
    conv_state = jnp.concatenate([nq, nk, nv], axis=-1)[:, first:first + GD_CONV - 1]
    return out, state, conv_state


MOE_TM = 256


def _route(logits, wt_ref, id_ref):
    lane = _iota2(logits.shape, 1).astype(F32)
    first_of = lambda hit: jnp.min(jnp.where(hit, lane, float(LANES)), axis=1, keepdims=True)
    gl = jnp.where(lane < MOE_GROUPS, logits, NEG)
    gmax = jnp.max(gl, axis=1, keepdims=True)
    g_val = 1.0 / jnp.sum(jnp.exp(gl - gmax), axis=1, keepdims=True)
    lo = MOE_GROUPS + first_of(gl == gmax) * MOE_PER_GROUP
    vals = jnp.where((lane >= lo) & (lane < lo + MOE_PER_GROUP), logits, NEG)
    top1 = jnp.max(vals, axis=1, keepdims=True)
    i1 = first_of(vals == top1)
    vals2 = jnp.where(lane == i1, NEG, vals)
    top2 = jnp.max(vals2, axis=1, keepdims=True)
    i2 = first_of(vals2 == top2)
    e2 = jnp.exp(top2 - top1)
    w1 = (1.0 / (1.0 + e2)) * g_val
    w2 = (e2 / (1.0 + e2)) * g_val
    wt_ref[...] = jnp.where(lane == 0, w1, jnp.where(lane == 1, w2, 0.0))
    ids = jnp.where(lane == 0, i1 - MOE_GROUPS, jnp.where(lane == 1, i2 - MOE_GROUPS, 0.0))
    id_ref[...] = ids.astype(jnp.int32)


def _ln_router_kernel(x_ref, y_ref, g_ref, b_ref, wr_ref, o_ref, wt_ref, id_ref):
    z = DEEPNORM_ALPHA * x_ref[...] + y_ref[...]
    zc = z - jnp.mean(z, axis=-1, keepdims=True)
    var = jnp.mean(zc * zc, axis=-1, keepdims=True)
    out = zc * lax.rsqrt(var + LN_EPS) * g_ref[...] + b_ref[...]
    o_ref[...] = out
    _route(jnp.dot(out.astype(BF16), wr_ref[...].astype(BF16), preferred_element_type=F32), wt_ref, id_ref)


def ln_residual_router(x, y, g, b, w_router, tm=256):
    m, d = x.shape
    row = pl.BlockSpec((tm, d), lambda i: (i, 0))
    vec = pl.BlockSpec((1, d), lambda i: (0, 0))
    pick = pl.BlockSpec((tm, LANES), lambda i: (i, 0))
    return pl.pallas_call(
        _ln_router_kernel,
        grid=(m // tm,),
        in_specs=[row, row, vec, vec, pl.BlockSpec((d, LANES), lambda i: (0, 0))],
        out_specs=[row, pick, pick],
        out_shape=[jax.ShapeDtypeStruct((m, d), F32),
                   jax.ShapeDtypeStruct((m, LANES), F32), jax.ShapeDtypeStruct((m, LANES), jnp.int32)],
        compiler_params=_cparams(("arbitrary",)),
        name="ln_residual_router",
    )(x, y, g.reshape(1, d), b.reshape(1, d), w_router)


def _moe_ffn_kernel(te_ref, cnt_ref, first_ref, next_ref, wslot_ref,
                    idx_ref, idx_next_ref, x_hbm, wg_hbm, wu_hbm, wd_hbm, o_ref,
                    xbuf, wgf, wuf, wdf, wgb, wub, wdb, gsem, wsem, *, layer):
    i = pl.program_id(0)
    n_tiles = pl.num_programs(0)
    slot = i % 2

    def gather(rows_ref, into, count):
        def start_row(r, carry):
            pltpu.make_async_copy(x_hbm.at[pl.ds(rows_ref[0, 0, r], 1)],
                                  xbuf.at[into, pl.ds(r, 1)], gsem.at[into]).start()
            return carry
        lax.fori_loop(0, count, start_row, 0)

    def gather_wait(into, count):
        def wait_row(r, carry):
            pltpu.make_async_copy(x_hbm.at[pl.ds(0, 1)], xbuf.at[into, pl.ds(0, 1)], gsem.at[into]).wait()
            return carry
        lax.fori_loop(0, count, wait_row, 0)

    def weight_copies(expert, ws):
        return [pltpu.make_async_copy(src.at[layer, expert], dst.at[ws], wsem.at[ws])
                for src, dst in ((wg_hbm, wgf), (wu_hbm, wuf), (wd_hbm, wdf))]

    @pl.when(i == 0)
    def _():
        xbuf[...] = jnp.zeros_like(xbuf)
        gather(idx_ref, 0, cnt_ref[0])
        for copy in weight_copies(te_ref[0], 0):
            copy.start()

    nxt = jnp.minimum(i + 1, n_tiles - 1)

    @pl.when(i + 1 < n_tiles)
    def _():
        gather(idx_next_ref, 1 - slot, cnt_ref[nxt])

    @pl.when(cnt_ref[i] > 0)
    def _():
        gather_wait(slot, cnt_ref[i])

        @pl.when(first_ref[i] == 1)
        def _():
            ws = wslot_ref[i]
            for copy in weight_copies(te_ref[i], ws):
                copy.wait()
            wgb[...] = wgf[ws].astype(BF16)
            wub[...] = wuf[ws].astype(BF16)
            wdb[...] = wdf[ws].astype(BF16)

            @pl.when(next_ref[i] >= 0)
            def _():
                for copy in weight_copies(next_ref[i], 1 - ws):
                    copy.start()

        x = xbuf[slot].astype(BF16)
        gate = jnp.dot(x, wgb[...], preferred_element_type=F32)
        up = jnp.dot(x, wub[...], preferred_element_type=F32)
        hid = gate * _sigmoid(gate) * up
        o_ref[...] = jnp.dot(hid.astype(BF16), wdb[...], preferred_element_type=F32)

    @pl.when(cnt_ref[i] == 0)
    def _():
        o_ref[...] = jnp.zeros_like(o_ref)


def moe_ffn(x, row_token, tables, layer, w_gate, w_up, w_down):
    n_tiles = row_token.shape[0]
    d = x.shape[1]
    ff = w_gate.shape[3]
    any_space = pl.BlockSpec(memory_space=pl.ANY)
    grid_spec = pltpu.PrefetchScalarGridSpec(
        num_scalar_prefetch=5,
        grid=(n_tiles,),
        in_specs=[pl.BlockSpec((1, 1, MOE_TM), lambda i, *_: (i, 0, 0), memory_space=pltpu.SMEM),
                  pl.BlockSpec((1, 1, MOE_TM), lambda i, *_: (jnp.minimum(i + 1, n_tiles - 1), 0, 0),
                               memory_space=pltpu.SMEM),
                  any_space, any_space, any_space, any_space],
        out_specs=pl.BlockSpec((MOE_TM, d), lambda i, *_: (i, 0)),
        scratch_shapes=[pltpu.VMEM((2, MOE_TM, d), F32),
                        pltpu.VMEM((2, d, ff), F32), pltpu.VMEM((2, d, ff), F32), pltpu.VMEM((2, ff, d), F32),
                        pltpu.VMEM((d, ff), BF16), pltpu.VMEM((d, ff), BF16), pltpu.VMEM((ff, d), BF16),
                        pltpu.SemaphoreType.DMA((2,)), pltpu.SemaphoreType.DMA((2,))],
    )
    return pl.pallas_call(
        functools.partial(_moe_ffn_kernel, layer=layer),
        grid_spec=grid_spec,
        out_shape=jax.ShapeDtypeStruct((n_tiles * MOE_TM, d), F32),
        compiler_params=_cparams(("arbitrary",)),
        name="moe_ffn",
    )(*tables, row_token, row_token, x, w_gate, w_up, w_down)


def _ln_moe_ple_kernel(x_ref, ya_ref, yb_ref, wt_ref, g_ref, b_ref, p_ref, wg_ref, wp_ref, o_ref, ob_ref):
    wt = wt_ref[...]
    z = DEEPNORM_ALPHA * x_ref[...] + (wt[:, 0:1] * ya_ref[...] + wt[:, 1:2] * yb_ref[...])
    zc = z - jnp.mean(z, axis=-1, keepdims=True)
    var = jnp.mean(zc * zc, axis=-1, keepdims=True)
    hid = zc * lax.rsqrt(var + LN_EPS) * g_ref[...] + b_ref[...]
    gate = _sigmoid(jnp.dot(hid.astype(BF16), wg_ref[...], preferred_element_type=F32))
    emb = jnp.dot(p_ref[...].astype(BF16), wp_ref[...], preferred_element_type=F32)
    out = hid + gate * emb
    o_ref[...] = out
    ob_ref[...] = out.astype(BF16)


def ln_moe_ple(x, ya, yb, wt, g, b, p, wg, wp, tm=512):
    m, d = x.shape
    pd = p.shape[1]
    tm = _row_tile(m, tm)
    row = pl.BlockSpec((tm, d), lambda i: (i, 0))
    vec = pl.BlockSpec((1, d), lambda i: (0, 0))
    once = pl.Buffered(1)
    return pl.pallas_call(
        _ln_moe_ple_kernel,
        grid=(m // tm,),
        in_specs=[row, row, row, pl.BlockSpec((tm, LANES), lambda i: (i, 0)), vec, vec,
                  pl.BlockSpec((tm, pd), lambda i: (i, 0)),
                  pl.BlockSpec((d, d), lambda i: (0, 0), pipeline_mode=once),
                  pl.BlockSpec((pd, d), lambda i: (0, 0), pipeline_mode=once)],
        out_specs=[row, row],
        out_shape=[jax.ShapeDtypeStruct((m, d), F32), jax.ShapeDtypeStruct((m, d), BF16)],
        compiler_params=_cparams(("arbitrary",)),
        name="ln_moe_ple",
    )(x, ya, yb, wt, g.reshape(1, d), b.reshape(1, d), p, wg, wp)


def _table_lookup(table, idx):
    hit = idx[:, None] == jnp.arange(table.shape[0], dtype=idx.dtype)[None, :]
    return jnp.sum(jnp.where(hit, table[None, :], 0), axis=1)


def router_weights(w_group, w_expert):
    return jnp.pad(jnp.concatenate([w_group, w_expert], axis=1),
                   ((0, 0), (0, LANES - MOE_GROUPS - MOE_EXPERTS)))


def hier_moe(x, ids, layer, w_gate, w_up, w_down):
    n = x.shape[0]

    n_pairs = n * MOE_TOPK
    n_tiles = -(-n_pairs // MOE_TM) + MOE_EXPERTS
    n_rows = n_tiles * MOE_TM
    i32 = jnp.int32
    eid = ids[:, :MOE_TOPK].reshape(n_pairs)
    order = jnp.argsort(eid, stable=True).astype(i32)
    rank = jnp.argsort(order).astype(i32)
    experts = jnp.arange(MOE_EXPERTS, dtype=i32)
    counts = jnp.sum((eid[:, None] == experts[None, :]).astype(i32), axis=0)
    starts = jnp.cumsum(counts) - counts
    tiles_per = (counts + MOE_TM - 1) // MOE_TM
    tile_starts = jnp.cumsum(tiles_per) - tiles_per
    dest = _table_lookup(tile_starts * MOE_TM - starts, eid) + rank
    tile_idx = jnp.arange(n_tiles, dtype=i32)
    tile_expert = jnp.sum((tile_idx[:, None] >= (tile_starts + tiles_per)[None, :]).astype(i32), axis=1)
    tile_valid = (tile_expert < MOE_EXPERTS).astype(i32)
    used = tiles_per > 0
    last_used = jnp.max(jnp.where(used, experts, 0))
    tile_expert = jnp.where(tile_valid == 1, tile_expert, last_used)
    tile_in_run = tile_idx - _table_lookup(tile_starts, tile_expert)
    off = tile_in_run[:, None] * MOE_TM + jnp.arange(MOE_TM, dtype=i32)[None, :]
    cnt = _table_lookup(counts, tile_expert)
    src = jnp.clip(_table_lookup(starts, tile_expert)[:, None] + off, 0, n_pairs - 1)
    row_ok = (off < cnt[:, None]) & (tile_valid[:, None] == 1)
    picked = jnp.take(order, src.reshape(n_rows), mode='clip').reshape(n_tiles, MOE_TM)
    row_token = jnp.where(row_ok, picked // MOE_TOPK, 0).reshape(n_tiles, 1, MOE_TM)
    tile_rows = jnp.clip(cnt - tile_in_run * MOE_TM, 0, MOE_TM) * tile_valid
    tile_first = tile_valid * (tile_in_run == 0).astype(i32)
    later_used = (experts[None, :] > experts[:, None]) & used[None, :]
    next_used = jnp.min(jnp.where(later_used, experts[None, :], MOE_EXPERTS), axis=1)
    next_used = jnp.where(next_used < MOE_EXPERTS, next_used, -1)
    run_index = jnp.cumsum(used.astype(i32)) - 1
    tables = (tile_expert, tile_rows, tile_first, _table_lookup(next_used, tile_expert),
              _table_lookup(run_index, tile_expert) % 2)

    y_sorted = moe_ffn(x, row_token, tables, layer, w_gate, w_up, w_down)
    dest2 = dest.reshape(n, MOE_TOPK)
    return (jnp.take(y_sorted, dest2[:, 0], axis=0, mode='clip'),
            jnp.take(y_sorted, dest2[:, 1], axis=0, mode='clip'))


SAMPLE_T_PAD = 8


def kernel(x_prompt, x_sample, state_mlstm_c, state_mlstm_n, state_mlstm_m, state_rwkv_s,
           state_rwkv_shift, state_gdn_s, state_gdn_conv, p_prompt, p_sample,
           w_in_ab, ml_b_i, ml_b_f, ml_norm_w, rw_mu, rw_w0, rw_w2, rw_a0, rw_a2, rw_g2,
           rw_k_k, rw_k_a, rw_r_k, rw_ln_w, rw_ln_b, w_out_ab,
           gd_w_in, gd_conv_w, gd_a_log, gd_dt_bias, gd_norm_w, gd_w_out,
           ln_mix_g, ln_mix_b, moe_w_group, moe_w_expert, moe_w_gate, moe_w_up, moe_w_down,
           ln_ffn_g, ln_ffn_b, ple_w_gate, ple_w_proj):
    pb, pt, d = x_prompt.shape
    sb, st, _ = x_sample.shape
    n_p = pb * pt
    n_s = sb * SAMPLE_T_PAD
    n = n_p + n_s
    pad_t = lambda a: jnp.pad(a, ((0, 0), (0, SAMPLE_T_PAD - st), (0, 0)))
    merge = lambda a_p, a_s: jnp.concatenate(
        [a_p.reshape(n_p, a_p.shape[-1]), pad_t(a_s).reshape(n_s, a_s.shape[-1])], axis=0)
    x = merge(x_prompt, x_sample)
    xb = x.astype(BF16)
    segs = (dict(row0=0, bsz=pb, t_pad=pt, t_valid=pt, ml_nb=1, rw_hb=16, rw_nb=1, gd_hpg=16),
            dict(row0=n_p, bsz=sb, t_pad=SAMPLE_T_PAD, t_valid=st, ml_nb=2, rw_hb=16, rw_nb=2, gd_hpg=32))
    zeros = lambda *shape: jnp.zeros(shape, F32)

    def last_rows(a, sg, count):
        assert sg['t_valid'] >= count
        idx = (sg['row0'] + jnp.arange(sg['bsz'])[:, None] * sg['t_pad']
               + (sg['t_valid'] - count + jnp.arange(count))[None, :])
        return jnp.take(a, idx.reshape(-1), axis=0, mode='clip').reshape(sg['bsz'], count, a.shape[-1])

    new_states = [[[] for _ in range(7)] for _ in segs]
    for layer in range(DEPTH):
        li = layer // 2
        if layer % 2 == 0:
            w_in_t = jnp.transpose(w_in_ab[li])
            proj_ml = matmul_wt(xb, w_in_t, 0, 3584, 896)
            proj_rw = matmul_wt(xb, w_in_t, ML_COLS, RW_COLS, RW_COLS // 2)
            rw_prm = dict(mu=rw_mu[li], w0=rw_w0[li], w2=rw_w2[li], a0=rw_a0[li], a2=rw_a2[li], g2=rw_g2[li],
                          k_k=rw_k_k[li], k_a=rw_k_a[li], r_k=rw_r_k[li])
            heads_out = jnp.zeros((n, ML_V + RW_WIDTH), BF16)
            for si, sg in enumerate(segs):
                bsz, t_pad, t_valid, row0 = sg['bsz'], sg['t_pad'], sg['t_valid'], sg['row0']
                if si == 0:
                    c0, n0, m0 = zeros(bsz, ML_HEADS, ML_DK, ML_DV), zeros(bsz, ML_HEADS, ML_DK), zeros(bsz, ML_HEADS)
                    s0, sh0 = zeros(bsz, RW_HEADS, RW_DH, RW_DH), zeros(bsz, RW_COLS)
                else:
                    c0, n0, m0 = state_mlstm_c[li], state_mlstm_n[li], state_mlstm_m[li]
                    s0, sh0 = state_rwkv_s[li], state_rwkv_shift[li]
                heads_out, c, nn, m = mlstm_group(proj_ml, heads_out, row0, bsz, t_pad, t_valid,
                                                  min(ML_CHUNK, t_pad), c0, n0, m0,
                                                  ml_b_i[li], ml_b_f[li], ml_norm_w[li], sg['ml_nb'])
                prep = rwkv_prep(proj_rw, row0, bsz, t_pad, sh0, rw_prm, 256)
                heads_out, rs = rwkv_chunk(prep, heads_out, ML_V, row0, bsz, t_pad, t_valid,
                                           min(RW_CHUNK, t_pad), s0, rw_ln_w[li], rw_ln_b[li],
                                           sg['rw_hb'], sg['rw_nb'])
                new_shift = last_rows(proj_rw, sg, 1)[:, 0]
                for slot, val in zip(range(5), (c, nn, m, rs, new_shift)):
                    new_states[si][slot].append(val)
            mix = matmul(heads_out, w_out_ab[li], d, 1024)
        else:
            w_in_t = jnp.transpose(gd_w_in[li])
            n_qkvz = GD_CONV_DIM + GD_V_WIDTH
            proj = matmul_wt(xb, w_in_t, 0, n_qkvz, 1024)
            gates = matmul_wt(xb, w_in_t, n_qkvz, LANES, LANES)
            gd_out = jnp.zeros((n, GD_V_WIDTH), BF16)
            for si, sg in enumerate(segs):
                bsz, t_pad, t_valid, row0 = sg['bsz'], sg['t_pad'], sg['t_valid'], sg['row0']
                if si == 0:
                    s0, buf0 = zeros(bsz, GD_V_HEADS, GD_DK, GD_DV), zeros(bsz, GD_CONV - 1, GD_CONV_DIM)
                else:
                    s0, buf0 = state_gdn_s[li], state_gdn_conv[li]
                gd_out, gs, gcb = gdn_chunk(proj, gates, buf0, gd_conv_w[li], gd_out, row0, bsz, t_pad, t_valid,
                                            min(GD_CHUNK, t_pad), s0, gd_a_log[li], gd_dt_bias[li],
                                            gd_norm_w[li], sg['gd_hpg'])
                new_states[si][5].append(gs)
                new_states[si][6].append(gcb)
            mix = matmul(gd_out, gd_w_out[li], d, 512)
        x, wt, ids = ln_residual_router(x, mix, ln_mix_g[layer], ln_mix_b[layer],
                                        router_weights(moe_w_group[layer], moe_w_expert[layer]))
        ya, yb = hier_moe(x, ids, layer, moe_w_gate, moe_w_up, moe_w_down)
        x, xb = ln_moe_ple(x, ya, yb, wt, ln_ffn_g[layer], ln_ffn_b[layer],
                           merge(p_prompt[layer], p_sample[layer]),
                           ple_w_gate[layer].astype(BF16), ple_w_proj[layer].astype(BF16))
    y_prompt = x[:n_p].reshape(pb, pt, d)
    y_sample = x[n_p:].reshape(sb, SAMPLE_T_PAD, d)[:, :st]
    stack = lambda vals: jnp.stack(vals)
    return (y_prompt, y_sample) + tuple(stack(v) for v in new_states[0]) + tuple(stack(v) for v in new_states[1])
```

```python
import functools

import jax
import jax.numpy as jnp
import numpy as np
from jax import lax
from jax.experimental import pallas as pl
from jax.experimental.pallas import tpu as pltpu

F32 = jnp.float32
BF16 = jnp.bfloat16

D_MODEL = 2048
DEPTH = 2
DEEPNORM_ALPHA = (2 * DEPTH) ** 0.25
LN_EPS = 1e-5
MIX_HALF = D_MODEL // 2
ML_HEADS = 4
ML_DV = MIX_HALF // ML_HEADS
ML_DK = ML_DV // 2
ML_CHUNK = 64
ML_GATE_CAP = 15.0
ML_NORM_EPS = 1e-6
ML_QK = ML_HEADS * ML_DK
ML_V = ML_HEADS * ML_DV
ML_COLS = 2 * ML_QK + 2 * ML_V + 2 * ML_HEADS
RW_DH = 64
RW_HEADS = MIX_HALF // RW_DH
RW_WIDTH = RW_HEADS * RW_DH
RW_DECAY_LORA = 64
RW_A_LORA = 64
RW_GATE_LORA = 128
RW_GN_EPS = 64e-5
RW_COLS = 3 * RW_WIDTH + RW_DECAY_LORA + RW_A_LORA + RW_GATE_LORA
RW_CHUNK = 64
GD_DK = 128
GD_DV = 128
GD_K_HEADS = D_MODEL // GD_DK
GD_V_HEADS = 2 * GD_K_HEADS
GD_CONV = 4
GD_CHUNK = 64
GD_EPS = 1e-6
GD_QK = GD_K_HEADS * GD_DK
GD_V_WIDTH = GD_V_HEADS * GD_DV
GD_CONV_DIM = 2 * GD_QK + GD_V_WIDTH
MOE_GROUPS = 4
MOE_PER_GROUP = 8
MOE_EXPERTS = MOE_GROUPS * MOE_PER_GROUP
MOE_TOPK = 2
MOE_FF = D_MODEL // 4
PLE_DIM = 256

LANES = 128
SUBLANES = 8
NEG = -1e30
VMEM_LIMIT = 56 * 1024 * 1024


def _cparams(sem):
    return pltpu.CompilerParams(dimension_semantics=sem, vmem_limit_bytes=VMEM_LIMIT)


def _dot(a, b):
    return jnp.dot(a.astype(BF16), b.astype(BF16), preferred_element_type=F32)


def _dot_nt(a, b):
    return lax.dot_general(a.astype(BF16), b.astype(BF16), (((1,), (1,)), ((), ())),
                           preferred_element_type=F32)


def _dot_tn(a, b):
    return lax.dot_general(a.astype(BF16), b.astype(BF16), (((0,), (0,)), ((), ())),
                           preferred_element_type=F32)


def _split3(x):
    hi = x.astype(BF16)
    r1 = x - hi.astype(F32)
    mid = r1.astype(BF16)
    lo = (r1 - mid.astype(F32)).astype(BF16)
    return hi, mid, lo


def _dot_exact_rhs(a, b01):
    hi, mid, lo = _split3(a)
    b = b01.astype(BF16)
    return (jnp.dot(hi, b, preferred_element_type=F32) + jnp.dot(mid, b, preferred_element_type=F32)
            + jnp.dot(lo, b, preferred_element_type=F32))


def _dot_exact_lhs(a01, b):
    hi, mid, lo = _split3(b)
    a = a01.astype(BF16)
    return (jnp.dot(a, hi, preferred_element_type=F32) + jnp.dot(a, mid, preferred_element_type=F32)
            + jnp.dot(a, lo, preferred_element_type=F32))


def _iota2(shape, dim):
    return lax.broadcasted_iota(jnp.int32, shape, dim)


def _col_to_row(col, eye):
    return jnp.sum(jnp.where(eye, col, 0.0), axis=0, keepdims=True)


def _sigmoid(x):
    return 1.0 / (1.0 + jnp.exp(-x))


def _softplus(x):
    return jnp.maximum(x, 0.0) + jnp.log1p(jnp.exp(-jnp.abs(x)))


def _neumann_inverse(n_mat, size):
    eye = (_iota2((size, size), 0) == _iota2((size, size), 1)).astype(F32)
    t = eye + n_mat
    x = n_mat
    steps = max(int(np.ceil(np.log2(size))) - 1, 0)
    for _ in range(steps):
        x = _dot(x, x)
        t = t + _dot(t, x)
    return t


def _mm_kernel(x_ref, w_ref, o_ref, wb_ref, *, n_valid):
    @pl.when(pl.program_id(1) == 0)
    def _():
        wb_ref[...] = w_ref[...].astype(BF16)

    acc = jnp.dot(x_ref[...].astype(BF16), wb_ref[...], preferred_element_type=F32)
    if n_valid is not None:
        acc = jnp.where(_iota2(acc.shape, 1) < n_valid, acc, 0.0)
    o_ref[...] = acc.astype(o_ref.dtype)


def _row_tile(m, want):
    while m % want:
        want //= 2
    return want


def matmul(x, w, n_out, tn, tm=1024, out_dtype=F32, col0=0):
    m, k = x.shape
    tm = _row_tile(m, tm)
    assert w.shape[0] == k and n_out % tn == 0 and m % tm == 0 and tn % LANES == 0 and col0 % tn == 0
    n_valid = None
    if col0 + n_out > w.shape[1]:
        assert n_out == tn
        n_valid = w.shape[1] - col0
    blk0 = col0 // tn
    return pl.pallas_call(
        functools.partial(_mm_kernel, n_valid=n_valid),
        grid=(n_out // tn, m // tm),
        in_specs=[pl.BlockSpec((tm, k), lambda j, i: (i, 0)),
                  pl.BlockSpec((k, tn), lambda j, i: (0, blk0 + j))],
        out_specs=pl.BlockSpec((tm, tn), lambda j, i: (i, j)),
        out_shape=jax.ShapeDtypeStruct((m, n_out), out_dtype),
        scratch_shapes=[pltpu.VMEM((k, tn), BF16)],
        compiler_params=_cparams(("arbitrary", "arbitrary")),
        name="matmul",
    )(x, w)


def _mm_wt_kernel(x_ref, wt_hbm, o_ref, wf_ref, wb_ref, sem, *, row0, tn, n_rows):
    j = pl.program_id(0)

    @pl.when(pl.program_id(1) == 0)
    def _():
        if n_rows < tn:
            wf_ref[...] = jnp.zeros_like(wf_ref)
        start = pl.multiple_of(row0 + j * tn, SUBLANES)
        copy = pltpu.make_async_copy(wt_hbm.at[pl.ds(start, n_rows)], wf_ref.at[pl.ds(0, n_rows)], sem)
        copy.start()
        copy.wait()
        k = wf_ref.shape[1]
        step = 2 * LANES
        for c in range(k // step):
            wb_ref[c * step:(c + 1) * step, :] = wf_ref[:, c * step:(c + 1) * step].T.astype(BF16)

    o_ref[...] = jnp.dot(x_ref[...].astype(BF16), wb_ref[...], preferred_element_type=F32)


def matmul_wt(x, wt, row0, n_out, tn, tm=1024):
    m, k = x.shape
    tm = _row_tile(m, tm)
    assert wt.shape[1] == k and n_out % tn == 0 and tn % LANES == 0 and row0 % SUBLANES == 0
    n_rows = tn
    if row0 + n_out > wt.shape[0]:
        assert n_out == tn
        n_rows = wt.shape[0] - row0
    assert n_rows % SUBLANES == 0
    return pl.pallas_call(
        functools.partial(_mm_wt_kernel, row0=row0, tn=tn, n_rows=n_rows),
        grid=(n_out // tn, m // tm),
        in_specs=[pl.BlockSpec((tm, k), lambda j, i: (i, 0)), pl.BlockSpec(memory_space=pl.ANY)],
        out_specs=pl.BlockSpec((tm, tn), lambda j, i: (i, j)),
        out_shape=jax.ShapeDtypeStruct((m, n_out), F32),
        scratch_shapes=[pltpu.VMEM((tn, k), F32), pltpu.VMEM((k, tn), BF16), pltpu.SemaphoreType.DMA(())],
        compiler_params=_cparams(("arbitrary", "arbitrary")),
        name="matmul_wt",
    )(x, wt)


def _mlstm_kernel(q_ref, k_ref, v_ref, o_ref, g_ref, c0_ref, n0_ref, m0_ref, nw_ref, gb_ref, _dst_ref,
                  h_ref, c_ref, n_ref, m_ref, *, chunk, t_valid, nb):
    ci = pl.program_id(1)

    @pl.when(ci == 0)
    def _():
        c_ref[...] = c0_ref[...]
        n_ref[...] = n0_ref[...]
        m_ref[...] = m0_ref[...]

    size = chunk
    rows = _iota2((size, size), 0)
    cols = _iota2((size, size), 1)
    eye = rows == cols
    causal = cols <= rows
    gates = g_ref[:, 0:2 * ML_HEADS] + gb_ref[...]
    capped = ML_GATE_CAP * jnp.tanh(gates / ML_GATE_CAP)
    valid = (ci * size + (_iota2((nb * size, 1), 0) & (size - 1))) < t_valid
    ipre_all = jnp.where(valid, capped[:, 0:ML_HEADS], NEG)
    logf_all = jnp.where(valid, -_softplus(-capped[:, ML_HEADS:2 * ML_HEADS]), 0.0)
    units = [(sq, h) for sq in range(nb) for h in range(ML_HEADS)]
    rs = [slice(sq * size, (sq + 1) * size) for sq, _ in units]
    ksl = [slice(h * ML_DK, (h + 1) * ML_DK) for _, h in units]
    vsl = [slice(h * ML_DV, (h + 1) * ML_DV) for _, h in units]
    ids = range(len(units))
    q = [q_ref[rs[u], ksl[u]] for u in ids]
    k = [k_ref[rs[u], ksl[u]] * (ML_DK ** -0.5) for u in ids]
    v = [v_ref[rs[u], vsl[u]] for u in ids]
    qk = [_dot_nt(q[u], k[u]) for u in ids]
    c_prev = [c_ref[sq, h] for sq, h in units]
    n_prev = [n_ref[sq, h:h + 1, :] for sq, h in units]
    qc = [_dot(q[u], c_prev[u]) for u in ids]
    s, w_inter, m_t, kw, decay = [], [], [], [], []
    for u, (sq, h) in enumerate(units):
        ig_col = ipre_all[rs[u], h:h + 1]
        lf_col = logf_all[rs[u], h:h + 1]
        ig_row = _col_to_row(ig_col, eye)
        lf_row = _col_to_row(lf_col, eye)
        b_col = jnp.sum(jnp.where(causal, lf_row, 0.0), axis=1, keepdims=True)
        b_row = jnp.sum(jnp.where(rows <= cols, lf_col, 0.0), axis=0, keepdims=True)
        d = jnp.where(causal, b_col - b_row + ig_row, NEG)
        inter = b_col + m_ref[sq, :, h:h + 1]
        m_h = jnp.maximum(inter, jnp.max(d, axis=1, keepdims=True))
        w_h = jnp.exp(inter - m_h)
        s.append(qk[u] * jnp.exp(d - m_h))
        b_last = b_col[size - 1:size, :]
        m_last = m_h[size - 1:size, :]
        w_last = jnp.exp(b_last - b_col + ig_col - m_last)
        kw.append(k[u] * w_last)
        w_inter.append(w_h)
        m_t.append(m_h)
        decay.append(w_h[size - 1:size, :])
        m_ref[sq, :, h:h + 1] = m_last
    sv = [_dot(s[u], v[u]) for u in ids]
    kv = [_dot_tn(kw[u], v[u]) for u in ids]
    for u, (sq, h) in enumerate(units):
        num = w_inter[u] * qc[u] + sv[u]
        den = (w_inter[u] * jnp.sum(q[u] * n_prev[u], axis=1, keepdims=True)
               + jnp.sum(s[u], axis=1, keepdims=True))
        hid = num / jnp.maximum(jnp.abs(den), jnp.exp(-m_t[u]))
        c_ref[sq, h] = decay[u] * c_prev[u] + kv[u]
        n_ref[sq, h:h + 1, :] = decay[u] * n_prev[u] + jnp.sum(kw[u], axis=0, keepdims=True)
        hid = hid * lax.rsqrt(jnp.mean(hid * hid, axis=-1, keepdims=True) + ML_NORM_EPS)
        hid = hid * nw_ref[:, vsl[u]]
        hid = hid * _sigmoid(o_ref[rs[u], vsl[u]])
        h_ref[rs[u], vsl[u]] = hid.astype(h_ref.dtype)


def mlstm_group(proj, dst, row0, bsz, t_pad, t_valid, chunk, c0, n0, m0, b_i, b_f, norm_w, nb=1):
    nc = t_pad // chunk
    rows_blk = nb * chunk
    blk0 = row0 // rows_blk
    assert row0 % rows_blk == 0 and t_pad % chunk == 0 and bsz % nb == 0 and (nb == 1 or nc == 1)
    assert chunk & (chunk - 1) == 0
    rmap = lambda c0_: (lambda b, c: (blk0 + b * nc + c, c0_))
    qk_w, v_w = ML_QK, ML_V
    in_specs = [
        pl.BlockSpec((rows_blk, qk_w), rmap(0)),
        pl.BlockSpec((rows_blk, qk_w), rmap(1)),
        pl.BlockSpec((rows_blk, v_w), rmap(1)),
        pl.BlockSpec((rows_blk, v_w), rmap(2)),
        pl.BlockSpec((rows_blk, LANES), rmap((2 * qk_w + 2 * v_w) // LANES)),
        pl.BlockSpec((nb, ML_HEADS, ML_DK, ML_DV), lambda b, c: (b, 0, 0, 0)),
        pl.BlockSpec((nb, ML_HEADS, ML_DK), lambda b, c: (b, 0, 0)),
        pl.BlockSpec((nb, 1, ML_HEADS), lambda b, c: (b, 0, 0)),
        pl.BlockSpec((1, v_w), lambda b, c: (0, 0)),
        pl.BlockSpec((1, 2 * ML_HEADS), lambda b, c: (0, 0)),
        pl.BlockSpec(memory_space=pl.ANY),
    ]
    out_specs = [
        pl.BlockSpec((rows_blk, v_w), lambda b, c: (blk0 + b * nc + c, 0)),
        pl.BlockSpec((nb, ML_HEADS, ML_DK, ML_DV), lambda b, c: (b, 0, 0, 0)),
        pl.BlockSpec((nb, ML_HEADS, ML_DK), lambda b, c: (b, 0, 0)),
        pl.BlockSpec((nb, 1, ML_HEADS), lambda b, c: (b, 0, 0)),
    ]
    out_shape = [
        jax.ShapeDtypeStruct(dst.shape, dst.dtype),
        jax.ShapeDtypeStruct((bsz, ML_HEADS, ML_DK, ML_DV), F32),
        jax.ShapeDtypeStruct((bsz, ML_HEADS, ML_DK), F32),
        jax.ShapeDtypeStruct((bsz, 1, ML_HEADS), F32),
    ]
    gate_bias = jnp.concatenate([b_i, b_f]).reshape(1, 2 * ML_HEADS)
    h, c, n, m = pl.pallas_call(
        functools.partial(_mlstm_kernel, chunk=chunk, t_valid=t_valid, nb=nb),
        grid=(bsz // nb, nc),
        in_specs=in_specs,
        out_specs=out_specs,
        out_shape=out_shape,
        input_output_aliases={10: 0},
        compiler_params=_cparams(("arbitrary", "arbitrary")),
        name="mlstm",
    )(proj, proj, proj, proj, proj, c0, n0, m0.reshape(bsz, 1, ML_HEADS), norm_w.reshape(1, v_w), gate_bias, dst)
    return h, c, n, m.reshape(bsz, ML_HEADS)


def _head_indicator(width, head):
    idx = np.arange(width) // head
    return jnp.asarray((idx[:, None] == np.arange(LANES)[None, :]).astype(np.float32), dtype=BF16)


def _head_sums(x, ind_ref, ind_t_ref):
    return _dot_exact_rhs(_dot_exact_rhs(x, ind_ref[...]), ind_t_ref[...])


def _rwkv_prep_kernel(cur_ref, tail_ref, first_ref, mu_ref, w0_ref, a0_ref, kk_ref, ka_ref, rk_ref,
                      w2_ref, a2_ref, g2_ref, ind_ref, ind_t_ref,
                      r_ref, wl_ref, k_ref, v_ref, an_ref, bb_ref, bonus_ref, g_ref,
                      *, tt, t_pad, multi):
    cur = cur_ref[...]
    rolled = pltpu.roll(cur, 1, 0)
    row = _iota2((tt, 1), 0)
    if multi:
        prev = jnp.where((row & (t_pad - 1)) == 0, first_ref[...], rolled)
    else:
        starts_seq = (pl.program_id(0) % (t_pad // tt)) == 0
        head = jnp.where(starts_seq, first_ref[0], tail_ref[SUBLANES - 1:SUBLANES, :])
        prev = jnp.where(row == 0, head, rolled)
    mixed = cur + (prev - cur) * mu_ref[...]
    w = RW_WIDTH
    r = mixed[:, 0:w]
    k = mixed[:, w:2 * w]
    v = mixed[:, 2 * w:3 * w]
    c0 = 3 * w
    wl = mixed[:, c0:c0 + RW_DECAY_LORA]
    al = mixed[:, c0 + RW_DECAY_LORA:c0 + RW_DECAY_LORA + RW_A_LORA]
    gl = mixed[:, c0 + RW_DECAY_LORA + RW_A_LORA:]
    w_log = -jnp.exp(-_softplus(-(w0_ref[...] + _dot(jnp.tanh(wl), w2_ref[...]))) - 0.5)
    a = _sigmoid(a0_ref[...] + _dot(al, a2_ref[...]))
    g = _dot(_sigmoid(gl), g2_ref[...])
    kk = k * kk_ref[...]
    kkn = kk / jnp.maximum(jnp.sqrt(_head_sums(kk * kk, ind_ref, ind_t_ref)), 1e-12)
    k2 = k * (1.0 + (a - 1.0) * ka_ref[...])
    r_ref[...] = r
    wl_ref[...] = w_log
    k_ref[...] = k2
    v_ref[...] = v
    an_ref[...] = -kkn
    bb_ref[...] = kkn * a
    bonus_ref[...] = _head_sums(r * k2 * rk_ref[...], ind_ref, ind_t_ref) * v
    g_ref[...] = g


def rwkv_prep(proj, row0, bsz, t_pad, shift0, prm, tt):
    n = bsz * t_pad
    multi = tt > t_pad
    assert row0 % tt == 0 and n % tt == 0 and (tt % t_pad == 0 if multi else t_pad % tt == 0)
    blk0 = row0 // tt
    c = RW_COLS
    if multi:
        first = jnp.zeros((bsz, t_pad, c), F32).at[:, 0].set(shift0).reshape(n, c)
        first_spec = pl.BlockSpec((tt, c), lambda i: (i, 0))
    else:
        first = shift0.reshape(bsz, 1, c)
        first_spec = pl.BlockSpec((1, 1, c), lambda i: (i // (t_pad // tt), 0, 0))
    tail_blk = tt // SUBLANES
    vec = lambda width: pl.BlockSpec((1, width), lambda i: (0, 0))
    full = lambda a, b: pl.BlockSpec((a, b), lambda i: (0, 0))
    w = RW_WIDTH
    out_spec = pl.BlockSpec((tt, w), lambda i: (i, 0))
    return pl.pallas_call(
        functools.partial(_rwkv_prep_kernel, tt=tt, t_pad=t_pad, multi=multi),
        grid=(n // tt,),
        in_specs=[pl.BlockSpec((tt, c), lambda i: (blk0 + i, 0)),
                  pl.BlockSpec((SUBLANES, c), lambda i: (jnp.maximum((blk0 + i) * tail_blk - 1, 0), 0)),
                  first_spec, vec(c), vec(w), vec(w), vec(w), vec(w), vec(w),
                  full(RW_DECAY_LORA, w), full(RW_A_LORA, w), full(RW_GATE_LORA, w),
                  full(w, LANES), full(LANES, w)],
        out_specs=[out_spec] * 8,
        out_shape=[jax.ShapeDtypeStruct((n, w), F32)] * 8,
        compiler_params=_cparams(("arbitrary",)),
        name="rwkv_prep",
    )(proj, proj, first, prm['mu'].reshape(1, c), prm['w0'].reshape(1, w), prm['a0'].reshape(1, w),
      prm['k_k'].reshape(1, w), prm['k_a'].reshape(1, w), prm['r_k'].reshape(1, w),
      prm['w2'], prm['a2'], prm['g2'], _head_indicator(w, RW_DH), _head_indicator(w, RW_DH).T)


def _neumann_inverse_many(n_mats, size):
    eye = (_iota2((size, size), 0) == _iota2((size, size), 1)).astype(F32)
    ts = [eye + n_mat for n_mat in n_mats]
    xs = list(n_mats)
    steps = max(int(np.ceil(np.log2(size))) - 1, 0)
    for _ in range(steps):
        xs = [_dot(x, x) for x in xs]
        ts = [t + _dot(t, x) for t, x in zip(ts, xs)]
    return ts


def _rwkv_chunk_kernel(r_ref, w_ref, k_ref, v_ref, a_ref, b_ref, bonus_ref, g_ref, s0_ref,
                       lnw_ref, lnb_ref, _dst_ref, o_ref, s_ref, acc_ref, *, chunk, t_valid, hb, nb):
    ci = pl.program_id(2)

    @pl.when(ci == 0)
    def _():
        s_ref[...] = s0_ref[...]

    size = chunk
    rows = _iota2((size, size), 0)
    cols = _iota2((size, size), 1)
    strict = cols < rows
    rows2 = _iota2((size, 2 * size), 0)
    cols2 = _iota2((size, 2 * size), 1)
    mask_ak = (cols2 >= size) & (cols2 - size < rows2)
    mask_o = jnp.where(cols2 >= size, cols2 - size, cols2) <= rows2
    slab = nb * size
    rows_s = _iota2((slab, slab), 0)
    cols_s = _iota2((slab, slab), 1)
    shift = size.bit_length() - 1
    same_seq_tril = ((rows_s >> shift) == (cols_s >> shift)) & (cols_s <= rows_s)
    valid = (ci * size + (_iota2((slab, 1), 0) & (size - 1))) < t_valid
    w = jnp.where(valid, w_ref[...], 0.0)
    a = jnp.where(valid, a_ref[...], 0.0)
    b = jnp.where(valid, b_ref[...], 0.0)
    k = jnp.where(valid, k_ref[...], 0.0)
    v = v_ref[...]
    lam = _dot_exact_lhs(same_seq_tril.astype(F32), w)
    e_pos = jnp.exp(lam)
    e_neg = jnp.exp(-lam)
    at = a * jnp.exp(lam - w)
    bt = b * e_neg
    kt = k * e_neg
    rt = r_ref[...] * e_pos
    units = [(sq, h) for sq in range(nb) for h in range(hb)]
    ids = range(len(units))
    rs = [slice(sq * size, (sq + 1) * size) for sq, _ in units]
    sls = [slice(h * RW_DH, (h + 1) * RW_DH) for _, h in units]
    at_h = [at[rs[u], sls[u]] for u in ids]
    rt_h = [rt[rs[u], sls[u]] for u in ids]
    v_h = [v[rs[u], sls[u]] for u in ids]
    bk = [jnp.concatenate([bt[rs[u], sls[u]], kt[rs[u], sls[u]]], axis=0) for u in ids]
    s0 = [s_ref[sq, h] for sq, h in units]
    pa = [_dot_nt(at_h[u], bk[u]) for u in ids]
    pr = [_dot_nt(rt_h[u], bk[u]) for u in ids]
    as0 = [_dot_nt(at_h[u], s0[u]) for u in ids]
    rs0 = [_dot_nt(rt_h[u], s0[u]) for u in ids]
    zv = [jnp.concatenate([jnp.zeros_like(v_h[u]), v_h[u]], axis=0) for u in ids]
    rhs = [as0[u] + _dot(jnp.where(mask_ak, pa[u], 0.0), zv[u]) for u in ids]
    t_inv = _neumann_inverse_many([jnp.where(strict, pa[u][:, :size], 0.0) for u in ids], size)
    uu = [_dot(t_inv[u], rhs[u]) for u in ids]
    uv = [jnp.concatenate([uu[u], v_h[u]], axis=0) for u in ids]
    o = [rs0[u] + _dot(jnp.where(mask_o, pr[u], 0.0), uv[u]) for u in ids]
    ds = [_dot_tn(uv[u], bk[u]) for u in ids]
    for u, (sq, h) in enumerate(units):
        last = (sq + 1) * size - 1
        s_ref[sq, h] = (s0[u] + ds[u]) * e_pos[last:last + 1, sls[u]]
        oc = o[u] - jnp.mean(o[u], axis=-1, keepdims=True)
        acc_ref[rs[u], sls[u]] = oc * lax.rsqrt(jnp.mean(oc * oc, axis=-1, keepdims=True) + RW_GN_EPS)
    out = (acc_ref[...] * lnw_ref[...] + lnb_ref[...] + bonus_ref[...]) * g_ref[...]
    o_ref[...] = out.astype(o_ref.dtype)


def rwkv_chunk(prep, dst, dst_col0, row0, bsz, t_pad, t_valid, chunk, s0, ln_w, ln_b, hb, nb=1):
    nc = t_pad // chunk
    hg = RW_HEADS // hb
    wb = hb * RW_DH
    rows_blk = nb * chunk
    blk0 = row0 // rows_blk
    assert dst_col0 % wb == 0 and row0 % rows_blk == 0 and bsz % nb == 0 and (nb == 1 or nc == 1)
    assert chunk & (chunk - 1) == 0
    tok = pl.BlockSpec((rows_blk, wb), lambda b, g, c: (b * nc + c, g))
    st = pl.BlockSpec((nb, hb, RW_DH, RW_DH), lambda b, g, c: (b, g, 0, 0))
    vec = pl.BlockSpec((1, wb), lambda b, g, c: (0, g))
    return pl.pallas_call(
        functools.partial(_rwkv_chunk_kernel, chunk=chunk, t_valid=t_valid, hb=hb, nb=nb),
        grid=(bsz // nb, hg, nc),
        in_specs=[tok] * 8 + [st, vec, vec, pl.BlockSpec(memory_space=pl.ANY)],
        out_specs=[pl.BlockSpec((rows_blk, wb), lambda b, g, c: (blk0 + b * nc + c, dst_col0 // wb + g)), st],
        out_shape=[jax.ShapeDtypeStruct(dst.shape, dst.dtype),
                   jax.ShapeDtypeStruct((bsz, RW_HEADS, RW_DH, RW_DH), F32)],
        scratch_shapes=[pltpu.VMEM((rows_blk, wb), F32)],
        input_output_aliases={11: 0},
        compiler_params=_cparams(("arbitrary", "arbitrary", "arbitrary")),
        name="rwkv_chunk",
    )(*prep, s0, ln_w.reshape(1, RW_WIDTH), ln_b.reshape(1, RW_WIDTH), dst)


def _conv_silu(cur_ref, hist_ref, cw_ref, size):
    cur = cur_ref[...]
    hist = hist_ref[...]
    cw = cw_ref[...]
    row8 = _iota2((SUBLANES, 1), 0)
    acc = cur * cw[GD_CONV - 1:GD_CONV, :]
    for j in range(1, GD_CONV):
        back = GD_CONV - 1 - j
        rolled = pltpu.roll(cur, j, 0)
        head = jnp.where(row8 < j, pltpu.roll(hist, j, 0), rolled[:SUBLANES])
        prev = head if size == SUBLANES else jnp.concatenate([head, rolled[SUBLANES:]], axis=0)
        acc = acc + prev * cw[back:back + 1, :]
    hist_ref[...] = cur[size - SUBLANES:size]
    return acc * _sigmoid(acc)


def _l2_normalize(x, scale):
    return x * lax.rsqrt(jnp.sum(x * x, axis=-1, keepdims=True) + GD_EPS) * scale


def _gdn_chunk_kernel(q_ref, k_ref, v_ref, z_ref, gl_ref, bq_ref, bk_ref, bv_ref, cq_ref, ck_ref, cv_ref,
                      al_ref, dt_ref, nw_ref, s0_ref, _dst_ref,
                      o_ref, s_ref, nq_ref, nk_ref, nv_ref, hq_ref, hk_ref, hv_ref,
                      *, chunk, t_valid, hpg, tail_chunk, tail_row0):
    ci = pl.program_id(2)
    hg = pl.program_id(1)

    @pl.when(ci == 0)
    def _():
        s_ref[...] = s0_ref[...]
        for hist_ref, buf_ref in ((hq_ref, bq_ref), (hk_ref, bk_ref), (hv_ref, bv_ref)):
            hist_ref[...] = jnp.zeros_like(hist_ref)
            hist_ref[SUBLANES - (GD_CONV - 1):, :] = buf_ref[0]

    @pl.when(ci == tail_chunk)
    def _():
        tail = slice(tail_row0, tail_row0 + SUBLANES)
        nq_ref[0] = q_ref[tail, :]
        nk_ref[0] = k_ref[tail, :]
        nv_ref[0] = v_ref[tail, :]

    size = chunk
    q_act = _conv_silu(q_ref, hq_ref, cq_ref, size)
    k_act = _conv_silu(k_ref, hk_ref, ck_ref, size)
    v_act = _conv_silu(v_ref, hv_ref, cv_ref, size)
    rows = _iota2((size, size), 0)
    cols = _iota2((size, size), 1)
    eye = rows == cols
    incl = cols <= rows
    strict = cols < rows
    valid = (ci * size + _iota2((size, 1), 0)) < t_valid
    logits = gl_ref[...]
    beta_all = _sigmoid(logits)
    g_all = -jnp.exp(al_ref[...]) * _softplus(logits + dt_ref[...])
    lane = _iota2((size, LANES), 1)
    rep = GD_V_HEADS // GD_K_HEADS
    kheads = range(hpg // rep)
    heads = range(hpg)
    q = [_l2_normalize(q_act[:, kh * GD_DK:(kh + 1) * GD_DK], GD_DK ** -0.5) for kh in kheads]
    k = [_l2_normalize(k_act[:, kh * GD_DK:(kh + 1) * GD_DK], 1.0) for kh in kheads]
    qk = [_dot_nt(q[kh], k[kh]) for kh in kheads]
    kk = [_dot_nt(k[kh], k[kh]) for kh in kheads]
    vsl = [slice(hl * GD_DV, (hl + 1) * GD_DV) for hl in heads]
    s = [s_ref[0, hl] for hl in heads]
    beta, gc, decay = [], [], []
    for hl in heads:
        head = hg * hpg + hl
        beta_h = jnp.sum(jnp.where(lane == head, beta_all, 0.0), axis=1, keepdims=True)
        g = jnp.sum(jnp.where(lane == head + GD_V_HEADS, g_all, 0.0), axis=1, keepdims=True)
        beta_h = jnp.where(valid, beta_h, 0.0)
        g = jnp.where(valid, g, 0.0)
        g_row = _col_to_row(g, eye)
        gc_h = jnp.sum(jnp.where(incl, g_row, 0.0), axis=1, keepdims=True)
        gc_row = jnp.sum(jnp.where(rows <= cols, g, 0.0), axis=0, keepdims=True)
        beta.append(beta_h)
        gc.append(gc_h)
        decay.append(jnp.where(incl, jnp.exp(jnp.where(incl, gc_h - gc_row, 0.0)), 0.0))
    qs = [_dot(q[hl // rep] * jnp.exp(gc[hl]), s[hl]) for hl in heads]
    t_inv = _neumann_inverse_many(
        [-jnp.where(strict, kk[hl // rep] * beta[hl] * decay[hl], 0.0) for hl in heads], size)
    uw = [_dot(t_inv[hl], jnp.concatenate([v_act[:, vsl[hl]] * beta[hl],
                                           k[hl // rep] * (beta[hl] * jnp.exp(gc[hl]))], axis=1))
          for hl in heads]
    ws = [_dot(uw[hl][:, GD_DV:], s[hl]) for hl in heads]
    v_new = [uw[hl][:, :GD_DV] - ws[hl] for hl in heads]
    av = [_dot(qk[hl // rep] * decay[hl], v_new[hl]) for hl in heads]
    g_last = [gc[hl][size - 1:size, :] for hl in heads]
    kv = [_dot_tn(k[hl // rep] * jnp.exp(g_last[hl] - gc[hl]), v_new[hl]) for hl in heads]
    for hl in heads:
        s_ref[0, hl] = s[hl] * jnp.exp(g_last[hl]) + kv[hl]
        o = qs[hl] + av[hl]
        o = o * lax.rsqrt(jnp.mean(o * o, axis=-1, keepdims=True) + GD_EPS) * nw_ref[...]
        z = z_ref[:, vsl[hl]]
        o_ref[:, vsl[hl]] = (o * (z * _sigmoid(z))).astype(o_ref.dtype)


def gdn_chunk(proj, gates, conv_buf, conv_w, dst, row0, bsz, t_pad, t_valid, chunk, s0, a_log, dt_bias,
              norm_w, hpg):
    nc = t_pad // chunk
    hgs = GD_V_HEADS // hpg
    rep = GD_V_HEADS // GD_K_HEADS
    kw = (hpg // rep) * GD_DK
    vw = hpg * GD_DV
    blk0 = row0 // chunk
    assert row0 % chunk == 0 and chunk % SUBLANES == 0
    pad_row = lambda x: jnp.pad(x, (GD_V_HEADS, LANES - 2 * GD_V_HEADS)).reshape(1, LANES)
    hist = conv_buf
    st = pl.BlockSpec((1, hpg, GD_DK, GD_DV), lambda b, g, c: (b, g, 0, 0))
    vec = pl.BlockSpec((1, LANES), lambda b, g, c: (0, 0))
    q_col, k_col, v_col = (lambda g: g), (lambda g: GD_QK // kw + g), (lambda g: 2 * GD_QK // vw + g)
    tok = lambda width, col: pl.BlockSpec((chunk, width), lambda b, g, c: (blk0 + b * nc + c, col(g)))
    buf = lambda width, col: pl.BlockSpec((1, GD_CONV - 1, width), lambda b, g, c: (b, 0, col(g)))
    taps = lambda width, col: pl.BlockSpec((GD_CONV, width), lambda b, g, c: (0, col(g)))
    last = (t_valid - 1) % chunk
    tail_row0 = (last // SUBLANES) * SUBLANES
    first = last - (GD_CONV - 2) - tail_row0
    assert first >= 0
    new_buf = lambda width: jax.ShapeDtypeStruct((bsz, SUBLANES, width), F32)
    out, state, nq, nk, nv = pl.pallas_call(
        functools.partial(_gdn_chunk_kernel, chunk=chunk, t_valid=t_valid, hpg=hpg,
                          tail_chunk=(t_valid - 1) // chunk, tail_row0=tail_row0),
        grid=(bsz, hgs, nc),
        in_specs=[tok(kw, q_col), tok(kw, k_col), tok(vw, v_col),
                  tok(vw, lambda g: GD_CONV_DIM // vw + g),
                  pl.BlockSpec((chunk, LANES), lambda b, g, c: (blk0 + b * nc + c, 0)),
                  buf(kw, q_col), buf(kw, k_col), buf(vw, v_col),
                  taps(kw, q_col), taps(kw, k_col), taps(vw, v_col),
                  vec, vec, vec, st, pl.BlockSpec(memory_space=pl.ANY)],
        out_specs=[pl.BlockSpec((chunk, vw), lambda b, g, c: (blk0 + b * nc + c, g)), st,
                   pl.BlockSpec((1, SUBLANES, kw), lambda b, g, c: (b, 0, g)),
                   pl.BlockSpec((1, SUBLANES, kw), lambda b, g, c: (b, 0, g)),
                   pl.BlockSpec((1, SUBLANES, vw), lambda b, g, c: (b, 0, g))],
        out_shape=[jax.ShapeDtypeStruct(dst.shape, dst.dtype),
                   jax.ShapeDtypeStruct((bsz, GD_V_HEADS, GD_DK, GD_DV), F32),
                   new_buf(GD_QK), new_buf(GD_QK), new_buf(GD_V_WIDTH)],
        scratch_shapes=[pltpu.VMEM((SUBLANES, kw), F32), pltpu.VMEM((SUBLANES, kw), F32),
                        pltpu.VMEM((SUBLANES, vw), F32)],
        input_output_aliases={15: 0},
        compiler_params=_cparams(("arbitrary", "arbitrary", "arbitrary")),
        name="gdn_chunk",
    )(proj, proj, proj, proj, gates, hist, hist, hist, conv_w, conv_w, conv_w,
      pad_row(a_log), pad_row(dt_bias), norm_w.reshape(1, GD_DV), s0, dst)
    conv_state = jnp.concatenate([nq, nk, nv], axis=-1)[:, first:first + GD_CONV - 1]
    return out, state, conv_state


MOE_TM = 256


def _route(logits, wt_ref, id_ref):
    lane = _iota2(logits.shape, 1).astype(F32)
    first_of = lambda hit: jnp.min(jnp.where(hit, lane, float(LANES)), axis=1, keepdims=True)
    gl = jnp.where(lane < MOE_GROUPS, logits, NEG)
    gmax = jnp.max(gl, axis=1, keepdims=True)
    g_val = 1.0 / jnp.sum(jnp.exp(gl - gmax), axis=1, keepdims=True)
    lo = MOE_GROUPS + first_of(gl == gmax) * MOE_PER_GROUP
    vals = jnp.where((lane >= lo) & (lane < lo + MOE_PER_GROUP), logits, NEG)
    top1 = jnp.max(vals, axis=1, keepdims=True)
    i1 = first_of(vals == top1)
    vals2 = jnp.where(lane == i1, NEG, vals)
    top2 = jnp.max(vals2, axis=1, keepdims=True)
    i2 = first_of(vals2 == top2)
    e2 = jnp.exp(top2 - top1)
    w1 = (1.0 / (1.0 + e2)) * g_val
    w2 = (e2 / (1.0 + e2)) * g_val
    wt_ref[...] = jnp.where(lane == 0, w1, jnp.where(lane == 1, w2, 0.0))
    ids = jnp.where(lane == 0, i1 - MOE_GROUPS, jnp.where(lane == 1, i2 - MOE_GROUPS, 0.0))
    id_ref[...] = ids.astype(jnp.int32)


def _ln_router_kernel(x_ref, y_ref, g_ref, b_ref, wr_ref, o_ref, wt_ref, id_ref):
    z = DEEPNORM_ALPHA * x_ref[...] + y_ref[...]
    zc = z - jnp.mean(z, axis=-1, keepdims=True)
    var = jnp.mean(zc * zc, axis=-1, keepdims=True)
    out = zc * lax.rsqrt(var + LN_EPS) * g_ref[...] + b_ref[...]
    o_ref[...] = out
    _route(jnp.dot(out.astype(BF16), wr_ref[...].astype(BF16), preferred_element_type=F32), wt_ref, id_ref)


def ln_residual_router(x, y, g, b, w_router, tm=256):
    m, d = x.shape
    row = pl.BlockSpec((tm, d), lambda i: (i, 0))
    vec = pl.BlockSpec((1, d), lambda i: (0, 0))
    pick = pl.BlockSpec((tm, LANES), lambda i: (i, 0))
    return pl.pallas_call(
        _ln_router_kernel,
        grid=(m // tm,),
        in_specs=[row, row, vec, vec, pl.BlockSpec((d, LANES), lambda i: (0, 0))],
        out_specs=[row, pick, pick],
        out_shape=[jax.ShapeDtypeStruct((m, d), F32),
                   jax.ShapeDtypeStruct((m, LANES), F32), jax.ShapeDtypeStruct((m, LANES), jnp.int32)],
        compiler_params=_cparams(("arbitrary",)),
        name="ln_residual_router",
    )(x, y, g.reshape(1, d), b.reshape(1, d), w_router)


def _moe_ffn_kernel(te_ref, cnt_ref, first_ref, next_ref, wslot_ref,
                    idx_ref, idx_next_ref, x_hbm, wg_hbm, wu_hbm, wd_hbm, o_ref,
                    xbuf, wgf, wuf, wdf, wgb, wub, wdb, gsem, wsem, *, layer):
    i = pl.program_id(0)
    n_tiles = pl.num_programs(0)
    slot = i % 2

    def gather(rows_ref, into, count):
        def start_row(r, carry):
            pltpu.make_async_copy(x_hbm.at[pl.ds(rows_ref[0, 0, r], 1)],
                                  xbuf.at[into, pl.ds(r, 1)], gsem.at[into]).start()
            return carry
        lax.fori_loop(0, count, start_row, 0)

    def gather_wait(into, count):
        def wait_row(r, carry):
            pltpu.make_async_copy(x_hbm.at[pl.ds(0, 1)], xbuf.at[into, pl.ds(0, 1)], gsem.at[into]).wait()
            return carry
        lax.fori_loop(0, count, wait_row, 0)

    def weight_copies(expert, ws):
        return [pltpu.make_async_copy(src.at[layer, expert], dst.at[ws], wsem.at[ws])
                for src, dst in ((wg_hbm, wgf), (wu_hbm, wuf), (wd_hbm, wdf))]

    @pl.when(i == 0)
    def _():
        xbuf[...] = jnp.zeros_like(xbuf)
        gather(idx_ref, 0, cnt_ref[0])
        for copy in weight_copies(te_ref[0], 0):
            copy.start()

    nxt = jnp.minimum(i + 1, n_tiles - 1)

    @pl.when(i + 1 < n_tiles)
    def _():
        gather(idx_next_ref, 1 - slot, cnt_ref[nxt])

    @pl.when(cnt_ref[i] > 0)
    def _():
        gather_wait(slot, cnt_ref[i])

        @pl.when(first_ref[i] == 1)
        def _():
            ws = wslot_ref[i]
            for copy in weight_copies(te_ref[i], ws):
                copy.wait()
            wgb[...] = wgf[ws].astype(BF16)
            wub[...] = wuf[ws].astype(BF16)
            wdb[...] = wdf[ws].astype(BF16)

            @pl.when(next_ref[i] >= 0)
            def _():
                for copy in weight_copies(next_ref[i], 1 - ws):
                    copy.start()

        x = xbuf[slot].astype(BF16)
        gate = jnp.dot(x, wgb[...], preferred_element_type=F32)
        up = jnp.dot(x, wub[...], preferred_element_type=F32)
        hid = gate * _sigmoid(gate) * up
        o_ref[...] = jnp.dot(hid.astype(BF16), wdb[...], preferred_element_type=F32)

    @pl.when(cnt_ref[i] == 0)
    def _():
        o_ref[...] = jnp.zeros_like(o_ref)


def moe_ffn(x, row_token, tables, layer, w_gate, w_up, w_down):
    n_tiles = row_token.shape[0]
    d = x.shape[1]
    ff = w_gate.shape[3]
    any_space = pl.BlockSpec(memory_space=pl.ANY)
    grid_spec = pltpu.PrefetchScalarGridSpec(
        num_scalar_prefetch=5,
        grid=(n_tiles,),
        in_specs=[pl.BlockSpec((1, 1, MOE_TM), lambda i, *_: (i, 0, 0), memory_space=pltpu.SMEM),
                  pl.BlockSpec((1, 1, MOE_TM), lambda i, *_: (jnp.minimum(i + 1, n_tiles - 1), 0, 0),
                               memory_space=pltpu.SMEM),
                  any_space, any_space, any_space, any_space],
        out_specs=pl.BlockSpec((MOE_TM, d), lambda i, *_: (i, 0)),
        scratch_shapes=[pltpu.VMEM((2, MOE_TM, d), F32),
                        pltpu.VMEM((2, d, ff), F32), pltpu.VMEM((2, d, ff), F32), pltpu.VMEM((2, ff, d), F32),
                        pltpu.VMEM((d, ff), BF16), pltpu.VMEM((d, ff), BF16), pltpu.VMEM((ff, d), BF16),
                        pltpu.SemaphoreType.DMA((2,)), pltpu.SemaphoreType.DMA((2,))],
    )
    return pl.pallas_call(
        functools.partial(_moe_ffn_kernel, layer=layer),
        grid_spec=grid_spec,
        out_shape=jax.ShapeDtypeStruct((n_tiles * MOE_TM, d), F32),
        compiler_params=_cparams(("arbitrary",)),
        name="moe_ffn",
    )(*tables, row_token, row_token, x, w_gate, w_up, w_down)


def _ln_moe_ple_kernel(x_ref, ya_ref, yb_ref, wt_ref, g_ref, b_ref, p_ref, wg_ref, wp_ref, o_ref, ob_ref):
    wt = wt_ref[...]
    z = DEEPNORM_ALPHA * x_ref[...] + (wt[:, 0:1] * ya_ref[...] + wt[:, 1:2] * yb_ref[...])
    zc = z - jnp.mean(z, axis=-1, keepdims=True)
    var = jnp.mean(zc * zc, axis=-1, keepdims=True)
    hid = zc * lax.rsqrt(var + LN_EPS) * g_ref[...] + b_ref[...]
    gate = _sigmoid(jnp.dot(hid.astype(BF16), wg_ref[...], preferred_element_type=F32))
    emb = jnp.dot(p_ref[...].astype(BF16), wp_ref[...], preferred_element_type=F32)
    out = hid + gate * emb
    o_ref[...] = out
    ob_ref[...] = out.astype(BF16)


def ln_moe_ple(x, ya, yb, wt, g, b, p, wg, wp, tm=512):
    m, d = x.shape
    pd = p.shape[1]
    tm = _row_tile(m, tm)
    row = pl.BlockSpec((tm, d), lambda i: (i, 0))
    vec = pl.BlockSpec((1, d), lambda i: (0, 0))
    once = pl.Buffered(1)
    return pl.pallas_call(
        _ln_moe_ple_kernel,
        grid=(m // tm,),
        in_specs=[row, row, row, pl.BlockSpec((tm, LANES), lambda i: (i, 0)), vec, vec,
                  pl.BlockSpec((tm, pd), lambda i: (i, 0)),
                  pl.BlockSpec((d, d), lambda i: (0, 0), pipeline_mode=once),
                  pl.BlockSpec((pd, d), lambda i: (0, 0), pipeline_mode=once)],
        out_specs=[row, row],
        out_shape=[jax.ShapeDtypeStruct((m, d), F32), jax.ShapeDtypeStruct((m, d), BF16)],
        compiler_params=_cparams(("arbitrary",)),
        name="ln_moe_ple",
    )(x, ya, yb, wt, g.reshape(1, d), b.reshape(1, d), p, wg, wp)


def _table_lookup(table, idx):
    hit = idx[:, None] == jnp.arange(table.shape[0], dtype=idx.dtype)[None, :]
    return jnp.sum(jnp.where(hit, table[None, :], 0), axis=1)


def router_weights(w_group, w_expert):
    return jnp.pad(jnp.concatenate([w_group, w_expert], axis=1),
                   ((0, 0), (0, LANES - MOE_GROUPS - MOE_EXPERTS)))


def hier_moe(x, ids, layer, w_gate, w_up, w_down):
    n = x.shape[0]

    n_pairs = n * MOE_TOPK
    n_tiles = -(-n_pairs // MOE_TM) + MOE_EXPERTS
    n_rows = n_tiles * MOE_TM
    i32 = jnp.int32
    eid = ids[:, :MOE_TOPK].reshape(n_pairs)
    order = jnp.argsort(eid, stable=True).astype(i32)
    rank = jnp.argsort(order).astype(i32)
    experts = jnp.arange(MOE_EXPERTS, dtype=i32)
    counts = jnp.sum((eid[:, None] == experts[None, :]).astype(i32), axis=0)
    starts = jnp.cumsum(counts) - counts
    tiles_per = (counts + MOE_TM - 1) // MOE_TM
    tile_starts = jnp.cumsum(tiles_per) - tiles_per
    dest = _table_lookup(tile_starts * MOE_TM - starts, eid) + rank
    tile_idx = jnp.arange(n_tiles, dtype=i32)
    tile_expert = jnp.sum((tile_idx[:, None] >= (tile_starts + tiles_per)[None, :]).astype(i32), axis=1)
    tile_valid = (tile_expert < MOE_EXPERTS).astype(i32)
    used = tiles_per > 0
    last_used = jnp.max(jnp.where(used, experts, 0))
    tile_expert = jnp.where(tile_valid == 1, tile_expert, last_used)
    tile_in_run = tile_idx - _table_lookup(tile_starts, tile_expert)
    off = tile_in_run[:, None] * MOE_TM + jnp.arange(MOE_TM, dtype=i32)[None, :]
    cnt = _table_lookup(counts, tile_expert)
    src = jnp.clip(_table_lookup(starts, tile_expert)[:, None] + off, 0, n_pairs - 1)
    row_ok = (off < cnt[:, None]) & (tile_valid[:, None] == 1)
    picked = jnp.take(order, src.reshape(n_rows), mode='clip').reshape(n_tiles, MOE_TM)
    row_token = jnp.where(row_ok, picked // MOE_TOPK, 0).reshape(n_tiles, 1, MOE_TM)
    tile_rows = jnp.clip(cnt - tile_in_run * MOE_TM, 0, MOE_TM) * tile_valid
    tile_first = tile_valid * (tile_in_run == 0).astype(i32)
    later_used = (experts[None, :] > experts[:, None]) & used[None, :]
    next_used = jnp.min(jnp.where(later_used, experts[None, :], MOE_EXPERTS), axis=1)
    next_used = jnp.where(next_used < MOE_EXPERTS, next_used, -1)
    run_index = jnp.cumsum(used.astype(i32)) - 1
    tables = (tile_expert, tile_rows, tile_first, _table_lookup(next_used, tile_expert),
              _table_lookup(run_index, tile_expert) % 2)

    y_sorted = moe_ffn(x, row_token, tables, layer, w_gate, w_up, w_down)
    dest2 = dest.reshape(n, MOE_TOPK)
    return (jnp.take(y_sorted, dest2[:, 0], axis=0, mode='clip'),
            jnp.take(y_sorted, dest2[:, 1], axis=0, mode='clip'))


SAMPLE_T_PAD = 8


def kernel(x_prompt, x_sample, state_mlstm_c, state_mlstm_n, state_mlstm_m, state_rwkv_s,
           state_rwkv_shift, state_gdn_s, state_gdn_conv, p_prompt, p_sample,
           w_in_ab, ml_b_i, ml_b_f, ml_norm_w, rw_mu, rw_w0, rw_w2, rw_a0, rw_a2, rw_g2,
           rw_k_k, rw_k_a, rw_r_k, rw_ln_w, rw_ln_b, w_out_ab,
           gd_w_in, gd_conv_w, gd_a_log, gd_dt_bias, gd_norm_w, gd_w_out,
           ln_mix_g, ln_mix_b, moe_w_group, moe_w_expert, moe_w_gate, moe_w_up, moe_w_down,
           ln_ffn_g, ln_ffn_b, ple_w_gate, ple_w_proj):
    pb, pt, d = x_prompt.shape
    sb, st, _ = x_sample.shape
    n_p = pb * pt
    n_s = sb * SAMPLE_T_PAD
    n = n_p + n_s
    pad_t = lambda a: jnp.pad(a, ((0, 0), (0, SAMPLE_T_PAD - st), (0, 0)))
    merge = lambda a_p, a_s: jnp.concatenate(
        [a_p.reshape(n_p, a_p.shape[-1]), pad_t(a_s).reshape(n_s, a_s.shape[-1])], axis=0)
    x = merge(x_prompt, x_sample)
    xb = x.astype(BF16)
    segs = (dict(row0=0, bsz=pb, t_pad=pt, t_valid=pt, ml_nb=1, rw_hb=16, rw_nb=1, gd_hpg=16),
            dict(row0=n_p, bsz=sb, t_pad=SAMPLE_T_PAD, t_valid=st, ml_nb=2, rw_hb=16, rw_nb=2, gd_hpg=32))
    zeros = lambda *shape: jnp.zeros(shape, F32)

    def last_rows(a, sg, count):
        assert sg['t_valid'] >= count
        idx = (sg['row0'] + jnp.arange(sg['bsz'])[:, None] * sg['t_pad']
               + (sg['t_valid'] - count + jnp.arange(count))[None, :])
        return jnp.take(a, idx.reshape(-1), axis=0, mode='clip').reshape(sg['bsz'], count, a.shape[-1])

    new_states = [[[] for _ in range(7)] for _ in segs]
    for layer in range(DEPTH):
        li = layer // 2
        if layer % 2 == 0:
            w_in_t = jnp.transpose(w_in_ab[li])
            proj_ml = matmul_wt(xb, w_in_t, 0, 3328, 1664)
            proj_rw = matmul_wt(xb, w_in_t, ML_COLS, RW_COLS, RW_COLS // 2)
            rw_prm = dict(mu=rw_mu[li], w0=rw_w0[li], w2=rw_w2[li], a0=rw_a0[li], a2=rw_a2[li], g2=rw_g2[li],
                          k_k=rw_k_k[li], k_a=rw_k_a[li], r_k=rw_r_k[li])
            heads_out = jnp.zeros((n, ML_V + RW_WIDTH), BF16)
            for si, sg in enumerate(segs):
                bsz, t_pad, t_valid, row0 = sg['bsz'], sg['t_pad'], sg['t_valid'], sg['row0']
                if si == 0:
                    c0, n0, m0 = zeros(bsz, ML_HEADS, ML_DK, ML_DV), zeros(bsz, ML_HEADS, ML_DK), zeros(bsz, ML_HEADS)
                    s0, sh0 = zeros(bsz, RW_HEADS, RW_DH, RW_DH), zeros(bsz, RW_COLS)
                else:
                    c0, n0, m0 = state_mlstm_c[li], state_mlstm_n[li], state_mlstm_m[li]
                    s0, sh0 = state_rwkv_s[li], state_rwkv_shift[li]
                heads_out, c, nn, m = mlstm_group(proj_ml, heads_out, row0, bsz, t_pad, t_valid,
                                                  min(ML_CHUNK, t_pad), c0, n0, m0,
                                                  ml_b_i[li], ml_b_f[li], ml_norm_w[li], sg['ml_nb'])
                prep = rwkv_prep(proj_rw, row0, bsz, t_pad, sh0, rw_prm, 256)
                heads_out, rs = rwkv_chunk(prep, heads_out, ML_V, row0, bsz, t_pad, t_valid,
                                           min(RW_CHUNK, t_pad), s0, rw_ln_w[li], rw_ln_b[li],
                                           sg['rw_hb'], sg['rw_nb'])
                new_shift = last_rows(proj_rw, sg, 1)[:, 0]
                for slot, val in zip(range(5), (c, nn, m, rs, new_shift)):
                    new_states[si][slot].append(val)
            mix = matmul(heads_out, w_out_ab[li], d, 1024)
        else:
            w_in_t = jnp.transpose(gd_w_in[li])
            n_qkvz = GD_CONV_DIM + GD_V_WIDTH
            proj = matmul_wt(xb, w_in_t, 0, n_qkvz, 1024)
            gates = matmul_wt(xb, w_in_t, n_qkvz, LANES, LANES)
            gd_out = jnp.zeros((n, GD_V_WIDTH), BF16)
            for si, sg in enumerate(segs):
                bsz, t_pad, t_valid, row0 = sg['bsz'], sg['t_pad'], sg['t_valid'], sg['row0']
                if si == 0:
                    s0, buf0 = zeros(bsz, GD_V_HEADS, GD_DK, GD_DV), zeros(bsz, GD_CONV - 1, GD_CONV_DIM)
                else:
                    s0, buf0 = state_gdn_s[li], state_gdn_conv[li]
                gd_out, gs, gcb = gdn_chunk(proj, gates, buf0, gd_conv_w[li], gd_out, row0, bsz, t_pad, t_valid,
                                            min(GD_CHUNK, t_pad), s0, gd_a_log[li], gd_dt_bias[li],
                                            gd_norm_w[li], sg['gd_hpg'])
                new_states[si][5].append(gs)
                new_states[si][6].append(gcb)
            mix = matmul(gd_out, gd_w_out[li], d, 512)
        x, wt, ids = ln_residual_router(x, mix, ln_mix_g[layer], ln_mix_b[layer],
                                        router_weights(moe_w_group[layer], moe_w_expert[layer]))
        ya, yb = hier_moe(x, ids, layer, moe_w_gate, moe_w_up, moe_w_down)
        x, xb = ln_moe_ple(x, ya, yb, wt, ln_ffn_g[layer], ln_ffn_b[layer],
                           merge(p_prompt[layer], p_sample[layer]),
                           ple_w_gate[layer].astype(BF16), ple_w_proj[layer].astype(BF16))
    y_prompt = x[:n_p].reshape(pb, pt, d)
    y_sample = x[n_p:].reshape(sb, SAMPLE_T_PAD, d)[:, :st]
    stack = lambda vals: jnp.stack(vals)
    return (y_prompt, y_sample) + tuple(stack(v) for v in new_states[0]) + tuple(stack(v) for v in new_states[1])
```

```python
import functools

import jax
import jax.numpy as jnp
import numpy as np
from jax import lax
from jax.experimental import pallas as pl
from jax.experimental.pallas import tpu as pltpu

F32 = jnp.float32
BF16 = jnp.bfloat16

D_MODEL = 2048
DEPTH = 2
DEEPNORM_ALPHA = (2 * DEPTH) ** 0.25
LN_EPS = 1e-5
MIX_HALF = D_MODEL // 2
ML_HEADS = 4
ML_DV = MIX_HALF // ML_HEADS
ML_DK = ML_DV // 2
ML_CHUNK = 64
ML_GATE_CAP = 15.0
ML_NORM_EPS = 1e-6
ML_QK = ML_HEADS * ML_DK
ML_V = ML_HEADS * ML_DV
ML_COLS = 2 * ML_QK + 2 * ML_V + 2 * ML_HEADS
RW_DH = 64
RW_HEADS = MIX_HALF // RW_DH
RW_WIDTH = RW_HEADS * RW_DH
RW_DECAY_LORA = 64
RW_A_LORA = 64
RW_GATE_LORA = 128
RW_GN_EPS = 64e-5
RW_COLS = 3 * RW_WIDTH + RW_DECAY_LORA + RW_A_LORA + RW_GATE_LORA
RW_CHUNK = 64
GD_DK = 128
GD_DV = 128
GD_K_HEADS = D_MODEL // GD_DK
GD_V_HEADS = 2 * GD_K_HEADS
GD_CONV = 4
GD_CHUNK = 64
GD_EPS = 1e-6
GD_QK = GD_K_HEADS * GD_DK
GD_V_WIDTH = GD_V_HEADS * GD_DV
GD_CONV_DIM = 2 * GD_QK + GD_V_WIDTH
MOE_GROUPS = 4
MOE_PER_GROUP = 8
MOE_EXPERTS = MOE_GROUPS * MOE_PER_GROUP
MOE_TOPK = 2
MOE_FF = D_MODEL // 4
PLE_DIM = 256

LANES = 128
SUBLANES = 8
NEG = -1e30
VMEM_LIMIT = 56 * 1024 * 1024


def _cparams(sem):
    return pltpu.CompilerParams(dimension_semantics=sem, vmem_limit_bytes=VMEM_LIMIT)


def _dot(a, b):
    return jnp.dot(a.astype(BF16), b.astype(BF16), preferred_element_type=F32)


def _dot_nt(a, b):
    return lax.dot_general(a.astype(BF16), b.astype(BF16), (((1,), (1,)), ((), ())),
                           preferred_element_type=F32)


def _dot_tn(a, b):
    return lax.dot_general(a.astype(BF16), b.astype(BF16), (((0,), (0,)), ((), ())),
                           preferred_element_type=F32)


def _split3(x):
    hi = x.astype(BF16)
    r1 = x - hi.astype(F32)
    mid = r1.astype(BF16)
    lo = (r1 - mid.astype(F32)).astype(BF16)
    return hi, mid, lo


def _dot_exact_rhs(a, b01):
    hi, mid, lo = _split3(a)
    b = b01.astype(BF16)
    return (jnp.dot(hi, b, preferred_element_type=F32) + jnp.dot(mid, b, preferred_element_type=F32)
            + jnp.dot(lo, b, preferred_element_type=F32))


def _dot_exact_lhs(a01, b):
    hi, mid, lo = _split3(b)
    a = a01.astype(BF16)
    return (jnp.dot(a, hi, preferred_element_type=F32) + jnp.dot(a, mid, preferred_element_type=F32)
            + jnp.dot(a, lo, preferred_element_type=F32))


def _iota2(shape, dim):
    return lax.broadcasted_iota(jnp.int32, shape, dim)


def _col_to_row(col, eye):
    return jnp.sum(jnp.where(eye, col, 0.0), axis=0, keepdims=True)


def _sigmoid(x):
    return 1.0 / (1.0 + jnp.exp(-x))


def _softplus(x):
    return jnp.maximum(x, 0.0) + jnp.log1p(jnp.exp(-jnp.abs(x)))


def _mm_kernel(x_ref, w_ref, o_ref, wb_ref):
    @pl.when(pl.program_id(1) == 0)
    def _():
        wb_ref[...] = w_ref[...].astype(BF16)

    o_ref[...] = jnp.dot(x_ref[...].astype(BF16), wb_ref[...],
                         preferred_element_type=F32).astype(o_ref.dtype)


def _row_tile(m, want):
    while m % want:
        want //= 2
    return want


def matmul(x, w, n_out, tn, tm=1024, out_dtype=F32):
    m, k = x.shape
    tm = _row_tile(m, tm)
    assert w.shape[0] == k and n_out % tn == 0 and m % tm == 0 and tn % LANES == 0 and n_out <= w.shape[1]
    return pl.pallas_call(
        _mm_kernel,
        grid=(n_out // tn, m // tm),
        in_specs=[pl.BlockSpec((tm, k), lambda j, i: (i, 0)),
                  pl.BlockSpec((k, tn), lambda j, i: (0, j))],
        out_specs=pl.BlockSpec((tm, tn), lambda j, i: (i, j)),
        out_shape=jax.ShapeDtypeStruct((m, n_out), out_dtype),
        scratch_shapes=[pltpu.VMEM((k, tn), BF16)],
        compiler_params=_cparams(("arbitrary", "arbitrary")),
        name="matmul",
    )(x, w)


def _mm_wt_kernel(x_ref, wt_hbm, o_ref, wf_ref, wb_ref, sem, *, row0, tn, n_rows):
    j = pl.program_id(0)

    @pl.when(pl.program_id(1) == 0)
    def _():
        if n_rows < tn:
            wf_ref[...] = jnp.zeros_like(wf_ref)
        start = pl.multiple_of(row0 + j * tn, SUBLANES)
        copy = pltpu.make_async_copy(wt_hbm.at[pl.ds(start, n_rows)], wf_ref.at[pl.ds(0, n_rows)], sem)
        copy.start()
        copy.wait()
        k = wf_ref.shape[1]
        step = 2 * LANES
        for c in range(k // step):
            wb_ref[c * step:(c + 1) * step, :] = wf_ref[:, c * step:(c + 1) * step].T.astype(BF16)

    o_ref[...] = jnp.dot(x_ref[...].astype(BF16), wb_ref[...], preferred_element_type=F32)


def matmul_wt(x, wt, row0, n_out, tn, tm=1024):
    m, k = x.shape
    tm = _row_tile(m, tm)
    assert wt.shape[1] == k and n_out % tn == 0 and tn % LANES == 0 and row0 % SUBLANES == 0
    n_rows = tn
    if row0 + n_out > wt.shape[0]:
        assert n_out == tn
        n_rows = wt.shape[0] - row0
    assert n_rows % SUBLANES == 0
    return pl.pallas_call(
        functools.partial(_mm_wt_kernel, row0=row0, tn=tn, n_rows=n_rows),
        grid=(n_out // tn, m // tm),
        in_specs=[pl.BlockSpec((tm, k), lambda j, i: (i, 0)), pl.BlockSpec(memory_space=pl.ANY)],
        out_specs=pl.BlockSpec((tm, tn), lambda j, i: (i, j)),
        out_shape=jax.ShapeDtypeStruct((m, n_out), F32),
        scratch_shapes=[pltpu.VMEM((tn, k), F32), pltpu.VMEM((k, tn), BF16), pltpu.SemaphoreType.DMA(())],
        compiler_params=_cparams(("arbitrary", "arbitrary")),
        name="matmul_wt",
    )(x, wt)


def _mlstm_kernel(q_ref, k_ref, v_ref, o_ref, g_ref, c0_ref, n0_ref, m0_ref, nw_ref, gb_ref, _dst_ref,
                  h_ref, c_ref, n_ref, m_ref, *, chunk, t_valid, nb):
    ci = pl.program_id(1)

    @pl.when(ci == 0)
    def _():
        c_ref[...] = c0_ref[...]
        n_ref[...] = n0_ref[...]
        m_ref[...] = m0_ref[...]

    size = chunk
    rows = _iota2((size, size), 0)
    cols = _iota2((size, size), 1)
    eye = rows == cols
    causal = cols <= rows
    gates = g_ref[:, 0:2 * ML_HEADS] + gb_ref[...]
    capped = ML_GATE_CAP * jnp.tanh(gates / ML_GATE_CAP)
    valid = (ci * size + (_iota2((nb * size, 1), 0) & (size - 1))) < t_valid
    ipre_all = jnp.where(valid, capped[:, 0:ML_HEADS], NEG)
    logf_all = jnp.where(valid, -_softplus(-capped[:, ML_HEADS:2 * ML_HEADS]), 0.0)
    units = [(sq, h) for sq in range(nb) for h in range(ML_HEADS)]
    rs = [slice(sq * size, (sq + 1) * size) for sq, _ in units]
    ksl = [slice(h * ML_DK, (h + 1) * ML_DK) for _, h in units]
    vsl = [slice(h * ML_DV, (h + 1) * ML_DV) for _, h in units]
    ids = range(len(units))
    q = [q_ref[rs[u], ksl[u]] for u in ids]
    k = [k_ref[rs[u], ksl[u]] * (ML_DK ** -0.5) for u in ids]
    v = [v_ref[rs[u], vsl[u]] for u in ids]
    qk = [_dot_nt(q[u], k[u]) for u in ids]
    c_prev = [c_ref[sq, h] for sq, h in units]
    n_prev = [n_ref[sq, h:h + 1, :] for sq, h in units]
    qc = [_dot(q[u], c_prev[u]) for u in ids]
    s, w_inter, m_t, kw, decay = [], [], [], [], []
    for u, (sq, h) in enumerate(units):
        ig_col = ipre_all[rs[u], h:h + 1]
        lf_col = logf_all[rs[u], h:h + 1]
        ig_row = _col_to_row(ig_col, eye)
        lf_row = _col_to_row(lf_col, eye)
        b_col = jnp.sum(jnp.where(causal, lf_row, 0.0), axis=1, keepdims=True)
        b_row = jnp.sum(jnp.where(rows <= cols, lf_col, 0.0), axis=0, keepdims=True)
        d = jnp.where(causal, b_col - b_row + ig_row, NEG)
        inter = b_col + m_ref[sq, :, h:h + 1]
        m_h = jnp.maximum(inter, jnp.max(d, axis=1, keepdims=True))
        w_h = jnp.exp(inter - m_h)
        s.append(qk[u] * jnp.exp(d - m_h))
        b_last = b_col[size - 1:size, :]
        m_last = m_h[size - 1:size, :]
        w_last = jnp.exp(b_last - b_col + ig_col - m_last)
        kw.append(k[u] * w_last)
        w_inter.append(w_h)
        m_t.append(m_h)
        decay.append(w_h[size - 1:size, :])
        m_ref[sq, :, h:h + 1] = m_last
    sv = [_dot(s[u], v[u]) for u in ids]
    kv = [_dot_tn(kw[u], v[u]) for u in ids]
    for u, (sq, h) in enumerate(units):
        num = w_inter[u] * qc[u] + sv[u]
        den = (w_inter[u] * jnp.sum(q[u] * n_prev[u], axis=1, keepdims=True)
               + jnp.sum(s[u], axis=1, keepdims=True))
        hid = num / jnp.maximum(jnp.abs(den), jnp.exp(-m_t[u]))
        c_ref[sq, h] = decay[u] * c_prev[u] + kv[u]
        n_ref[sq, h:h + 1, :] = decay[u] * n_prev[u] + jnp.sum(kw[u], axis=0, keepdims=True)
        hid = hid * lax.rsqrt(jnp.mean(hid * hid, axis=-1, keepdims=True) + ML_NORM_EPS)
        hid = hid * nw_ref[:, vsl[u]]
        hid = hid * _sigmoid(o_ref[rs[u], vsl[u]])
        h_ref[rs[u], vsl[u]] = hid.astype(h_ref.dtype)


def mlstm_group(proj, dst, row0, bsz, t_pad, t_valid, chunk, c0, n0, m0, b_i, b_f, norm_w, nb=1):
    nc = t_pad // chunk
    rows_blk = nb * chunk
    blk0 = row0 // rows_blk
    assert row0 % rows_blk == 0 and t_pad % chunk == 0 and bsz % nb == 0 and (nb == 1 or nc == 1)
    assert chunk & (chunk - 1) == 0
    rmap = lambda c0_: (lambda b, c: (blk0 + b * nc + c, c0_))
    qk_w, v_w = ML_QK, ML_V
    in_specs = [
        pl.BlockSpec((rows_blk, qk_w), rmap(0)),
        pl.BlockSpec((rows_blk, qk_w), rmap(1)),
        pl.BlockSpec((rows_blk, v_w), rmap(1)),
        pl.BlockSpec((rows_blk, v_w), rmap(2)),
        pl.BlockSpec((rows_blk, LANES), rmap((2 * qk_w + 2 * v_w) // LANES)),
        pl.BlockSpec((nb, ML_HEADS, ML_DK, ML_DV), lambda b, c: (b, 0, 0, 0)),
        pl.BlockSpec((nb, ML_HEADS, ML_DK), lambda b, c: (b, 0, 0)),
        pl.BlockSpec((nb, 1, ML_HEADS), lambda b, c: (b, 0, 0)),
        pl.BlockSpec((1, v_w), lambda b, c: (0, 0)),
        pl.BlockSpec((1, 2 * ML_HEADS), lambda b, c: (0, 0)),
        pl.BlockSpec(memory_space=pl.ANY),
    ]
    out_specs = [
        pl.BlockSpec((rows_blk, v_w), lambda b, c: (blk0 + b * nc + c, 0)),
        pl.BlockSpec((nb, ML_HEADS, ML_DK, ML_DV), lambda b, c: (b, 0, 0, 0)),
        pl.BlockSpec((nb, ML_HEADS, ML_DK), lambda b, c: (b, 0, 0)),
        pl.BlockSpec((nb, 1, ML_HEADS), lambda b, c: (b, 0, 0)),
    ]
    out_shape = [
        jax.ShapeDtypeStruct(dst.shape, dst.dtype),
        jax.ShapeDtypeStruct((bsz, ML_HEADS, ML_DK, ML_DV), F32),
        jax.ShapeDtypeStruct((bsz, ML_HEADS, ML_DK), F32),
        jax.ShapeDtypeStruct((bsz, 1, ML_HEADS), F32),
    ]
    gate_bias = jnp.concatenate([b_i, b_f]).reshape(1, 2 * ML_HEADS)
    h, c, n, m = pl.pallas_call(
        functools.partial(_mlstm_kernel, chunk=chunk, t_valid=t_valid, nb=nb),
        grid=(bsz // nb, nc),
        in_specs=in_specs,
        out_specs=out_specs,
        out_shape=out_shape,
        input_output_aliases={10: 0},
        compiler_params=_cparams(("arbitrary", "arbitrary")),
        name="mlstm",
    )(proj, proj, proj, proj, proj, c0, n0, m0.reshape(bsz, 1, ML_HEADS), norm_w.reshape(1, v_w), gate_bias, dst)
    return h, c, n, m.reshape(bsz, ML_HEADS)


def _head_indicator(width, head):
    idx = np.arange(width) // head
    return jnp.asarray((idx[:, None] == np.arange(LANES)[None, :]).astype(np.float32), dtype=BF16)


def _head_sums(x, ind_ref, ind_t_ref):
    return _dot_exact_rhs(_dot_exact_rhs(x, ind_ref[...]), ind_t_ref[...])


def _rwkv_prep_kernel(cur_ref, tail_ref, first_ref, mu_ref, w0_ref, a0_ref, kk_ref, ka_ref, rk_ref,
                      w2_ref, a2_ref, g2_ref, ind_ref, ind_t_ref,
                      r_ref, wl_ref, k_ref, v_ref, an_ref, bb_ref, bonus_ref, g_ref,
                      *, tt, t_pad, multi):
    cur = cur_ref[...]
    rolled = pltpu.roll(cur, 1, 0)
    row = _iota2((tt, 1), 0)
    if multi:
        prev = jnp.where((row & (t_pad - 1)) == 0, first_ref[...], rolled)
    else:
        starts_seq = (pl.program_id(0) % (t_pad // tt)) == 0
        head = jnp.where(starts_seq, first_ref[0], tail_ref[SUBLANES - 1:SUBLANES, :])
        prev = jnp.where(row == 0, head, rolled)
    mixed = cur + (prev - cur) * mu_ref[...]
    w = RW_WIDTH
    r = mixed[:, 0:w]
    k = mixed[:, w:2 * w]
    v = mixed[:, 2 * w:3 * w]
    c0 = 3 * w
    wl = mixed[:, c0:c0 + RW_DECAY_LORA]
    al = mixed[:, c0 + RW_DECAY_LORA:c0 + RW_DECAY_LORA + RW_A_LORA]
    gl = mixed[:, c0 + RW_DECAY_LORA + RW_A_LORA:]
    w_log = -jnp.exp(-_softplus(-(w0_ref[...] + _dot(jnp.tanh(wl), w2_ref[...]))) - 0.5)
    a = _sigmoid(a0_ref[...] + _dot(al, a2_ref[...]))
    g = _dot(_sigmoid(gl), g2_ref[...])
    kk = k * kk_ref[...]
    kkn = kk / jnp.maximum(jnp.sqrt(_head_sums(kk * kk, ind_ref, ind_t_ref)), 1e-12)
    k2 = k * (1.0 + (a - 1.0) * ka_ref[...])
    r_ref[...] = r
    wl_ref[...] = w_log
    k_ref[...] = k2
    v_ref[...] = v
    an_ref[...] = -kkn
    bb_ref[...] = kkn * a
    bonus_ref[...] = _head_sums(r * k2 * rk_ref[...], ind_ref, ind_t_ref) * v
    g_ref[...] = g


def rwkv_prep(proj, row0, bsz, t_pad, shift0, prm, tt):
    n = bsz * t_pad
    multi = tt > t_pad
    assert row0 % tt == 0 and n % tt == 0 and (tt % t_pad == 0 if multi else t_pad % tt == 0)
    blk0 = row0 // tt
    c = RW_COLS
    if multi:
        first = jnp.zeros((bsz, t_pad, c), F32).at[:, 0].set(shift0).reshape(n, c)
        first_spec = pl.BlockSpec((tt, c), lambda i: (i, 0))
    else:
        first = shift0.reshape(bsz, 1, c)
        first_spec = pl.BlockSpec((1, 1, c), lambda i: (i // (t_pad // tt), 0, 0))
    tail_blk = tt // SUBLANES
    vec = lambda width: pl.BlockSpec((1, width), lambda i: (0, 0))
    full = lambda a, b: pl.BlockSpec((a, b), lambda i: (0, 0))
    w = RW_WIDTH
    out_spec = pl.BlockSpec((tt, w), lambda i: (i, 0))
    return pl.pallas_call(
        functools.partial(_rwkv_prep_kernel, tt=tt, t_pad=t_pad, multi=multi),
        grid=(n // tt,),
        in_specs=[pl.BlockSpec((tt, c), lambda i: (blk0 + i, 0)),
                  pl.BlockSpec((SUBLANES, c), lambda i: (jnp.maximum((blk0 + i) * tail_blk - 1, 0), 0)),
                  first_spec, vec(c), vec(w), vec(w), vec(w), vec(w), vec(w),
                  full(RW_DECAY_LORA, w), full(RW_A_LORA, w), full(RW_GATE_LORA, w),
                  full(w, LANES), full(LANES, w)],
        out_specs=[out_spec] * 8,
        out_shape=[jax.ShapeDtypeStruct((n, w), F32)] * 8,
        compiler_params=_cparams(("arbitrary",)),
        name="rwkv_prep",
    )(proj, proj, first, prm['mu'].reshape(1, c), prm['w0'].reshape(1, w), prm['a0'].reshape(1, w),
      prm['k_k'].reshape(1, w), prm['k_a'].reshape(1, w), prm['r_k'].reshape(1, w),
      prm['w2'], prm['a2'], prm['g2'], _head_indicator(w, RW_DH), _head_indicator(w, RW_DH).T)


def _neumann_inverse_many(n_mats, size):
    eye = (_iota2((size, size), 0) == _iota2((size, size), 1)).astype(F32)
    ts = [eye + n_mat for n_mat in n_mats]
    xs = list(n_mats)
    steps = max(int(np.ceil(np.log2(size))) - 1, 0)
    for _ in range(steps):
        xs = [_dot(x, x) for x in xs]
        ts = [t + _dot(t, x) for t, x in zip(ts, xs)]
    return ts


def _rwkv_chunk_kernel(r_ref, w_ref, k_ref, v_ref, a_ref, b_ref, bonus_ref, g_ref, s0_ref,
                       lnw_ref, lnb_ref, _dst_ref, o_ref, s_ref, acc_ref, *, chunk, t_valid, hb, nb):
    ci = pl.program_id(2)

    @pl.when(ci == 0)
    def _():
        s_ref[...] = s0_ref[...]

    size = chunk
    rows = _iota2((size, size), 0)
    cols = _iota2((size, size), 1)
    strict = cols < rows
    rows2 = _iota2((size, 2 * size), 0)
    cols2 = _iota2((size, 2 * size), 1)
    mask_ak = (cols2 >= size) & (cols2 - size < rows2)
    mask_o = jnp.where(cols2 >= size, cols2 - size, cols2) <= rows2
    slab = nb * size
    rows_s = _iota2((slab, slab), 0)
    cols_s = _iota2((slab, slab), 1)
    shift = size.bit_length() - 1
    same_seq_tril = ((rows_s >> shift) == (cols_s >> shift)) & (cols_s <= rows_s)
    valid = (ci * size + (_iota2((slab, 1), 0) & (size - 1))) < t_valid
    w = jnp.where(valid, w_ref[...], 0.0)
    a = jnp.where(valid, a_ref[...], 0.0)
    b = jnp.where(valid, b_ref[...], 0.0)
    k = jnp.where(valid, k_ref[...], 0.0)
    v = v_ref[...]
    lam = _dot_exact_lhs(same_seq_tril.astype(F32), w)
    e_pos = jnp.exp(lam)
    e_neg = jnp.exp(-lam)
    at = a * jnp.exp(lam - w)
    bt = b * e_neg
    kt = k * e_neg
    rt = r_ref[...] * e_pos
    units = [(sq, h) for sq in range(nb) for h in range(hb)]
    ids = range(len(units))
    rs = [slice(sq * size, (sq + 1) * size) for sq, _ in units]
    sls = [slice(h * RW_DH, (h + 1) * RW_DH) for _, h in units]
    at_h = [at[rs[u], sls[u]] for u in ids]
    rt_h = [rt[rs[u], sls[u]] for u in ids]
    v_h = [v[rs[u], sls[u]] for u in ids]
    bk = [jnp.concatenate([bt[rs[u], sls[u]], kt[rs[u], sls[u]]], axis=0) for u in ids]
    s0 = [s_ref[sq, h] for sq, h in units]
    pa = [_dot_nt(at_h[u], bk[u]) for u in ids]
    pr = [_dot_nt(rt_h[u], bk[u]) for u in ids]
    as0 = [_dot_nt(at_h[u], s0[u]) for u in ids]
    rs0 = [_dot_nt(rt_h[u], s0[u]) for u in ids]
    zv = [jnp.concatenate([jnp.zeros_like(v_h[u]), v_h[u]], axis=0) for u in ids]
    rhs = [as0[u] + _dot(jnp.where(mask_ak, pa[u], 0.0), zv[u]) for u in ids]
    t_inv = _neumann_inverse_many([jnp.where(strict, pa[u][:, :size], 0.0) for u in ids], size)
    uu = [_dot(t_inv[u], rhs[u]) for u in ids]
    uv = [jnp.concatenate([uu[u], v_h[u]], axis=0) for u in ids]
    o = [rs0[u] + _dot(jnp.where(mask_o, pr[u], 0.0), uv[u]) for u in ids]
    ds = [_dot_tn(uv[u], bk[u]) for u in ids]
    for u, (sq, h) in enumerate(units):
        last = (sq + 1) * size - 1
        s_ref[sq, h] = (s0[u] + ds[u]) * e_pos[last:last + 1, sls[u]]
        oc = o[u] - jnp.mean(o[u], axis=-1, keepdims=True)
        acc_ref[rs[u], sls[u]] = oc * lax.rsqrt(jnp.mean(oc * oc, axis=-1, keepdims=True) + RW_GN_EPS)
    out = (acc_ref[...] * lnw_ref[...] + lnb_ref[...] + bonus_ref[...]) * g_ref[...]
    o_ref[...] = out.astype(o_ref.dtype)


def rwkv_chunk(prep, dst, dst_col0, row0, bsz, t_pad, t_valid, chunk, s0, ln_w, ln_b, hb, nb=1):
    nc = t_pad // chunk
    hg = RW_HEADS // hb
    wb = hb * RW_DH
    rows_blk = nb * chunk
    blk0 = row0 // rows_blk
    assert dst_col0 % wb == 0 and row0 % rows_blk == 0 and bsz % nb == 0 and (nb == 1 or nc == 1)
    assert chunk & (chunk - 1) == 0
    tok = pl.BlockSpec((rows_blk, wb), lambda b, g, c: (b * nc + c, g))
    st = pl.BlockSpec((nb, hb, RW_DH, RW_DH), lambda b, g, c: (b, g, 0, 0))
    vec = pl.BlockSpec((1, wb), lambda b, g, c: (0, g))
    return pl.pallas_call(
        functools.partial(_rwkv_chunk_kernel, chunk=chunk, t_valid=t_valid, hb=hb, nb=nb),
        grid=(bsz // nb, hg, nc),
        in_specs=[tok] * 8 + [st, vec, vec, pl.BlockSpec(memory_space=pl.ANY)],
        out_specs=[pl.BlockSpec((rows_blk, wb), lambda b, g, c: (blk0 + b * nc + c, dst_col0 // wb + g)), st],
        out_shape=[jax.ShapeDtypeStruct(dst.shape, dst.dtype),
                   jax.ShapeDtypeStruct((bsz, RW_HEADS, RW_DH, RW_DH), F32)],
        scratch_shapes=[pltpu.VMEM((rows_blk, wb), F32)],
        input_output_aliases={11: 0},
        compiler_params=_cparams(("arbitrary", "arbitrary", "arbitrary")),
        name="rwkv_chunk",
    )(*prep, s0, ln_w.reshape(1, RW_WIDTH), ln_b.reshape(1, RW_WIDTH), dst)


def _conv_silu(cur_ref, hist_ref, cw_ref, size):
    cur = cur_ref[...]
    hist = hist_ref[...]
    cw = cw_ref[...]
    row8 = _iota2((SUBLANES, 1), 0)
    acc = cur * cw[GD_CONV - 1:GD_CONV, :]
    for j in range(1, GD_CONV):
        back = GD_CONV - 1 - j
        rolled = pltpu.roll(cur, j, 0)
        head = jnp.where(row8 < j, pltpu.roll(hist, j, 0), rolled[:SUBLANES])
        prev = head if size == SUBLANES else jnp.concatenate([head, rolled[SUBLANES:]], axis=0)
        acc = acc + prev * cw[back:back + 1, :]
    hist_ref[...] = cur[size - SUBLANES:size]
    return acc * _sigmoid(acc)


def _l2_normalize(x, scale):
    return x * lax.rsqrt(jnp.sum(x * x, axis=-1, keepdims=True) + GD_EPS) * scale


def _gdn_chunk_kernel(q_ref, k_ref, v_ref, z_ref, gl_ref, bq_ref, bk_ref, bv_ref, cq_ref, ck_ref, cv_ref,
                      al_ref, dt_ref, nw_ref, s0_ref, _dst_ref,
                      o_ref, s_ref, nq_ref, nk_ref, nv_ref, hq_ref, hk_ref, hv_ref,
                      *, chunk, t_valid, hpg, tail_chunk, tail_row0):
    ci = pl.program_id(2)
    hg = pl.program_id(1)

    @pl.when(ci == 0)
    def _():
        s_ref[...] = s0_ref[...]
        for hist_ref, buf_ref in ((hq_ref, bq_ref), (hk_ref, bk_ref), (hv_ref, bv_ref)):
            hist_ref[...] = jnp.zeros_like(hist_ref)
            hist_ref[SUBLANES - (GD_CONV - 1):, :] = buf_ref[0]

    @pl.when(ci == tail_chunk)
    def _():
        tail = slice(tail_row0, tail_row0 + SUBLANES)
        nq_ref[0] = q_ref[tail, :]
        nk_ref[0] = k_ref[tail, :]
        nv_ref[0] = v_ref[tail, :]

    size = chunk
    q_act = _conv_silu(q_ref, hq_ref, cq_ref, size)
    k_act = _conv_silu(k_ref, hk_ref, ck_ref, size)
    v_act = _conv_silu(v_ref, hv_ref, cv_ref, size)
    rows = _iota2((size, size), 0)
    cols = _iota2((size, size), 1)
    eye = rows == cols
    incl = cols <= rows
    strict = cols < rows
    valid = (ci * size + _iota2((size, 1), 0)) < t_valid
    logits = gl_ref[...]
    beta_all = _sigmoid(logits)
    g_all = -jnp.exp(al_ref[...]) * _softplus(logits + dt_ref[...])
    lane = _iota2((size, LANES), 1)
    rep = GD_V_HEADS // GD_K_HEADS
    kheads = range(hpg // rep)
    heads = range(hpg)
    q = [_l2_normalize(q_act[:, kh * GD_DK:(kh + 1) * GD_DK], GD_DK ** -0.5) for kh in kheads]
    k = [_l2_normalize(k_act[:, kh * GD_DK:(kh + 1) * GD_DK], 1.0) for kh in kheads]
    qk = [_dot_nt(q[kh], k[kh]) for kh in kheads]
    kk = [_dot_nt(k[kh], k[kh]) for kh in kheads]
    vsl = [slice(hl * GD_DV, (hl + 1) * GD_DV) for hl in heads]
    s = [s_ref[0, hl] for hl in heads]
    beta, gc, decay = [], [], []
    for hl in heads:
        head = hg * hpg + hl
        beta_h = jnp.sum(jnp.where(lane == head, beta_all, 0.0), axis=1, keepdims=True)
        g = jnp.sum(jnp.where(lane == head + GD_V_HEADS, g_all, 0.0), axis=1, keepdims=True)
        beta_h = jnp.where(valid, beta_h, 0.0)
        g = jnp.where(valid, g, 0.0)
        g_row = _col_to_row(g, eye)
        gc_h = jnp.sum(jnp.where(incl, g_row, 0.0), axis=1, keepdims=True)
        gc_row = jnp.sum(jnp.where(rows <= cols, g, 0.0), axis=0, keepdims=True)
        beta.append(beta_h)
        gc.append(gc_h)
        decay.append(jnp.where(incl, jnp.exp(jnp.where(incl, gc_h - gc_row, 0.0)), 0.0))
    qs = [_dot(q[hl // rep] * jnp.exp(gc[hl]), s[hl]) for hl in heads]
    t_inv = _neumann_inverse_many(
        [-jnp.where(strict, kk[hl // rep] * beta[hl] * decay[hl], 0.0) for hl in heads], size)
    uw = [_dot(t_inv[hl], jnp.concatenate([v_act[:, vsl[hl]] * beta[hl],
                                           k[hl // rep] * (beta[hl] * jnp.exp(gc[hl]))], axis=1))
          for hl in heads]
    ws = [_dot(uw[hl][:, GD_DV:], s[hl]) for hl in heads]
    v_new = [uw[hl][:, :GD_DV] - ws[hl] for hl in heads]
    av = [_dot(qk[hl // rep] * decay[hl], v_new[hl]) for hl in heads]
    g_last = [gc[hl][size - 1:size, :] for hl in heads]
    kv = [_dot_tn(k[hl // rep] * jnp.exp(g_last[hl] - gc[hl]), v_new[hl]) for hl in heads]
    for hl in heads:
        s_ref[0, hl] = s[hl] * jnp.exp(g_last[hl]) + kv[hl]
        o = qs[hl] + av[hl]
        o = o * lax.rsqrt(jnp.mean(o * o, axis=-1, keepdims=True) + GD_EPS) * nw_ref[...]
        z = z_ref[:, vsl[hl]]
        o_ref[:, vsl[hl]] = (o * (z * _sigmoid(z))).astype(o_ref.dtype)


def gdn_chunk(proj, gates, conv_buf, conv_w, dst, row0, bsz, t_pad, t_valid, chunk, s0, a_log, dt_bias,
              norm_w, hpg):
    nc = t_pad // chunk
    hgs = GD_V_HEADS // hpg
    rep = GD_V_HEADS // GD_K_HEADS
    kw = (hpg // rep) * GD_DK
    vw = hpg * GD_DV
    blk0 = row0 // chunk
    assert row0 % chunk == 0 and chunk % SUBLANES == 0
    pad_row = lambda x: jnp.pad(x, (GD_V_HEADS, LANES - 2 * GD_V_HEADS)).reshape(1, LANES)
    hist = conv_buf
    st = pl.BlockSpec((1, hpg, GD_DK, GD_DV), lambda b, g, c: (b, g, 0, 0))
    vec = pl.BlockSpec((1, LANES), lambda b, g, c: (0, 0))
    q_col, k_col, v_col = (lambda g: g), (lambda g: GD_QK // kw + g), (lambda g: 2 * GD_QK // vw + g)
    tok = lambda width, col: pl.BlockSpec((chunk, width), lambda b, g, c: (blk0 + b * nc + c, col(g)))
    buf = lambda width, col: pl.BlockSpec((1, GD_CONV - 1, width), lambda b, g, c: (b, 0, col(g)))
    taps = lambda width, col: pl.BlockSpec((GD_CONV, width), lambda b, g, c: (0, col(g)))
    last = (t_valid - 1) % chunk
    tail_row0 = (last // SUBLANES) * SUBLANES
    first = last - (GD_CONV - 2) - tail_row0
    assert first >= 0
    new_buf = lambda width: jax.ShapeDtypeStruct((bsz, SUBLANES, width), F32)
    out, state, nq, nk, nv = pl.pallas_call(
        functools.partial(_gdn_chunk_kernel, chunk=chunk, t_valid=t_valid, hpg=hpg,
                          tail_chunk=(t_valid - 1) // chunk, tail_row0=tail_row0),
        grid=(bsz, hgs, nc),
        in_specs=[tok(kw, q_col), tok(kw, k_col), tok(vw, v_col),
                  tok(vw, lambda g: GD_CONV_DIM // vw + g),
                  pl.BlockSpec((chunk, LANES), lambda b, g, c: (blk0 + b * nc + c, 0)),
                  buf(kw, q_col), buf(kw, k_col), buf(vw, v_col),
                  taps(kw, q_col), taps(kw, k_col), taps(vw, v_col),
                  vec, vec, vec, st, pl.BlockSpec(memory_space=pl.ANY)],
        out_specs=[pl.BlockSpec((chunk, vw), lambda b, g, c: (blk0 + b * nc + c, g)), st,
                   pl.BlockSpec((1, SUBLANES, kw), lambda b, g, c: (b, 0, g)),
                   pl.BlockSpec((1, SUBLANES, kw), lambda b, g, c: (b, 0, g)),
                   pl.BlockSpec((1, SUBLANES, vw), lambda b, g, c: (b, 0, g))],
        out_shape=[jax.ShapeDtypeStruct(dst.shape, dst.dtype),
                   jax.ShapeDtypeStruct((bsz, GD_V_HEADS, GD_DK, GD_DV), F32),
                   new_buf(GD_QK), new_buf(GD_QK), new_buf(GD_V_WIDTH)],
        scratch_shapes=[pltpu.VMEM((SUBLANES, kw), F32), pltpu.VMEM((SUBLANES, kw), F32),
                        pltpu.VMEM((SUBLANES, vw), F32)],
        input_output_aliases={15: 0},
        compiler_params=_cparams(("arbitrary", "arbitrary", "arbitrary")),
        name="gdn_chunk",
    )(proj, proj, proj, proj, gates, hist, hist, hist, conv_w, conv_w, conv_w,
      pad_row(a_log), pad_row(dt_bias), norm_w.reshape(1, GD_DV), s0, dst)
    conv_state = jnp.concatenate([nq, nk, nv], axis=-1)[:, first:first + GD_CONV - 1]
    return out, state, conv_state


MOE_TM = 256


def _route(logits, wt_ref, id_ref):
    lane = _iota2(logits.shape, 1).astype(F32)
    first_of = lambda hit: jnp.min(jnp.where(hit, lane, float(LANES)), axis=1, keepdims=True)
    gl = jnp.where(lane < MOE_GROUPS, logits, NEG)
    gmax = jnp.max(gl, axis=1, keepdims=True)
    g_val = 1.0 / jnp.sum(jnp.exp(gl - gmax), axis=1, keepdims=True)
    lo = MOE_GROUPS + first_of(gl == gmax) * MOE_PER_GROUP
    vals = jnp.where((lane >= lo) & (lane < lo + MOE_PER_GROUP), logits, NEG)
    top1 = jnp.max(vals, axis=1, keepdims=True)
    i1 = first_of(vals == top1)
    vals2 = jnp.where(lane == i1, NEG, vals)
    top2 = jnp.max(vals2, axis=1, keepdims=True)
    i2 = first_of(vals2 == top2)
    e2 = jnp.exp(top2 - top1)
    w1 = (1.0 / (1.0 + e2)) * g_val
    w2 = (e2 / (1.0 + e2)) * g_val
    wt_ref[...] = jnp.where(lane == 0, w1, jnp.where(lane == 1, w2, 0.0))
    ids = jnp.where(lane == 0, i1 - MOE_GROUPS, jnp.where(lane == 1, i2 - MOE_GROUPS, 0.0))
    id_ref[...] = ids.astype(jnp.int32)


def _ln_router_kernel(x_ref, y_ref, g_ref, b_ref, wr_ref, o_ref, wt_ref, id_ref):
    z = DEEPNORM_ALPHA * x_ref[...] + y_ref[...]
    zc = z - jnp.mean(z, axis=-1, keepdims=True)
    var = jnp.mean(zc * zc, axis=-1, keepdims=True)
    out = zc * lax.rsqrt(var + LN_EPS) * g_ref[...] + b_ref[...]
    o_ref[...] = out
    _route(jnp.dot(out.astype(BF16), wr_ref[...].astype(BF16), preferred_element_type=F32), wt_ref, id_ref)


def ln_residual_router(x, y, g, b, w_router, tm=256):
    m, d = x.shape
    row = pl.BlockSpec((tm, d), lambda i: (i, 0))
    vec = pl.BlockSpec((1, d), lambda i: (0, 0))
    pick = pl.BlockSpec((tm, LANES), lambda i: (i, 0))
    return pl.pallas_call(
        _ln_router_kernel,
        grid=(m // tm,),
        in_specs=[row, row, vec, vec, pl.BlockSpec((d, LANES), lambda i: (0, 0))],
        out_specs=[row, pick, pick],
        out_shape=[jax.ShapeDtypeStruct((m, d), F32),
                   jax.ShapeDtypeStruct((m, LANES), F32), jax.ShapeDtypeStruct((m, LANES), jnp.int32)],
        compiler_params=_cparams(("arbitrary",)),
        name="ln_residual_router",
    )(x, y, g.reshape(1, d), b.reshape(1, d), w_router)


def _moe_ffn_kernel(te_ref, cnt_ref, first_ref, next_ref, wslot_ref,
                    idx_ref, idx_next_ref, x_hbm, wg_hbm, wu_hbm, wd_hbm, o_ref,
                    xbuf, wgf, wuf, wdf, wgb, wub, wdb, gsem, wsem, *, layer):
    i = pl.program_id(0)
    n_tiles = pl.num_programs(0)
    slot = i % 2

    def gather(rows_ref, into, count):
        def start_row(r, carry):
            pltpu.make_async_copy(x_hbm.at[pl.ds(rows_ref[0, 0, r], 1)],
                                  xbuf.at[into, pl.ds(r, 1)], gsem.at[into]).start()
            return carry
        lax.fori_loop(0, count, start_row, 0)

    def gather_wait(into, count):
        def wait_row(r, carry):
            pltpu.make_async_copy(x_hbm.at[pl.ds(0, 1)], xbuf.at[into, pl.ds(0, 1)], gsem.at[into]).wait()
            return carry
        lax.fori_loop(0, count, wait_row, 0)

    def weight_copies(expert, ws):
        return [pltpu.make_async_copy(src.at[layer, expert], dst.at[ws], wsem.at[ws])
                for src, dst in ((wg_hbm, wgf), (wu_hbm, wuf), (wd_hbm, wdf))]

    @pl.when(i == 0)
    def _():
        xbuf[...] = jnp.zeros_like(xbuf)
        gather(idx_ref, 0, cnt_ref[0])
        for copy in weight_copies(te_ref[0], 0):
            copy.start()

    nxt = jnp.minimum(i + 1, n_tiles - 1)

    @pl.when(i + 1 < n_tiles)
    def _():
        gather(idx_next_ref, 1 - slot, cnt_ref[nxt])

    @pl.when(cnt_ref[i] > 0)
    def _():
        gather_wait(slot, cnt_ref[i])

        @pl.when(first_ref[i] == 1)
        def _():
            ws = wslot_ref[i]
            for copy in weight_copies(te_ref[i], ws):
                copy.wait()
            wgb[...] = wgf[ws].astype(BF16)
            wub[...] = wuf[ws].astype(BF16)
            wdb[...] = wdf[ws].astype(BF16)

            @pl.when(next_ref[i] >= 0)
            def _():
                for copy in weight_copies(next_ref[i], 1 - ws):
                    copy.start()

        x = xbuf[slot].astype(BF16)
        gate = jnp.dot(x, wgb[...], preferred_element_type=F32)
        up = jnp.dot(x, wub[...], preferred_element_type=F32)
        hid = gate * _sigmoid(gate) * up
        o_ref[...] = jnp.dot(hid.astype(BF16), wdb[...], preferred_element_type=F32)

    @pl.when(cnt_ref[i] == 0)
    def _():
        o_ref[...] = jnp.zeros_like(o_ref)


def moe_ffn(x, row_token, tables, layer, w_gate, w_up, w_down):
    n_tiles = row_token.shape[0]
    d = x.shape[1]
    ff = w_gate.shape[3]
    any_space = pl.BlockSpec(memory_space=pl.ANY)
    grid_spec = pltpu.PrefetchScalarGridSpec(
        num_scalar_prefetch=5,
        grid=(n_tiles,),
        in_specs=[pl.BlockSpec((1, 1, MOE_TM), lambda i, *_: (i, 0, 0), memory_space=pltpu.SMEM),
                  pl.BlockSpec((1, 1, MOE_TM), lambda i, *_: (jnp.minimum(i + 1, n_tiles - 1), 0, 0),
                               memory_space=pltpu.SMEM),
                  any_space, any_space, any_space, any_space],
        out_specs=pl.BlockSpec((MOE_TM, d), lambda i, *_: (i, 0)),
        scratch_shapes=[pltpu.VMEM((2, MOE_TM, d), F32),
                        pltpu.VMEM((2, d, ff), F32), pltpu.VMEM((2, d, ff), F32), pltpu.VMEM((2, ff, d), F32),
                        pltpu.VMEM((d, ff), BF16), pltpu.VMEM((d, ff), BF16), pltpu.VMEM((ff, d), BF16),
                        pltpu.SemaphoreType.DMA((2,)), pltpu.SemaphoreType.DMA((2,))],
    )
    return pl.pallas_call(
        functools.partial(_moe_ffn_kernel, layer=layer),
        grid_spec=grid_spec,
        out_shape=jax.ShapeDtypeStruct((n_tiles * MOE_TM, d), F32),
        compiler_params=_cparams(("arbitrary",)),
        name="moe_ffn",
    )(*tables, row_token, row_token, x, w_gate, w_up, w_down)


def _ln_moe_ple_kernel(x_ref, ya_ref, yb_ref, wt_ref, g_ref, b_ref, p_ref, wg_ref, wp_ref, o_ref, ob_ref):
    wt = wt_ref[...]
    z = DEEPNORM_ALPHA * x_ref[...] + (wt[:, 0:1] * ya_ref[...] + wt[:, 1:2] * yb_ref[...])
    zc = z - jnp.mean(z, axis=-1, keepdims=True)
    var = jnp.mean(zc * zc, axis=-1, keepdims=True)
    hid = zc * lax.rsqrt(var + LN_EPS) * g_ref[...] + b_ref[...]
    gate = _sigmoid(jnp.dot(hid.astype(BF16), wg_ref[...], preferred_element_type=F32))
    emb = jnp.dot(p_ref[...].astype(BF16), wp_ref[...], preferred_element_type=F32)
    out = hid + gate * emb
    o_ref[...] = out
    ob_ref[...] = out.astype(BF16)


def ln_moe_ple(x, ya, yb, wt, g, b, p, wg, wp, tm=512):
    m, d = x.shape
    pd = p.shape[1]
    tm = _row_tile(m, tm)
    row = pl.BlockSpec((tm, d), lambda i: (i, 0))
    vec = pl.BlockSpec((1, d), lambda i: (0, 0))
    once = pl.Buffered(1)
    return pl.pallas_call(
        _ln_moe_ple_kernel,
        grid=(m // tm,),
        in_specs=[row, row, row, pl.BlockSpec((tm, LANES), lambda i: (i, 0)), vec, vec,
                  pl.BlockSpec((tm, pd), lambda i: (i, 0)),
                  pl.BlockSpec((d, d), lambda i: (0, 0), pipeline_mode=once),
                  pl.BlockSpec((pd, d), lambda i: (0, 0), pipeline_mode=once)],
        out_specs=[row, row],
        out_shape=[jax.ShapeDtypeStruct((m, d), F32), jax.ShapeDtypeStruct((m, d), BF16)],
        compiler_params=_cparams(("arbitrary",)),
        name="ln_moe_ple",
    )(x, ya, yb, wt, g.reshape(1, d), b.reshape(1, d), p, wg, wp)


def _table_lookup(table, idx):
    hit = idx[:, None] == jnp.arange(table.shape[0], dtype=idx.dtype)[None, :]
    return jnp.sum(jnp.where(hit, table[None, :], 0), axis=1)


def router_weights(w_group, w_expert):
    return jnp.pad(jnp.concatenate([w_group, w_expert], axis=1),
                   ((0, 0), (0, LANES - MOE_GROUPS - MOE_EXPERTS)))


def hier_moe(x, ids, layer, w_gate, w_up, w_down):
    n = x.shape[0]

    n_pairs = n * MOE_TOPK
    n_tiles = -(-n_pairs // MOE_TM) + MOE_EXPERTS
    n_rows = n_tiles * MOE_TM
    i32 = jnp.int32
    eid = ids[:, :MOE_TOPK].reshape(n_pairs)
    order = jnp.argsort(eid, stable=True).astype(i32)
    rank = jnp.argsort(order).astype(i32)
    experts = jnp.arange(MOE_EXPERTS, dtype=i32)
    counts = jnp.sum((eid[:, None] == experts[None, :]).astype(i32), axis=0)
    starts = jnp.cumsum(counts) - counts
    tiles_per = (counts + MOE_TM - 1) // MOE_TM
    tile_starts = jnp.cumsum(tiles_per) - tiles_per
    dest = _table_lookup(tile_starts * MOE_TM - starts, eid) + rank
    tile_idx = jnp.arange(n_tiles, dtype=i32)
    tile_expert = jnp.sum((tile_idx[:, None] >= (tile_starts + tiles_per)[None, :]).astype(i32), axis=1)
    tile_valid = (tile_expert < MOE_EXPERTS).astype(i32)
    used = tiles_per > 0
    last_used = jnp.max(jnp.where(used, experts, 0))
    tile_expert = jnp.where(tile_valid == 1, tile_expert, last_used)
    tile_in_run = tile_idx - _table_lookup(tile_starts, tile_expert)
    off = tile_in_run[:, None] * MOE_TM + jnp.arange(MOE_TM, dtype=i32)[None, :]
    cnt = _table_lookup(counts, tile_expert)
    src = jnp.clip(_table_lookup(starts, tile_expert)[:, None] + off, 0, n_pairs - 1)
    row_ok = (off < cnt[:, None]) & (tile_valid[:, None] == 1)
    picked = jnp.take(order, src.reshape(n_rows), mode='clip').reshape(n_tiles, MOE_TM)
    row_token = jnp.where(row_ok, picked // MOE_TOPK, 0).reshape(n_tiles, 1, MOE_TM)
    tile_rows = jnp.clip(cnt - tile_in_run * MOE_TM, 0, MOE_TM) * tile_valid
    tile_first = tile_valid * (tile_in_run == 0).astype(i32)
    later_used = (experts[None, :] > experts[:, None]) & used[None, :]
    next_used = jnp.min(jnp.where(later_used, experts[None, :], MOE_EXPERTS), axis=1)
    next_used = jnp.where(next_used < MOE_EXPERTS, next_used, -1)
    run_index = jnp.cumsum(used.astype(i32)) - 1
    tables = (tile_expert, tile_rows, tile_first, _table_lookup(next_used, tile_expert),
              _table_lookup(run_index, tile_expert) % 2)

    y_sorted = moe_ffn(x, row_token, tables, layer, w_gate, w_up, w_down)
    dest2 = dest.reshape(n, MOE_TOPK)
    return (jnp.take(y_sorted, dest2[:, 0], axis=0, mode='clip'),
            jnp.take(y_sorted, dest2[:, 1], axis=0, mode='clip'))


SAMPLE_T_PAD = 8


def kernel(x_prompt, x_sample, state_mlstm_c, state_mlstm_n, state_mlstm_m, state_rwkv_s,
           state_rwkv_shift, state_gdn_s, state_gdn_conv, p_prompt, p_sample,
           w_in_ab, ml_b_i, ml_b_f, ml_norm_w, rw_mu, rw_w0, rw_w2, rw_a0, rw_a2, rw_g2,
           rw_k_k, rw_k_a, rw_r_k, rw_ln_w, rw_ln_b, w_out_ab,
           gd_w_in, gd_conv_w, gd_a_log, gd_dt_bias, gd_norm_w, gd_w_out,
           ln_mix_g, ln_mix_b, moe_w_group, moe_w_expert, moe_w_gate, moe_w_up, moe_w_down,
           ln_ffn_g, ln_ffn_b, ple_w_gate, ple_w_proj):
    pb, pt, d = x_prompt.shape
    sb, st, _ = x_sample.shape
    n_p = pb * pt
    n_s = sb * SAMPLE_T_PAD
    n = n_p + n_s
    pad_t = lambda a: jnp.pad(a, ((0, 0), (0, SAMPLE_T_PAD - st), (0, 0)))
    merge = lambda a_p, a_s: jnp.concatenate(
        [a_p.reshape(n_p, a_p.shape[-1]), pad_t(a_s).reshape(n_s, a_s.shape[-1])], axis=0)
    x = merge(x_prompt, x_sample)
    xb = x.astype(BF16)
    segs = (dict(row0=0, bsz=pb, t_pad=pt, t_valid=pt, ml_nb=1, rw_hb=16, rw_nb=1, gd_hpg=16),
            dict(row0=n_p, bsz=sb, t_pad=SAMPLE_T_PAD, t_valid=st, ml_nb=2, rw_hb=16, rw_nb=4, gd_hpg=32))
    zeros = lambda *shape: jnp.zeros(shape, F32)

    def last_rows(a, sg, count):
        assert sg['t_valid'] >= count
        idx = (sg['row0'] + jnp.arange(sg['bsz'])[:, None] * sg['t_pad']
               + (sg['t_valid'] - count + jnp.arange(count))[None, :])
        return jnp.take(a, idx.reshape(-1), axis=0, mode='clip').reshape(sg['bsz'], count, a.shape[-1])

    new_states = [[[] for _ in range(7)] for _ in segs]
    for layer in range(DEPTH):
        li = layer // 2
        if layer % 2 == 0:
            w_in_t = jnp.transpose(w_in_ab[li])
            proj_ml = matmul_wt(xb, w_in_t, 0, 3328, 1664)
            proj_rw = matmul_wt(xb, w_in_t, ML_COLS, RW_COLS, RW_COLS // 2)
            rw_prm = dict(mu=rw_mu[li], w0=rw_w0[li], w2=rw_w2[li], a0=rw_a0[li], a2=rw_a2[li], g2=rw_g2[li],
                          k_k=rw_k_k[li], k_a=rw_k_a[li], r_k=rw_r_k[li])
            heads_out = jnp.zeros((n, ML_V + RW_WIDTH), BF16)
            for si, sg in enumerate(segs):
                bsz, t_pad, t_valid, row0 = sg['bsz'], sg['t_pad'], sg['t_valid'], sg['row0']
                if si == 0:
                    c0, n0, m0 = zeros(bsz, ML_HEADS, ML_DK, ML_DV), zeros(bsz, ML_HEADS, ML_DK), zeros(bsz, ML_HEADS)
                    s0, sh0 = zeros(bsz, RW_HEADS, RW_DH, RW_DH), zeros(bsz, RW_COLS)
                else:
                    c0, n0, m0 = state_mlstm_c[li], state_mlstm_n[li], state_mlstm_m[li]
                    s0, sh0 = state_rwkv_s[li], state_rwkv_shift[li]
                heads_out, c, nn, m = mlstm_group(proj_ml, heads_out, row0, bsz, t_pad, t_valid,
                                                  min(ML_CHUNK, t_pad), c0, n0, m0,
                                                  ml_b_i[li], ml_b_f[li], ml_norm_w[li], sg['ml_nb'])
                prep = rwkv_prep(proj_rw, row0, bsz, t_pad, sh0, rw_prm, 256)
                heads_out, rs = rwkv_chunk(prep, heads_out, ML_V, row0, bsz, t_pad, t_valid,
                                           min(RW_CHUNK, t_pad), s0, rw_ln_w[li], rw_ln_b[li],
                                           sg['rw_hb'], sg['rw_nb'])
                new_shift = last_rows(proj_rw, sg, 1)[:, 0]
                for slot, val in zip(range(5), (c, nn, m, rs, new_shift)):
                    new_states[si][slot].append(val)
            mix = matmul(heads_out, w_out_ab[li], d, 1024)
        else:
            w_in_t = jnp.transpose(gd_w_in[li])
            n_qkvz = GD_CONV_DIM + GD_V_WIDTH
            proj = matmul_wt(xb, w_in_t, 0, n_qkvz, 1024)
            gates = matmul_wt(xb, w_in_t, n_qkvz, LANES, LANES)
            gd_out = jnp.zeros((n, GD_V_WIDTH), BF16)
            for si, sg in enumerate(segs):
                bsz, t_pad, t_valid, row0 = sg['bsz'], sg['t_pad'], sg['t_valid'], sg['row0']
                if si == 0:
                    s0, buf0 = zeros(bsz, GD_V_HEADS, GD_DK, GD_DV), zeros(bsz, GD_CONV - 1, GD_CONV_DIM)
                else:
                    s0, buf0 = state_gdn_s[li], state_gdn_conv[li]
                gd_out, gs, gcb = gdn_chunk(proj, gates, buf0, gd_conv_w[li], gd_out, row0, bsz, t_pad, t_valid,
                                            min(GD_CHUNK, t_pad), s0, gd_a_log[li], gd_dt_bias[li],
                                            gd_norm_w[li], sg['gd_hpg'])
                new_states[si][5].append(gs)
                new_states[si][6].append(gcb)
            mix = matmul(gd_out, gd_w_out[li], d, 512)
        x, wt, ids = ln_residual_router(x, mix, ln_mix_g[layer], ln_mix_b[layer],
                                        router_weights(moe_w_group[layer], moe_w_expert[layer]))
        ya, yb = hier_moe(x, ids, layer, moe_w_gate, moe_w_up, moe_w_down)
        x, xb = ln_moe_ple(x, ya, yb, wt, ln_ffn_g[layer], ln_ffn_b[layer],
                           merge(p_prompt[layer], p_sample[layer]),
                           ple_w_gate[layer].astype(BF16), ple_w_proj[layer].astype(BF16))
    y_prompt = x[:n_p].reshape(pb, pt, d)
    y_sample = x[n_p:].reshape(sb, SAMPLE_T_PAD, d)[:, :st]
    stack = lambda vals: jnp.stack(vals)
    return (y_prompt, y_sample) + tuple(stack(v) for v in new_states[0]) + tuple(stack(v) for v in new_states[1])
```
